```python
import math
import jax, jax.numpy as jnp
from jax import lax
import numpy as np

D_MODEL = 2048
BATCH = 2
SEQ = 8192
DEPTH = 1

CONV_CH = 1024
CONV_WIDTH = 31
N_ATTN_HEADS = 12
HEADS_PER_GROUP = 4
HEAD_DIM = 128
ATTN_WIDTH = N_ATTN_HEADS * HEAD_DIM
DILATED_GROUPS = ((128, 1), (512, 4), (2048, 16))
NUM_BUCKETS = 32
REL_MAX_DISTANCE = 1024
N_BRANCHES = 2
IN_COLS = 2 * CONV_CH + 3 * ATTN_WIDTH + N_BRANCHES * D_MODEL
N_GROUPS = 8
EXPERTS_PER_GROUP = 8
N_EXPERTS = N_GROUPS * EXPERTS_PER_GROUP
TOP_K = 2
EXPERT_FF = 512
MOE_BLOCK = 128

EPS = 1e-6
NEG_INF = -1e30

kernel_name = 'hybrid_conv_dilattn_hmoe_encoder'


def _rmsnorm(x, g):
    xf = x.astype(jnp.float32)
    y = xf * lax.rsqrt(jnp.mean(xf * xf, axis=-1, keepdims=True) + EPS)
    return (y * g.astype(jnp.float32)).astype(x.dtype)


def _layernorm(x, g, b):
    xf = x.astype(jnp.float32)
    mu = jnp.mean(xf, axis=-1, keepdims=True)
    var = jnp.mean(jnp.square(xf - mu), axis=-1, keepdims=True)
    y = (xf - mu) * lax.rsqrt(var + EPS)
    return (y * g.astype(jnp.float32) + b.astype(jnp.float32)).astype(x.dtype)


def _t5_bucket(rel):
    nb = NUM_BUCKETS // 2
    max_exact = nb // 2
    bucket = jnp.where(rel > 0, nb, 0)
    n = jnp.abs(rel)
    nf = jnp.maximum(n, 1).astype(jnp.float32)
    large = max_exact + (jnp.log(nf / max_exact) / math.log(REL_MAX_DISTANCE / max_exact)
                         * (nb - max_exact)).astype(jnp.int32)
    large = jnp.minimum(large, nb - 1)
    return bucket + jnp.where(n < max_exact, n, large)


def _to_subseq(t, r):
    b, s = t.shape[:2]
    t = t.reshape((b, s // r, r) + t.shape[2:])
    t = jnp.moveaxis(t, 2, 1)
    return t.reshape((b * r, s // r) + t.shape[3:])


def _from_subseq(t, b, r):
    l = t.shape[1]
    t = t.reshape((b, r, l) + t.shape[2:])
    t = jnp.moveaxis(t, 1, 2)
    return t.reshape((b, l * r) + t.shape[3:])


def _neighbour_blocks(t):
    tp = jnp.pad(t, ((0, 0), (1, 1)) + ((0, 0),) * (t.ndim - 2))
    return jnp.concatenate([tp[:, :-2], tp[:, 1:-1], tp[:, 2:]], axis=2)


def _banded_attention(q, k, v, bias, half):
    n, l, h, hd = q.shape
    blk = half
    nb = -(-l // blk)
    lp = nb * blk
    pad = ((0, 0), (0, lp - l), (0, 0), (0, 0))
    qb = jnp.pad(q, pad).reshape(n, nb, blk, h, hd)
    kb = _neighbour_blocks(jnp.pad(k, pad).reshape(n, nb, blk, h, hd))
    vb = _neighbour_blocks(jnp.pad(v, pad).reshape(n, nb, blk, h, hd))
    valid = _neighbour_blocks((jnp.arange(lp) < l).reshape(1, nb, blk))
    delta = (jnp.arange(3 * blk) - blk)[None, :] - jnp.arange(blk)[:, None]
    band = jnp.abs(delta) <= half
    s = jnp.einsum('nbqhd,nbkhd->nbhqk', qb, kb, preferred_element_type=jnp.float32)
    s = s * (HEAD_DIM ** -0.5) + bias[None, None]
    mask = band[None, None, None] & valid[:, :, None, None, :]
    s = jnp.where(mask, s, NEG_INF)
    m = jnp.max(s, axis=-1, keepdims=True)
    p = jnp.exp(s - m)
    denom = jnp.sum(p, axis=-1, keepdims=True)
    o = jnp.einsum('nbhqk,nbkhd->nbqhd', (p / denom).astype(v.dtype), vb)
    lse = (m + jnp.log(denom))[..., 0]
    o = o.reshape(n, lp, h, hd)[:, :l]
    lse = jnp.moveaxis(lse, 2, 3).reshape(n, lp, h)[:, :l]
    return o, lse


def _dilated_mixture_attention(q, k, v, rel_bias):
    b = q.shape[0]
    outs, lses = [], []
    for gi, (window, dil) in enumerate(DILATED_GROUPS):
        hs = slice(gi * HEADS_PER_GROUP, (gi + 1) * HEADS_PER_GROUP)
        half = window // (2 * dil)
        delta = (jnp.arange(3 * half) - half)[None, :] - jnp.arange(half)[:, None]
        bias = jnp.transpose(rel_bias[_t5_bucket(delta * dil)][:, :, hs], (2, 0, 1)).astype(jnp.float32)
        o, lse = _banded_attention(_to_subseq(q[:, :, hs], dil), _to_subseq(k[:, :, hs], dil),
                                   _to_subseq(v[:, :, hs], dil), bias, half)
        outs.append(_from_subseq(o, b, dil))
        lses.append(_from_subseq(lse, b, dil))
    w = jax.nn.softmax(jnp.stack(lses, axis=0), axis=0)
    return jnp.einsum('gbsh,gbshd->bshd', w.astype(q.dtype), jnp.stack(outs, axis=0))


def _conformer_conv(u_glu, w_dw, b_dw, ln_g, ln_b, w_pw, b_pw):
    a, g = jnp.split(u_glu, 2, axis=-1)
    u = a * jax.nn.sigmoid(g)
    u = lax.conv_general_dilated(u, w_dw[:, None, :], (1,), [(CONV_WIDTH // 2, CONV_WIDTH // 2)],
                                 dimension_numbers=('NWC', 'WIO', 'NWC'),
                                 feature_group_count=CONV_CH) + b_dw
    u = jax.nn.silu(_layernorm(u, ln_g, ln_b))
    return u @ w_pw + b_pw


def _mixing(h, w_in, b_in, w_dw, b_dw, ln_g, ln_b, w_conv_out, b_conv_out, rel_bias, w_attn_out, w_out):
    b, s, _ = h.shape
    z = h @ w_in + b_in
    o1 = 2 * CONV_CH
    o2 = o1 + ATTN_WIDTH
    o3 = o2 + ATTN_WIDTH
    o4 = o3 + ATTN_WIDTH
    u_glu, q, k, v, gates = jnp.split(z, [o1, o2, o3, o4], axis=-1)
    branch_a = _conformer_conv(u_glu, w_dw, b_dw, ln_g, ln_b, w_conv_out, b_conv_out)
    hshape = (b, s, N_ATTN_HEADS, HEAD_DIM)
    att = _dilated_mixture_attention(q.reshape(hshape), k.reshape(hshape), v.reshape(hshape), rel_bias)
    branch_b = att.reshape(b, s, HEADS_PER_GROUP * HEAD_DIM) @ w_attn_out
    g_a, g_b = jnp.split(jax.nn.sigmoid(gates), 2, axis=-1)
    return (g_a * branch_a + g_b * branch_b) @ w_out


def _hier_moe(h, w_rg, b_rg, w_re, b_re, w1, w3, w2):
    b, s, d = h.shape
    t = b * s
    ht = h.reshape(t, d)
    glog = (ht @ w_rg).astype(jnp.float32) + b_rg
    gprob = jax.nn.softmax(glog, axis=-1)
    g_sel = jnp.argmax(glog, axis=-1)
    p_g = jnp.take_along_axis(gprob, g_sel[:, None], axis=1)[:, 0]
    elog_all = (ht @ w_re).astype(jnp.float32).reshape(t, N_GROUPS, EXPERTS_PER_GROUP) + b_re
    elog = jnp.take_along_axis(elog_all, g_sel[:, None, None], axis=1)[:, 0]
    top_v, top_i = lax.top_k(elog, TOP_K)
    weight = p_g[:, None] * jax.nn.softmax(top_v, axis=-1)
    expert_id = g_sel[:, None] * EXPERTS_PER_GROUP + top_i
    m = t * TOP_K
    flat_e = expert_id.reshape(m)
    flat_tok = jnp.repeat(jnp.arange(t), TOP_K)
    flat_w = weight.reshape(m)
    order = jnp.argsort(flat_e)
    se, stok, sw = flat_e[order], flat_tok[order], flat_w[order]
    counts = jnp.bincount(flat_e, length=N_EXPERTS)
    starts = jnp.cumsum(counts) - counts
    pcounts = (counts + MOE_BLOCK - 1) // MOE_BLOCK * MOE_BLOCK
    pends = jnp.cumsum(pcounts)
    pstarts = pends - pcounts
    dest = pstarts[se] + jnp.arange(m) - starts[se]
    n_blocks = (m + N_EXPERTS * (MOE_BLOCK - 1) + MOE_BLOCK - 1) // MOE_BLOCK
    p_rows = n_blocks * MOE_BLOCK
    block_e = jnp.minimum(jnp.searchsorted(pends, jnp.arange(n_blocks) * MOE_BLOCK, side='right'),
                          N_EXPERTS - 1)
    xs = jnp.zeros((p_rows, d), ht.dtype).at[dest].set(ht[stok])

    def expert_block(args):
        xb, e = args
        return (jax.nn.silu(xb @ w1[e]) * (xb @ w3[e])) @ w2[e]

    ys = lax.map(expert_block, (xs.reshape(n_blocks, MOE_BLOCK, d), block_e)).reshape(p_rows, d)
    out = jax.ops.segment_sum(ys[dest] * sw[:, None].astype(ys.dtype), stok, num_segments=t)
    return out.reshape(b, s, d)


def setup_inputs(seed: int = 0) -> dict:
    key = jax.random.key(seed)
    ks = jax.random.split(key, 28)
    f32 = jnp.float32
    d = D_MODEL
    nl = DEPTH

    def nrm(k, shape, scale):
        return jax.random.normal(k, shape, f32) * scale

    return {
        'x': nrm(ks[0], (BATCH, SEQ, d), 1.0),
        'c': nrm(ks[1], (BATCH, d), 1.0),
        'w_ada': nrm(ks[2], (nl, d, 6 * d), 0.5 * d ** -0.5),
        'b_ada': nrm(ks[3], (nl, 6 * d), 0.01),
        'g_pre_mix': 1.0 + nrm(ks[4], (nl, d), 0.05),
        'g_post_mix': 1.0 + nrm(ks[5], (nl, d), 0.05),
        'w_in': nrm(ks[6], (nl, d, IN_COLS), d ** -0.5),
        'b_in': nrm(ks[7], (nl, IN_COLS), 0.01),
        'w_dw': nrm(ks[8], (nl, CONV_WIDTH, CONV_CH), CONV_WIDTH ** -0.5),
        'b_dw': nrm(ks[9], (nl, CONV_CH), 0.01),
        'ln_conv_g': 1.0 + nrm(ks[10], (nl, CONV_CH), 0.05),
        'ln_conv_b': nrm(ks[11], (nl, CONV_CH), 0.01),
        'w_conv_out': nrm(ks[12], (nl, CONV_CH, d), CONV_CH ** -0.5),
        'b_conv_out': nrm(ks[13], (nl, d), 0.01),
        'rel_bias': nrm(ks[14], (NUM_BUCKETS, N_ATTN_HEADS), 0.5),
        'w_attn_out': nrm(ks[15], (nl, HEADS_PER_GROUP * HEAD_DIM, d), (HEADS_PER_GROUP * HEAD_DIM) ** -0.5),
        'w_out': nrm(ks[16], (nl, d, d), d ** -0.5),
        'g_pre_ffn': 1.0 + nrm(ks[17], (nl, d), 0.05),
        'g_post_ffn': 1.0 + nrm(ks[18], (nl, d), 0.05),
        'w_router_group': nrm(ks[19], (nl, d, N_GROUPS), d ** -0.5),
        'b_router_group': nrm(ks[20], (nl, N_GROUPS), 0.01),
        'w_router_expert': nrm(ks[21], (nl, d, N_EXPERTS), d ** -0.5),
        'b_router_expert': nrm(ks[22], (nl, N_GROUPS, EXPERTS_PER_GROUP), 0.01),
        'w1': nrm(ks[23], (nl, N_EXPERTS, d, EXPERT_FF), d ** -0.5),
        'w3': nrm(ks[24], (nl, N_EXPERTS, d, EXPERT_FF), d ** -0.5),
        'w2': nrm(ks[25], (nl, N_EXPERTS, EXPERT_FF, d), EXPERT_FF ** -0.5),
    }


def reference(x, c, w_ada, b_ada, g_pre_mix, g_post_mix, w_in, b_in, w_dw, b_dw, ln_conv_g, ln_conv_b,
              w_conv_out, b_conv_out, rel_bias, w_attn_out, w_out, g_pre_ffn, g_post_ffn,
              w_router_group, b_router_group, w_router_expert, b_router_expert, w1, w3, w2):
    for l in range(DEPTH):
        mod = jax.nn.silu(c) @ w_ada[l] + b_ada[l]
        shift1, scale1, gate1, shift2, scale2, gate2 = jnp.split(mod[:, None, :], 6, axis=-1)
        h = _rmsnorm(x, g_pre_mix[l]) * (1.0 + scale1) + shift1
        y = _mixing(h, w_in[l], b_in[l], w_dw[l], b_dw[l], ln_conv_g[l], ln_conv_b[l],
                    w_conv_out[l], b_conv_out[l], rel_bias, w_attn_out[l], w_out[l])
        x = x + gate1 * _rmsnorm(y, g_post_mix[l])
        h = _rmsnorm(x, g_pre_ffn[l]) * (1.0 + scale2) + shift2
        y = _hier_moe(h, w_router_group[l], b_router_group[l], w_router_expert[l], b_router_expert[l],
                      w1[l], w3[l], w2[l])
        x = x + gate2 * _rmsnorm(y, g_post_ffn[l])
    return x
```

```python
import functools
import math

import jax
import jax.numpy as jnp
from jax import lax
from jax.experimental import pallas as pl
from jax.experimental.pallas import tpu as pltpu

D_MODEL = 2048
CONV_CH = 1024
CONV_WIDTH = 31
N_ATTN_HEADS = 12
HEADS_PER_GROUP = 4
HEAD_DIM = 128
ATTN_WIDTH = N_ATTN_HEADS * HEAD_DIM
DILATED_GROUPS = ((128, 1), (512, 4), (2048, 16))
NUM_BUCKETS = 32
REL_MAX_DISTANCE = 1024
N_GROUPS = 8
EXPERTS_PER_GROUP = 8
N_EXPERTS = N_GROUPS * EXPERTS_PER_GROUP
TOP_K = 2
EXPERT_FF = 512
EPS = 1e-6
NEG_INF = -1e30

IN_COLS = 2 * CONV_CH + 3 * ATTN_WIDTH + 2 * D_MODEL
COL_GLU = 2 * D_MODEL
COL_Q = COL_GLU + 2 * CONV_CH
COL_K = COL_Q + ATTN_WIDTH
COL_V = COL_K + ATTN_WIDTH

LANES = 128
SUBLANES_BF16 = 16
V7X_VMEM_LIMIT_BYTES = 56 * 1024 * 1024

ATTN_HALF = 64
GROUP_W = HEADS_PER_GROUP * HEAD_DIM
MOE_ROWS = 128
ROUTE_LANES = 128

bf16 = jnp.bfloat16
f32 = jnp.float32


def _cparams(sem):
    return pltpu.CompilerParams(dimension_semantics=sem, vmem_limit_bytes=V7X_VMEM_LIMIT_BYTES)


def _sigmoid(x):
    return 1.0 / (1.0 + jnp.exp(-x))


def _ada_kernel(c_ref, w_ref, b_ref, o_ref):
    c = c_ref[...]
    a = (c * _sigmoid(c)).astype(bf16)
    o_ref[...] = jnp.dot(a, w_ref[...].astype(bf16), preferred_element_type=f32) + b_ref[...]


def _ada(c_pad, w_ada, b_ada):
    rows, d = c_pad.shape
    n = w_ada.shape[1]
    tn = 1024
    return pl.pallas_call(
        _ada_kernel,
        grid=(n // tn,),
        in_specs=[pl.BlockSpec((rows, d), lambda j: (0, 0)),
                  pl.BlockSpec((d, tn), lambda j: (0, j)),
                  pl.BlockSpec((1, tn), lambda j: (0, j))],
        out_specs=pl.BlockSpec((rows, tn), lambda j: (0, j)),
        out_shape=jax.ShapeDtypeStruct((rows, n), f32),
        compiler_params=_cparams(("arbitrary",)),
        name="ada_mod",
    )(c_pad, w_ada, b_ada)


def _inproj_kernel(x_ref, g_ref, scale_ref, shift_ref, w_ref, b_ref, z_ref, h_scr):
    @pl.when(pl.program_id(1) == 0)
    def _():
        x = x_ref[...]
        ms = jnp.mean(x * x, axis=-1, keepdims=True)
        h = x * lax.rsqrt(ms + EPS) * g_ref[...]
        h_scr[...] = (h * (1.0 + scale_ref[...]) + shift_ref[...]).astype(bf16)

    z = jnp.dot(h_scr[...], w_ref[...], preferred_element_type=f32) + b_ref[...]
    z_ref[...] = z.astype(bf16)


def _inproj(x2, g, scale, shift, w, b, *, seq, tm, tn):
    t, d = x2.shape
    n = w.shape[1]
    bmap = lambda i, j: ((i * tm) // seq, 0, 0)
    return pl.pallas_call(
        _inproj_kernel,
        grid=(t // tm, n // tn),
        in_specs=[pl.BlockSpec((tm, d), lambda i, j: (i, 0)),
                  pl.BlockSpec((1, d), lambda i, j: (0, 0)),
                  pl.BlockSpec((None, 1, d), bmap),
                  pl.BlockSpec((None, 1, d), bmap),
                  pl.BlockSpec((d, tn), lambda i, j: (0, j)),
                  pl.BlockSpec((1, tn), lambda i, j: (0, j))],
        out_specs=pl.BlockSpec((tm, tn), lambda i, j: (i, j)),
        out_shape=jax.ShapeDtypeStruct((t, n), bf16),
        scratch_shapes=[pltpu.VMEM((tm, d), bf16)],
        compiler_params=_cparams(("arbitrary", "arbitrary")),
        name="inproj",
    )(x2, g, scale, shift, w, b)


CONV_HALO = 16
CONV_CHUNK = 16
CONV_SUB = CONV_CH // LANES
NORM_CHUNK = 64


def _conv_kernel(ap_ref, ac_ref, an_ref, gp_ref, gc_ref, gn_ref, w_ref, bdw_ref, lng_ref, lnb_ref,
                 o_ref, u_scr, y_scr, *, tm):
    i = pl.program_id(1)
    last = pl.num_programs(1) - 1

    def glu(a_ref, g_ref):
        return a_ref[...].astype(f32) * _sigmoid(g_ref[...].astype(f32))

    def put(row0, val):
        for j in range(CONV_SUB):
            u_scr[pl.ds(row0 * CONV_SUB + j, val.shape[0], stride=CONV_SUB), :] = (
                val[:, j * LANES:(j + 1) * LANES])

    put(0, jnp.where(i > 0, glu(ap_ref, gp_ref), 0.0))
    put(CONV_HALO, glu(ac_ref, gc_ref))
    put(CONV_HALO + tm, jnp.where(i < last, glu(an_ref, gn_ref), 0.0))

    def chunk(c, carry):
        r0 = c * CONV_CHUNK
        acc = jnp.zeros((CONV_CHUNK, CONV_SUB, LANES), f32) + bdw_ref[...][None]
        for k in range(CONV_WIDTH):
            start = pl.multiple_of((r0 + k + 1) * CONV_SUB, CONV_SUB)
            xk = u_scr[pl.ds(start, CONV_CHUNK * CONV_SUB), :].reshape(CONV_CHUNK, CONV_SUB, LANES)
            acc = acc + w_ref[k][None] * xk
        out0 = pl.multiple_of(r0 * CONV_SUB, CONV_CHUNK * CONV_SUB)
        y_scr[pl.ds(out0, CONV_CHUNK * CONV_SUB), :] = acc.reshape(CONV_CHUNK * CONV_SUB, LANES)
        return carry

    lax.fori_loop(0, tm // CONV_CHUNK, chunk, 0)

    def norm(c, carry):
        r0 = pl.multiple_of(c * NORM_CHUNK, NORM_CHUNK)
        acc = jnp.concatenate(
            [y_scr[pl.ds(r0 * CONV_SUB + j, NORM_CHUNK, stride=CONV_SUB), :] for j in range(CONV_SUB)],
            axis=1)
        mu = jnp.mean(acc, axis=-1, keepdims=True)
        cen = acc - mu
        var = jnp.mean(cen * cen, axis=-1, keepdims=True)
        y = cen * lax.rsqrt(var + EPS) * lng_ref[...] + lnb_ref[...]
        o_ref[pl.ds(r0, NORM_CHUNK), :] = (y * _sigmoid(y)).astype(bf16)
        return carry

    lax.fori_loop(0, tm // NORM_CHUNK, norm, 0)


def _conv(z3, w_dw, b_dw, ln_g, ln_b, *, tm):
    b, s, _ = z3.shape
    hb = tm // CONV_HALO
    nhb = s // CONV_HALO
    ca, cg = COL_GLU // CONV_CH, COL_GLU // CONV_CH + 1

    def halo(col, which):
        if which < 0:
            return pl.BlockSpec((None, CONV_HALO, CONV_CH),
                                lambda bb, i: (bb, jnp.maximum(i * hb - 1, 0), col))
        return pl.BlockSpec((None, CONV_HALO, CONV_CH),
                            lambda bb, i: (bb, jnp.minimum((i + 1) * hb, nhb - 1), col))

    cur = lambda col: pl.BlockSpec((None, tm, CONV_CH), lambda bb, i: (bb, i, col))
    vec = pl.BlockSpec((1, CONV_CH), lambda bb, i: (0, 0))
    return pl.pallas_call(
        functools.partial(_conv_kernel, tm=tm),
        grid=(b, s // tm),
        in_specs=[halo(ca, -1), cur(ca), halo(ca, 1), halo(cg, -1), cur(cg), halo(cg, 1),
                  pl.BlockSpec((CONV_WIDTH, CONV_SUB, LANES), lambda bb, i: (0, 0, 0)),
                  pl.BlockSpec((CONV_SUB, LANES), lambda bb, i: (0, 0)), vec, vec],
        out_specs=pl.BlockSpec((None, tm, CONV_CH), lambda bb, i: (bb, i, 0)),
        out_shape=jax.ShapeDtypeStruct((b, s, CONV_CH), bf16),
        scratch_shapes=[pltpu.VMEM(((tm + 2 * CONV_HALO) * CONV_SUB, LANES), f32),
                        pltpu.VMEM((tm * CONV_SUB, LANES), f32)],
        compiler_params=_cparams(("arbitrary", "arbitrary")),
        name="conv_branch",
    )(z3, z3, z3, z3, z3, z3, w_dw.reshape(CONV_WIDTH, CONV_SUB, LANES), b_dw.reshape(CONV_SUB, LANES),
      ln_g, ln_b)


def _attn_kernel(q_ref, kp_ref, kc_ref, kn_ref, vp_ref, vc_ref, vn_ref, bias_ref, o_ref, lse_ref,
                 k_scr, v_scr, *, tq, sub_len):
    it = pl.program_id(2)
    h = ATTN_HALF
    k_scr[0:h, :] = kp_ref[...]
    k_scr[h:h + tq, :] = kc_ref[...]
    k_scr[h + tq:, :] = kn_ref[...]
    v_scr[0:h, :] = vp_ref[...]
    v_scr[h:h + tq, :] = vc_ref[...]
    v_scr[h + tq:, :] = vn_ref[...]

    scale = HEAD_DIM ** -0.5
    col = lax.broadcasted_iota(jnp.int32, (h, 3 * h), 1)
    lane = lax.broadcasted_iota(jnp.int32, (h, LANES), 1)
    for jb in range(tq // h):
        key0 = it * tq + (jb - 1) * h
        valid = (col + key0 >= 0) & (col + key0 < sub_len)
        lse_tile = jnp.zeros((h, LANES), f32)
        for hh in range(HEADS_PER_GROUP):
            cs = slice(hh * HEAD_DIM, (hh + 1) * HEAD_DIM)
            q = q_ref[jb * h:(jb + 1) * h, cs]
            kw = k_scr[jb * h:(jb + 3) * h, cs]
            s = lax.dot_general(q, kw, (((1,), (1,)), ((), ())), preferred_element_type=f32)
            s = jnp.where(valid, s * scale + bias_ref[hh], NEG_INF)
            m = jnp.max(s, axis=-1, keepdims=True)
            p = jnp.exp(s - m)
            den = jnp.sum(p, axis=-1, keepdims=True)
            pv = jnp.dot(p.astype(bf16), v_scr[jb * h:(jb + 3) * h, cs], preferred_element_type=f32)
            o_ref[jb * h:(jb + 1) * h, cs] = (pv / den).astype(bf16)
            lse_tile = jnp.where(lane == hh, m + jnp.log(den), lse_tile)
        lse_ref[jb * h:(jb + 1) * h, :] = lse_tile


def _attn_group(z, bias, gi, dil, *, batch, seq):
    sub_len = seq // dil
    tq = min(512, sub_len)
    zc = IN_COLS // GROUP_W
    zv = z.reshape(batch, sub_len, dil * IN_COLS)
    hb = tq // ATTN_HALF
    nhb = sub_len // ATTN_HALF

    def cur(col0):
        cb = col0 // GROUP_W + gi
        return pl.BlockSpec((None, tq, GROUP_W), lambda b, r, i: (b, i, r * zc + cb))

    def prev(col0):
        cb = col0 // GROUP_W + gi
        return pl.BlockSpec((None, ATTN_HALF, GROUP_W),
                            lambda b, r, i: (b, jnp.maximum(i * hb - 1, 0), r * zc + cb))

    def nxt(col0):
        cb = col0 // GROUP_W + gi
        return pl.BlockSpec((None, ATTN_HALF, GROUP_W),
                            lambda b, r, i: (b, jnp.minimum((i + 1) * hb, nhb - 1), r * zc + cb))

    o, lse = pl.pallas_call(
        functools.partial(_attn_kernel, tq=tq, sub_len=sub_len),
        grid=(batch, dil, sub_len // tq),
        in_specs=[cur(COL_Q), prev(COL_K), cur(COL_K), nxt(COL_K), prev(COL_V), cur(COL_V), nxt(COL_V),
                  pl.BlockSpec((HEADS_PER_GROUP, ATTN_HALF, 3 * ATTN_HALF), lambda b, r, i: (0, 0, 0))],
        out_specs=[pl.BlockSpec((None, tq, GROUP_W), lambda b, r, i: (b, i, r)),
                   pl.BlockSpec((None, tq, LANES), lambda b, r, i: (b, i, r))],
        out_shape=[jax.ShapeDtypeStruct((batch, sub_len, dil * GROUP_W), bf16),
                   jax.ShapeDtypeStruct((batch, sub_len, dil * LANES), f32)],
        scratch_shapes=[pltpu.VMEM((tq + 2 * ATTN_HALF, GROUP_W), bf16),
                        pltpu.VMEM((tq + 2 * ATTN_HALF, GROUP_W), bf16)],
        compiler_params=_cparams(("arbitrary", "arbitrary", "arbitrary")),
        name=f"attn_g{gi}",
    )(zv, zv, zv, zv, zv, zv, zv, bias)
    return o.reshape(batch * seq, GROUP_W), lse.reshape(batch * seq, LANES)


def _t5_bucket(rel):
    nb = NUM_BUCKETS // 2
    max_exact = nb // 2
    bucket = jnp.where(rel > 0, nb, 0)
    n = jnp.abs(rel)
    nf = jnp.maximum(n, 1).astype(f32)
    large = max_exact + (jnp.log(nf / max_exact) / math.log(REL_MAX_DISTANCE / max_exact)
                         * (nb - max_exact)).astype(jnp.int32)
    large = jnp.minimum(large, nb - 1)
    return bucket + jnp.where(n < max_exact, n, large)


def _band_bias(rel_bias, gi, dil):
    h = ATTN_HALF
    delta = (jnp.arange(3 * h) - h)[None, :] - jnp.arange(h)[:, None]
    tbl = rel_bias[_t5_bucket(delta * dil)][:, :, gi * HEADS_PER_GROUP:(gi + 1) * HEADS_PER_GROUP]
    tbl = jnp.transpose(tbl, (2, 0, 1)).astype(f32)
    return jnp.where((jnp.abs(delta) <= h)[None], tbl, NEG_INF)


def _pack_pairs(v):
    n = v.shape[1] // 2
    lo = lax.bitcast_convert_type(v[:, :n].astype(bf16).astype(f32), jnp.uint32)
    hi = lax.bitcast_convert_type(v[:, n:].astype(bf16).astype(f32), jnp.uint32)
    return (lo >> 16) | (hi & jnp.uint32(0xFFFF0000))


def _unpack_pairs(u):
    lo = lax.bitcast_convert_type(u << 16, f32)
    hi = lax.bitcast_convert_type(u & jnp.uint32(0xFFFF0000), f32)
    return jnp.concatenate([lo, hi], axis=1)


def _merge_kernel(x_ref, ca_ref, o0_ref, o1_ref, o2_ref, l0_ref, l1_ref, l2_ref, ga_ref, gb_ref,
                  wco_ref, bco_ref, wao_ref, wo_ref, gpost_ref, gate1_ref, gpre_ref, scale2_ref,
                  shift2_ref, wr_ref, br_ref, x1_ref, h2p_ref, logit_ref):
    branch_a = jnp.dot(ca_ref[...], wco_ref[...], preferred_element_type=f32) + bco_ref[...]

    l0, l1, l2 = l0_ref[...], l1_ref[...], l2_ref[...]
    lmax = jnp.maximum(jnp.maximum(l0, l1), l2)
    e0, e1, e2 = jnp.exp(l0 - lmax), jnp.exp(l1 - lmax), jnp.exp(l2 - lmax)
    inv = 1.0 / (e0 + e1 + e2)
    parts = []
    for hh in range(HEADS_PER_GROUP):
        cs = slice(hh * HEAD_DIM, (hh + 1) * HEAD_DIM)
        ls = slice(hh, hh + 1)
        parts.append((e0[:, ls] * inv[:, ls]) * o0_ref[:, cs].astype(f32)
                     + (e1[:, ls] * inv[:, ls]) * o1_ref[:, cs].astype(f32)
                     + (e2[:, ls] * inv[:, ls]) * o2_ref[:, cs].astype(f32))
    att = jnp.concatenate(parts, axis=1).astype(bf16)
    branch_b = jnp.dot(att, wao_ref[...], preferred_element_type=f32)

    mix = _sigmoid(ga_ref[...].astype(f32)) * branch_a + _sigmoid(gb_ref[...].astype(f32)) * branch_b
    y = jnp.dot(mix.astype(bf16), wo_ref[...], preferred_element_type=f32)

    yn = y * lax.rsqrt(jnp.mean(y * y, axis=-1, keepdims=True) + EPS) * gpost_ref[...]
    x1 = x_ref[...] + gate1_ref[...] * yn
    x1_ref[...] = x1
    hn = x1 * lax.rsqrt(jnp.mean(x1 * x1, axis=-1, keepdims=True) + EPS) * gpre_ref[...]
    h2 = hn * (1.0 + scale2_ref[...]) + shift2_ref[...]
    h2p_ref[...] = _pack_pairs(h2)
    logit_ref[...] = jnp.dot(h2.astype(bf16), wr_ref[...], preferred_element_type=f32) + br_ref[...]


def _merge(x2, conv_act, os_, lses, z, w_co, b_co, w_ao, w_o, g_post, gate1, g_pre, scale2, shift2,
           w_r, b_r, *, seq, tm):
    t, d = x2.shape
    rows = lambda w: pl.BlockSpec((tm, w), lambda i: (i, 0))
    full = lambda a: pl.BlockSpec(a.shape, lambda i: (0,) * a.ndim)
    perb = pl.BlockSpec((None, 1, d), lambda i: ((i * tm) // seq, 0, 0))
    return pl.pallas_call(
        _merge_kernel,
        grid=(t // tm,),
        in_specs=[rows(d), rows(CONV_CH), rows(GROUP_W), rows(GROUP_W), rows(GROUP_W),
                  rows(LANES), rows(LANES), rows(LANES),
                  pl.BlockSpec((tm, d), lambda i: (i, 0)), pl.BlockSpec((tm, d), lambda i: (i, 1)),
                  full(w_co), full(b_co), full(w_ao), full(w_o), full(g_post), perb, full(g_pre),
                  perb, perb, full(w_r), full(b_r)],
        out_specs=[rows(d), rows(d // 2), rows(ROUTE_LANES)],
        out_shape=[jax.ShapeDtypeStruct((t, d), f32),
                   jax.ShapeDtypeStruct((t, d // 2), jnp.uint32),
                   jax.ShapeDtypeStruct((t, ROUTE_LANES), f32)],
        compiler_params=_cparams(("arbitrary",)),
        name="merge",
    )(x2, conv_act, *os_, *lses, z, z, w_co, b_co, w_ao, w_o, g_post, gate1, g_pre, scale2, shift2,
      w_r, b_r)


def _route_kernel(logit_ref, rec_ref, cnt_ref, carry):
    i = pl.program_id(0)

    @pl.when(i == 0)
    def _():
        carry[...] = jnp.zeros_like(carry)

    lg = logit_ref[...]
    tm = lg.shape[0]
    lane = lax.broadcasted_iota(jnp.int32, lg.shape, 1).astype(f32)
    big = float(2 * ROUTE_LANES)

    def first_max(mask):
        v = jnp.max(jnp.where(mask, lg, -jnp.inf), axis=-1, keepdims=True)
        idx = jnp.min(jnp.where(mask & (lg == v), lane, big), axis=-1, keepdims=True)
        return v, idx

    gmask = lane < N_GROUPS
    gmax, gsel = first_max(gmask)
    p_g = 1.0 / jnp.sum(jnp.where(gmask, jnp.exp(lg - gmax), 0.0), axis=-1, keepdims=True)
    e_lo = N_GROUPS + EXPERTS_PER_GROUP * gsel
    emask = (lane >= e_lo) & (lane < e_lo + EXPERTS_PER_GROUP)
    v0, i0 = first_max(emask)
    v1, i1 = first_max(emask & (lane != i0))
    t1 = jnp.exp(v1 - v0)
    w0 = p_g / (1.0 + t1)
    w1 = p_g * t1 / (1.0 + t1)
    e0 = i0 - N_GROUPS
    e1 = i1 - N_GROUPS

    hit0 = lane == e0
    hit1 = lane == e1
    onehot = jnp.where(hit0 | hit1, 1.0, 0.0)
    r_i = lax.broadcasted_iota(jnp.int32, (tm, tm), 0)
    c_i = lax.broadcasted_iota(jnp.int32, (tm, tm), 1)
    tril = jnp.where(c_i < r_i, 1.0, 0.0).astype(bf16)
    before = jnp.dot(tril, onehot.astype(bf16), preferred_element_type=f32) + carry[...]
    rank0 = jnp.sum(jnp.where(hit0, before, 0.0), axis=-1, keepdims=True)
    rank1 = jnp.sum(jnp.where(hit1, before, 0.0), axis=-1, keepdims=True)
    carry[...] = carry[...] + jnp.sum(onehot, axis=0, keepdims=True)
    cnt_ref[...] = carry[...]

    rec = jnp.zeros_like(lg)
    for slot, val in enumerate((e0, e1, rank0, rank1, w0, w1)):
        rec = jnp.where(lane == slot, val, rec)
    rec_ref[...] = rec


def _route(logits, *, tm):
    t = logits.shape[0]
    return pl.pallas_call(
        _route_kernel,
        grid=(t // tm,),
        in_specs=[pl.BlockSpec((tm, ROUTE_LANES), lambda i: (i, 0))],
        out_specs=[pl.BlockSpec((tm, ROUTE_LANES), lambda i: (i, 0)),
                   pl.BlockSpec((1, ROUTE_LANES), lambda i: (0, 0))],
        out_shape=[jax.ShapeDtypeStruct((t, ROUTE_LANES), f32),
                   jax.ShapeDtypeStruct((1, ROUTE_LANES), f32)],
        scratch_shapes=[pltpu.VMEM((1, ROUTE_LANES), f32)],
        compiler_params=_cparams(("arbitrary",)),
        name="route",
    )(logits)


def _dispatch_kernel(zero_blk_ref, dest_ref, h_ref, xs_ref, zero_scr, sem, zsem, *, tm):
    def row_copy(r, d):
        return pltpu.make_async_copy(h_ref.at[pl.ds(r, 1), :], xs_ref.at[pl.ds(d, 1), :], sem)

    def zero_copy(blk):
        return pltpu.make_async_copy(zero_scr, xs_ref.at[pl.ds(blk * MOE_ROWS, MOE_ROWS), :], zsem)

    @pl.when(pl.program_id(0) == 0)
    def _():
        zero_scr[...] = jnp.zeros_like(zero_scr)
        n_cand = zero_blk_ref.shape[0]

        def start(j, c):
            @pl.when(zero_blk_ref[j] >= 0)
            def _():
                zero_copy(zero_blk_ref[j]).start()
            return c

        def wait(j, c):
            @pl.when(zero_blk_ref[j] >= 0)
            def _():
                zero_copy(0).wait()
            return c

        lax.fori_loop(0, n_cand, start, 0)
        lax.fori_loop(0, n_cand, wait, 0)

    def issue(r, c):
        row_copy(r, dest_ref[TOP_K * r]).start()
        row_copy(r, dest_ref[TOP_K * r + 1]).start()
        return c

    lax.fori_loop(0, tm, issue, 0)
    lax.fori_loop(0, TOP_K * tm, lambda j, c: (row_copy(0, 0).wait(), c)[1], 0)


def _dispatch(zero_blk, dest, h2p, *, p_rows, tm):
    t, w = h2p.shape
    grid_spec = pltpu.PrefetchScalarGridSpec(
        num_scalar_prefetch=1,
        grid=(t // tm,),
        in_specs=[pl.BlockSpec((TOP_K * tm,), lambda i, *_: (i,), memory_space=pltpu.SMEM),
                  pl.BlockSpec((tm, w), lambda i, *_: (i, 0))],
        out_specs=pl.BlockSpec(memory_space=pl.ANY),
        scratch_shapes=[pltpu.VMEM((MOE_ROWS, w), jnp.uint32), pltpu.SemaphoreType.DMA(()),
                        pltpu.SemaphoreType.DMA(())],
    )
    return pl.pallas_call(
        functools.partial(_dispatch_kernel, tm=tm),
        grid_spec=grid_spec,
        out_shape=jax.ShapeDtypeStruct((p_rows, w), jnp.uint32),
        compiler_params=_cparams(("arbitrary",)),
        name="dispatch",
    )(zero_blk, dest, h2p)


def _expert_kernel(be_ref, nused_ref, xs_ref, w1_ref, w3_ref, w2_ref, ys_ref, w1b, w3b, w2b):
    i = pl.program_id(0)
    changed = (i == 0) | (be_ref[i] != be_ref[jnp.maximum(i - 1, 0)])

    @pl.when(changed)
    def _():
        w1b[...] = w1_ref[...].astype(bf16)
        w3b[...] = w3_ref[...].astype(bf16)
        w2b[...] = w2_ref[...].astype(bf16)

    @pl.when(i < nused_ref[0])
    def _():
        x = _unpack_pairs(xs_ref[...]).astype(bf16)
        h1 = jnp.dot(x, w1b[...], preferred_element_type=f32)
        h3 = jnp.dot(x, w3b[...], preferred_element_type=f32)
        a = (h1 * _sigmoid(h1) * h3).astype(bf16)
        ys_ref[...] = _pack_pairs(jnp.dot(a, w2b[...], preferred_element_type=f32))

    @pl.when(i >= nused_ref[0])
    def _():
        ys_ref[...] = jnp.zeros_like(ys_ref)


def _experts(block_e, n_used, xs, w1, w3, w2):
    p_rows, w = xs.shape
    d, ff = w1.shape[1], w1.shape[2]
    grid_spec = pltpu.PrefetchScalarGridSpec(
        num_scalar_prefetch=2,
        grid=(p_rows // MOE_ROWS,),
        in_specs=[pl.BlockSpec((MOE_ROWS, w), lambda i, be, nu: (jnp.minimum(i, nu[0] - 1), 0)),
                  pl.BlockSpec((None, d, ff), lambda i, be, nu: (be[i], 0, 0)),
                  pl.BlockSpec((None, d, ff), lambda i, be, nu: (be[i], 0, 0)),
                  pl.BlockSpec((None, ff, d), lambda i, be, nu: (be[i], 0, 0))],
        out_specs=pl.BlockSpec((MOE_ROWS, w), lambda i, be, nu: (i, 0)),
        scratch_shapes=[pltpu.VMEM((d, ff), bf16), pltpu.VMEM((d, ff), bf16), pltpu.VMEM((ff, d), bf16)],
    )
    return pl.pallas_call(
        _expert_kernel,
        grid_spec=grid_spec,
        out_shape=jax.ShapeDtypeStruct((p_rows, w), jnp.uint32),
        compiler_params=_cparams(("arbitrary",)),
        name="experts",
    )(block_e, n_used, xs, w1, w3, w2)


def _combine_kernel(dest_ref, ys_ref, rec_ref, x1_ref, gpost_ref, gate2_ref, o_ref, buf, sem, *, tm):
    def row_copy(d, k, r):
        return pltpu.make_async_copy(ys_ref.at[pl.ds(d, 1), :], buf.at[k, pl.ds(r, 1), :], sem)

    def issue(r, c):
        row_copy(dest_ref[TOP_K * r], 0, r).start()
        row_copy(dest_ref[TOP_K * r + 1], 1, r).start()
        return c

    lax.fori_loop(0, tm, issue, 0)
    lax.fori_loop(0, TOP_K * tm, lambda j, c: (row_copy(0, 0, 0).wait(), c)[1], 0)

    rec = rec_ref[...]
    y = rec[:, 4:5] * _unpack_pairs(buf[0]) + rec[:, 5:6] * _unpack_pairs(buf[1])
    yn = y * lax.rsqrt(jnp.mean(y * y, axis=-1, keepdims=True) + EPS) * gpost_ref[...]
    o_ref[...] = x1_ref[...] + gate2_ref[...] * yn


def _combine(dest, ys, rec, x1, g_post, gate2, *, seq, tm):
    t, d = x1.shape
    w = ys.shape[1]
    return pl.pallas_call(
        functools.partial(_combine_kernel, tm=tm),
        grid=(t // tm,),
        in_specs=[pl.BlockSpec((TOP_K * tm,), lambda i: (i,), memory_space=pltpu.SMEM),
                  pl.BlockSpec(memory_space=pl.ANY),
                  pl.BlockSpec((tm, ROUTE_LANES), lambda i: (i, 0)),
                  pl.BlockSpec((tm, d), lambda i: (i, 0)),
                  pl.BlockSpec((1, d), lambda i: (0, 0)),
                  pl.BlockSpec((None, 1, d), lambda i: ((i * tm) // seq, 0, 0))],
        out_specs=pl.BlockSpec((tm, d), lambda i: (i, 0)),
        out_shape=jax.ShapeDtypeStruct((t, d), f32),
        scratch_shapes=[pltpu.VMEM((TOP_K, tm, w), jnp.uint32), pltpu.SemaphoreType.DMA(())],
        compiler_params=_cparams(("arbitrary",)),
        name="combine",
    )(dest, ys, rec, x1, g_post, gate2)


def _layer(x, c, w_ada, b_ada, g_pre_mix, g_post_mix, w_in, b_in, w_dw, b_dw, ln_conv_g, ln_conv_b,
           w_conv_out, b_conv_out, rel_bias, w_attn_out, w_out, g_pre_ffn, g_post_ffn,
           w_router_group, b_router_group, w_router_expert, b_router_expert, w1, w3, w2):
    batch, seq, d = x.shape
    t = batch * seq
    row = lambda v: v.reshape(1, -1)

    c_pad = jnp.zeros((8, d), f32).at[:batch].set(c)
    mod = _ada(c_pad, w_ada, row(b_ada))[:batch]
    shift1, scale1, gate1, shift2, scale2, gate2 = [m.reshape(batch, 1, d) for m in jnp.split(mod, 6, axis=-1)]

    split = 2 * CONV_CH + 3 * ATTN_WIDTH
    w_in_p = jnp.concatenate([w_in[:, split:], w_in[:, :split]], axis=1).astype(bf16)
    b_in_p = row(jnp.concatenate([b_in[split:], b_in[:split]]))

    x2 = x.reshape(t, d)
    z = _inproj(x2, row(g_pre_mix), scale1, shift1, w_in_p, b_in_p, seq=seq, tm=1024, tn=1536)

    conv_act = _conv(z.reshape(batch, seq, IN_COLS), w_dw, row(b_dw), row(ln_conv_g), row(ln_conv_b),
                     tm=512).reshape(t, CONV_CH)

    os_, lses = [], []
    for gi, (_, dil) in enumerate(DILATED_GROUPS):
        o, lse = _attn_group(z, _band_bias(rel_bias, gi, dil), gi, dil, batch=batch, seq=seq)
        os_.append(o)
        lses.append(lse)

    pad = ROUTE_LANES - N_GROUPS - N_EXPERTS
    w_r = jnp.concatenate([w_router_group, w_router_expert, jnp.zeros((d, pad), f32)], axis=1).astype(bf16)
    b_r = row(jnp.concatenate([b_router_group, b_router_expert.reshape(-1), jnp.zeros((pad,), f32)]))
    x1, h2p, logits = _merge(x2, conv_act, os_, lses, z, w_conv_out.astype(bf16), row(b_conv_out),
                             w_attn_out.astype(bf16), w_out.astype(bf16), row(g_post_mix), gate1,
                             row(g_pre_ffn), scale2, shift2, w_r, b_r, seq=seq, tm=256)

    rec, cnt = _route(logits, tm=512)

    counts = cnt[0, :N_EXPERTS].astype(jnp.int32)
    pcounts = (counts + MOE_ROWS - 1) // MOE_ROWS * MOE_ROWS
    pends = jnp.cumsum(pcounts)
    pstarts = pends - pcounts
    n_blocks = (t * TOP_K + N_EXPERTS * (MOE_ROWS - 1) + MOE_ROWS - 1) // MOE_ROWS
    block_e = jnp.minimum(jnp.searchsorted(pends, jnp.arange(n_blocks) * MOE_ROWS, side='right'),
                          N_EXPERTS - 1).astype(jnp.int32)
    n_used = (pends[-1:] // MOE_ROWS).astype(jnp.int32)
    eid = rec[:, 0:TOP_K].astype(jnp.int32)
    dest = (pstarts[eid] + rec[:, TOP_K:2 * TOP_K].astype(jnp.int32)).reshape(t * TOP_K)

    tail = n_blocks - N_EXPERTS + jnp.arange(N_EXPERTS, dtype=jnp.int32)
    zero_blk = jnp.concatenate([jnp.where(pcounts > counts, pends // MOE_ROWS - 1, -1),
                                jnp.where(tail >= n_used[0], tail, -1)]).astype(jnp.int32)

    xs = _dispatch(zero_blk, dest, h2p, p_rows=n_blocks * MOE_ROWS, tm=512)
    ys = _experts(block_e, n_used, xs, w1, w3, w2)
    out = _combine(dest, ys, rec, x1, row(g_post_ffn), gate2, seq=seq, tm=512)
    return out.reshape(batch, seq, d)


def kernel(x, c, w_ada, b_ada, g_pre_mix, g_post_mix, w_in, b_in, w_dw, b_dw, ln_conv_g, ln_conv_b,
           w_conv_out, b_conv_out, rel_bias, w_attn_out, w_out, g_pre_ffn, g_post_ffn,
           w_router_group, b_router_group, w_router_expert, b_router_expert, w1, w3, w2):
    depth = w_ada.shape[0]
    for l in range(depth):
        pick = (lambda a: a.reshape(a.shape[1:])) if depth == 1 else (lambda a, l=l: a[l])
        x = _layer(x, c, pick(w_ada), pick(b_ada), pick(g_pre_mix), pick(g_post_mix), pick(w_in),
                   pick(b_in), pick(w_dw), pick(b_dw), pick(ln_conv_g), pick(ln_conv_b),
                   pick(w_conv_out), pick(b_conv_out), rel_bias, pick(w_attn_out), pick(w_out),
                   pick(g_pre_ffn), pick(g_post_ffn), pick(w_router_group), pick(b_router_group),
                   pick(w_router_expert), pick(b_router_expert), pick(w1), pick(w3), pick(w2))
    return x
```

```python
import functools
import math

import numpy as np
import jax
import jax.numpy as jnp
from jax import lax
from jax.experimental import pallas as pl
from jax.experimental.pallas import tpu as pltpu

D_MODEL = 2048
CONV_CH = 1024
CONV_WIDTH = 31
N_ATTN_HEADS = 12
HEADS_PER_GROUP = 4
HEAD_DIM = 128
ATTN_WIDTH = N_ATTN_HEADS * HEAD_DIM
DILATED_GROUPS = ((128, 1), (512, 4), (2048, 16))
NUM_BUCKETS = 32
REL_MAX_DISTANCE = 1024
N_GROUPS = 8
EXPERTS_PER_GROUP = 8
N_EXPERTS = N_GROUPS * EXPERTS_PER_GROUP
TOP_K = 2
EXPERT_FF = 512
EPS = 1e-6
NEG_INF = -1e30

LANES = 128
SUBLANES = 8
V7X_VMEM_LIMIT_BYTES = 56 * 1024 * 1024

GROUP_W = HEADS_PER_GROUP * HEAD_DIM
ZT_COLS = 2 * D_MODEL + 2 * CONV_CH
COL_GLU = 2 * D_MODEL
INPROJ_TN = 3 * GROUP_W
ZT_TILES = ZT_COLS // INPROJ_TN
SLABS_PER_TILE = INPROJ_TN // (2 * LANES)
ATTN_HALF = 64
ATTN_QB = 128
ATTN_WIN = ATTN_QB + 2 * ATTN_HALF
MOE_ROWS = 128
ROUTE_LANES = 128
ROUTE_FIELDS = 8
PACKED_W = D_MODEL // 2
ROW_TILE = PACKED_W // LANES

bf16 = jnp.bfloat16
f32 = jnp.float32
u32 = jnp.uint32
HI_MASK = 0xFFFF0000


def _cparams(sem):
    return pltpu.CompilerParams(dimension_semantics=sem, vmem_limit_bytes=V7X_VMEM_LIMIT_BYTES)


def _sigmoid(x):
    return 1.0 / (1.0 + jnp.exp(-x))


def _pack_pairs(v):
    n = v.shape[1] // 2
    lo = lax.bitcast_convert_type(v[:, :n].astype(bf16).astype(f32), u32)
    hi = lax.bitcast_convert_type(v[:, n:].astype(bf16).astype(f32), u32)
    return (lo >> 16) | (hi & u32(HI_MASK))


def _unpack_lo(u):
    return lax.bitcast_convert_type(u << 16, f32)


def _unpack_hi(u):
    return lax.bitcast_convert_type(u & u32(HI_MASK), f32)


def _store_position_major(ref, v):
    packed = _pack_pairs(v)
    for j in range(ROW_TILE):
        ref[pl.ds(j, v.shape[0], stride=ROW_TILE), :] = packed[:, j * LANES:(j + 1) * LANES]


def _load_position_major(ref, n):
    chunks = [ref[pl.ds(j, n, stride=ROW_TILE), :] for j in range(ROW_TILE)]
    return jnp.concatenate([_unpack_lo(c) for c in chunks] + [_unpack_hi(c) for c in chunks], axis=1)


def _ada_kernel(c_ref, w_ref, b_ref, o_ref):
    c = c_ref[...]
    a = (c * _sigmoid(c)).astype(bf16)
    o_ref[...] = jnp.dot(a, w_ref[...].astype(bf16), preferred_element_type=f32) + b_ref[...]


def _ada(c_pad, w_ada, b_ada):
    rows, d = c_pad.shape
    n = w_ada.shape[1]
    tn = 1024
    return pl.pallas_call(
        _ada_kernel,
        grid=(n // tn,),
        in_specs=[pl.BlockSpec((rows, d), lambda j: (0, 0)),
                  pl.BlockSpec((d, tn), lambda j: (0, j)),
                  pl.BlockSpec((1, tn), lambda j: (0, j))],
        out_specs=pl.BlockSpec((rows, tn), lambda j: (0, j)),
        out_shape=jax.ShapeDtypeStruct((rows, n), f32),
        compiler_params=_cparams(("arbitrary",)),
        name="ada_mod",
    )(c_pad, w_ada, b_ada)


def _inproj_kernel(x_ref, g_ref, scale_ref, shift_ref, w_ref, b_ref, zt_ref, qkv_ref, h_scr):
    j = pl.program_id(1)

    @pl.when(j == 0)
    def _():
        x = x_ref[...]
        ms = jnp.mean(x * x, axis=-1, keepdims=True)
        h = x * lax.rsqrt(ms + EPS) * g_ref[...]
        h_scr[...] = (h * (1.0 + scale_ref[...]) + shift_ref[...]).astype(bf16)

    z = jnp.dot(h_scr[...], w_ref[...], preferred_element_type=f32) + b_ref[...]

    @pl.when(j < ZT_TILES)
    def _():
        zt_ref[...] = z.astype(bf16)

    @pl.when(j >= ZT_TILES)
    def _():
        for c in range(SLABS_PER_TILE):
            qkv_ref[c] = _pack_pairs(z[:, 2 * c * LANES:2 * (c + 1) * LANES])


def _inproj(x2, g, scale, shift, w, b, *, seq, tm):
    t, d = x2.shape
    n = w.shape[1]
    tn = INPROJ_TN
    n_groups = n // tn - ZT_TILES
    bmap = lambda i, j: ((i * tm) // seq, 0, 0)
    return pl.pallas_call(
        _inproj_kernel,
        grid=(t // tm, n // tn),
        in_specs=[pl.BlockSpec((tm, d), lambda i, j: (i, 0)),
                  pl.BlockSpec((1, d), lambda i, j: (0, 0)),
                  pl.BlockSpec((None, 1, d), bmap),
                  pl.BlockSpec((None, 1, d), bmap),
                  pl.BlockSpec((d, tn), lambda i, j: (0, j)),
                  pl.BlockSpec((1, tn), lambda i, j: (0, j))],
        out_specs=[pl.BlockSpec((tm, tn), lambda i, j: (i, jnp.minimum(j, ZT_TILES - 1))),
                   pl.BlockSpec((SLABS_PER_TILE, tm, LANES),
                                lambda i, j: (jnp.maximum(j - ZT_TILES, 0), i, 0))],
        out_shape=[jax.ShapeDtypeStruct((t, ZT_COLS), bf16),
                   jax.ShapeDtypeStruct((n_groups * SLABS_PER_TILE, t, LANES), u32)],
        scratch_shapes=[pltpu.VMEM((tm, d), bf16)],
        compiler_params=_cparams(("arbitrary", "arbitrary")),
        name="inproj",
    )(x2, g, scale, shift, w, b)


CONV_HALO = 16
CONV_CHUNK = 16
CONV_SUB = CONV_CH // LANES
NORM_CHUNK = 64


def _conv_kernel(ap_ref, ac_ref, an_ref, gp_ref, gc_ref, gn_ref, w_ref, bdw_ref, lng_ref, lnb_ref,
                 o_ref, u_scr, y_scr, *, tm):
    i = pl.program_id(1)
    last = pl.num_programs(1) - 1

    def glu(a_ref, g_ref):
        return a_ref[...].astype(f32) * _sigmoid(g_ref[...].astype(f32))

    def put(row0, val):
        for j in range(CONV_SUB):
            u_scr[pl.ds(row0 * CONV_SUB + j, val.shape[0], stride=CONV_SUB), :] = (
                val[:, j * LANES:(j + 1) * LANES])

    put(0, jnp.where(i > 0, glu(ap_ref, gp_ref), 0.0))
    put(CONV_HALO, glu(ac_ref, gc_ref))
    put(CONV_HALO + tm, jnp.where(i < last, glu(an_ref, gn_ref), 0.0))

    def chunk(c, carry):
        r0 = c * CONV_CHUNK
        acc = jnp.zeros((CONV_CHUNK, CONV_SUB, LANES), f32) + bdw_ref[...][None]
        for k in range(CONV_WIDTH):
            start = pl.multiple_of((r0 + k + 1) * CONV_SUB, CONV_SUB)
            xk = u_scr[pl.ds(start, CONV_CHUNK * CONV_SUB), :].reshape(CONV_CHUNK, CONV_SUB, LANES)
            acc = acc + w_ref[k][None] * xk
        out0 = pl.multiple_of(r0 * CONV_SUB, CONV_CHUNK * CONV_SUB)
        y_scr[pl.ds(out0, CONV_CHUNK * CONV_SUB), :] = acc.reshape(CONV_CHUNK * CONV_SUB, LANES)
        return carry

    lax.fori_loop(0, tm // CONV_CHUNK, chunk, 0)

    def norm(c, carry):
        r0 = pl.multiple_of(c * NORM_CHUNK, NORM_CHUNK)
        acc = jnp.concatenate(
            [y_scr[pl.ds(r0 * CONV_SUB + j, NORM_CHUNK, stride=CONV_SUB), :] for j in range(CONV_SUB)],
            axis=1)
        mu = jnp.mean(acc, axis=-1, keepdims=True)
        cen = acc - mu
        var = jnp.mean(cen * cen, axis=-1, keepdims=True)
        y = cen * lax.rsqrt(var + EPS) * lng_ref[...] + lnb_ref[...]
        o_ref[pl.ds(r0, NORM_CHUNK), :] = (y * _sigmoid(y)).astype(bf16)
        return carry

    lax.fori_loop(0, tm // NORM_CHUNK, norm, 0)


def _conv(z3, w_dw, b_dw, ln_g, ln_b, *, tm):
    b, s, _ = z3.shape
    hb = tm // CONV_HALO
    nhb = s // CONV_HALO
    ca, cg = COL_GLU // CONV_CH, COL_GLU // CONV_CH + 1

    def halo(col, which):
        if which < 0:
            return pl.BlockSpec((None, CONV_HALO, CONV_CH),
                                lambda bb, i: (bb, jnp.maximum(i * hb - 1, 0), col))
        return pl.BlockSpec((None, CONV_HALO, CONV_CH),
                            lambda bb, i: (bb, jnp.minimum((i + 1) * hb, nhb - 1), col))

    cur = lambda col: pl.BlockSpec((None, tm, CONV_CH), lambda bb, i: (bb, i, col))
    vec = pl.BlockSpec((1, CONV_CH), lambda bb, i: (0, 0))
    return pl.pallas_call(
        functools.partial(_conv_kernel, tm=tm),
        grid=(b, s // tm),
        in_specs=[halo(ca, -1), cur(ca), halo(ca, 1), halo(cg, -1), cur(cg), halo(cg, 1),
                  pl.BlockSpec((CONV_WIDTH, CONV_SUB, LANES), lambda bb, i: (0, 0, 0)),
                  pl.BlockSpec((CONV_SUB, LANES), lambda bb, i: (0, 0)), vec, vec],
        out_specs=pl.BlockSpec((None, tm, CONV_CH), lambda bb, i: (bb, i, 0)),
        out_shape=jax.ShapeDtypeStruct((b, s, CONV_CH), bf16),
        scratch_shapes=[pltpu.VMEM(((tm + 2 * CONV_HALO) * CONV_SUB, LANES), f32),
                        pltpu.VMEM((tm * CONV_SUB, LANES), f32)],
        compiler_params=_cparams(("arbitrary", "arbitrary")),
        name="conv_branch",
    )(z3, z3, z3, z3, z3, z3, w_dw.reshape(CONV_WIDTH, CONV_SUB, LANES), b_dw.reshape(CONV_SUB, LANES),
      ln_g, ln_b)


def _t5_bucket_table(dil):
    delta = (np.arange(ATTN_WIN)[None, :] - ATTN_HALF) - np.arange(ATTN_QB)[:, None]
    rel = delta * dil
    nb = NUM_BUCKETS // 2
    max_exact = nb // 2
    n = np.abs(rel)
    nf = np.maximum(n, 1).astype(np.float32)
    large = max_exact + (np.log(nf / np.float32(max_exact)) / np.float32(math.log(REL_MAX_DISTANCE / max_exact))
                         * np.float32(nb - max_exact)).astype(np.int32)
    large = np.minimum(large, nb - 1)
    bucket = np.where(rel > 0, nb, 0) + np.where(n < max_exact, n, large)
    return np.where(np.abs(delta) <= ATTN_HALF, bucket, -1).astype(np.int32)


def _attn_kernel(rb_ref, bkt_ref, q_ref, kp_ref, kc_ref, kn_ref, vp_ref, vc_ref, vn_ref, o_ref, lse_ref,
                 k_scr, v_scr, bias_scr, *, gi, dil, ts, sub_len):
    i = pl.program_id(1)
    hb = ATTN_HALF * dil
    tq = ts // dil

    @pl.when((pl.program_id(0) == 0) & (i == 0))
    def _():
        bk = bkt_ref[...]
        for hh in range(HEADS_PER_GROUP):
            acc = jnp.full((ATTN_QB, ATTN_WIN), NEG_INF, f32)
            for b_id in range(NUM_BUCKETS):
                acc = jnp.where(bk == b_id, rb_ref[b_id * N_ATTN_HEADS + gi * HEADS_PER_GROUP + hh], acc)
            bias_scr[hh] = acc

    k_scr[:, 0:hb, :] = kp_ref[...]
    k_scr[:, hb:hb + ts, :] = kc_ref[...]
    k_scr[:, hb + ts:, :] = kn_ref[...]
    v_scr[:, 0:hb, :] = vp_ref[...]
    v_scr[:, hb:hb + ts, :] = vc_ref[...]
    v_scr[:, hb + ts:, :] = vn_ref[...]

    scale = HEAD_DIM ** -0.5
    col = lax.broadcasted_iota(jnp.int32, (ATTN_QB, ATTN_WIN), 1)
    lane = lax.broadcasted_iota(jnp.int32, (ATTN_QB, LANES), 1)

    def rows(start, n):
        if dil == 1:
            return pl.ds(pl.multiple_of(start, ATTN_QB), n)
        return pl.ds(start, n, stride=dil)

    def heads(words):
        out = []
        for w in words:
            out += [_unpack_lo(w).astype(bf16), _unpack_hi(w).astype(bf16)]
        return out

    def unit(u, carry):
        r = jnp.bitwise_and(u, dil - 1)
        jb = lax.shift_right_logical(u, jnp.int32(dil.bit_length() - 1))
        start = jb * (ATTN_QB * dil) + r
        q = heads([q_ref[sl, rows(start, ATTN_QB), :] for sl in range(2)])
        kw = heads([k_scr[sl, rows(start, ATTN_WIN), :] for sl in range(2)])
        vw = heads([v_scr[sl, rows(start, ATTN_WIN), :] for sl in range(2)])
        key0 = i * tq + jb * ATTN_QB - ATTN_HALF
        valid = (col + key0 >= 0) & (col + key0 < sub_len)
        s = [lax.dot_general(q[hh], kw[hh], (((1,), (1,)), ((), ())), preferred_element_type=f32)
             for hh in range(HEADS_PER_GROUP)]
        s = [jnp.where(valid, s[hh] * scale + bias_scr[hh], NEG_INF) for hh in range(HEADS_PER_GROUP)]
        m = [jnp.max(x, axis=-1, keepdims=True) for x in s]
        p = [jnp.exp(s[hh] - m[hh]) for hh in range(HEADS_PER_GROUP)]
        den = [jnp.sum(x, axis=-1, keepdims=True) for x in p]
        pv = [jnp.dot(p[hh].astype(bf16), vw[hh], preferred_element_type=f32)
              for hh in range(HEADS_PER_GROUP)]
        lse_tile = jnp.zeros((ATTN_QB, LANES), f32)
        for hh in range(HEADS_PER_GROUP):
            o_ref[hh, rows(start, ATTN_QB), :] = pv[hh] / den[hh]
            lse_tile = jnp.where(lane == hh, m[hh] + jnp.log(den[hh]), lse_tile)
        lse_ref[rows(start, ATTN_QB), :] = lse_tile
        return carry

    lax.fori_loop(0, ts // ATTN_QB, unit, 0)


def _attn_group(qkv, rel_bias_flat, gi, dil, *, batch, seq):
    t = batch * seq
    ts = max(1024, ATTN_QB * dil)
    hb = ATTN_HALF * dil
    steps = seq // ts
    hpt = ts // hb
    hps = seq // hb
    slab = lambda which: gi * 3 + which

    def cur(which):
        return pl.BlockSpec((2, ts, LANES), lambda b, i: (slab(which), b * steps + i, 0))

    def prev(which):
        return pl.BlockSpec((2, hb, LANES),
                            lambda b, i: (slab(which), b * hps + jnp.maximum(i * hpt - 1, 0), 0))

    def nxt(which):
        return pl.BlockSpec((2, hb, LANES),
                            lambda b, i: (slab(which), b * hps + jnp.minimum((i + 1) * hpt, hps - 1), 0))

    bkt = jnp.asarray(_t5_bucket_table(dil))
    return pl.pallas_call(
        functools.partial(_attn_kernel, gi=gi, dil=dil, ts=ts, sub_len=seq // dil),
        grid=(batch, steps),
        in_specs=[pl.BlockSpec(memory_space=pltpu.SMEM),
                  pl.BlockSpec((ATTN_QB, ATTN_WIN), lambda b, i: (0, 0)),
                  cur(0), prev(1), cur(1), nxt(1), prev(2), cur(2), nxt(2)],
        out_specs=[pl.BlockSpec((HEADS_PER_GROUP, ts, LANES), lambda b, i: (0, b * steps + i, 0)),
                   pl.BlockSpec((ts, LANES), lambda b, i: (b * steps + i, 0))],
        out_shape=[jax.ShapeDtypeStruct((HEADS_PER_GROUP, t, LANES), f32),
                   jax.ShapeDtypeStruct((t, LANES), f32)],
        scratch_shapes=[pltpu.VMEM((2, ts + 2 * hb, LANES), u32),
                        pltpu.VMEM((2, ts + 2 * hb, LANES), u32),
                        pltpu.VMEM((HEADS_PER_GROUP, ATTN_QB, ATTN_WIN), f32)],
        compiler_params=_cparams(("arbitrary", "arbitrary")),
        name=f"attn_g{gi}",
    )(rel_bias_flat, bkt, qkv, qkv, qkv, qkv, qkv, qkv, qkv)


def _merge_kernel(x_ref, ca_ref, o0_ref, o1_ref, o2_ref, l0_ref, l1_ref, l2_ref, ga_ref, gb_ref,
                  wco_ref, bco_ref, wao_ref, wo_ref, gpost_ref, gate1_ref, gpre_ref, scale2_ref,
                  shift2_ref, wr_ref, br_ref, x1_ref, h2p_ref, logit_ref):
    branch_a = jnp.dot(ca_ref[...], wco_ref[...], preferred_element_type=f32) + bco_ref[...]

    l0, l1, l2 = l0_ref[...], l1_ref[...], l2_ref[...]
    lmax = jnp.maximum(jnp.maximum(l0, l1), l2)
    e0, e1, e2 = jnp.exp(l0 - lmax), jnp.exp(l1 - lmax), jnp.exp(l2 - lmax)
    inv = 1.0 / (e0 + e1 + e2)
    parts = []
    for hh in range(HEADS_PER_GROUP):
        ls = slice(hh, hh + 1)
        parts.append((e0[:, ls] * inv[:, ls]) * o0_ref[hh] + (e1[:, ls] * inv[:, ls]) * o1_ref[hh]
                     + (e2[:, ls] * inv[:, ls]) * o2_ref[hh])
    att = jnp.concatenate(parts, axis=1).astype(bf16)
    branch_b = jnp.dot(att, wao_ref[...], preferred_element_type=f32)

    mix = _sigmoid(ga_ref[...].astype(f32)) * branch_a + _sigmoid(gb_ref[...].astype(f32)) * branch_b
    y = jnp.dot(mix.astype(bf16), wo_ref[...], preferred_element_type=f32)

    yn = y * lax.rsqrt(jnp.mean(y * y, axis=-1, keepdims=True) + EPS) * gpost_ref[...]
    x1 = x_ref[...] + gate1_ref[...] * yn
    x1_ref[...] = x1
    hn = x1 * lax.rsqrt(jnp.mean(x1 * x1, axis=-1, keepdims=True) + EPS) * gpre_ref[...]
    h2 = hn * (1.0 + scale2_ref[...]) + shift2_ref[...]
    _store_position_major(h2p_ref, h2)
    logit_ref[...] = jnp.dot(h2.astype(bf16), wr_ref[...], preferred_element_type=f32) + br_ref[...]


def _merge(x2, conv_act, os_, lses, zt, w_co, b_co, w_ao, w_o, g_post, gate1, g_pre, scale2, shift2,
           w_r, b_r, *, seq, tm):
    t, d = x2.shape
    rows = lambda w: pl.BlockSpec((tm, w), lambda i: (i, 0))
    heads = pl.BlockSpec((HEADS_PER_GROUP, tm, LANES), lambda i: (0, i, 0))
    full = lambda a: pl.BlockSpec(a.shape, lambda i: (0,) * a.ndim)
    perb = pl.BlockSpec((None, 1, d), lambda i: ((i * tm) // seq, 0, 0))
    return pl.pallas_call(
        _merge_kernel,
        grid=(t // tm,),
        in_specs=[rows(d), rows(CONV_CH), heads, heads, heads, rows(LANES), rows(LANES), rows(LANES),
                  pl.BlockSpec((tm, d), lambda i: (i, 0)), pl.BlockSpec((tm, d), lambda i: (i, 1)),
                  full(w_co), full(b_co), full(w_ao), full(w_o), full(g_post), perb, full(g_pre),
                  perb, perb, full(w_r), full(b_r)],
        out_specs=[rows(d), pl.BlockSpec((tm * ROW_TILE, LANES), lambda i: (i, 0)), rows(ROUTE_LANES)],
        out_shape=[jax.ShapeDtypeStruct((t, d), f32),
                   jax.ShapeDtypeStruct((t * ROW_TILE, LANES), u32),
                   jax.ShapeDtypeStruct((t, ROUTE_LANES), f32)],
        compiler_params=_cparams(("arbitrary",)),
        name="merge",
    )(x2, conv_act, *os_, *lses, zt, zt, w_co, b_co, w_ao, w_o, g_post, gate1, g_pre, scale2, shift2,
      w_r, b_r)


def _route_kernel(logit_ref, rec_ref, rect_ref, cnt_ref, carry):
    i = pl.program_id(0)

    @pl.when(i == 0)
    def _():
        carry[...] = jnp.zeros_like(carry)

    lg = logit_ref[...]
    tm = lg.shape[0]
    lane = lax.broadcasted_iota(jnp.int32, lg.shape, 1).astype(f32)
    big = float(2 * ROUTE_LANES)

    def first_max(mask):
        v = jnp.max(jnp.where(mask, lg, -jnp.inf), axis=-1, keepdims=True)
        idx = jnp.min(jnp.where(mask & (lg == v), lane, big), axis=-1, keepdims=True)
        return v, idx

    gmask = lane < N_GROUPS
    gmax, gsel = first_max(gmask)
    p_g = 1.0 / jnp.sum(jnp.where(gmask, jnp.exp(lg - gmax), 0.0), axis=-1, keepdims=True)
    e_lo = N_GROUPS + EXPERTS_PER_GROUP * gsel
    emask = (lane >= e_lo) & (lane < e_lo + EXPERTS_PER_GROUP)
    v0, i0 = first_max(emask)
    v1, i1 = first_max(emask & (lane != i0))
    t1 = jnp.exp(v1 - v0)
    w0 = p_g / (1.0 + t1)
    w1 = p_g * t1 / (1.0 + t1)
    e0 = i0 - N_GROUPS
    e1 = i1 - N_GROUPS

    hit0 = lane == e0
    hit1 = lane == e1
    onehot = jnp.where(hit0 | hit1, 1.0, 0.0)
    r_i = lax.broadcasted_iota(jnp.int32, (tm, tm), 0)
    c_i = lax.broadcasted_iota(jnp.int32, (tm, tm), 1)
    tril = jnp.where(c_i < r_i, 1.0, 0.0).astype(bf16)
    before = jnp.dot(tril, onehot.astype(bf16), preferred_element_type=f32) + carry[...]
    rank0 = jnp.sum(jnp.where(hit0, before, 0.0), axis=-1, keepdims=True)
    rank1 = jnp.sum(jnp.where(hit1, before, 0.0), axis=-1, keepdims=True)
    carry[...] = carry[...] + jnp.sum(onehot, axis=0, keepdims=True)
    cnt_ref[...] = carry[...]

    rec = jnp.zeros_like(lg)
    for slot, val in enumerate((e0, e1, rank0, rank1, w0, w1)):
        rec = jnp.where(lane == slot, val, rec)
    rec_ref[...] = rec
    rect_ref[...] = jnp.transpose(rec)[0:ROUTE_FIELDS, :]


def _route(logits, *, tm):
    t = logits.shape[0]
    return pl.pallas_call(
        _route_kernel,
        grid=(t // tm,),
        in_specs=[pl.BlockSpec((tm, ROUTE_LANES), lambda i: (i, 0))],
        out_specs=[pl.BlockSpec((tm, ROUTE_LANES), lambda i: (i, 0)),
                   pl.BlockSpec((ROUTE_FIELDS, tm), lambda i: (0, i)),
                   pl.BlockSpec((1, ROUTE_LANES), lambda i: (0, 0))],
        out_shape=[jax.ShapeDtypeStruct((t, ROUTE_LANES), f32),
                   jax.ShapeDtypeStruct((ROUTE_FIELDS, t), f32),
                   jax.ShapeDtypeStruct((1, ROUTE_LANES), f32)],
        scratch_shapes=[pltpu.VMEM((1, ROUTE_LANES), f32)],
        compiler_params=_cparams(("arbitrary",)),
        name="route",
    )(logits)


DMA_UNROLL = 8


def _dispatch_kernel(zero_blk_ref, dest_ref, h_ref, xs_ref, zero_scr, sem, zsem, *, tm):
    def row_copy(r, d):
        return pltpu.make_async_copy(h_ref.at[pl.ds(pl.multiple_of(r * ROW_TILE, ROW_TILE), ROW_TILE), :],
                                     xs_ref.at[pl.ds(pl.multiple_of(d * ROW_TILE, ROW_TILE), ROW_TILE), :],
                                     sem)

    def zero_copy(blk):
        n = MOE_ROWS * ROW_TILE
        return pltpu.make_async_copy(zero_scr, xs_ref.at[pl.ds(pl.multiple_of(blk * n, n), n), :], zsem)

    @pl.when(pl.program_id(0) == 0)
    def _():
        zero_scr[...] = jnp.zeros_like(zero_scr)
        n_cand = zero_blk_ref.shape[0]

        def start(j, c):
            @pl.when(zero_blk_ref[j] >= 0)
            def _():
                zero_copy(zero_blk_ref[j]).start()
            return c

        def wait(j, c):
            @pl.when(zero_blk_ref[j] >= 0)
            def _():
                zero_copy(0).wait()
            return c

        lax.fori_loop(0, n_cand, start, 0)
        lax.fori_loop(0, n_cand, wait, 0)

    def issue(r, c):
        row_copy(r, dest_ref[TOP_K * r]).start()
        row_copy(r, dest_ref[TOP_K * r + 1]).start()
        return c

    lax.fori_loop(0, tm, issue, 0, unroll=DMA_UNROLL)
    for _ in range(TOP_K):
        pltpu.make_async_copy(h_ref, xs_ref.at[pl.ds(0, tm * ROW_TILE), :], sem).wait()


def _dispatch(zero_blk, dest, h2p, *, p_rows, tm):
    grid_spec = pltpu.PrefetchScalarGridSpec(
        num_scalar_prefetch=1,
        grid=(h2p.shape[0] // (tm * ROW_TILE),),
        in_specs=[pl.BlockSpec((TOP_K * tm,), lambda i, *_: (i,), memory_space=pltpu.SMEM),
                  pl.BlockSpec((tm * ROW_TILE, LANES), lambda i, *_: (i, 0))],
        out_specs=pl.BlockSpec(memory_space=pl.ANY),
        scratch_shapes=[pltpu.VMEM((MOE_ROWS * ROW_TILE, LANES), u32), pltpu.SemaphoreType.DMA(()),
                        pltpu.SemaphoreType.DMA(())],
    )
    return pl.pallas_call(
        functools.partial(_dispatch_kernel, tm=tm),
        grid_spec=grid_spec,
        out_shape=jax.ShapeDtypeStruct((p_rows * ROW_TILE, LANES), u32),
        compiler_params=_cparams(("arbitrary",)),
        name="dispatch",
    )(zero_blk, dest, h2p)


def _expert_kernel(be_ref, nused_ref, xs_ref, w1_ref, w3_ref, w2_ref, ys_ref, w1b, w3b, w2b):
    i = pl.program_id(0)
    changed = (i == 0) | (be_ref[i] != be_ref[jnp.maximum(i - 1, 0)])

    @pl.when(changed)
    def _():
        w1b[...] = w1_ref[...].astype(bf16)
        w3b[...] = w3_ref[...].astype(bf16)
        w2b[...] = w2_ref[...].astype(bf16)

    @pl.when(i < nused_ref[0])
    def _():
        x = _load_position_major(xs_ref, MOE_ROWS).astype(bf16)
        h1 = jnp.dot(x, w1b[...], preferred_element_type=f32)
        h3 = jnp.dot(x, w3b[...], preferred_element_type=f32)
        a = (h1 * _sigmoid(h1) * h3).astype(bf16)
        _store_position_major(ys_ref, jnp.dot(a, w2b[...], preferred_element_type=f32))

    @pl.when(i >= nused_ref[0])
    def _():
        ys_ref[...] = jnp.zeros_like(ys_ref)


def _experts(block_e, n_used, xs, w1, w3, w2):
    blk = MOE_ROWS * ROW_TILE
    d, ff = w1.shape[1], w1.shape[2]
    grid_spec = pltpu.PrefetchScalarGridSpec(
        num_scalar_prefetch=2,
        grid=(xs.shape[0] // blk,),
        in_specs=[pl.BlockSpec((blk, LANES), lambda i, be, nu: (jnp.minimum(i, nu[0] - 1), 0)),
                  pl.BlockSpec((None, d, ff), lambda i, be, nu: (be[i], 0, 0)),
                  pl.BlockSpec((None, d, ff), lambda i, be, nu: (be[i], 0, 0)),
                  pl.BlockSpec((None, ff, d), lambda i, be, nu: (be[i], 0, 0))],
        out_specs=pl.BlockSpec((blk, LANES), lambda i, be, nu: (i, 0)),
        scratch_shapes=[pltpu.VMEM((d, ff), bf16), pltpu.VMEM((d, ff), bf16), pltpu.VMEM((ff, d), bf16)],
    )
    return pl.pallas_call(
        _expert_kernel,
        grid_spec=grid_spec,
        out_shape=jax.ShapeDtypeStruct(xs.shape, u32),
        compiler_params=_cparams(("arbitrary",)),
        name="experts",
    )(block_e, n_used, xs, w1, w3, w2)


def _combine_kernel(dest_ref, ys_ref, rec_ref, x1_ref, gpost_ref, gate2_ref, o_ref, buf, sem, *, tm):
    def row_copy(d, k, r):
        return pltpu.make_async_copy(ys_ref.at[pl.ds(pl.multiple_of(d * ROW_TILE, ROW_TILE), ROW_TILE), :],
                                     buf.at[k, pl.ds(pl.multiple_of(r * ROW_TILE, ROW_TILE), ROW_TILE), :],
                                     sem)

    def issue(r, c):
        row_copy(dest_ref[TOP_K * r], 0, r).start()
        row_copy(dest_ref[TOP_K * r + 1], 1, r).start()
        return c

    lax.fori_loop(0, tm, issue, 0, unroll=DMA_UNROLL)
    for k in range(TOP_K):
        pltpu.make_async_copy(ys_ref.at[pl.ds(0, tm * ROW_TILE), :], buf.at[k], sem).wait()

    rec = rec_ref[...]
    y = (rec[:, 4:5] * _load_position_major(buf.at[0], tm)
         + rec[:, 5:6] * _load_position_major(buf.at[1], tm))
    yn = y * lax.rsqrt(jnp.mean(y * y, axis=-1, keepdims=True) + EPS) * gpost_ref[...]
    o_ref[...] = x1_ref[...] + gate2_ref[...] * yn


def _combine(dest, ys, rec, x1, g_post, gate2, *, seq, tm):
    t, d = x1.shape
    return pl.pallas_call(
        functools.partial(_combine_kernel, tm=tm),
        grid=(t // tm,),
        in_specs=[pl.BlockSpec((TOP_K * tm,), lambda i: (i,), memory_space=pltpu.SMEM),
                  pl.BlockSpec(memory_space=pl.ANY),
                  pl.BlockSpec((tm, ROUTE_LANES), lambda i: (i, 0)),
                  pl.BlockSpec((tm, d), lambda i: (i, 0)),
                  pl.BlockSpec((1, d), lambda i: (0, 0)),
                  pl.BlockSpec((None, 1, d), lambda i: ((i * tm) // seq, 0, 0))],
        out_specs=pl.BlockSpec((tm, d), lambda i: (i, 0)),
        out_shape=jax.ShapeDtypeStruct((t, d), f32),
        scratch_shapes=[pltpu.VMEM((TOP_K, tm * ROW_TILE, LANES), u32), pltpu.SemaphoreType.DMA(())],
        compiler_params=_cparams(("arbitrary",)),
        name="combine",
    )(dest, ys, rec, x1, g_post, gate2)


def _layer(x, c, w_ada, b_ada, g_pre_mix, g_post_mix, w_in, b_in, w_dw, b_dw, ln_conv_g, ln_conv_b,
           w_conv_out, b_conv_out, rel_bias, w_attn_out, w_out, g_pre_ffn, g_post_ffn,
           w_router_group, b_router_group, w_router_expert, b_router_expert, w1, w3, w2):
    batch, seq, d = x.shape
    t = batch * seq
    row = lambda v: v.reshape(1, -1)

    c_pad = jnp.zeros((SUBLANES, d), f32).at[:batch].set(c)
    mod = _ada(c_pad, w_ada, row(b_ada))[:batch]
    shift1, scale1, gate1, shift2, scale2, gate2 = [m.reshape(batch, 1, d) for m in jnp.split(mod, 6, axis=-1)]

    o_q = 2 * CONV_CH
    o_gate = o_q + 3 * ATTN_WIDTH

    def permute(a):
        parts = [a[..., o_gate:], a[..., :o_q]]
        for gi in range(len(DILATED_GROUPS)):
            for which in range(3):
                lo = o_q + which * ATTN_WIDTH + gi * GROUP_W
                parts.append(a[..., lo:lo + GROUP_W])
        return jnp.concatenate(parts, axis=-1)

    x2 = x.reshape(t, d)
    zt, qkv = _inproj(x2, row(g_pre_mix), scale1, shift1, permute(w_in).astype(bf16), row(permute(b_in)),
                      seq=seq, tm=512)

    conv_act = _conv(zt.reshape(batch, seq, ZT_COLS), w_dw, row(b_dw), row(ln_conv_g), row(ln_conv_b),
                     tm=512).reshape(t, CONV_CH)

    os_, lses = [], []
    for gi, (_, dil) in enumerate(DILATED_GROUPS):
        o, lse = _attn_group(qkv, rel_bias.reshape(-1), gi, dil, batch=batch, seq=seq)
        os_.append(o)
        lses.append(lse)

    pad = ROUTE_LANES - N_GROUPS - N_EXPERTS
    w_r = jnp.concatenate([w_router_group, w_router_expert, jnp.zeros((d, pad), f32)], axis=1).astype(bf16)
    b_r = row(jnp.concatenate([b_router_group, b_router_expert.reshape(-1), jnp.zeros((pad,), f32)]))
    x1, h2p, logits = _merge(x2, conv_act, os_, lses, zt, w_conv_out.astype(bf16), row(b_conv_out),
                             w_attn_out.astype(bf16), w_out.astype(bf16), row(g_post_mix), gate1,
                             row(g_pre_ffn), scale2, shift2, w_r, b_r, seq=seq, tm=256)

    rec, rec_t, cnt = _route(logits, tm=512)

    counts = cnt[0, :N_EXPERTS].astype(jnp.int32)
    pcounts = (counts + MOE_ROWS - 1) // MOE_ROWS * MOE_ROWS
    pends = jnp.cumsum(pcounts)
    pstarts = pends - pcounts
    n_blocks = (t * TOP_K + N_EXPERTS * (MOE_ROWS - 1) + MOE_ROWS - 1) // MOE_ROWS
    blk_row0 = jnp.arange(n_blocks, dtype=jnp.int32) * MOE_ROWS
    block_e = jnp.minimum(jnp.sum(pends[None, :] <= blk_row0[:, None], axis=1), N_EXPERTS - 1).astype(jnp.int32)
    n_used = (pends[-1:] // MOE_ROWS).astype(jnp.int32)
    eid = rec_t[0:TOP_K].astype(jnp.int32)
    rank = rec_t[TOP_K:2 * TOP_K].astype(jnp.int32)
    start_of = jnp.sum(jnp.where(eid[..., None] == jnp.arange(N_EXPERTS), pstarts, 0), axis=-1)
    dest = jnp.transpose(start_of + rank).reshape(t * TOP_K)
    tail = n_blocks - N_EXPERTS + jnp.arange(N_EXPERTS, dtype=jnp.int32)
    zero_blk = jnp.concatenate([jnp.where(pcounts > counts, pends // MOE_ROWS - 1, -1),
                                jnp.where(tail >= n_used[0], tail, -1)]).astype(jnp.int32)

    xs = _dispatch(zero_blk, dest, h2p, p_rows=n_blocks * MOE_ROWS, tm=512)
    ys = _experts(block_e, n_used, xs, w1, w3, w2)
    out = _combine(dest, ys, rec, x1, row(g_post_ffn), gate2, seq=seq, tm=512)
    return out.reshape(batch, seq, d)


def kernel(x, c, w_ada, b_ada, g_pre_mix, g_post_mix, w_in, b_in, w_dw, b_dw, ln_conv_g, ln_conv_b,
           w_conv_out, b_conv_out, rel_bias, w_attn_out, w_out, g_pre_ffn, g_post_ffn,
           w_router_group, b_router_group, w_router_expert, b_router_expert, w1, w3, w2):
    depth = w_ada.shape[0]
    for l in range(depth):
        pick = (lambda a: a.reshape(a.shape[1:])) if depth == 1 else (lambda a, l=l: a[l])
        x = _layer(x, c, pick(w_ada), pick(b_ada), pick(g_pre_mix), pick(g_post_mix), pick(w_in),
                   pick(b_in), pick(w_dw), pick(b_dw), pick(ln_conv_g), pick(ln_conv_b),
                   pick(w_conv_out), pick(b_conv_out), rel_bias, pick(w_attn_out), pick(w_out),
                   pick(g_pre_ffn), pick(g_post_ffn), pick(w_router_group), pick(b_router_group),
                   pick(w_router_expert), pick(b_router_expert), pick(w1), pick(w3), pick(w2))
    return x
```

```python
import functools
import math

import numpy as np
import jax
import jax.numpy as jnp
from jax import lax
from jax.experimental import pallas as pl
from jax.experimental.pallas import tpu as pltpu

D_MODEL = 2048
CONV_CH = 1024
CONV_WIDTH = 31
N_ATTN_HEADS = 12
HEADS_PER_GROUP = 4
HEAD_DIM = 128
ATTN_WIDTH = N_ATTN_HEADS * HEAD_DIM
DILATED_GROUPS = ((128, 1), (512, 4), (2048, 16))
NUM_BUCKETS = 32
REL_MAX_DISTANCE = 1024
N_GROUPS = 8
EXPERTS_PER_GROUP = 8
N_EXPERTS = N_GROUPS * EXPERTS_PER_GROUP
TOP_K = 2
EXPERT_FF = 512
EPS = 1e-6
NEG_INF = -1e30

LANES = 128
SUBLANES = 8
V7X_VMEM_LIMIT_BYTES = 56 * 1024 * 1024

GROUP_W = HEADS_PER_GROUP * HEAD_DIM
ZT_COLS = 2 * D_MODEL + 2 * CONV_CH
COL_GLU = 2 * D_MODEL
INPROJ_TN = 3 * GROUP_W
ZT_TILES = ZT_COLS // INPROJ_TN
SLABS_PER_TILE = INPROJ_TN // (2 * LANES)
ATTN_HALF = 64
ATTN_QB = 128
ATTN_WIN = ATTN_QB + 2 * ATTN_HALF
MOE_ROWS = 128
ROUTE_LANES = 128
ROUTE_FIELDS = 8
PACKED_W = D_MODEL // 2
ROW_TILE = PACKED_W // LANES

bf16 = jnp.bfloat16
f32 = jnp.float32
u32 = jnp.uint32
HI_MASK = 0xFFFF0000


def _cparams(sem):
    return pltpu.CompilerParams(dimension_semantics=sem, vmem_limit_bytes=V7X_VMEM_LIMIT_BYTES)


def _sigmoid(x):
    return 1.0 / (1.0 + jnp.exp(-x))


def _pack_pairs(v):
    n = v.shape[1] // 2
    lo = lax.bitcast_convert_type(v[:, :n].astype(bf16).astype(f32), u32)
    hi = lax.bitcast_convert_type(v[:, n:].astype(bf16).astype(f32), u32)
    return (lo >> 16) | (hi & u32(HI_MASK))


def _unpack_lo(u):
    return lax.bitcast_convert_type(u << 16, f32)


def _unpack_hi(u):
    return lax.bitcast_convert_type(u & u32(HI_MASK), f32)


def _store_position_major(ref, v):
    packed = _pack_pairs(v)
    for j in range(ROW_TILE):
        ref[pl.ds(j, v.shape[0], stride=ROW_TILE), :] = packed[:, j * LANES:(j + 1) * LANES]


def _load_position_major(ref, n):
    chunks = [ref[pl.ds(j, n, stride=ROW_TILE), :] for j in range(ROW_TILE)]
    return jnp.concatenate([_unpack_lo(c) for c in chunks] + [_unpack_hi(c) for c in chunks], axis=1)


def _ada_kernel(c_ref, w_ref, b_ref, o_ref):
    c = c_ref[...]
    a = (c * _sigmoid(c)).astype(bf16)
    o_ref[...] = jnp.dot(a, w_ref[...].astype(bf16), preferred_element_type=f32) + b_ref[...]


def _ada(c_pad, w_ada, b_ada):
    rows, d = c_pad.shape
    n = w_ada.shape[1]
    tn = 1024
    return pl.pallas_call(
        _ada_kernel,
        grid=(n // tn,),
        in_specs=[pl.BlockSpec((rows, d), lambda j: (0, 0)),
                  pl.BlockSpec((d, tn), lambda j: (0, j)),
                  pl.BlockSpec((1, tn), lambda j: (0, j))],
        out_specs=pl.BlockSpec((rows, tn), lambda j: (0, j)),
        out_shape=jax.ShapeDtypeStruct((rows, n), f32),
        compiler_params=_cparams(("arbitrary",)),
        name="ada_mod",
    )(c_pad, w_ada, b_ada)


def _inproj_kernel(x_ref, g_ref, scale_ref, shift_ref, w_ref, b_ref, zt_ref, qkv_ref, h_scr):
    j = pl.program_id(1)

    @pl.when(j == 0)
    def _():
        x = x_ref[...]
        ms = jnp.mean(x * x, axis=-1, keepdims=True)
        h = x * lax.rsqrt(ms + EPS) * g_ref[...]
        h_scr[...] = (h * (1.0 + scale_ref[...]) + shift_ref[...]).astype(bf16)

    def chunk(c):
        cs = slice(c * GROUP_W, (c + 1) * GROUP_W)
        return jnp.dot(h_scr[...], w_ref[:, cs], preferred_element_type=f32) + b_ref[:, cs]

    @pl.when(j < ZT_TILES)
    def _():
        for c in range(INPROJ_TN // GROUP_W):
            zt_ref[:, c * GROUP_W:(c + 1) * GROUP_W] = chunk(c).astype(bf16)

    @pl.when(j >= ZT_TILES)
    def _():
        per = GROUP_W // (2 * LANES)
        for c in range(INPROJ_TN // GROUP_W):
            z = chunk(c)
            for s in range(per):
                qkv_ref[c * per + s] = _pack_pairs(z[:, 2 * s * LANES:2 * (s + 1) * LANES])


def _inproj(x2, g, scale, shift, w, b, *, seq, tm):
    t, d = x2.shape
    n = w.shape[1]
    tn = INPROJ_TN
    n_groups = n // tn - ZT_TILES
    bmap = lambda i, j: ((i * tm) // seq, 0, 0)
    return pl.pallas_call(
        _inproj_kernel,
        grid=(t // tm, n // tn),
        in_specs=[pl.BlockSpec((tm, d), lambda i, j: (i, 0)),
                  pl.BlockSpec((1, d), lambda i, j: (0, 0)),
                  pl.BlockSpec((None, 1, d), bmap),
                  pl.BlockSpec((None, 1, d), bmap),
                  pl.BlockSpec((d, tn), lambda i, j: (0, j)),
                  pl.BlockSpec((1, tn), lambda i, j: (0, j))],
        out_specs=[pl.BlockSpec((tm, tn), lambda i, j: (i, jnp.minimum(j, ZT_TILES - 1))),
                   pl.BlockSpec((SLABS_PER_TILE, tm, LANES),
                                lambda i, j: (jnp.maximum(j - ZT_TILES, 0), i, 0))],
        out_shape=[jax.ShapeDtypeStruct((t, ZT_COLS), bf16),
                   jax.ShapeDtypeStruct((n_groups * SLABS_PER_TILE, t, LANES), u32)],
        scratch_shapes=[pltpu.VMEM((tm, d), bf16)],
        compiler_params=_cparams(("arbitrary", "arbitrary")),
        name="inproj",
    )(x2, g, scale, shift, w, b)


CONV_HALO = 16
CONV_CHUNK = 16
CONV_SUB = CONV_CH // LANES
NORM_CHUNK = 64


def _conv_kernel(ap_ref, ac_ref, an_ref, gp_ref, gc_ref, gn_ref, w_ref, bdw_ref, lng_ref, lnb_ref,
                 o_ref, u_scr, y_scr, *, tm):
    i = pl.program_id(1)
    last = pl.num_programs(1) - 1

    def glu(a_ref, g_ref):
        return a_ref[...].astype(f32) * _sigmoid(g_ref[...].astype(f32))

    def put(row0, val):
        for j in range(CONV_SUB):
            u_scr[pl.ds(row0 * CONV_SUB + j, val.shape[0], stride=CONV_SUB), :] = (
                val[:, j * LANES:(j + 1) * LANES])

    put(0, jnp.where(i > 0, glu(ap_ref, gp_ref), 0.0))
    put(CONV_HALO, glu(ac_ref, gc_ref))
    put(CONV_HALO + tm, jnp.where(i < last, glu(an_ref, gn_ref), 0.0))

    def chunk(c, carry):
        r0 = c * CONV_CHUNK
        acc = jnp.zeros((CONV_CHUNK, CONV_SUB, LANES), f32) + bdw_ref[...][None]
        for k in range(CONV_WIDTH):
            start = pl.multiple_of((r0 + k + 1) * CONV_SUB, CONV_SUB)
            xk = u_scr[pl.ds(start, CONV_CHUNK * CONV_SUB), :].reshape(CONV_CHUNK, CONV_SUB, LANES)
            acc = acc + w_ref[k][None] * xk
        out0 = pl.multiple_of(r0 * CONV_SUB, CONV_CHUNK * CONV_SUB)
        y_scr[pl.ds(out0, CONV_CHUNK * CONV_SUB), :] = acc.reshape(CONV_CHUNK * CONV_SUB, LANES)
        return carry

    lax.fori_loop(0, tm // CONV_CHUNK, chunk, 0)

    def norm(c, carry):
        r0 = pl.multiple_of(c * NORM_CHUNK, NORM_CHUNK)
        acc = jnp.concatenate(
            [y_scr[pl.ds(r0 * CONV_SUB + j, NORM_CHUNK, stride=CONV_SUB), :] for j in range(CONV_SUB)],
            axis=1)
        mu = jnp.mean(acc, axis=-1, keepdims=True)
        cen = acc - mu
        var = jnp.mean(cen * cen, axis=-1, keepdims=True)
        y = cen * lax.rsqrt(var + EPS) * lng_ref[...] + lnb_ref[...]
        o_ref[pl.ds(r0, NORM_CHUNK), :] = (y * _sigmoid(y)).astype(bf16)
        return carry

    lax.fori_loop(0, tm // NORM_CHUNK, norm, 0)


def _conv(z3, w_dw, b_dw, ln_g, ln_b, *, tm):
    b, s, _ = z3.shape
    hb = tm // CONV_HALO
    nhb = s // CONV_HALO
    ca, cg = COL_GLU // CONV_CH, COL_GLU // CONV_CH + 1

    def halo(col, which):
        if which < 0:
            return pl.BlockSpec((None, CONV_HALO, CONV_CH),
                                lambda bb, i: (bb, jnp.maximum(i * hb - 1, 0), col))
        return pl.BlockSpec((None, CONV_HALO, CONV_CH),
                            lambda bb, i: (bb, jnp.minimum((i + 1) * hb, nhb - 1), col))

    cur = lambda col: pl.BlockSpec((None, tm, CONV_CH), lambda bb, i: (bb, i, col))
    vec = pl.BlockSpec((1, CONV_CH), lambda bb, i: (0, 0))
    return pl.pallas_call(
        functools.partial(_conv_kernel, tm=tm),
        grid=(b, s // tm),
        in_specs=[halo(ca, -1), cur(ca), halo(ca, 1), halo(cg, -1), cur(cg), halo(cg, 1),
                  pl.BlockSpec((CONV_WIDTH, CONV_SUB, LANES), lambda bb, i: (0, 0, 0)),
                  pl.BlockSpec((CONV_SUB, LANES), lambda bb, i: (0, 0)), vec, vec],
        out_specs=pl.BlockSpec((None, tm, CONV_CH), lambda bb, i: (bb, i, 0)),
        out_shape=jax.ShapeDtypeStruct((b, s, CONV_CH), bf16),
        scratch_shapes=[pltpu.VMEM(((tm + 2 * CONV_HALO) * CONV_SUB, LANES), f32),
                        pltpu.VMEM((tm * CONV_SUB, LANES), f32)],
        compiler_params=_cparams(("arbitrary", "arbitrary")),
        name="conv_branch",
    )(z3, z3, z3, z3, z3, z3, w_dw.reshape(CONV_WIDTH, CONV_SUB, LANES), b_dw.reshape(CONV_SUB, LANES),
      ln_g, ln_b)


def _t5_bucket_table(dil):
    delta = (np.arange(ATTN_WIN)[None, :] - ATTN_HALF) - np.arange(ATTN_QB)[:, None]
    rel = delta * dil
    nb = NUM_BUCKETS // 2
    max_exact = nb // 2
    n = np.abs(rel)
    nf = np.maximum(n, 1).astype(np.float32)
    large = max_exact + (np.log(nf / np.float32(max_exact)) / np.float32(math.log(REL_MAX_DISTANCE / max_exact))
                         * np.float32(nb - max_exact)).astype(np.int32)
    large = np.minimum(large, nb - 1)
    bucket = np.where(rel > 0, nb, 0) + np.where(n < max_exact, n, large)
    return np.where(np.abs(delta) <= ATTN_HALF, bucket, -1).astype(np.int32)


def _attn_kernel(rb_ref, bkt_ref, q_ref, kp_ref, kc_ref, kn_ref, vp_ref, vc_ref, vn_ref, o_ref, lse_ref,
                 k_scr, v_scr, bias_scr, *, gi, dil, ts, sub_len):
    i = pl.program_id(1)
    hb = ATTN_HALF * dil
    tq = ts // dil

    @pl.when((pl.program_id(0) == 0) & (i == 0))
    def _():
        bk = bkt_ref[...]
        for hh in range(HEADS_PER_GROUP):
            acc = jnp.full((ATTN_QB, ATTN_WIN), NEG_INF, f32)
            for b_id in range(NUM_BUCKETS):
                acc = jnp.where(bk == b_id, rb_ref[b_id * N_ATTN_HEADS + gi * HEADS_PER_GROUP + hh], acc)
            bias_scr[hh] = acc

    k_scr[:, 0:hb, :] = kp_ref[...]
    k_scr[:, hb:hb + ts, :] = kc_ref[...]
    k_scr[:, hb + ts:, :] = kn_ref[...]
    v_scr[:, 0:hb, :] = vp_ref[...]
    v_scr[:, hb:hb + ts, :] = vc_ref[...]
    v_scr[:, hb + ts:, :] = vn_ref[...]

    scale = HEAD_DIM ** -0.5
    col = lax.broadcasted_iota(jnp.int32, (ATTN_QB, ATTN_WIN), 1)
    lane = lax.broadcasted_iota(jnp.int32, (ATTN_QB, LANES), 1)

    def rows(start, n):
        if dil == 1:
            return pl.ds(pl.multiple_of(start, ATTN_QB), n)
        return pl.ds(start, n, stride=dil)

    def heads(words):
        out = []
        for w in words:
            out += [_unpack_lo(w).astype(bf16), _unpack_hi(w).astype(bf16)]
        return out

    def unit(u, carry):
        r = jnp.bitwise_and(u, dil - 1)
        jb = lax.shift_right_logical(u, jnp.int32(dil.bit_length() - 1))
        start = jb * (ATTN_QB * dil) + r
        q = heads([q_ref[sl, rows(start, ATTN_QB), :] for sl in range(2)])
        kw = heads([k_scr[sl, rows(start, ATTN_WIN), :] for sl in range(2)])
        vw = heads([v_scr[sl, rows(start, ATTN_WIN), :] for sl in range(2)])
        key0 = i * tq + jb * ATTN_QB - ATTN_HALF
        valid = (col + key0 >= 0) & (col + key0 < sub_len)
        s = [lax.dot_general(q[hh], kw[hh], (((1,), (1,)), ((), ())), preferred_element_type=f32)
             for hh in range(HEADS_PER_GROUP)]
        s = [jnp.where(valid, s[hh] * scale + bias_scr[hh], NEG_INF) for hh in range(HEADS_PER_GROUP)]
        m = [jnp.max(x, axis=-1, keepdims=True) for x in s]
        p = [jnp.exp(s[hh] - m[hh]) for hh in range(HEADS_PER_GROUP)]
        den = [jnp.sum(x, axis=-1, keepdims=True) for x in p]
        pv = [jnp.dot(p[hh].astype(bf16), vw[hh], preferred_element_type=f32)
              for hh in range(HEADS_PER_GROUP)]
        lse_tile = jnp.zeros((ATTN_QB, LANES), f32)
        for hh in range(HEADS_PER_GROUP):
            o_ref[hh, rows(start, ATTN_QB), :] = pv[hh] / den[hh]
            lse_tile = jnp.where(lane == hh, m[hh] + jnp.log(den[hh]), lse_tile)
        lse_ref[rows(start, ATTN_QB), :] = lse_tile
        return carry

    lax.fori_loop(0, ts // ATTN_QB, unit, 0)


def _attn_group(qkv, rel_bias_flat, gi, dil, *, batch, seq):
    t = batch * seq
    ts = max(1024, ATTN_QB * dil)
    hb = ATTN_HALF * dil
    steps = seq // ts
    hpt = ts // hb
    hps = seq // hb
    slab = lambda which: gi * 3 + which

    def cur(which):
        return pl.BlockSpec((2, ts, LANES), lambda b, i: (slab(which), b * steps + i, 0))

    def prev(which):
        return pl.BlockSpec((2, hb, LANES),
                            lambda b, i: (slab(which), b * hps + jnp.maximum(i * hpt - 1, 0), 0))

    def nxt(which):
        return pl.BlockSpec((2, hb, LANES),
                            lambda b, i: (slab(which), b * hps + jnp.minimum((i + 1) * hpt, hps - 1), 0))

    bkt = jnp.asarray(_t5_bucket_table(dil))
    return pl.pallas_call(
        functools.partial(_attn_kernel, gi=gi, dil=dil, ts=ts, sub_len=seq // dil),
        grid=(batch, steps),
        in_specs=[pl.BlockSpec(memory_space=pltpu.SMEM),
                  pl.BlockSpec((ATTN_QB, ATTN_WIN), lambda b, i: (0, 0)),
                  cur(0), prev(1), cur(1), nxt(1), prev(2), cur(2), nxt(2)],
        out_specs=[pl.BlockSpec((HEADS_PER_GROUP, ts, LANES), lambda b, i: (0, b * steps + i, 0)),
                   pl.BlockSpec((ts, LANES), lambda b, i: (b * steps + i, 0))],
        out_shape=[jax.ShapeDtypeStruct((HEADS_PER_GROUP, t, LANES), f32),
                   jax.ShapeDtypeStruct((t, LANES), f32)],
        scratch_shapes=[pltpu.VMEM((2, ts + 2 * hb, LANES), u32),
                        pltpu.VMEM((2, ts + 2 * hb, LANES), u32),
                        pltpu.VMEM((HEADS_PER_GROUP, ATTN_QB, ATTN_WIN), f32)],
        compiler_params=_cparams(("arbitrary", "arbitrary")),
        name=f"attn_g{gi}",
    )(rel_bias_flat, bkt, qkv, qkv, qkv, qkv, qkv, qkv, qkv)


def _merge_kernel(x_ref, ca_ref, o0_ref, o1_ref, o2_ref, l0_ref, l1_ref, l2_ref, ga_ref, gb_ref,
                  wco_ref, bco_ref, wao_ref, wo_ref, gpost_ref, gate1_ref, gpre_ref, scale2_ref,
                  shift2_ref, wr_ref, br_ref, x1_ref, h2p_ref, logit_ref):
    branch_a = jnp.dot(ca_ref[...], wco_ref[...], preferred_element_type=f32) + bco_ref[...]

    l0, l1, l2 = l0_ref[...], l1_ref[...], l2_ref[...]
    lmax = jnp.maximum(jnp.maximum(l0, l1), l2)
    e0, e1, e2 = jnp.exp(l0 - lmax), jnp.exp(l1 - lmax), jnp.exp(l2 - lmax)
    inv = 1.0 / (e0 + e1 + e2)
    parts = []
    for hh in range(HEADS_PER_GROUP):
        ls = slice(hh, hh + 1)
        parts.append((e0[:, ls] * inv[:, ls]) * o0_ref[hh] + (e1[:, ls] * inv[:, ls]) * o1_ref[hh]
                     + (e2[:, ls] * inv[:, ls]) * o2_ref[hh])
    att = jnp.concatenate(parts, axis=1).astype(bf16)
    branch_b = jnp.dot(att, wao_ref[...], preferred_element_type=f32)

    mix = _sigmoid(ga_ref[...].astype(f32)) * branch_a + _sigmoid(gb_ref[...].astype(f32)) * branch_b
    y = jnp.dot(mix.astype(bf16), wo_ref[...], preferred_element_type=f32)

    yn = y * lax.rsqrt(jnp.mean(y * y, axis=-1, keepdims=True) + EPS) * gpost_ref[...]
    x1 = x_ref[...] + gate1_ref[...] * yn
    x1_ref[...] = x1
    hn = x1 * lax.rsqrt(jnp.mean(x1 * x1, axis=-1, keepdims=True) + EPS) * gpre_ref[...]
    h2 = hn * (1.0 + scale2_ref[...]) + shift2_ref[...]
    _store_position_major(h2p_ref, h2)
    logit_ref[...] = jnp.dot(h2.astype(bf16), wr_ref[...], preferred_element_type=f32) + br_ref[...]


def _merge(x2, conv_act, os_, lses, zt, w_co, b_co, w_ao, w_o, g_post, gate1, g_pre, scale2, shift2,
           w_r, b_r, *, seq, tm):
    t, d = x2.shape
    rows = lambda w: pl.BlockSpec((tm, w), lambda i: (i, 0))
    heads = pl.BlockSpec((HEADS_PER_GROUP, tm, LANES), lambda i: (0, i, 0))
    full = lambda a: pl.BlockSpec(a.shape, lambda i: (0,) * a.ndim)
    perb = pl.BlockSpec((None, 1, d), lambda i: ((i * tm) // seq, 0, 0))
    return pl.pallas_call(
        _merge_kernel,
        grid=(t // tm,),
        in_specs=[rows(d), rows(CONV_CH), heads, heads, heads, rows(LANES), rows(LANES), rows(LANES),
                  pl.BlockSpec((tm, d), lambda i: (i, 0)), pl.BlockSpec((tm, d), lambda i: (i, 1)),
                  full(w_co), full(b_co), full(w_ao), full(w_o), full(g_post), perb, full(g_pre),
                  perb, perb, full(w_r), full(b_r)],
        out_specs=[rows(d), pl.BlockSpec((tm * ROW_TILE, LANES), lambda i: (i, 0)), rows(ROUTE_LANES)],
        out_shape=[jax.ShapeDtypeStruct((t, d), f32),
                   jax.ShapeDtypeStruct((t * ROW_TILE, LANES), u32),
                   jax.ShapeDtypeStruct((t, ROUTE_LANES), f32)],
        compiler_params=_cparams(("arbitrary",)),
        name="merge",
    )(x2, conv_act, *os_, *lses, zt, zt, w_co, b_co, w_ao, w_o, g_post, gate1, g_pre, scale2, shift2,
      w_r, b_r)


def _route_kernel(logit_ref, rec_ref, rect_ref, cnt_ref, carry):
    i = pl.program_id(0)

    @pl.when(i == 0)
    def _():
        carry[...] = jnp.zeros_like(carry)

    lg = logit_ref[...]
    tm = lg.shape[0]
    lane = lax.broadcasted_iota(jnp.int32, lg.shape, 1).astype(f32)
    big = float(2 * ROUTE_LANES)

    def first_max(mask):
        v = jnp.max(jnp.where(mask, lg, -jnp.inf), axis=-1, keepdims=True)
        idx = jnp.min(jnp.where(mask & (lg == v), lane, big), axis=-1, keepdims=True)
        return v, idx

    gmask = lane < N_GROUPS
    gmax, gsel = first_max(gmask)
    p_g = 1.0 / jnp.sum(jnp.where(gmask, jnp.exp(lg - gmax), 0.0), axis=-1, keepdims=True)
    e_lo = N_GROUPS + EXPERTS_PER_GROUP * gsel
    emask = (lane >= e_lo) & (lane < e_lo + EXPERTS_PER_GROUP)
    v0, i0 = first_max(emask)
    v1, i1 = first_max(emask & (lane != i0))
    t1 = jnp.exp(v1 - v0)
    w0 = p_g / (1.0 + t1)
    w1 = p_g * t1 / (1.0 + t1)
    e0 = i0 - N_GROUPS
    e1 = i1 - N_GROUPS

    hit0 = lane == e0
    hit1 = lane == e1
    onehot = jnp.where(hit0 | hit1, 1.0, 0.0)
    r_i = lax.broadcasted_iota(jnp.int32, (tm, tm), 0)
    c_i = lax.broadcasted_iota(jnp.int32, (tm, tm), 1)
    tril = jnp.where(c_i < r_i, 1.0, 0.0).astype(bf16)
    before = jnp.dot(tril, onehot.astype(bf16), preferred_element_type=f32) + carry[...]
    rank0 = jnp.sum(jnp.where(hit0, before, 0.0), axis=-1, keepdims=True)
    rank1 = jnp.sum(jnp.where(hit1, before, 0.0), axis=-1, keepdims=True)
    carry[...] = carry[...] + jnp.sum(onehot, axis=0, keepdims=True)
    cnt_ref[...] = carry[...]

    rec = jnp.zeros_like(lg)
    for slot, val in enumerate((e0, e1, rank0, rank1, w0, w1)):
        rec = jnp.where(lane == slot, val, rec)
    rec_ref[...] = rec
    rect_ref[...] = jnp.transpose(rec)[0:ROUTE_FIELDS, :]


def _route(logits, *, tm):
    t = logits.shape[0]
    return pl.pallas_call(
        _route_kernel,
        grid=(t // tm,),
        in_specs=[pl.BlockSpec((tm, ROUTE_LANES), lambda i: (i, 0))],
        out_specs=[pl.BlockSpec((tm, ROUTE_LANES), lambda i: (i, 0)),
                   pl.BlockSpec((ROUTE_FIELDS, tm), lambda i: (0, i)),
                   pl.BlockSpec((1, ROUTE_LANES), lambda i: (0, 0))],
        out_shape=[jax.ShapeDtypeStruct((t, ROUTE_LANES), f32),
                   jax.ShapeDtypeStruct((ROUTE_FIELDS, t), f32),
                   jax.ShapeDtypeStruct((1, ROUTE_LANES), f32)],
        scratch_shapes=[pltpu.VMEM((1, ROUTE_LANES), f32)],
        compiler_params=_cparams(("arbitrary",)),
        name="route",
    )(logits)


DMA_UNROLL = 8


def _dispatch_kernel(zero_blk_ref, dest_ref, h_ref, xs_ref, zero_scr, sem, zsem, *, tm):
    def row_copy(r, d):
        return pltpu.make_async_copy(h_ref.at[pl.ds(pl.multiple_of(r * ROW_TILE, ROW_TILE), ROW_TILE), :],
                                     xs_ref.at[pl.ds(pl.multiple_of(d * ROW_TILE, ROW_TILE), ROW_TILE), :],
                                     sem)

    def zero_copy(blk):
        n = MOE_ROWS * ROW_TILE
        return pltpu.make_async_copy(zero_scr, xs_ref.at[pl.ds(pl.multiple_of(blk * n, n), n), :], zsem)

    @pl.when(pl.program_id(0) == 0)
    def _():
        zero_scr[...] = jnp.zeros_like(zero_scr)
        n_cand = zero_blk_ref.shape[0]

        def start(j, c):
            @pl.when(zero_blk_ref[j] >= 0)
            def _():
                zero_copy(zero_blk_ref[j]).start()
            return c

        def wait(j, c):
            @pl.when(zero_blk_ref[j] >= 0)
            def _():
                zero_copy(0).wait()
            return c

        lax.fori_loop(0, n_cand, start, 0)
        lax.fori_loop(0, n_cand, wait, 0)

    def issue(r, c):
        row_copy(r, dest_ref[TOP_K * r]).start(priority=0)
        row_copy(r, dest_ref[TOP_K * r + 1]).start(priority=1)
        return c

    lax.fori_loop(0, tm, issue, 0, unroll=DMA_UNROLL)
    for _ in range(TOP_K):
        pltpu.make_async_copy(h_ref, xs_ref.at[pl.ds(0, tm * ROW_TILE), :], sem).wait()


def _dispatch(zero_blk, dest, h2p, *, p_rows, tm):
    grid_spec = pltpu.PrefetchScalarGridSpec(
        num_scalar_prefetch=1,
        grid=(h2p.shape[0] // (tm * ROW_TILE),),
        in_specs=[pl.BlockSpec((TOP_K * tm,), lambda i, *_: (i,), memory_space=pltpu.SMEM),
                  pl.BlockSpec((tm * ROW_TILE, LANES), lambda i, *_: (i, 0))],
        out_specs=pl.BlockSpec(memory_space=pl.ANY),
        scratch_shapes=[pltpu.VMEM((MOE_ROWS * ROW_TILE, LANES), u32), pltpu.SemaphoreType.DMA(()),
                        pltpu.SemaphoreType.DMA(())],
    )
    return pl.pallas_call(
        functools.partial(_dispatch_kernel, tm=tm),
        grid_spec=grid_spec,
        out_shape=jax.ShapeDtypeStruct((p_rows * ROW_TILE, LANES), u32),
        compiler_params=_cparams(("arbitrary",)),
        name="dispatch",
    )(zero_blk, dest, h2p)


def _expert_kernel(bstart_ref, nblk_ref, tail_ref, xs_ref, w1_ref, w3_ref, w2_ref, ys_ref,
                   w1b, w3b, w2b, xbuf, ybuf, sem_in, sem_out):
    e = pl.program_id(0)
    n = nblk_ref[e]
    b0 = bstart_ref[e]
    blk = MOE_ROWS * ROW_TILE

    def hbm_rows(ref, b):
        return ref.at[pl.ds(pl.multiple_of(b * blk, blk), blk), :]

    def in_copy(b, slot):
        return pltpu.make_async_copy(hbm_rows(xs_ref, b0 + b), xbuf.at[slot], sem_in.at[slot])

    def out_copy(b, slot):
        return pltpu.make_async_copy(ybuf.at[slot], hbm_rows(ys_ref, b0 + b), sem_out.at[slot])

    @pl.when(n > 0)
    def _():
        in_copy(0, 0).start()

    w1b[...] = w1_ref[...].astype(bf16)
    w3b[...] = w3_ref[...].astype(bf16)
    w2b[...] = w2_ref[...].astype(bf16)

    def body(b, carry):
        slot = jnp.bitwise_and(b, 1)
        in_copy(b, slot).wait()

        @pl.when(b + 1 < n)
        def _():
            in_copy(b + 1, 1 - slot).start()

        @pl.when(b >= 2)
        def _():
            out_copy(b - 2, slot).wait()

        x = _load_position_major(xbuf.at[slot], MOE_ROWS).astype(bf16)
        h1 = jnp.dot(x, w1b[...], preferred_element_type=f32)
        h3 = jnp.dot(x, w3b[...], preferred_element_type=f32)
        a = (h1 * _sigmoid(h1) * h3).astype(bf16)
        _store_position_major(ybuf.at[slot], jnp.dot(a, w2b[...], preferred_element_type=f32))
        out_copy(b, slot).start()
        return carry

    lax.fori_loop(0, n, body, 0)

    @pl.when(n >= 2)
    def _():
        out_copy(n - 2, jnp.bitwise_and(n, 1)).wait()

    @pl.when(n >= 1)
    def _():
        out_copy(n - 1, jnp.bitwise_and(n - 1, 1)).wait()

    @pl.when(e == pl.num_programs(0) - 1)
    def _():
        ybuf[0] = jnp.zeros(ybuf.shape[1:], u32)
        first, count = tail_ref[0], tail_ref[1]

        def zero_copy(b):
            return pltpu.make_async_copy(ybuf.at[0], hbm_rows(ys_ref, first + b), sem_out.at[0])

        lax.fori_loop(0, count, lambda b, c: (zero_copy(b).start(), c)[1], 0)
        lax.fori_loop(0, count, lambda b, c: (zero_copy(0).wait(), c)[1], 0)


def _experts(bstart, nblk, tail, xs, w1, w3, w2):
    blk = MOE_ROWS * ROW_TILE
    n_exp, d, ff = w1.shape
    grid_spec = pltpu.PrefetchScalarGridSpec(
        num_scalar_prefetch=3,
        grid=(n_exp,),
        in_specs=[pl.BlockSpec(memory_space=pl.ANY),
                  pl.BlockSpec((None, d, ff), lambda e, *_: (e, 0, 0)),
                  pl.BlockSpec((None, d, ff), lambda e, *_: (e, 0, 0)),
                  pl.BlockSpec((None, ff, d), lambda e, *_: (e, 0, 0))],
        out_specs=pl.BlockSpec(memory_space=pl.ANY),
        scratch_shapes=[pltpu.VMEM((d, ff), bf16), pltpu.VMEM((d, ff), bf16), pltpu.VMEM((ff, d), bf16),
                        pltpu.VMEM((2, blk, LANES), u32), pltpu.VMEM((2, blk, LANES), u32),
                        pltpu.SemaphoreType.DMA((2,)), pltpu.SemaphoreType.DMA((2,))],
    )
    return pl.pallas_call(
        _expert_kernel,
        grid_spec=grid_spec,
        out_shape=jax.ShapeDtypeStruct(xs.shape, u32),
        compiler_params=_cparams(("arbitrary",)),
        name="experts",
    )(bstart, nblk, tail, xs, w1, w3, w2)


def _combine_kernel(dest_ref, dest_next_ref, ys_ref, rec_ref, x1_ref, gpost_ref, gate2_ref, o_ref, buf, sem,
                    *, tm):
    i = pl.program_id(0)
    slot = jnp.bitwise_and(i, 1)

    def gather(d_ref, s):
        def row_copy(d, k, r):
            return pltpu.make_async_copy(
                ys_ref.at[pl.ds(pl.multiple_of(d * ROW_TILE, ROW_TILE), ROW_TILE), :],
                buf.at[s, k, pl.ds(pl.multiple_of(r * ROW_TILE, ROW_TILE), ROW_TILE), :], sem.at[s])

        def issue(r, c):
            row_copy(d_ref[TOP_K * r], 0, r).start(priority=0)
            row_copy(d_ref[TOP_K * r + 1], 1, r).start(priority=1)
            return c

        lax.fori_loop(0, tm, issue, 0, unroll=DMA_UNROLL)

    @pl.when(i == 0)
    def _():
        gather(dest_ref, 0)

    @pl.when(i + 1 < pl.num_programs(0))
    def _():
        gather(dest_next_ref, 1 - slot)

    for k in range(TOP_K):
        pltpu.make_async_copy(ys_ref.at[pl.ds(0, tm * ROW_TILE), :], buf.at[slot, k], sem.at[slot]).wait()

    rec = rec_ref[...]
    y = (rec[:, 4:5] * _load_position_major(buf.at[slot, 0], tm)
         + rec[:, 5:6] * _load_position_major(buf.at[slot, 1], tm))
    yn = y * lax.rsqrt(jnp.mean(y * y, axis=-1, keepdims=True) + EPS) * gpost_ref[...]
    o_ref[...] = x1_ref[...] + gate2_ref[...] * yn


def _combine(dest, ys, rec, x1, g_post, gate2, *, seq, tm):
    t, d = x1.shape
    last = t // tm - 1
    return pl.pallas_call(
        functools.partial(_combine_kernel, tm=tm),
        grid=(t // tm,),
        in_specs=[pl.BlockSpec((TOP_K * tm,), lambda i: (i,), memory_space=pltpu.SMEM),
                  pl.BlockSpec((TOP_K * tm,), lambda i: (jnp.minimum(i + 1, last),), memory_space=pltpu.SMEM),
                  pl.BlockSpec(memory_space=pl.ANY),
                  pl.BlockSpec((tm, ROUTE_LANES), lambda i: (i, 0)),
                  pl.BlockSpec((tm, d), lambda i: (i, 0)),
                  pl.BlockSpec((1, d), lambda i: (0, 0)),
                  pl.BlockSpec((None, 1, d), lambda i: ((i * tm) // seq, 0, 0))],
        out_specs=pl.BlockSpec((tm, d), lambda i: (i, 0)),
        out_shape=jax.ShapeDtypeStruct((t, d), f32),
        scratch_shapes=[pltpu.VMEM((2, TOP_K, tm * ROW_TILE, LANES), u32), pltpu.SemaphoreType.DMA((2,))],
        compiler_params=_cparams(("arbitrary",)),
        name="combine",
    )(dest, dest, ys, rec, x1, g_post, gate2)


def _layer(x, c, w_ada, b_ada, g_pre_mix, g_post_mix, w_in, b_in, w_dw, b_dw, ln_conv_g, ln_conv_b,
           w_conv_out, b_conv_out, rel_bias, w_attn_out, w_out, g_pre_ffn, g_post_ffn,
           w_router_group, b_router_group, w_router_expert, b_router_expert, w1, w3, w2):
    batch, seq, d = x.shape
    t = batch * seq
    row = lambda v: v.reshape(1, -1)

    c_pad = jnp.zeros((SUBLANES, d), f32).at[:batch].set(c)
    mod = _ada(c_pad, w_ada, row(b_ada))[:batch]
    shift1, scale1, gate1, shift2, scale2, gate2 = [m.reshape(batch, 1, d) for m in jnp.split(mod, 6, axis=-1)]

    o_q = 2 * CONV_CH
    o_gate = o_q + 3 * ATTN_WIDTH

    def permute(a):
        parts = [a[..., o_gate:], a[..., :o_q]]
        for gi in range(len(DILATED_GROUPS)):
            for which in range(3):
                lo = o_q + which * ATTN_WIDTH + gi * GROUP_W
                parts.append(a[..., lo:lo + GROUP_W])
        return jnp.concatenate(parts, axis=-1)

    x2 = x.reshape(t, d)
    zt, qkv = _inproj(x2, row(g_pre_mix), scale1, shift1, permute(w_in).astype(bf16), row(permute(b_in)),
                      seq=seq, tm=1024)

    conv_act = _conv(zt.reshape(batch, seq, ZT_COLS), w_dw, row(b_dw), row(ln_conv_g), row(ln_conv_b),
                     tm=512).reshape(t, CONV_CH)

    os_, lses = [], []
    for gi, (_, dil) in enumerate(DILATED_GROUPS):
        o, lse = _attn_group(qkv, rel_bias.reshape(-1), gi, dil, batch=batch, seq=seq)
        os_.append(o)
        lses.append(lse)

    pad = ROUTE_LANES - N_GROUPS - N_EXPERTS
    w_r = jnp.concatenate([w_router_group, w_router_expert, jnp.zeros((d, pad), f32)], axis=1).astype(bf16)
    b_r = row(jnp.concatenate([b_router_group, b_router_expert.reshape(-1), jnp.zeros((pad,), f32)]))
    x1, h2p, logits = _merge(x2, conv_act, os_, lses, zt, w_conv_out.astype(bf16), row(b_conv_out),
                             w_attn_out.astype(bf16), w_out.astype(bf16), row(g_post_mix), gate1,
                             row(g_pre_ffn), scale2, shift2, w_r, b_r, seq=seq, tm=256)

    rec, rec_t, cnt = _route(logits, tm=512)

    counts = cnt[0, :N_EXPERTS].astype(jnp.int32)
    pcounts = (counts + MOE_ROWS - 1) // MOE_ROWS * MOE_ROWS
    pends = jnp.cumsum(pcounts)
    pstarts = pends - pcounts
    n_blocks = (t * TOP_K + N_EXPERTS * (MOE_ROWS - 1) + MOE_ROWS - 1) // MOE_ROWS
    n_used = (pends[-1:] // MOE_ROWS).astype(jnp.int32)
    tail_blocks = jnp.concatenate([n_used, n_blocks - n_used])
    eid = rec_t[0:TOP_K].astype(jnp.int32)
    rank = rec_t[TOP_K:2 * TOP_K].astype(jnp.int32)
    start_of = jnp.sum(jnp.where(eid[..., None] == jnp.arange(N_EXPERTS), pstarts, 0), axis=-1)
    dest = jnp.transpose(start_of + rank).reshape(t * TOP_K)
    tail = n_blocks - N_EXPERTS + jnp.arange(N_EXPERTS, dtype=jnp.int32)
    zero_blk = jnp.concatenate([jnp.where(pcounts > counts, pends // MOE_ROWS - 1, -1),
                                jnp.where(tail >= n_used[0], tail, -1)]).astype(jnp.int32)

    xs = _dispatch(zero_blk, dest, h2p, p_rows=n_blocks * MOE_ROWS, tm=512)
    ys = _experts((pstarts // MOE_ROWS).astype(jnp.int32), (pcounts // MOE_ROWS).astype(jnp.int32),
                  tail_blocks, xs, w1, w3, w2)
    out = _combine(dest, ys, rec, x1, row(g_post_ffn), gate2, seq=seq, tm=512)
    return out.reshape(batch, seq, d)


def kernel(x, c, w_ada, b_ada, g_pre_mix, g_post_mix, w_in, b_in, w_dw, b_dw, ln_conv_g, ln_conv_b,
           w_conv_out, b_conv_out, rel_bias, w_attn_out, w_out, g_pre_ffn, g_post_ffn,
           w_router_group, b_router_group, w_router_expert, b_router_expert, w1, w3, w2):
    depth = w_ada.shape[0]
    for l in range(depth):
        pick = (lambda a: a.reshape(a.shape[1:])) if depth == 1 else (lambda a, l=l: a[l])
        x = _layer(x, c, pick(w_ada), pick(b_ada), pick(g_pre_mix), pick(g_post_mix), pick(w_in),
                   pick(b_in), pick(w_dw), pick(b_dw), pick(ln_conv_g), pick(ln_conv_b),
                   pick(w_conv_out), pick(b_conv_out), rel_bias, pick(w_attn_out), pick(w_out),
                   pick(g_pre_ffn), pick(g_post_ffn), pick(w_router_group), pick(b_router_group),
                   pick(w_router_expert), pick(b_router_expert), pick(w1), pick(w3), pick(w2))
    return x
```

```python
import functools
import math

import numpy as np
import jax
import jax.numpy as jnp
from jax import lax
from jax.experimental import pallas as pl
from jax.experimental.pallas import tpu as pltpu

D_MODEL = 2048
CONV_CH = 1024
CONV_WIDTH = 31
N_ATTN_HEADS = 12
HEADS_PER_GROUP = 4
HEAD_DIM = 128
ATTN_WIDTH = N_ATTN_HEADS * HEAD_DIM
DILATED_GROUPS = ((128, 1), (512, 4), (2048, 16))
NUM_BUCKETS = 32
REL_MAX_DISTANCE = 1024
N_GROUPS = 8
EXPERTS_PER_GROUP = 8
N_EXPERTS = N_GROUPS * EXPERTS_PER_GROUP
TOP_K = 2
EXPERT_FF = 512
EPS = 1e-6
NEG_INF = -1e30

LANES = 128
SUBLANES = 8
V7X_VMEM_LIMIT_BYTES = 56 * 1024 * 1024

GROUP_W = HEADS_PER_GROUP * HEAD_DIM
ZT_COLS = 2 * D_MODEL + 2 * CONV_CH
COL_GLU = 2 * D_MODEL
INPROJ_TN = 3 * GROUP_W
ZT_TILES = ZT_COLS // INPROJ_TN
SLABS_PER_TILE = INPROJ_TN // (2 * LANES)
ATTN_HALF = 64
ATTN_QB = 128
ATTN_WIN = ATTN_QB + 2 * ATTN_HALF
MOE_ROWS = 128
ROUTE_LANES = 128
ROUTE_FIELDS = 8
PACKED_W = D_MODEL // 2
ROW_TILE = PACKED_W // LANES

bf16 = jnp.bfloat16
f32 = jnp.float32
u32 = jnp.uint32
HI_MASK = 0xFFFF0000


def _cparams(sem):
    return pltpu.CompilerParams(dimension_semantics=sem, vmem_limit_bytes=V7X_VMEM_LIMIT_BYTES)


def _sigmoid(x):
    return 1.0 / (1.0 + jnp.exp(-x))


def _pack_pairs(v):
    n = v.shape[1] // 2
    lo = lax.bitcast_convert_type(v[:, :n].astype(bf16).astype(f32), u32)
    hi = lax.bitcast_convert_type(v[:, n:].astype(bf16).astype(f32), u32)
    return (lo >> 16) | (hi & u32(HI_MASK))


def _unpack_lo(u):
    return lax.bitcast_convert_type(u << 16, f32)


def _unpack_hi(u):
    return lax.bitcast_convert_type(u & u32(HI_MASK), f32)


def _store_position_major(ref, v):
    packed = _pack_pairs(v)
    for j in range(ROW_TILE):
        ref[pl.ds(j, v.shape[0], stride=ROW_TILE), :] = packed[:, j * LANES:(j + 1) * LANES]


def _load_position_major(ref, n):
    chunks = [ref[pl.ds(j, n, stride=ROW_TILE), :] for j in range(ROW_TILE)]
    return jnp.concatenate([_unpack_lo(c) for c in chunks] + [_unpack_hi(c) for c in chunks], axis=1)


def _ada_kernel(c_ref, w_ref, b_ref, o_ref):
    c = c_ref[...]
    a = (c * _sigmoid(c)).astype(bf16)
    o_ref[...] = jnp.dot(a, w_ref[...].astype(bf16), preferred_element_type=f32) + b_ref[...]


def _ada(c_pad, w_ada, b_ada):
    rows, d = c_pad.shape
    n = w_ada.shape[1]
    tn = 1024
    return pl.pallas_call(
        _ada_kernel,
        grid=(n // tn,),
        in_specs=[pl.BlockSpec((rows, d), lambda j: (0, 0)),
                  pl.BlockSpec((d, tn), lambda j: (0, j)),
                  pl.BlockSpec((1, tn), lambda j: (0, j))],
        out_specs=pl.BlockSpec((rows, tn), lambda j: (0, j)),
        out_shape=jax.ShapeDtypeStruct((rows, n), f32),
        compiler_params=_cparams(("arbitrary",)),
        name="ada_mod",
    )(c_pad, w_ada, b_ada)


def _inproj_kernel(x_ref, g_ref, scale_ref, shift_ref, w_ref, b_ref, zt_ref, qkv_ref, h_scr):
    j = pl.program_id(1)

    @pl.when(j == 0)
    def _():
        x = x_ref[...]
        ms = jnp.mean(x * x, axis=-1, keepdims=True)
        h = x * lax.rsqrt(ms + EPS) * g_ref[...]
        h_scr[...] = (h * (1.0 + scale_ref[...]) + shift_ref[...]).astype(bf16)

    def chunk(c):
        cs = slice(c * GROUP_W, (c + 1) * GROUP_W)
        return jnp.dot(h_scr[...], w_ref[:, cs], preferred_element_type=f32) + b_ref[:, cs]

    @pl.when(j < ZT_TILES)
    def _():
        for c in range(INPROJ_TN // GROUP_W):
            zt_ref[:, c * GROUP_W:(c + 1) * GROUP_W] = chunk(c).astype(bf16)

    @pl.when(j >= ZT_TILES)
    def _():
        per = GROUP_W // (2 * LANES)
        for c in range(INPROJ_TN // GROUP_W):
            z = chunk(c)
            for s in range(per):
                qkv_ref[c * per + s] = _pack_pairs(z[:, 2 * s * LANES:2 * (s + 1) * LANES])


def _inproj(x2, g, scale, shift, w, b, *, seq, tm):
    t, d = x2.shape
    n = w.shape[1]
    tn = INPROJ_TN
    n_groups = n // tn - ZT_TILES
    bmap = lambda i, j: ((i * tm) // seq, 0, 0)
    return pl.pallas_call(
        _inproj_kernel,
        grid=(t // tm, n // tn),
        in_specs=[pl.BlockSpec((tm, d), lambda i, j: (i, 0)),
                  pl.BlockSpec((1, d), lambda i, j: (0, 0)),
                  pl.BlockSpec((None, 1, d), bmap),
                  pl.BlockSpec((None, 1, d), bmap),
                  pl.BlockSpec((d, tn), lambda i, j: (0, j)),
                  pl.BlockSpec((1, tn), lambda i, j: (0, j))],
        out_specs=[pl.BlockSpec((tm, tn), lambda i, j: (i, jnp.minimum(j, ZT_TILES - 1))),
                   pl.BlockSpec((SLABS_PER_TILE, tm, LANES),
                                lambda i, j: (jnp.maximum(j - ZT_TILES, 0), i, 0))],
        out_shape=[jax.ShapeDtypeStruct((t, ZT_COLS), bf16),
                   jax.ShapeDtypeStruct((n_groups * SLABS_PER_TILE, t, LANES), u32)],
        scratch_shapes=[pltpu.VMEM((tm, d), bf16)],
        compiler_params=_cparams(("arbitrary", "arbitrary")),
        name="inproj",
    )(x2, g, scale, shift, w, b)


CONV_HALO = 16
CONV_CHUNK = 16
CONV_SUB = CONV_CH // LANES
NORM_CHUNK = 64


def _conv_kernel(ap_ref, ac_ref, an_ref, gp_ref, gc_ref, gn_ref, w_ref, bdw_ref, lng_ref, lnb_ref,
                 o_ref, u_scr, y_scr, *, tm):
    i = pl.program_id(1)
    last = pl.num_programs(1) - 1

    def glu(a_ref, g_ref):
        return a_ref[...].astype(f32) * _sigmoid(g_ref[...].astype(f32))

    def put(row0, val):
        for j in range(CONV_SUB):
            u_scr[pl.ds(row0 * CONV_SUB + j, val.shape[0], stride=CONV_SUB), :] = (
                val[:, j * LANES:(j + 1) * LANES])

    put(0, jnp.where(i > 0, glu(ap_ref, gp_ref), 0.0))
    put(CONV_HALO, glu(ac_ref, gc_ref))
    put(CONV_HALO + tm, jnp.where(i < last, glu(an_ref, gn_ref), 0.0))

    def chunk(c, carry):
        r0 = c * CONV_CHUNK
        acc = jnp.zeros((CONV_CHUNK, CONV_SUB, LANES), f32) + bdw_ref[...][None]
        for k in range(CONV_WIDTH):
            start = pl.multiple_of((r0 + k + 1) * CONV_SUB, CONV_SUB)
            xk = u_scr[pl.ds(start, CONV_CHUNK * CONV_SUB), :].reshape(CONV_CHUNK, CONV_SUB, LANES)
            acc = acc + w_ref[k][None] * xk
        out0 = pl.multiple_of(r0 * CONV_SUB, CONV_CHUNK * CONV_SUB)
        y_scr[pl.ds(out0, CONV_CHUNK * CONV_SUB), :] = acc.reshape(CONV_CHUNK * CONV_SUB, LANES)
        return carry

    lax.fori_loop(0, tm // CONV_CHUNK, chunk, 0)

    def norm(c, carry):
        r0 = pl.multiple_of(c * NORM_CHUNK, NORM_CHUNK)
        acc = jnp.concatenate(
            [y_scr[pl.ds(r0 * CONV_SUB + j, NORM_CHUNK, stride=CONV_SUB), :] for j in range(CONV_SUB)],
            axis=1)
        mu = jnp.mean(acc, axis=-1, keepdims=True)
        cen = acc - mu
        var = jnp.mean(cen * cen, axis=-1, keepdims=True)
        y = cen * lax.rsqrt(var + EPS) * lng_ref[...] + lnb_ref[...]
        o_ref[pl.ds(r0, NORM_CHUNK), :] = (y * _sigmoid(y)).astype(bf16)
        return carry

    lax.fori_loop(0, tm // NORM_CHUNK, norm, 0, unroll=2)


def _conv(z3, w_dw, b_dw, ln_g, ln_b, *, tm):
    b, s, _ = z3.shape
    hb = tm // CONV_HALO
    nhb = s // CONV_HALO
    ca, cg = COL_GLU // CONV_CH, COL_GLU // CONV_CH + 1

    def halo(col, which):
        if which < 0:
            return pl.BlockSpec((None, CONV_HALO, CONV_CH),
                                lambda bb, i: (bb, jnp.maximum(i * hb - 1, 0), col))
        return pl.BlockSpec((None, CONV_HALO, CONV_CH),
                            lambda bb, i: (bb, jnp.minimum((i + 1) * hb, nhb - 1), col))

    cur = lambda col: pl.BlockSpec((None, tm, CONV_CH), lambda bb, i: (bb, i, col))
    vec = pl.BlockSpec((1, CONV_CH), lambda bb, i: (0, 0))
    return pl.pallas_call(
        functools.partial(_conv_kernel, tm=tm),
        grid=(b, s // tm),
        in_specs=[halo(ca, -1), cur(ca), halo(ca, 1), halo(cg, -1), cur(cg), halo(cg, 1),
                  pl.BlockSpec((CONV_WIDTH, CONV_SUB, LANES), lambda bb, i: (0, 0, 0)),
                  pl.BlockSpec((CONV_SUB, LANES), lambda bb, i: (0, 0)), vec, vec],
        out_specs=pl.BlockSpec((None, tm, CONV_CH), lambda bb, i: (bb, i, 0)),
        out_shape=jax.ShapeDtypeStruct((b, s, CONV_CH), bf16),
        scratch_shapes=[pltpu.VMEM(((tm + 2 * CONV_HALO) * CONV_SUB, LANES), f32),
                        pltpu.VMEM((tm * CONV_SUB, LANES), f32)],
        compiler_params=_cparams(("arbitrary", "arbitrary")),
        name="conv_branch",
    )(z3, z3, z3, z3, z3, z3, w_dw.reshape(CONV_WIDTH, CONV_SUB, LANES), b_dw.reshape(CONV_SUB, LANES),
      ln_g, ln_b)


def _t5_bucket_table(dil):
    delta = (np.arange(ATTN_WIN)[None, :] - ATTN_HALF) - np.arange(ATTN_QB)[:, None]
    rel = delta * dil
    nb = NUM_BUCKETS // 2
    max_exact = nb // 2
    n = np.abs(rel)
    nf = np.maximum(n, 1).astype(np.float32)
    large = max_exact + (np.log(nf / np.float32(max_exact)) / np.float32(math.log(REL_MAX_DISTANCE / max_exact))
                         * np.float32(nb - max_exact)).astype(np.int32)
    large = np.minimum(large, nb - 1)
    bucket = np.where(rel > 0, nb, 0) + np.where(n < max_exact, n, large)
    return np.where(np.abs(delta) <= ATTN_HALF, bucket, -1).astype(np.int32)


def _attn_kernel(rb_ref, bkt_ref, q_ref, kp_ref, kc_ref, kn_ref, vp_ref, vc_ref, vn_ref, o_ref, lse_ref,
                 k_scr, v_scr, bias_scr, *, gi, dil, ts, sub_len):
    i = pl.program_id(1)
    hb = ATTN_HALF * dil
    tq = ts // dil

    @pl.when((pl.program_id(0) == 0) & (i == 0))
    def _():
        bk = bkt_ref[...]
        for hh in range(HEADS_PER_GROUP):
            acc = jnp.full((ATTN_QB, ATTN_WIN), NEG_INF, f32)
            for b_id in range(NUM_BUCKETS):
                acc = jnp.where(bk == b_id, rb_ref[b_id * N_ATTN_HEADS + gi * HEADS_PER_GROUP + hh], acc)
            bias_scr[hh] = acc

    k_scr[:, 0:hb, :] = kp_ref[...]
    k_scr[:, hb:hb + ts, :] = kc_ref[...]
    k_scr[:, hb + ts:, :] = kn_ref[...]
    v_scr[:, 0:hb, :] = vp_ref[...]
    v_scr[:, hb:hb + ts, :] = vc_ref[...]
    v_scr[:, hb + ts:, :] = vn_ref[...]

    scale = HEAD_DIM ** -0.5
    col = lax.broadcasted_iota(jnp.int32, (ATTN_QB, ATTN_WIN), 1)
    lane = lax.broadcasted_iota(jnp.int32, (ATTN_QB, LANES), 1)

    def rows(start, n):
        if dil == 1:
            return pl.ds(pl.multiple_of(start, ATTN_QB), n)
        return pl.ds(start, n, stride=dil)

    def heads(words):
        out = []
        for w in words:
            out += [_unpack_lo(w).astype(bf16), _unpack_hi(w).astype(bf16)]
        return out

    def unit(u, carry):
        r = jnp.bitwise_and(u, dil - 1)
        jb = lax.shift_right_logical(u, jnp.int32(dil.bit_length() - 1))
        start = jb * (ATTN_QB * dil) + r
        q = heads([q_ref[sl, rows(start, ATTN_QB), :] for sl in range(2)])
        kw = heads([k_scr[sl, rows(start, ATTN_WIN), :] for sl in range(2)])
        vw = heads([v_scr[sl, rows(start, ATTN_WIN), :] for sl in range(2)])
        key0 = i * tq + jb * ATTN_QB - ATTN_HALF
        valid = (col + key0 >= 0) & (col + key0 < sub_len)
        s = [lax.dot_general(q[hh], kw[hh], (((1,), (1,)), ((), ())), preferred_element_type=f32)
             for hh in range(HEADS_PER_GROUP)]
        s = [jnp.where(valid, s[hh] * scale + bias_scr[hh], NEG_INF) for hh in range(HEADS_PER_GROUP)]
        m = [jnp.max(x, axis=-1, keepdims=True) for x in s]
        p = [jnp.exp(s[hh] - m[hh]) for hh in range(HEADS_PER_GROUP)]
        den = [jnp.sum(x, axis=-1, keepdims=True) for x in p]
        pv = [jnp.dot(p[hh].astype(bf16), vw[hh], preferred_element_type=f32)
              for hh in range(HEADS_PER_GROUP)]
        lse_tile = jnp.zeros((ATTN_QB, LANES), f32)
        for hh in range(HEADS_PER_GROUP):
            o_ref[hh, rows(start, ATTN_QB), :] = pv[hh] / den[hh]
            lse_tile = jnp.where(lane == hh, m[hh] + jnp.log(den[hh]), lse_tile)
        lse_ref[rows(start, ATTN_QB), :] = lse_tile
        return carry

    lax.fori_loop(0, ts // ATTN_QB, unit, 0, unroll=2)


def _attn_group(qkv, rel_bias_flat, gi, dil, *, batch, seq):
    t = batch * seq
    ts = max(1024, ATTN_QB * dil)
    hb = ATTN_HALF * dil
    steps = seq // ts
    hpt = ts // hb
    hps = seq // hb
    slab = lambda which: gi * 3 + which

    def cur(which):
        return pl.BlockSpec((2, ts, LANES), lambda b, i: (slab(which), b * steps + i, 0))

    def prev(which):
        return pl.BlockSpec((2, hb, LANES),
                            lambda b, i: (slab(which), b * hps + jnp.maximum(i * hpt - 1, 0), 0))

    def nxt(which):
        return pl.BlockSpec((2, hb, LANES),
                            lambda b, i: (slab(which), b * hps + jnp.minimum((i + 1) * hpt, hps - 1), 0))

    bkt = jnp.asarray(_t5_bucket_table(dil))
    return pl.pallas_call(
        functools.partial(_attn_kernel, gi=gi, dil=dil, ts=ts, sub_len=seq // dil),
        grid=(batch, steps),
        in_specs=[pl.BlockSpec(memory_space=pltpu.SMEM),
                  pl.BlockSpec((ATTN_QB, ATTN_WIN), lambda b, i: (0, 0)),
                  cur(0), prev(1), cur(1), nxt(1), prev(2), cur(2), nxt(2)],
        out_specs=[pl.BlockSpec((HEADS_PER_GROUP, ts, LANES), lambda b, i: (0, b * steps + i, 0)),
                   pl.BlockSpec((ts, LANES), lambda b, i: (b * steps + i, 0))],
        out_shape=[jax.ShapeDtypeStruct((HEADS_PER_GROUP, t, LANES), f32),
                   jax.ShapeDtypeStruct((t, LANES), f32)],
        scratch_shapes=[pltpu.VMEM((2, ts + 2 * hb, LANES), u32),
                        pltpu.VMEM((2, ts + 2 * hb, LANES), u32),
                        pltpu.VMEM((HEADS_PER_GROUP, ATTN_QB, ATTN_WIN), f32)],
        compiler_params=_cparams(("arbitrary", "arbitrary")),
        name=f"attn_g{gi}",
    )(rel_bias_flat, bkt, qkv, qkv, qkv, qkv, qkv, qkv, qkv)


def _merge_kernel(x_ref, ca_ref, o0_ref, o1_ref, o2_ref, l0_ref, l1_ref, l2_ref, ga_ref, gb_ref,
                  wco_ref, bco_ref, wao_ref, wo_ref, gpost_ref, gate1_ref, gpre_ref, scale2_ref,
                  shift2_ref, wr_ref, br_ref, x1_ref, h2p_ref, logit_ref):
    branch_a = jnp.dot(ca_ref[...], wco_ref[...], preferred_element_type=f32) + bco_ref[...]

    l0, l1, l2 = l0_ref[...], l1_ref[...], l2_ref[...]
    lmax = jnp.maximum(jnp.maximum(l0, l1), l2)
    e0, e1, e2 = jnp.exp(l0 - lmax), jnp.exp(l1 - lmax), jnp.exp(l2 - lmax)
    inv = 1.0 / (e0 + e1 + e2)
    parts = []
    for hh in range(HEADS_PER_GROUP):
        ls = slice(hh, hh + 1)
        parts.append((e0[:, ls] * inv[:, ls]) * o0_ref[hh] + (e1[:, ls] * inv[:, ls]) * o1_ref[hh]
                     + (e2[:, ls] * inv[:, ls]) * o2_ref[hh])
    att = jnp.concatenate(parts, axis=1).astype(bf16)
    branch_b = jnp.dot(att, wao_ref[...], preferred_element_type=f32)

    mix = _sigmoid(ga_ref[...].astype(f32)) * branch_a + _sigmoid(gb_ref[...].astype(f32)) * branch_b
    y = jnp.dot(mix.astype(bf16), wo_ref[...], preferred_element_type=f32)

    yn = y * lax.rsqrt(jnp.mean(y * y, axis=-1, keepdims=True) + EPS) * gpost_ref[...]
    x1 = x_ref[...] + gate1_ref[...] * yn
    x1_ref[...] = x1
    hn = x1 * lax.rsqrt(jnp.mean(x1 * x1, axis=-1, keepdims=True) + EPS) * gpre_ref[...]
    h2 = hn * (1.0 + scale2_ref[...]) + shift2_ref[...]
    _store_position_major(h2p_ref, h2)
    logit_ref[...] = jnp.dot(h2.astype(bf16), wr_ref[...], preferred_element_type=f32) + br_ref[...]


def _merge(x2, conv_act, os_, lses, zt, w_co, b_co, w_ao, w_o, g_post, gate1, g_pre, scale2, shift2,
           w_r, b_r, *, seq, tm):
    t, d = x2.shape
    rows = lambda w: pl.BlockSpec((tm, w), lambda i: (i, 0))
    heads = pl.BlockSpec((HEADS_PER_GROUP, tm, LANES), lambda i: (0, i, 0))
    full = lambda a: pl.BlockSpec(a.shape, lambda i: (0,) * a.ndim)
    perb = pl.BlockSpec((None, 1, d), lambda i: ((i * tm) // seq, 0, 0))
    return pl.pallas_call(
        _merge_kernel,
        grid=(t // tm,),
        in_specs=[rows(d), rows(CONV_CH), heads, heads, heads, rows(LANES), rows(LANES), rows(LANES),
                  pl.BlockSpec((tm, d), lambda i: (i, 0)), pl.BlockSpec((tm, d), lambda i: (i, 1)),
                  full(w_co), full(b_co), full(w_ao), full(w_o), full(g_post), perb, full(g_pre),
                  perb, perb, full(w_r), full(b_r)],
        out_specs=[rows(d), pl.BlockSpec((tm * ROW_TILE, LANES), lambda i: (i, 0)), rows(ROUTE_LANES)],
        out_shape=[jax.ShapeDtypeStruct((t, d), f32),
                   jax.ShapeDtypeStruct((t * ROW_TILE, LANES), u32),
                   jax.ShapeDtypeStruct((t, ROUTE_LANES), f32)],
        compiler_params=_cparams(("arbitrary",)),
        name="merge",
    )(x2, conv_act, *os_, *lses, zt, zt, w_co, b_co, w_ao, w_o, g_post, gate1, g_pre, scale2, shift2,
      w_r, b_r)


def _route_kernel(logit_ref, rec_ref, rect_ref, cnt_ref, carry):
    i = pl.program_id(0)

    @pl.when(i == 0)
    def _():
        carry[...] = jnp.zeros_like(carry)

    lg = logit_ref[...]
    tm = lg.shape[0]
    lane = lax.broadcasted_iota(jnp.int32, lg.shape, 1).astype(f32)
    big = float(2 * ROUTE_LANES)

    def first_max(mask):
        v = jnp.max(jnp.where(mask, lg, -jnp.inf), axis=-1, keepdims=True)
        idx = jnp.min(jnp.where(mask & (lg == v), lane, big), axis=-1, keepdims=True)
        return v, idx

    gmask = lane < N_GROUPS
    gmax, gsel = first_max(gmask)
    p_g = 1.0 / jnp.sum(jnp.where(gmask, jnp.exp(lg - gmax), 0.0), axis=-1, keepdims=True)
    e_lo = N_GROUPS + EXPERTS_PER_GROUP * gsel
    emask = (lane >= e_lo) & (lane < e_lo + EXPERTS_PER_GROUP)
    v0, i0 = first_max(emask)
    v1, i1 = first_max(emask & (lane != i0))
    t1 = jnp.exp(v1 - v0)
    w0 = p_g / (1.0 + t1)
    w1 = p_g * t1 / (1.0 + t1)
    e0 = i0 - N_GROUPS
    e1 = i1 - N_GROUPS

    hit0 = lane == e0
    hit1 = lane == e1
    onehot = jnp.where(hit0 | hit1, 1.0, 0.0)
    r_i = lax.broadcasted_iota(jnp.int32, (tm, tm), 0)
    c_i = lax.broadcasted_iota(jnp.int32, (tm, tm), 1)
    tril = jnp.where(c_i < r_i, 1.0, 0.0).astype(bf16)
    before = jnp.dot(tril, onehot.astype(bf16), preferred_element_type=f32) + carry[...]
    rank0 = jnp.sum(jnp.where(hit0, before, 0.0), axis=-1, keepdims=True)
    rank1 = jnp.sum(jnp.where(hit1, before, 0.0), axis=-1, keepdims=True)
    carry[...] = carry[...] + jnp.sum(onehot, axis=0, keepdims=True)
    cnt_ref[...] = carry[...]

    rec = jnp.zeros_like(lg)
    for slot, val in enumerate((e0, e1, rank0, rank1, w0, w1)):
        rec = jnp.where(lane == slot, val, rec)
    rec_ref[...] = rec
    rect_ref[...] = jnp.transpose(rec)[0:ROUTE_FIELDS, :]


def _route(logits, *, tm):
    t = logits.shape[0]
    return pl.pallas_call(
        _route_kernel,
        grid=(t // tm,),
        in_specs=[pl.BlockSpec((tm, ROUTE_LANES), lambda i: (i, 0))],
        out_specs=[pl.BlockSpec((tm, ROUTE_LANES), lambda i: (i, 0)),
                   pl.BlockSpec((ROUTE_FIELDS, tm), lambda i: (0, i)),
                   pl.BlockSpec((1, ROUTE_LANES), lambda i: (0, 0))],
        out_shape=[jax.ShapeDtypeStruct((t, ROUTE_LANES), f32),
                   jax.ShapeDtypeStruct((ROUTE_FIELDS, t), f32),
                   jax.ShapeDtypeStruct((1, ROUTE_LANES), f32)],
        scratch_shapes=[pltpu.VMEM((1, ROUTE_LANES), f32)],
        compiler_params=_cparams(("arbitrary",)),
        name="route",
    )(logits)


DMA_UNROLL = 8


def _dispatch_kernel(zero_blk_ref, dest_ref, h_ref, xs_ref, zero_scr, sem, zsem, *, tm):
    def row_copy(r, d):
        return pltpu.make_async_copy(h_ref.at[pl.ds(pl.multiple_of(r * ROW_TILE, ROW_TILE), ROW_TILE), :],
                                     xs_ref.at[pl.ds(pl.multiple_of(d * ROW_TILE, ROW_TILE), ROW_TILE), :],
                                     sem)

    def zero_copy(blk):
        n = MOE_ROWS * ROW_TILE
        return pltpu.make_async_copy(zero_scr, xs_ref.at[pl.ds(pl.multiple_of(blk * n, n), n), :], zsem)

    @pl.when(pl.program_id(0) == 0)
    def _():
        zero_scr[...] = jnp.zeros_like(zero_scr)
        n_cand = zero_blk_ref.shape[0]

        def start(j, c):
            @pl.when(zero_blk_ref[j] >= 0)
            def _():
                zero_copy(zero_blk_ref[j]).start()
            return c

        def wait(j, c):
            @pl.when(zero_blk_ref[j] >= 0)
            def _():
                zero_copy(0).wait()
            return c

        lax.fori_loop(0, n_cand, start, 0)
        lax.fori_loop(0, n_cand, wait, 0)

    def issue(r, c):
        row_copy(r, dest_ref[TOP_K * r]).start(priority=0)
        row_copy(r, dest_ref[TOP_K * r + 1]).start(priority=1)
        return c

    lax.fori_loop(0, tm, issue, 0, unroll=DMA_UNROLL)
    for _ in range(TOP_K):
        pltpu.make_async_copy(h_ref, xs_ref.at[pl.ds(0, tm * ROW_TILE), :], sem).wait()


def _dispatch(zero_blk, dest, h2p, *, p_rows, tm):
    grid_spec = pltpu.PrefetchScalarGridSpec(
        num_scalar_prefetch=1,
        grid=(h2p.shape[0] // (tm * ROW_TILE),),
        in_specs=[pl.BlockSpec((TOP_K * tm,), lambda i, *_: (i,), memory_space=pltpu.SMEM),
                  pl.BlockSpec((tm * ROW_TILE, LANES), lambda i, *_: (i, 0))],
        out_specs=pl.BlockSpec(memory_space=pl.ANY),
        scratch_shapes=[pltpu.VMEM((MOE_ROWS * ROW_TILE, LANES), u32), pltpu.SemaphoreType.DMA(()),
                        pltpu.SemaphoreType.DMA(())],
    )
    return pl.pallas_call(
        functools.partial(_dispatch_kernel, tm=tm),
        grid_spec=grid_spec,
        out_shape=jax.ShapeDtypeStruct((p_rows * ROW_TILE, LANES), u32),
        compiler_params=_cparams(("arbitrary",)),
        name="dispatch",
    )(zero_blk, dest, h2p)


def _expert_kernel(bstart_ref, nblk_ref, tail_ref, xs_ref, w1_ref, w3_ref, w2_ref, ys_ref,
                   w1b, w3b, w2b, xbuf, ybuf, sem_in, sem_out):
    e = pl.program_id(0)
    n = nblk_ref[e]
    b0 = bstart_ref[e]
    blk = MOE_ROWS * ROW_TILE

    def hbm_rows(ref, b):
        return ref.at[pl.ds(pl.multiple_of(b * blk, blk), blk), :]

    def in_copy(b, slot):
        return pltpu.make_async_copy(hbm_rows(xs_ref, b0 + b), xbuf.at[slot], sem_in.at[slot])

    def out_copy(b, slot):
        return pltpu.make_async_copy(ybuf.at[slot], hbm_rows(ys_ref, b0 + b), sem_out.at[slot])

    @pl.when(n > 0)
    def _():
        in_copy(0, 0).start(priority=1)

    w1b[...] = w1_ref[...].astype(bf16)
    w3b[...] = w3_ref[...].astype(bf16)
    w2b[...] = w2_ref[...].astype(bf16)

    def body(b, carry):
        slot = jnp.bitwise_and(b, 1)
        in_copy(b, slot).wait()

        @pl.when(b + 1 < n)
        def _():
            in_copy(b + 1, 1 - slot).start(priority=1)

        @pl.when(b >= 2)
        def _():
            out_copy(b - 2, slot).wait()

        x = _load_position_major(xbuf.at[slot], MOE_ROWS).astype(bf16)
        h1 = jnp.dot(x, w1b[...], preferred_element_type=f32)
        h3 = jnp.dot(x, w3b[...], preferred_element_type=f32)
        a = (h1 * _sigmoid(h1) * h3).astype(bf16)
        _store_position_major(ybuf.at[slot], jnp.dot(a, w2b[...], preferred_element_type=f32))
        out_copy(b, slot).start(priority=1)
        return carry

    lax.fori_loop(0, n, body, 0)

    @pl.when(n >= 2)
    def _():
        out_copy(n - 2, jnp.bitwise_and(n, 1)).wait()

    @pl.when(n >= 1)
    def _():
        out_copy(n - 1, jnp.bitwise_and(n - 1, 1)).wait()

    @pl.when(e == pl.num_programs(0) - 1)
    def _():
        ybuf[0] = jnp.zeros(ybuf.shape[1:], u32)
        first, count = tail_ref[0], tail_ref[1]

        def zero_copy(b):
            return pltpu.make_async_copy(ybuf.at[0], hbm_rows(ys_ref, first + b), sem_out.at[0])

        lax.fori_loop(0, count, lambda b, c: (zero_copy(b).start(), c)[1], 0)
        lax.fori_loop(0, count, lambda b, c: (zero_copy(0).wait(), c)[1], 0)


def _experts(bstart, nblk, tail, xs, w1, w3, w2):
    blk = MOE_ROWS * ROW_TILE
    n_exp, d, ff = w1.shape
    grid_spec = pltpu.PrefetchScalarGridSpec(
        num_scalar_prefetch=3,
        grid=(n_exp,),
        in_specs=[pl.BlockSpec(memory_space=pl.ANY),
                  pl.BlockSpec((None, d, ff), lambda e, *_: (e, 0, 0)),
                  pl.BlockSpec((None, d, ff), lambda e, *_: (e, 0, 0)),
                  pl.BlockSpec((None, ff, d), lambda e, *_: (e, 0, 0))],
        out_specs=pl.BlockSpec(memory_space=pl.ANY),
        scratch_shapes=[pltpu.VMEM((d, ff), bf16), pltpu.VMEM((d, ff), bf16), pltpu.VMEM((ff, d), bf16),
                        pltpu.VMEM((2, blk, LANES), u32), pltpu.VMEM((2, blk, LANES), u32),
                        pltpu.SemaphoreType.DMA((2,)), pltpu.SemaphoreType.DMA((2,))],
    )
    return pl.pallas_call(
        _expert_kernel,
        grid_spec=grid_spec,
        out_shape=jax.ShapeDtypeStruct(xs.shape, u32),
        compiler_params=_cparams(("arbitrary",)),
        name="experts",
    )(bstart, nblk, tail, xs, w1, w3, w2)


def _combine_kernel(dest_ref, dest_next_ref, ys_ref, rec_ref, x1_ref, gpost_ref, gate2_ref, o_ref, buf, sem,
                    *, tm):
    i = pl.program_id(0)
    slot = jnp.bitwise_and(i, 1)

    def gather(d_ref, s):
        def row_copy(d, k, r):
            return pltpu.make_async_copy(
                ys_ref.at[pl.ds(pl.multiple_of(d * ROW_TILE, ROW_TILE), ROW_TILE), :],
                buf.at[s, k, pl.ds(pl.multiple_of(r * ROW_TILE, ROW_TILE), ROW_TILE), :], sem.at[s])

        def issue(r, c):
            row_copy(d_ref[TOP_K * r], 0, r).start(priority=0)
            row_copy(d_ref[TOP_K * r + 1], 1, r).start(priority=1)
            return c

        lax.fori_loop(0, tm, issue, 0, unroll=DMA_UNROLL)

    @pl.when(i == 0)
    def _():
        gather(dest_ref, 0)

    @pl.when(i + 1 < pl.num_programs(0))
    def _():
        gather(dest_next_ref, 1 - slot)

    for k in range(TOP_K):
        pltpu.make_async_copy(ys_ref.at[pl.ds(0, tm * ROW_TILE), :], buf.at[slot, k], sem.at[slot]).wait()

    rec = rec_ref[...]
    y = (rec[:, 4:5] * _load_position_major(buf.at[slot, 0], tm)
         + rec[:, 5:6] * _load_position_major(buf.at[slot, 1], tm))
    yn = y * lax.rsqrt(jnp.mean(y * y, axis=-1, keepdims=True) + EPS) * gpost_ref[...]
    o_ref[...] = x1_ref[...] + gate2_ref[...] * yn


def _combine(dest, ys, rec, x1, g_post, gate2, *, seq, tm):
    t, d = x1.shape
    last = t // tm - 1
    return pl.pallas_call(
        functools.partial(_combine_kernel, tm=tm),
        grid=(t // tm,),
        in_specs=[pl.BlockSpec((TOP_K * tm,), lambda i: (i,), memory_space=pltpu.SMEM),
                  pl.BlockSpec((TOP_K * tm,), lambda i: (jnp.minimum(i + 1, last),), memory_space=pltpu.SMEM),
                  pl.BlockSpec(memory_space=pl.ANY),
                  pl.BlockSpec((tm, ROUTE_LANES), lambda i: (i, 0)),
                  pl.BlockSpec((tm, d), lambda i: (i, 0)),
                  pl.BlockSpec((1, d), lambda i: (0, 0)),
                  pl.BlockSpec((None, 1, d), lambda i: ((i * tm) // seq, 0, 0))],
        out_specs=pl.BlockSpec((tm, d), lambda i: (i, 0)),
        out_shape=jax.ShapeDtypeStruct((t, d), f32),
        scratch_shapes=[pltpu.VMEM((2, TOP_K, tm * ROW_TILE, LANES), u32), pltpu.SemaphoreType.DMA((2,))],
        compiler_params=_cparams(("arbitrary",)),
        name="combine",
    )(dest, dest, ys, rec, x1, g_post, gate2)


def _layer(x, c, w_ada, b_ada, g_pre_mix, g_post_mix, w_in, b_in, w_dw, b_dw, ln_conv_g, ln_conv_b,
           w_conv_out, b_conv_out, rel_bias, w_attn_out, w_out, g_pre_ffn, g_post_ffn,
           w_router_group, b_router_group, w_router_expert, b_router_expert, w1, w3, w2):
    batch, seq, d = x.shape
    t = batch * seq
    row = lambda v: v.reshape(1, -1)

    c_pad = jnp.zeros((SUBLANES, d), f32).at[:batch].set(c)
    mod = _ada(c_pad, w_ada, row(b_ada))[:batch]
    shift1, scale1, gate1, shift2, scale2, gate2 = [m.reshape(batch, 1, d) for m in jnp.split(mod, 6, axis=-1)]

    o_q = 2 * CONV_CH
    o_gate = o_q + 3 * ATTN_WIDTH

    def permute(a):
        parts = [a[..., o_gate:], a[..., :o_q]]
        for gi in range(len(DILATED_GROUPS)):
            for which in range(3):
                lo = o_q + which * ATTN_WIDTH + gi * GROUP_W
                parts.append(a[..., lo:lo + GROUP_W])
        return jnp.concatenate(parts, axis=-1)

    x2 = x.reshape(t, d)
    zt, qkv = _inproj(x2, row(g_pre_mix), scale1, shift1, permute(w_in).astype(bf16), row(permute(b_in)),
                      seq=seq, tm=1024)

    conv_act = _conv(zt.reshape(batch, seq, ZT_COLS), w_dw, row(b_dw), row(ln_conv_g), row(ln_conv_b),
                     tm=512).reshape(t, CONV_CH)

    os_, lses = [], []
    for gi, (_, dil) in enumerate(DILATED_GROUPS):
        o, lse = _attn_group(qkv, rel_bias.reshape(-1), gi, dil, batch=batch, seq=seq)
        os_.append(o)
        lses.append(lse)

    pad = ROUTE_LANES - N_GROUPS - N_EXPERTS
    w_r = jnp.concatenate([w_router_group, w_router_expert, jnp.zeros((d, pad), f32)], axis=1).astype(bf16)
    b_r = row(jnp.concatenate([b_router_group, b_router_expert.reshape(-1), jnp.zeros((pad,), f32)]))
    x1, h2p, logits = _merge(x2, conv_act, os_, lses, zt, w_conv_out.astype(bf16), row(b_conv_out),
                             w_attn_out.astype(bf16), w_out.astype(bf16), row(g_post_mix), gate1,
                             row(g_pre_ffn), scale2, shift2, w_r, b_r, seq=seq, tm=256)

    rec, rec_t, cnt = _route(logits, tm=512)

    counts = cnt[0, :N_EXPERTS].astype(jnp.int32)
    pcounts = (counts + MOE_ROWS - 1) // MOE_ROWS * MOE_ROWS
    pends = jnp.cumsum(pcounts)
    pstarts = pends - pcounts
    n_blocks = (t * TOP_K + N_EXPERTS * (MOE_ROWS - 1) + MOE_ROWS - 1) // MOE_ROWS
    n_used = (pends[-1:] // MOE_ROWS).astype(jnp.int32)
    tail_blocks = jnp.concatenate([n_used, n_blocks - n_used])
    eid = rec_t[0:TOP_K].astype(jnp.int32)
    rank = rec_t[TOP_K:2 * TOP_K].astype(jnp.int32)
    start_of = jnp.sum(jnp.where(eid[..., None] == jnp.arange(N_EXPERTS), pstarts, 0), axis=-1)
    dest = jnp.transpose(start_of + rank).reshape(t * TOP_K)
    tail = n_blocks - N_EXPERTS + jnp.arange(N_EXPERTS, dtype=jnp.int32)
    zero_blk = jnp.concatenate([jnp.where(pcounts > counts, pends // MOE_ROWS - 1, -1),
                                jnp.where(tail >= n_used[0], tail, -1)]).astype(jnp.int32)

    xs = _dispatch(zero_blk, dest, h2p, p_rows=n_blocks * MOE_ROWS, tm=512)
    ys = _experts((pstarts // MOE_ROWS).astype(jnp.int32), (pcounts // MOE_ROWS).astype(jnp.int32),
                  tail_blocks, xs, w1, w3, w2)
    out = _combine(dest, ys, rec, x1, row(g_post_ffn), gate2, seq=seq, tm=512)
    return out.reshape(batch, seq, d)


def kernel(x, c, w_ada, b_ada, g_pre_mix, g_post_mix, w_in, b_in, w_dw, b_dw, ln_conv_g, ln_conv_b,
           w_conv_out, b_conv_out, rel_bias, w_attn_out, w_out, g_pre_ffn, g_post_ffn,
           w_router_group, b_router_group, w_router_expert, b_router_expert, w1, w3, w2):
    depth = w_ada.shape[0]
    for l in range(depth):
        pick = (lambda a: a.reshape(a.shape[1:])) if depth == 1 else (lambda a, l=l: a[l])
        x = _layer(x, c, pick(w_ada), pick(b_ada), pick(g_pre_mix), pick(g_post_mix), pick(w_in),
                   pick(b_in), pick(w_dw), pick(b_dw), pick(ln_conv_g), pick(ln_conv_b),
                   pick(w_conv_out), pick(b_conv_out), rel_bias, pick(w_attn_out), pick(w_out),
                   pick(g_pre_ffn), pick(g_post_ffn), pick(w_router_group), pick(b_router_group),
                   pick(w_router_expert), pick(b_router_expert), pick(w1), pick(w3), pick(w2))
    return x
```

```python
import functools
import math

import numpy as np
import jax
import jax.numpy as jnp
from jax import lax
from jax.experimental import pallas as pl
from jax.experimental.pallas import tpu as pltpu

D_MODEL = 2048
CONV_CH = 1024
CONV_WIDTH = 31
N_ATTN_HEADS = 12
HEADS_PER_GROUP = 4
HEAD_DIM = 128
ATTN_WIDTH = N_ATTN_HEADS * HEAD_DIM
DILATED_GROUPS = ((128, 1), (512, 4), (2048, 16))
NUM_BUCKETS = 32
REL_MAX_DISTANCE = 1024
N_GROUPS = 8
EXPERTS_PER_GROUP = 8
N_EXPERTS = N_GROUPS * EXPERTS_PER_GROUP
TOP_K = 2
EXPERT_FF = 512
EPS = 1e-6
NEG_INF = -1e30

LANES = 128
SUBLANES = 8
V7X_VMEM_LIMIT_BYTES = 56 * 1024 * 1024

GROUP_W = HEADS_PER_GROUP * HEAD_DIM
ZT_COLS = 2 * D_MODEL + 2 * CONV_CH
COL_GLU = 2 * D_MODEL
INPROJ_TN = 3 * GROUP_W
ZT_TILES = ZT_COLS // INPROJ_TN
SLABS_PER_TILE = INPROJ_TN // (2 * LANES)
ATTN_HALF = 64
ATTN_QB = 128
ATTN_WIN = ATTN_QB + 2 * ATTN_HALF
MOE_ROWS = 256
ROUTE_LANES = 128
ROUTE_FIELDS = 8
PACKED_W = D_MODEL // 2
ROW_TILE = PACKED_W // LANES

bf16 = jnp.bfloat16
f32 = jnp.float32
u32 = jnp.uint32
HI_MASK = 0xFFFF0000


def _cparams(sem):
    return pltpu.CompilerParams(dimension_semantics=sem, vmem_limit_bytes=V7X_VMEM_LIMIT_BYTES)


def _sigmoid(x):
    return 1.0 / (1.0 + jnp.exp(-x))


def _pack_pairs(v):
    n = v.shape[1] // 2
    lo = lax.bitcast_convert_type(v[:, :n].astype(bf16).astype(f32), u32)
    hi = lax.bitcast_convert_type(v[:, n:].astype(bf16).astype(f32), u32)
    return (lo >> 16) | (hi & u32(HI_MASK))


def _unpack_lo(u):
    return lax.bitcast_convert_type(u << 16, f32)


def _unpack_hi(u):
    return lax.bitcast_convert_type(u & u32(HI_MASK), f32)


def _store_position_major(ref, v):
    packed = _pack_pairs(v)
    for j in range(ROW_TILE):
        ref[pl.ds(j, v.shape[0], stride=ROW_TILE), :] = packed[:, j * LANES:(j + 1) * LANES]


def _load_position_major(ref, n):
    chunks = [ref[pl.ds(j, n, stride=ROW_TILE), :] for j in range(ROW_TILE)]
    return jnp.concatenate([_unpack_lo(c) for c in chunks] + [_unpack_hi(c) for c in chunks], axis=1)


def _ada_kernel(c_ref, w_ref, b_ref, o_ref):
    c = c_ref[...]
    a = (c * _sigmoid(c)).astype(bf16)
    o_ref[...] = jnp.dot(a, w_ref[...].astype(bf16), preferred_element_type=f32) + b_ref[...]


def _ada(c_pad, w_ada, b_ada):
    rows, d = c_pad.shape
    n = w_ada.shape[1]
    tn = 1024
    return pl.pallas_call(
        _ada_kernel,
        grid=(n // tn,),
        in_specs=[pl.BlockSpec((rows, d), lambda j: (0, 0)),
                  pl.BlockSpec((d, tn), lambda j: (0, j)),
                  pl.BlockSpec((1, tn), lambda j: (0, j))],
        out_specs=pl.BlockSpec((rows, tn), lambda j: (0, j)),
        out_shape=jax.ShapeDtypeStruct((rows, n), f32),
        compiler_params=_cparams(("arbitrary",)),
        name="ada_mod",
    )(c_pad, w_ada, b_ada)


def _inproj_kernel(x_ref, g_ref, scale_ref, shift_ref, wa_ref, wb_ref, wc_ref, ba_ref, bb_ref, bc_ref,
                   zt_ref, qkv_ref, h_scr):
    j = pl.program_id(1)

    @pl.when(j == 0)
    def _():
        x = x_ref[...]
        ms = jnp.mean(x * x, axis=-1, keepdims=True)
        h = x * lax.rsqrt(ms + EPS) * g_ref[...]
        h_scr[...] = (h * (1.0 + scale_ref[...]) + shift_ref[...]).astype(bf16)

    w_refs, b_refs = (wa_ref, wb_ref, wc_ref), (ba_ref, bb_ref, bc_ref)

    def chunk(c):
        return jnp.dot(h_scr[...], w_refs[c][...], preferred_element_type=f32) + b_refs[c][...]

    @pl.when(j < ZT_TILES)
    def _():
        for c in range(INPROJ_TN // GROUP_W):
            zt_ref[:, c * GROUP_W:(c + 1) * GROUP_W] = chunk(c).astype(bf16)

    @pl.when(j >= ZT_TILES)
    def _():
        per = GROUP_W // (2 * LANES)
        for c in range(INPROJ_TN // GROUP_W):
            z = chunk(c)
            for s in range(per):
                qkv_ref[c * per + s] = _pack_pairs(z[:, 2 * s * LANES:2 * (s + 1) * LANES])


def _inproj_col_block(j, c):
    n_blocks = (2 * CONV_CH + 3 * ATTN_WIDTH + 2 * D_MODEL) // GROUP_W
    gate0 = (2 * CONV_CH + 3 * ATTN_WIDTH) // GROUP_W
    q0 = 2 * CONV_CH // GROUP_W
    per_proj = ATTN_WIDTH // GROUP_W
    token_major = lax.rem(gate0 + (INPROJ_TN // GROUP_W) * j + c, n_blocks)
    return jnp.where(j < ZT_TILES, token_major, q0 + (j - ZT_TILES) + per_proj * c)


def _inproj(x2, g, scale, shift, w, b, *, seq, tm):
    t, d = x2.shape
    n = w.shape[1]
    tn = INPROJ_TN
    n_groups = n // tn - ZT_TILES
    bmap = lambda i, j: ((i * tm) // seq, 0, 0)
    chunks = range(tn // GROUP_W)
    w_specs = [pl.BlockSpec((d, GROUP_W), lambda i, j, c=c: (0, _inproj_col_block(j, c))) for c in chunks]
    b_specs = [pl.BlockSpec((1, GROUP_W), lambda i, j, c=c: (0, _inproj_col_block(j, c))) for c in chunks]
    return pl.pallas_call(
        _inproj_kernel,
        grid=(t // tm, n // tn),
        in_specs=[pl.BlockSpec((tm, d), lambda i, j: (i, 0)),
                  pl.BlockSpec((1, d), lambda i, j: (0, 0)),
                  pl.BlockSpec((None, 1, d), bmap),
                  pl.BlockSpec((None, 1, d), bmap)] + w_specs + b_specs,
        out_specs=[pl.BlockSpec((tm, tn), lambda i, j: (i, jnp.minimum(j, ZT_TILES - 1))),
                   pl.BlockSpec((SLABS_PER_TILE, tm, LANES),
                                lambda i, j: (jnp.maximum(j - ZT_TILES, 0), i, 0))],
        out_shape=[jax.ShapeDtypeStruct((t, ZT_COLS), bf16),
                   jax.ShapeDtypeStruct((n_groups * SLABS_PER_TILE, t, LANES), u32)],
        scratch_shapes=[pltpu.VMEM((tm, d), bf16)],
        compiler_params=_cparams(("arbitrary", "arbitrary")),
        name="inproj",
    )(x2, g, scale, shift, w, w, w, b, b, b)


CONV_HALO = 16
CONV_CHUNK = 16
CONV_SUB = CONV_CH // LANES
NORM_CHUNK = 64


def _conv_kernel(ap_ref, ac_ref, an_ref, gp_ref, gc_ref, gn_ref, w_ref, bdw_ref, lng_ref, lnb_ref,
                 o_ref, u_scr, y_scr, *, tm):
    i = pl.program_id(1)
    last = pl.num_programs(1) - 1

    def glu(a_ref, g_ref):
        return a_ref[...].astype(f32) * _sigmoid(g_ref[...].astype(f32))

    def put(row0, val):
        for j in range(CONV_SUB):
            u_scr[pl.ds(row0 * CONV_SUB + j, val.shape[0], stride=CONV_SUB), :] = (
                val[:, j * LANES:(j + 1) * LANES])

    put(0, jnp.where(i > 0, glu(ap_ref, gp_ref), 0.0))
    put(CONV_HALO, glu(ac_ref, gc_ref))
    put(CONV_HALO + tm, jnp.where(i < last, glu(an_ref, gn_ref), 0.0))

    def chunk(c, carry):
        r0 = c * CONV_CHUNK
        acc = jnp.zeros((CONV_CHUNK, CONV_SUB, LANES), f32) + bdw_ref[...][None]
        for k in range(CONV_WIDTH):
            start = pl.multiple_of((r0 + k + 1) * CONV_SUB, CONV_SUB)
            xk = u_scr[pl.ds(start, CONV_CHUNK * CONV_SUB), :].reshape(CONV_CHUNK, CONV_SUB, LANES)
            acc = acc + w_ref[k][None] * xk
        out0 = pl.multiple_of(r0 * CONV_SUB, CONV_CHUNK * CONV_SUB)
        y_scr[pl.ds(out0, CONV_CHUNK * CONV_SUB), :] = acc.reshape(CONV_CHUNK * CONV_SUB, LANES)
        return carry

    lax.fori_loop(0, tm // CONV_CHUNK, chunk, 0)

    def norm(c, carry):
        r0 = pl.multiple_of(c * NORM_CHUNK, NORM_CHUNK)
        acc = jnp.concatenate(
            [y_scr[pl.ds(r0 * CONV_SUB + j, NORM_CHUNK, stride=CONV_SUB), :] for j in range(CONV_SUB)],
            axis=1)
        mu = jnp.mean(acc, axis=-1, keepdims=True)
        cen = acc - mu
        var = jnp.mean(cen * cen, axis=-1, keepdims=True)
        y = cen * lax.rsqrt(var + EPS) * lng_ref[...] + lnb_ref[...]
        o_ref[pl.ds(r0, NORM_CHUNK), :] = (y * _sigmoid(y)).astype(bf16)
        return carry

    lax.fori_loop(0, tm // NORM_CHUNK, norm, 0, unroll=2)


def _conv(z3, w_dw, b_dw, ln_g, ln_b, *, tm):
    b, s, _ = z3.shape
    hb = tm // CONV_HALO
    nhb = s // CONV_HALO
    ca, cg = COL_GLU // CONV_CH, COL_GLU // CONV_CH + 1

    def halo(col, which):
        if which < 0:
            return pl.BlockSpec((None, CONV_HALO, CONV_CH),
                                lambda bb, i: (bb, jnp.maximum(i * hb - 1, 0), col))
        return pl.BlockSpec((None, CONV_HALO, CONV_CH),
                            lambda bb, i: (bb, jnp.minimum((i + 1) * hb, nhb - 1), col))

    cur = lambda col: pl.BlockSpec((None, tm, CONV_CH), lambda bb, i: (bb, i, col))
    vec = pl.BlockSpec((1, CONV_CH), lambda bb, i: (0, 0))
    return pl.pallas_call(
        functools.partial(_conv_kernel, tm=tm),
        grid=(b, s // tm),
        in_specs=[halo(ca, -1), cur(ca), halo(ca, 1), halo(cg, -1), cur(cg), halo(cg, 1),
                  pl.BlockSpec((CONV_WIDTH, CONV_SUB, LANES), lambda bb, i: (0, 0, 0)),
                  pl.BlockSpec((CONV_SUB, LANES), lambda bb, i: (0, 0)), vec, vec],
        out_specs=pl.BlockSpec((None, tm, CONV_CH), lambda bb, i: (bb, i, 0)),
        out_shape=jax.ShapeDtypeStruct((b, s, CONV_CH), bf16),
        scratch_shapes=[pltpu.VMEM(((tm + 2 * CONV_HALO) * CONV_SUB, LANES), f32),
                        pltpu.VMEM((tm * CONV_SUB, LANES), f32)],
        compiler_params=_cparams(("arbitrary", "arbitrary")),
        name="conv_branch",
    )(z3, z3, z3, z3, z3, z3, w_dw.reshape(CONV_WIDTH, CONV_SUB, LANES), b_dw.reshape(CONV_SUB, LANES),
      ln_g, ln_b)


def _t5_bucket_table(dil):
    delta = (np.arange(ATTN_WIN)[None, :] - ATTN_HALF) - np.arange(ATTN_QB)[:, None]
    rel = delta * dil
    nb = NUM_BUCKETS // 2
    max_exact = nb // 2
    n = np.abs(rel)
    nf = np.maximum(n, 1).astype(np.float32)
    large = max_exact + (np.log(nf / np.float32(max_exact)) / np.float32(math.log(REL_MAX_DISTANCE / max_exact))
                         * np.float32(nb - max_exact)).astype(np.int32)
    large = np.minimum(large, nb - 1)
    bucket = np.where(rel > 0, nb, 0) + np.where(n < max_exact, n, large)
    return np.where(np.abs(delta) <= ATTN_HALF, bucket, -1).astype(np.int32)


def _attn_kernel(rb_ref, bkt_ref, q_ref, kp_ref, kc_ref, kn_ref, vp_ref, vc_ref, vn_ref, o_ref, lse_ref,
                 k_scr, v_scr, bias_scr, *, gi, dil, ts, sub_len):
    i = pl.program_id(1)
    hb = ATTN_HALF * dil
    tq = ts // dil

    @pl.when((pl.program_id(0) == 0) & (i == 0))
    def _():
        bk = bkt_ref[...]
        for hh in range(HEADS_PER_GROUP):
            acc = jnp.full((ATTN_QB, ATTN_WIN), NEG_INF, f32)
            for b_id in range(NUM_BUCKETS):
                acc = jnp.where(bk == b_id, rb_ref[b_id * N_ATTN_HEADS + gi * HEADS_PER_GROUP + hh], acc)
            bias_scr[hh] = acc

    k_scr[:, 0:hb, :] = kp_ref[...]
    k_scr[:, hb:hb + ts, :] = kc_ref[...]
    k_scr[:, hb + ts:, :] = kn_ref[...]
    v_scr[:, 0:hb, :] = vp_ref[...]
    v_scr[:, hb:hb + ts, :] = vc_ref[...]
    v_scr[:, hb + ts:, :] = vn_ref[...]

    scale = HEAD_DIM ** -0.5
    col = lax.broadcasted_iota(jnp.int32, (ATTN_QB, ATTN_WIN), 1)
    lane = lax.broadcasted_iota(jnp.int32, (ATTN_QB, LANES), 1)

    def rows(start, n):
        if dil == 1:
            return pl.ds(pl.multiple_of(start, ATTN_QB), n)
        return pl.ds(start, n, stride=dil)

    def heads(words):
        out = []
        for w in words:
            out += [_unpack_lo(w).astype(bf16), _unpack_hi(w).astype(bf16)]
        return out

    def unit(u, carry):
        r = jnp.bitwise_and(u, dil - 1)
        jb = lax.shift_right_logical(u, jnp.int32(dil.bit_length() - 1))
        start = jb * (ATTN_QB * dil) + r
        q = heads([q_ref[sl, rows(start, ATTN_QB), :] for sl in range(2)])
        kw = heads([k_scr[sl, rows(start, ATTN_WIN), :] for sl in range(2)])
        vw = heads([v_scr[sl, rows(start, ATTN_WIN), :] for sl in range(2)])
        key0 = i * tq + jb * ATTN_QB - ATTN_HALF
        valid = (col + key0 >= 0) & (col + key0 < sub_len)
        s = [lax.dot_general(q[hh], kw[hh], (((1,), (1,)), ((), ())), preferred_element_type=f32)
             for hh in range(HEADS_PER_GROUP)]
        s = [jnp.where(valid, s[hh] * scale + bias_scr[hh], NEG_INF) for hh in range(HEADS_PER_GROUP)]
        m = [jnp.max(x, axis=-1, keepdims=True) for x in s]
        p = [jnp.exp(s[hh] - m[hh]) for hh in range(HEADS_PER_GROUP)]
        den = [jnp.sum(x, axis=-1, keepdims=True) for x in p]
        pv = [jnp.dot(p[hh].astype(bf16), vw[hh], preferred_element_type=f32)
              for hh in range(HEADS_PER_GROUP)]
        lse_tile = jnp.zeros((ATTN_QB, LANES), f32)
        for hh in range(HEADS_PER_GROUP):
            o_ref[hh, rows(start, ATTN_QB), :] = pv[hh] / den[hh]
            lse_tile = jnp.where(lane == hh, m[hh] + jnp.log(den[hh]), lse_tile)
        lse_ref[rows(start, ATTN_QB), :] = lse_tile
        return carry

    lax.fori_loop(0, ts // ATTN_QB, unit, 0, unroll=2)


def _attn_group(qkv, rel_bias_flat, gi, dil, *, batch, seq):
    t = batch * seq
    ts = max(1024, ATTN_QB * dil)
    hb = ATTN_HALF * dil
    steps = seq // ts
    hpt = ts // hb
    hps = seq // hb
    slab = lambda which: gi * 3 + which

    def cur(which):
        return pl.BlockSpec((2, ts, LANES), lambda b, i: (slab(which), b * steps + i, 0))

    def prev(which):
        return pl.BlockSpec((2, hb, LANES),
                            lambda b, i: (slab(which), b * hps + jnp.maximum(i * hpt - 1, 0), 0))

    def nxt(which):
        return pl.BlockSpec((2, hb, LANES),
                            lambda b, i: (slab(which), b * hps + jnp.minimum((i + 1) * hpt, hps - 1), 0))

    bkt = jnp.asarray(_t5_bucket_table(dil))
    return pl.pallas_call(
        functools.partial(_attn_kernel, gi=gi, dil=dil, ts=ts, sub_len=seq // dil),
        grid=(batch, steps),
        in_specs=[pl.BlockSpec(memory_space=pltpu.SMEM),
                  pl.BlockSpec((ATTN_QB, ATTN_WIN), lambda b, i: (0, 0)),
                  cur(0), prev(1), cur(1), nxt(1), prev(2), cur(2), nxt(2)],
        out_specs=[pl.BlockSpec((HEADS_PER_GROUP, ts, LANES), lambda b, i: (0, b * steps + i, 0)),
                   pl.BlockSpec((ts, LANES), lambda b, i: (b * steps + i, 0))],
        out_shape=[jax.ShapeDtypeStruct((HEADS_PER_GROUP, t, LANES), f32),
                   jax.ShapeDtypeStruct((t, LANES), f32)],
        scratch_shapes=[pltpu.VMEM((2, ts + 2 * hb, LANES), u32),
                        pltpu.VMEM((2, ts + 2 * hb, LANES), u32),
                        pltpu.VMEM((HEADS_PER_GROUP, ATTN_QB, ATTN_WIN), f32)],
        compiler_params=_cparams(("arbitrary", "arbitrary")),
        name=f"attn_g{gi}",
    )(rel_bias_flat, bkt, qkv, qkv, qkv, qkv, qkv, qkv, qkv)


def _merge_kernel(x_ref, ca_ref, o0_ref, o1_ref, o2_ref, l0_ref, l1_ref, l2_ref, ga_ref, gb_ref,
                  wco_ref, bco_ref, wao_ref, wo_ref, gpost_ref, gate1_ref, gpre_ref, scale2_ref,
                  shift2_ref, wr_ref, br_ref, x1_ref, h2p_ref, logit_ref):
    branch_a = jnp.dot(ca_ref[...], wco_ref[...], preferred_element_type=f32) + bco_ref[...]

    l0, l1, l2 = l0_ref[...], l1_ref[...], l2_ref[...]
    lmax = jnp.maximum(jnp.maximum(l0, l1), l2)
    e0, e1, e2 = jnp.exp(l0 - lmax), jnp.exp(l1 - lmax), jnp.exp(l2 - lmax)
    inv = 1.0 / (e0 + e1 + e2)
    parts = []
    for hh in range(HEADS_PER_GROUP):
        ls = slice(hh, hh + 1)
        parts.append((e0[:, ls] * inv[:, ls]) * o0_ref[hh] + (e1[:, ls] * inv[:, ls]) * o1_ref[hh]
                     + (e2[:, ls] * inv[:, ls]) * o2_ref[hh])
    att = jnp.concatenate(parts, axis=1).astype(bf16)
    branch_b = jnp.dot(att, wao_ref[...], preferred_element_type=f32)

    mix = _sigmoid(ga_ref[...].astype(f32)) * branch_a + _sigmoid(gb_ref[...].astype(f32)) * branch_b
    y = jnp.dot(mix.astype(bf16), wo_ref[...], preferred_element_type=f32)

    yn = y * lax.rsqrt(jnp.mean(y * y, axis=-1, keepdims=True) + EPS) * gpost_ref[...]
    x1 = x_ref[...] + gate1_ref[...] * yn
    x1_ref[...] = x1
    hn = x1 * lax.rsqrt(jnp.mean(x1 * x1, axis=-1, keepdims=True) + EPS) * gpre_ref[...]
    h2 = hn * (1.0 + scale2_ref[...]) + shift2_ref[...]
    _store_position_major(h2p_ref, h2)
    logit_ref[...] = jnp.dot(h2.astype(bf16), wr_ref[...], preferred_element_type=f32) + br_ref[...]


def _merge(x2, conv_act, os_, lses, zt, w_co, b_co, w_ao, w_o, g_post, gate1, g_pre, scale2, shift2,
           w_r, b_r, *, seq, tm):
    t, d = x2.shape
    rows = lambda w: pl.BlockSpec((tm, w), lambda i: (i, 0))
    heads = pl.BlockSpec((HEADS_PER_GROUP, tm, LANES), lambda i: (0, i, 0))
    full = lambda a: pl.BlockSpec(a.shape, lambda i: (0,) * a.ndim)
    perb = pl.BlockSpec((None, 1, d), lambda i: ((i * tm) // seq, 0, 0))
    return pl.pallas_call(
        _merge_kernel,
        grid=(t // tm,),
        in_specs=[rows(d), rows(CONV_CH), heads, heads, heads, rows(LANES), rows(LANES), rows(LANES),
                  pl.BlockSpec((tm, d), lambda i: (i, 0)), pl.BlockSpec((tm, d), lambda i: (i, 1)),
                  full(w_co), full(b_co), full(w_ao), full(w_o), full(g_post), perb, full(g_pre),
                  perb, perb, full(w_r), full(b_r)],
        out_specs=[rows(d), pl.BlockSpec((tm * ROW_TILE, LANES), lambda i: (i, 0)), rows(ROUTE_LANES)],
        out_shape=[jax.ShapeDtypeStruct((t, d), f32),
                   jax.ShapeDtypeStruct((t * ROW_TILE, LANES), u32),
                   jax.ShapeDtypeStruct((t, ROUTE_LANES), f32)],
        compiler_params=_cparams(("arbitrary",)),
        name="merge",
    )(x2, conv_act, *os_, *lses, zt, zt, w_co, b_co, w_ao, w_o, g_post, gate1, g_pre, scale2, shift2,
      w_r, b_r)


def _route_kernel(logit_ref, rec_ref, rect_ref, cnt_ref, carry):
    i = pl.program_id(0)

    @pl.when(i == 0)
    def _():
        carry[...] = jnp.zeros_like(carry)

    lg = logit_ref[...]
    tm = lg.shape[0]
    lane = lax.broadcasted_iota(jnp.int32, lg.shape, 1).astype(f32)
    big = float(2 * ROUTE_LANES)

    def first_max(mask):
        v = jnp.max(jnp.where(mask, lg, -jnp.inf), axis=-1, keepdims=True)
        idx = jnp.min(jnp.where(mask & (lg == v), lane, big), axis=-1, keepdims=True)
        return v, idx

    gmask = lane < N_GROUPS
    gmax, gsel = first_max(gmask)
    p_g = 1.0 / jnp.sum(jnp.where(gmask, jnp.exp(lg - gmax), 0.0), axis=-1, keepdims=True)
    e_lo = N_GROUPS + EXPERTS_PER_GROUP * gsel
    emask = (lane >= e_lo) & (lane < e_lo + EXPERTS_PER_GROUP)
    v0, i0 = first_max(emask)
    v1, i1 = first_max(emask & (lane != i0))
    t1 = jnp.exp(v1 - v0)
    w0 = p_g / (1.0 + t1)
    w1 = p_g * t1 / (1.0 + t1)
    e0 = i0 - N_GROUPS
    e1 = i1 - N_GROUPS

    hit0 = lane == e0
    hit1 = lane == e1
    onehot = jnp.where(hit0 | hit1, 1.0, 0.0)
    r_i = lax.broadcasted_iota(jnp.int32, (tm, tm), 0)
    c_i = lax.broadcasted_iota(jnp.int32, (tm, tm), 1)
    tril = jnp.where(c_i < r_i, 1.0, 0.0).astype(bf16)
    before = jnp.dot(tril, onehot.astype(bf16), preferred_element_type=f32) + carry[...]
    rank0 = jnp.sum(jnp.where(hit0, before, 0.0), axis=-1, keepdims=True)
    rank1 = jnp.sum(jnp.where(hit1, before, 0.0), axis=-1, keepdims=True)
    carry[...] = carry[...] + jnp.sum(onehot, axis=0, keepdims=True)
    cnt_ref[...] = carry[...]

    rec = jnp.zeros_like(lg)
    for slot, val in enumerate((e0, e1, rank0, rank1, w0, w1)):
        rec = jnp.where(lane == slot, val, rec)
    rec_ref[...] = rec
    rect_ref[...] = jnp.transpose(rec)[0:ROUTE_FIELDS, :]


def _route(logits, *, tm):
    t = logits.shape[0]
    return pl.pallas_call(
        _route_kernel,
        grid=(t // tm,),
        in_specs=[pl.BlockSpec((tm, ROUTE_LANES), lambda i: (i, 0))],
        out_specs=[pl.BlockSpec((tm, ROUTE_LANES), lambda i: (i, 0)),
                   pl.BlockSpec((ROUTE_FIELDS, tm), lambda i: (0, i)),
                   pl.BlockSpec((1, ROUTE_LANES), lambda i: (0, 0))],
        out_shape=[jax.ShapeDtypeStruct((t, ROUTE_LANES), f32),
                   jax.ShapeDtypeStruct((ROUTE_FIELDS, t), f32),
                   jax.ShapeDtypeStruct((1, ROUTE_LANES), f32)],
        scratch_shapes=[pltpu.VMEM((1, ROUTE_LANES), f32)],
        compiler_params=_cparams(("arbitrary",)),
        name="route",
    )(logits)


DMA_UNROLL = 8


def _dispatch_kernel(zero_blk_ref, dest_ref, h_ref, xs_ref, zero_scr, sem, zsem, *, tm):
    def row_copy(r, d):
        return pltpu.make_async_copy(h_ref.at[pl.ds(pl.multiple_of(r * ROW_TILE, ROW_TILE), ROW_TILE), :],
                                     xs_ref.at[pl.ds(pl.multiple_of(d * ROW_TILE, ROW_TILE), ROW_TILE), :],
                                     sem)

    def zero_copy(blk):
        n = MOE_ROWS * ROW_TILE
        return pltpu.make_async_copy(zero_scr, xs_ref.at[pl.ds(pl.multiple_of(blk * n, n), n), :], zsem)

    @pl.when(pl.program_id(0) == 0)
    def _():
        zero_scr[...] = jnp.zeros_like(zero_scr)
        n_cand = zero_blk_ref.shape[0]

        def start(j, c):
            @pl.when(zero_blk_ref[j] >= 0)
            def _():
                zero_copy(zero_blk_ref[j]).start()
            return c

        def wait(j, c):
            @pl.when(zero_blk_ref[j] >= 0)
            def _():
                zero_copy(0).wait()
            return c

        lax.fori_loop(0, n_cand, start, 0)
        lax.fori_loop(0, n_cand, wait, 0)

    def issue(r, c):
        row_copy(r, dest_ref[TOP_K * r]).start(priority=0)
        row_copy(r, dest_ref[TOP_K * r + 1]).start(priority=1)
        return c

    lax.fori_loop(0, tm, issue, 0, unroll=DMA_UNROLL)
    for _ in range(TOP_K):
        pltpu.make_async_copy(h_ref, xs_ref.at[pl.ds(0, tm * ROW_TILE), :], sem).wait()


def _dispatch(zero_blk, dest, h2p, *, p_rows, tm):
    grid_spec = pltpu.PrefetchScalarGridSpec(
        num_scalar_prefetch=1,
        grid=(h2p.shape[0] // (tm * ROW_TILE),),
        in_specs=[pl.BlockSpec((TOP_K * tm,), lambda i, *_: (i,), memory_space=pltpu.SMEM),
                  pl.BlockSpec((tm * ROW_TILE, LANES), lambda i, *_: (i, 0))],
        out_specs=pl.BlockSpec(memory_space=pl.ANY),
        scratch_shapes=[pltpu.VMEM((MOE_ROWS * ROW_TILE, LANES), u32), pltpu.SemaphoreType.DMA(()),
                        pltpu.SemaphoreType.DMA(())],
    )
    return pl.pallas_call(
        functools.partial(_dispatch_kernel, tm=tm),
        grid_spec=grid_spec,
        out_shape=jax.ShapeDtypeStruct((p_rows * ROW_TILE, LANES), u32),
        compiler_params=_cparams(("arbitrary",)),
        name="dispatch",
    )(zero_blk, dest, h2p)


def _expert_kernel(bstart_ref, nblk_ref, tail_ref, xs_ref, w1_ref, w3_ref, w2_ref, ys_ref,
                   wf1, wf3, wf2, w1b, w3b, w2b, xbuf, ybuf, sem_in, sem_out, sem_w):
    e = pl.program_id(0)
    last = pl.num_programs(0) - 1
    ws = jnp.bitwise_and(e, 1)
    n = nblk_ref[e]
    g0 = bstart_ref[e]
    n_used = tail_ref[0]
    blk = MOE_ROWS * ROW_TILE

    def hbm_rows(ref, g):
        return ref.at[pl.ds(pl.multiple_of(g * blk, blk), blk), :]

    def in_copy(g, slot):
        return pltpu.make_async_copy(hbm_rows(xs_ref, g), xbuf.at[slot], sem_in.at[slot])

    def out_copy(g, slot):
        return pltpu.make_async_copy(ybuf.at[slot], hbm_rows(ys_ref, g), sem_out.at[slot])

    def weight_copies(ex, slot):
        return [pltpu.make_async_copy(w1_ref.at[ex], wf1.at[slot], sem_w.at[slot]),
                pltpu.make_async_copy(w3_ref.at[ex], wf3.at[slot], sem_w.at[slot]),
                pltpu.make_async_copy(w2_ref.at[ex], wf2.at[slot], sem_w.at[slot])]

    @pl.when(e == 0)
    def _():
        for cp in weight_copies(0, 0):
            cp.start(priority=1)

        @pl.when(n_used > 0)
        def _():
            in_copy(0, 0).start()

    @pl.when(e < last)
    def _():
        for cp in weight_copies(e + 1, 1 - ws):
            cp.start(priority=1)

    for cp in weight_copies(e, ws):
        cp.wait()
    w1b[...] = wf1[ws].astype(bf16)
    w3b[...] = wf3[ws].astype(bf16)
    w2b[...] = wf2[ws].astype(bf16)

    def body(b, carry):
        g = g0 + b
        slot = jnp.bitwise_and(g, 1)
        in_copy(g, slot).wait()

        @pl.when(g + 1 < n_used)
        def _():
            in_copy(g + 1, 1 - slot).start()

        @pl.when(g >= 2)
        def _():
            out_copy(g - 2, slot).wait()

        x = _load_position_major(xbuf.at[slot], MOE_ROWS).astype(bf16)
        h1 = jnp.dot(x, w1b[...], preferred_element_type=f32)
        h3 = jnp.dot(x, w3b[...], preferred_element_type=f32)
        a = (h1 * _sigmoid(h1) * h3).astype(bf16)
        _store_position_major(ybuf.at[slot], jnp.dot(a, w2b[...], preferred_element_type=f32))
        out_copy(g, slot).start()
        return carry

    lax.fori_loop(0, n, body, 0)

    @pl.when(e == last)
    def _():
        @pl.when(n_used >= 2)
        def _():
            out_copy(n_used - 2, jnp.bitwise_and(n_used, 1)).wait()

        @pl.when(n_used >= 1)
        def _():
            out_copy(n_used - 1, jnp.bitwise_and(n_used - 1, 1)).wait()

        ybuf[0] = jnp.zeros(ybuf.shape[1:], u32)
        count = tail_ref[1]

        def zero_copy(b):
            return pltpu.make_async_copy(ybuf.at[0], hbm_rows(ys_ref, n_used + b), sem_out.at[0])

        lax.fori_loop(0, count, lambda b, c: (zero_copy(b).start(), c)[1], 0)
        lax.fori_loop(0, count, lambda b, c: (zero_copy(0).wait(), c)[1], 0)


def _experts(bstart, nblk, tail, xs, w1, w3, w2):
    blk = MOE_ROWS * ROW_TILE
    n_exp, d, ff = w1.shape
    grid_spec = pltpu.PrefetchScalarGridSpec(
        num_scalar_prefetch=3,
        grid=(n_exp,),
        in_specs=[pl.BlockSpec(memory_space=pl.ANY)] * 4,
        out_specs=pl.BlockSpec(memory_space=pl.ANY),
        scratch_shapes=[pltpu.VMEM((2, d, ff), f32), pltpu.VMEM((2, d, ff), f32), pltpu.VMEM((2, ff, d), f32),
                        pltpu.VMEM((d, ff), bf16), pltpu.VMEM((d, ff), bf16), pltpu.VMEM((ff, d), bf16),
                        pltpu.VMEM((2, blk, LANES), u32), pltpu.VMEM((2, blk, LANES), u32),
                        pltpu.SemaphoreType.DMA((2,)), pltpu.SemaphoreType.DMA((2,)),
                        pltpu.SemaphoreType.DMA((2,))],
    )
    return pl.pallas_call(
        _expert_kernel,
        grid_spec=grid_spec,
        out_shape=jax.ShapeDtypeStruct(xs.shape, u32),
        compiler_params=_cparams(("arbitrary",)),
        name="experts",
    )(bstart, nblk, tail, xs, w1, w3, w2)


def _combine_kernel(dest_ref, dest_next_ref, ys_ref, rec_ref, x1_ref, gpost_ref, gate2_ref, o_ref, buf, sem,
                    *, tm):
    i = pl.program_id(0)
    slot = jnp.bitwise_and(i, 1)

    def gather(d_ref, s):
        def row_copy(d, k, r):
            return pltpu.make_async_copy(
                ys_ref.at[pl.ds(pl.multiple_of(d * ROW_TILE, ROW_TILE), ROW_TILE), :],
                buf.at[s, k, pl.ds(pl.multiple_of(r * ROW_TILE, ROW_TILE), ROW_TILE), :], sem.at[s])

        def issue(r, c):
            row_copy(d_ref[TOP_K * r], 0, r).start(priority=0)
            row_copy(d_ref[TOP_K * r + 1], 1, r).start(priority=1)
            return c

        lax.fori_loop(0, tm, issue, 0, unroll=DMA_UNROLL)

    @pl.when(i == 0)
    def _():
        gather(dest_ref, 0)

    @pl.when(i + 1 < pl.num_programs(0))
    def _():
        gather(dest_next_ref, 1 - slot)

    for k in range(TOP_K):
        pltpu.make_async_copy(ys_ref.at[pl.ds(0, tm * ROW_TILE), :], buf.at[slot, k], sem.at[slot]).wait()

    rec = rec_ref[...]
    y = (rec[:, 4:5] * _load_position_major(buf.at[slot, 0], tm)
         + rec[:, 5:6] * _load_position_major(buf.at[slot, 1], tm))
    yn = y * lax.rsqrt(jnp.mean(y * y, axis=-1, keepdims=True) + EPS) * gpost_ref[...]
    o_ref[...] = x1_ref[...] + gate2_ref[...] * yn


def _combine(dest, ys, rec, x1, g_post, gate2, *, seq, tm):
    t, d = x1.shape
    last = t // tm - 1
    return pl.pallas_call(
        functools.partial(_combine_kernel, tm=tm),
        grid=(t // tm,),
        in_specs=[pl.BlockSpec((TOP_K * tm,), lambda i: (i,), memory_space=pltpu.SMEM),
                  pl.BlockSpec((TOP_K * tm,), lambda i: (jnp.minimum(i + 1, last),), memory_space=pltpu.SMEM),
                  pl.BlockSpec(memory_space=pl.ANY),
                  pl.BlockSpec((tm, ROUTE_LANES), lambda i: (i, 0)),
                  pl.BlockSpec((tm, d), lambda i: (i, 0)),
                  pl.BlockSpec((1, d), lambda i: (0, 0)),
                  pl.BlockSpec((None, 1, d), lambda i: ((i * tm) // seq, 0, 0))],
        out_specs=pl.BlockSpec((tm, d), lambda i: (i, 0)),
        out_shape=jax.ShapeDtypeStruct((t, d), f32),
        scratch_shapes=[pltpu.VMEM((2, TOP_K, tm * ROW_TILE, LANES), u32), pltpu.SemaphoreType.DMA((2,))],
        compiler_params=_cparams(("arbitrary",)),
        name="combine",
    )(dest, dest, ys, rec, x1, g_post, gate2)


def _layer(x, c, w_ada, b_ada, g_pre_mix, g_post_mix, w_in, b_in, w_dw, b_dw, ln_conv_g, ln_conv_b,
           w_conv_out, b_conv_out, rel_bias, w_attn_out, w_out, g_pre_ffn, g_post_ffn,
           w_router_group, b_router_group, w_router_expert, b_router_expert, w1, w3, w2):
    batch, seq, d = x.shape
    t = batch * seq
    row = lambda v: v.reshape(1, -1)

    c_pad = jnp.zeros((SUBLANES, d), f32).at[:batch].set(c)
    mod = _ada(c_pad, w_ada, row(b_ada))[:batch]
    shift1, scale1, gate1, shift2, scale2, gate2 = [m.reshape(batch, 1, d) for m in jnp.split(mod, 6, axis=-1)]

    x2 = x.reshape(t, d)
    zt, qkv = _inproj(x2, row(g_pre_mix), scale1, shift1, w_in.astype(bf16), row(b_in), seq=seq, tm=1024)

    conv_act = _conv(zt.reshape(batch, seq, ZT_COLS), w_dw, row(b_dw), row(ln_conv_g), row(ln_conv_b),
                     tm=512).reshape(t, CONV_CH)

    os_, lses = [], []
    for gi, (_, dil) in enumerate(DILATED_GROUPS):
        o, lse = _attn_group(qkv, rel_bias.reshape(-1), gi, dil, batch=batch, seq=seq)
        os_.append(o)
        lses.append(lse)

    pad = ROUTE_LANES - N_GROUPS - N_EXPERTS
    w_r = jnp.concatenate([w_router_group, w_router_expert, jnp.zeros((d, pad), f32)], axis=1).astype(bf16)
    b_r = row(jnp.concatenate([b_router_group, b_router_expert.reshape(-1), jnp.zeros((pad,), f32)]))
    x1, h2p, logits = _merge(x2, conv_act, os_, lses, zt, w_conv_out.astype(bf16), row(b_conv_out),
                             w_attn_out.astype(bf16), w_out.astype(bf16), row(g_post_mix), gate1,
                             row(g_pre_ffn), scale2, shift2, w_r, b_r, seq=seq, tm=256)

    rec, rec_t, cnt = _route(logits, tm=512)

    counts = cnt[0, :N_EXPERTS].astype(jnp.int32)
    pcounts = (counts + MOE_ROWS - 1) // MOE_ROWS * MOE_ROWS
    pends = jnp.cumsum(pcounts)
    pstarts = pends - pcounts
    n_blocks = (t * TOP_K + N_EXPERTS * (MOE_ROWS - 1) + MOE_ROWS - 1) // MOE_ROWS
    n_used = (pends[-1:] // MOE_ROWS).astype(jnp.int32)
    tail_blocks = jnp.concatenate([n_used, n_blocks - n_used])
    eid = rec_t[0:TOP_K].astype(jnp.int32)
    rank = rec_t[TOP_K:2 * TOP_K].astype(jnp.int32)
    start_of = jnp.sum(jnp.where(eid[..., None] == jnp.arange(N_EXPERTS), pstarts, 0), axis=-1)
    dest = jnp.transpose(start_of + rank).reshape(t * TOP_K)
    tail = n_blocks - N_EXPERTS + jnp.arange(N_EXPERTS, dtype=jnp.int32)
    zero_blk = jnp.concatenate([jnp.where(pcounts > counts, pends // MOE_ROWS - 1, -1),
                                jnp.where(tail >= n_used[0], tail, -1)]).astype(jnp.int32)

    xs = _dispatch(zero_blk, dest, h2p, p_rows=n_blocks * MOE_ROWS, tm=512)
    ys = _experts((pstarts // MOE_ROWS).astype(jnp.int32), (pcounts // MOE_ROWS).astype(jnp.int32),
                  tail_blocks, xs, w1, w3, w2)
    out = _combine(dest, ys, rec, x1, row(g_post_ffn), gate2, seq=seq, tm=512)
    return out.reshape(batch, seq, d)


def kernel(x, c, w_ada, b_ada, g_pre_mix, g_post_mix, w_in, b_in, w_dw, b_dw, ln_conv_g, ln_conv_b,
           w_conv_out, b_conv_out, rel_bias, w_attn_out, w_out, g_pre_ffn, g_post_ffn,
           w_router_group, b_router_group, w_router_expert, b_router_expert, w1, w3, w2):
    depth = w_ada.shape[0]
    for l in range(depth):
        pick = (lambda a: a.reshape(a.shape[1:])) if depth == 1 else (lambda a, l=l: a[l])
        x = _layer(x, c, pick(w_ada), pick(b_ada), pick(g_pre_mix), pick(g_post_mix), pick(w_in),
                   pick(b_in), pick(w_dw), pick(b_dw), pick(ln_conv_g), pick(ln_conv_b),
                   pick(w_conv_out), pick(b_conv_out), rel_bias, pick(w_attn_out), pick(w_out),
                   pick(g_pre_ffn), pick(g_post_ffn), pick(w_router_group), pick(b_router_group),
                   pick(w_router_expert), pick(b_router_expert), pick(w1), pick(w3), pick(w2))
    return x
```

```python
import functools
import math

import numpy as np
import jax
import jax.numpy as jnp
from jax import lax
from jax.experimental import pallas as pl
from jax.experimental.pallas import tpu as pltpu

D_MODEL = 2048
CONV_CH = 1024
CONV_WIDTH = 31
N_ATTN_HEADS = 12
HEADS_PER_GROUP = 4
HEAD_DIM = 128
ATTN_WIDTH = N_ATTN_HEADS * HEAD_DIM
DILATED_GROUPS = ((128, 1), (512, 4), (2048, 16))
NUM_BUCKETS = 32
REL_MAX_DISTANCE = 1024
N_GROUPS = 8
EXPERTS_PER_GROUP = 8
N_EXPERTS = N_GROUPS * EXPERTS_PER_GROUP
TOP_K = 2
EXPERT_FF = 512
EPS = 1e-6
NEG_INF = -1e30

LANES = 128
SUBLANES = 8
V7X_VMEM_LIMIT_BYTES = 56 * 1024 * 1024

GROUP_W = HEADS_PER_GROUP * HEAD_DIM
ZT_COLS = 2 * D_MODEL + 2 * CONV_CH
COL_GLU = 2 * D_MODEL
INPROJ_TN = 3 * GROUP_W
ZT_TILES = ZT_COLS // INPROJ_TN
INPROJ_TM = 1024
ATTN_HALF = 64
ATTN_QB = 128
ATTN_WIN = ATTN_QB + 2 * ATTN_HALF
MOE_ROWS = 256
MERGE_TM = 256
MERGE_SUB = 2
ROUTE_LANES = 128
ROUTE_FIELDS = 8
PACKED_W = D_MODEL // 2
ROW_TILE = PACKED_W // LANES

bf16 = jnp.bfloat16
f32 = jnp.float32
u32 = jnp.uint32
HI_MASK = 0xFFFF0000


def _cparams(sem):
    return pltpu.CompilerParams(dimension_semantics=sem, vmem_limit_bytes=V7X_VMEM_LIMIT_BYTES)


def _sigmoid(x):
    return 1.0 / (1.0 + jnp.exp(-x))


def _pack_pairs(v):
    n = v.shape[1] // 2
    lo = lax.bitcast_convert_type(v[:, :n].astype(bf16).astype(f32), u32)
    hi = lax.bitcast_convert_type(v[:, n:].astype(bf16).astype(f32), u32)
    return (lo >> 16) | (hi & u32(HI_MASK))


def _unpack_lo(u):
    return lax.bitcast_convert_type(u << 16, f32)


def _unpack_hi(u):
    return lax.bitcast_convert_type(u & u32(HI_MASK), f32)


def _store_position_major(ref, v):
    packed = _pack_pairs(v)
    for j in range(ROW_TILE):
        ref[pl.ds(j, v.shape[0], stride=ROW_TILE), :] = packed[:, j * LANES:(j + 1) * LANES]


def _load_position_major(ref, n):
    chunks = [ref[pl.ds(j, n, stride=ROW_TILE), :] for j in range(ROW_TILE)]
    return jnp.concatenate([_unpack_lo(c) for c in chunks] + [_unpack_hi(c) for c in chunks], axis=1)


def _ada_kernel(c_ref, w_ref, b_ref, o_ref):
    c = c_ref[...]
    a = (c * _sigmoid(c)).astype(bf16)
    o_ref[...] = jnp.dot(a, w_ref[...].astype(bf16), preferred_element_type=f32) + b_ref[...]


def _ada(c_pad, w_ada, b_ada):
    rows, d = c_pad.shape
    n = w_ada.shape[1]
    tn = 1024
    return pl.pallas_call(
        _ada_kernel,
        grid=(n // tn,),
        in_specs=[pl.BlockSpec((rows, d), lambda j: (0, 0)),
                  pl.BlockSpec((d, tn), lambda j: (0, j)),
                  pl.BlockSpec((1, tn), lambda j: (0, j))],
        out_specs=pl.BlockSpec((rows, tn), lambda j: (0, j)),
        out_shape=jax.ShapeDtypeStruct((rows, n), f32),
        compiler_params=_cparams(("arbitrary",)),
        name="ada_mod",
    )(c_pad, w_ada, b_ada)


def _inproj_kernel(x_ref, g_ref, scale_ref, shift_ref, wa_ref, wb_ref, wc_ref, ba_ref, bb_ref, bc_ref,
                   zt_ref, qkv_ref, h_scr, slab_scr):
    j = pl.program_id(1)
    tm = x_ref.shape[0]

    @pl.when(j == 0)
    def _():
        x = x_ref[...]
        ms = jnp.mean(x * x, axis=-1, keepdims=True)
        h = x * lax.rsqrt(ms + EPS) * g_ref[...]
        h_scr[...] = (h * (1.0 + scale_ref[...]) + shift_ref[...]).astype(bf16)

    w_refs, b_refs = (wa_ref, wb_ref, wc_ref), (ba_ref, bb_ref, bc_ref)

    def chunk(c):
        return jnp.dot(h_scr[...], w_refs[c][...], preferred_element_type=f32) + b_refs[c][...]

    @pl.when(j < ZT_TILES)
    def _():
        for c in range(INPROJ_TN // GROUP_W):
            zt_ref[:, c * GROUP_W:(c + 1) * GROUP_W] = chunk(c).astype(bf16)

    for gi, (_, dil) in enumerate(DILATED_GROUPS):
        @pl.when(j == ZT_TILES + gi)
        def _(dil=dil):
            per = tm // dil
            for c in range(INPROJ_TN // GROUP_W):
                z = chunk(c)
                if dil == 1:
                    qkv_ref[:, c * GROUP_W:(c + 1) * GROUP_W] = z.astype(bf16)
                    continue
                slabs = range(GROUP_W // LANES)
                for s in slabs:
                    slab_scr[0, s] = z[:, s * LANES:(s + 1) * LANES]
                src, stride, first = 0, dil, lambda r: r
                if dil == 16:
                    quarter = tm // 4
                    for s in slabs:
                        for r4 in range(4):
                            slab_scr[1, s, r4 * quarter:(r4 + 1) * quarter, :] = (
                                slab_scr[0, s, pl.ds(r4, quarter, stride=4), :])
                    src, stride, first = 1, 4, lambda r: (r % 4) * quarter + r // 4
                for r in range(dil):
                    for s in slabs:
                        col = c * GROUP_W + s * LANES
                        qkv_ref[r * per:(r + 1) * per, col:col + LANES] = (
                            slab_scr[src, s, pl.ds(first(r), per, stride=stride), :].astype(bf16))


def _inproj_col_block(j, c):
    n_blocks = (2 * CONV_CH + 3 * ATTN_WIDTH + 2 * D_MODEL) // GROUP_W
    gate0 = (2 * CONV_CH + 3 * ATTN_WIDTH) // GROUP_W
    q0 = 2 * CONV_CH // GROUP_W
    per_proj = ATTN_WIDTH // GROUP_W
    token_major = lax.rem(gate0 + (INPROJ_TN // GROUP_W) * j + c, n_blocks)
    return jnp.where(j < ZT_TILES, token_major, q0 + (j - ZT_TILES) + per_proj * c)


def _inproj(x2, g, scale, shift, w, b, *, seq, tm):
    t, d = x2.shape
    n = w.shape[1]
    tn = INPROJ_TN
    n_groups = n // tn - ZT_TILES
    bmap = lambda i, j: ((i * tm) // seq, 0, 0)
    chunks = range(tn // GROUP_W)
    w_specs = [pl.BlockSpec((d, GROUP_W), lambda i, j, c=c: (0, _inproj_col_block(j, c))) for c in chunks]
    b_specs = [pl.BlockSpec((1, GROUP_W), lambda i, j, c=c: (0, _inproj_col_block(j, c))) for c in chunks]
    return pl.pallas_call(
        _inproj_kernel,
        grid=(t // tm, n // tn),
        in_specs=[pl.BlockSpec((tm, d), lambda i, j: (i, 0)),
                  pl.BlockSpec((1, d), lambda i, j: (0, 0)),
                  pl.BlockSpec((None, 1, d), bmap),
                  pl.BlockSpec((None, 1, d), bmap)] + w_specs + b_specs,
        out_specs=[pl.BlockSpec((tm, tn), lambda i, j: (i, jnp.minimum(j, ZT_TILES - 1))),
                   pl.BlockSpec((None, tm, tn), lambda i, j: (jnp.maximum(j - ZT_TILES, 0), i, 0))],
        out_shape=[jax.ShapeDtypeStruct((t, ZT_COLS), bf16),
                   jax.ShapeDtypeStruct((n_groups, t, tn), bf16)],
        scratch_shapes=[pltpu.VMEM((tm, d), bf16), pltpu.VMEM((2, GROUP_W // LANES, tm, LANES), f32)],
        compiler_params=_cparams(("arbitrary", "arbitrary")),
        name="inproj",
    )(x2, g, scale, shift, w, w, w, b, b, b)


CONV_HALO = 16
CONV_CHUNK = 16
CONV_SUB = CONV_CH // LANES
NORM_CHUNK = 64


def _conv_kernel(ap_ref, ac_ref, an_ref, gp_ref, gc_ref, gn_ref, w_ref, bdw_ref, lng_ref, lnb_ref,
                 o_ref, u_scr, y_scr, *, tm):
    i = pl.program_id(1)
    last = pl.num_programs(1) - 1

    def glu(a_ref, g_ref):
        return a_ref[...].astype(f32) * _sigmoid(g_ref[...].astype(f32))

    def put(row0, val):
        for j in range(CONV_SUB):
            u_scr[pl.ds(row0 * CONV_SUB + j, val.shape[0], stride=CONV_SUB), :] = (
                val[:, j * LANES:(j + 1) * LANES])

    put(0, jnp.where(i > 0, glu(ap_ref, gp_ref), 0.0))
    put(CONV_HALO, glu(ac_ref, gc_ref))
    put(CONV_HALO + tm, jnp.where(i < last, glu(an_ref, gn_ref), 0.0))

    def chunk(c, carry):
        r0 = c * CONV_CHUNK
        acc = jnp.zeros((CONV_CHUNK, CONV_SUB, LANES), f32) + bdw_ref[...][None]
        for k in range(CONV_WIDTH):
            start = pl.multiple_of((r0 + k + 1) * CONV_SUB, CONV_SUB)
            xk = u_scr[pl.ds(start, CONV_CHUNK * CONV_SUB), :].reshape(CONV_CHUNK, CONV_SUB, LANES)
            acc = acc + w_ref[k][None] * xk
        out0 = pl.multiple_of(r0 * CONV_SUB, CONV_CHUNK * CONV_SUB)
        y_scr[pl.ds(out0, CONV_CHUNK * CONV_SUB), :] = acc.reshape(CONV_CHUNK * CONV_SUB, LANES)
        return carry

    lax.fori_loop(0, tm // CONV_CHUNK, chunk, 0)

    def norm(c, carry):
        r0 = pl.multiple_of(c * NORM_CHUNK, NORM_CHUNK)
        acc = jnp.concatenate(
            [y_scr[pl.ds(r0 * CONV_SUB + j, NORM_CHUNK, stride=CONV_SUB), :] for j in range(CONV_SUB)],
            axis=1)
        mu = jnp.mean(acc, axis=-1, keepdims=True)
        cen = acc - mu
        var = jnp.mean(cen * cen, axis=-1, keepdims=True)
        y = cen * lax.rsqrt(var + EPS) * lng_ref[...] + lnb_ref[...]
        o_ref[pl.ds(r0, NORM_CHUNK), :] = (y * _sigmoid(y)).astype(bf16)
        return carry

    lax.fori_loop(0, tm // NORM_CHUNK, norm, 0, unroll=2)


def _conv(z3, w_dw, b_dw, ln_g, ln_b, *, tm):
    b, s, _ = z3.shape
    hb = tm // CONV_HALO
    nhb = s // CONV_HALO
    ca, cg = COL_GLU // CONV_CH, COL_GLU // CONV_CH + 1

    def halo(col, which):
        if which < 0:
            return pl.BlockSpec((None, CONV_HALO, CONV_CH),
                                lambda bb, i: (bb, jnp.maximum(i * hb - 1, 0), col))
        return pl.BlockSpec((None, CONV_HALO, CONV_CH),
                            lambda bb, i: (bb, jnp.minimum((i + 1) * hb, nhb - 1), col))

    cur = lambda col: pl.BlockSpec((None, tm, CONV_CH), lambda bb, i: (bb, i, col))
    vec = pl.BlockSpec((1, CONV_CH), lambda bb, i: (0, 0))
    return pl.pallas_call(
        functools.partial(_conv_kernel, tm=tm),
        grid=(b, s // tm),
        in_specs=[halo(ca, -1), cur(ca), halo(ca, 1), halo(cg, -1), cur(cg), halo(cg, 1),
                  pl.BlockSpec((CONV_WIDTH, CONV_SUB, LANES), lambda bb, i: (0, 0, 0)),
                  pl.BlockSpec((CONV_SUB, LANES), lambda bb, i: (0, 0)), vec, vec],
        out_specs=pl.BlockSpec((None, tm, CONV_CH), lambda bb, i: (bb, i, 0)),
        out_shape=jax.ShapeDtypeStruct((b, s, CONV_CH), bf16),
        scratch_shapes=[pltpu.VMEM(((tm + 2 * CONV_HALO) * CONV_SUB, LANES), f32),
                        pltpu.VMEM((tm * CONV_SUB, LANES), f32)],
        compiler_params=_cparams(("arbitrary", "arbitrary")),
        name="conv_branch",
    )(z3, z3, z3, z3, z3, z3, w_dw.reshape(CONV_WIDTH, CONV_SUB, LANES), b_dw.reshape(CONV_SUB, LANES),
      ln_g, ln_b)


def _t5_bucket_table(dil):
    delta = (np.arange(ATTN_WIN)[None, :] - ATTN_HALF) - np.arange(ATTN_QB)[:, None]
    rel = delta * dil
    nb = NUM_BUCKETS // 2
    max_exact = nb // 2
    n = np.abs(rel)
    nf = np.maximum(n, 1).astype(np.float32)
    large = max_exact + (np.log(nf / np.float32(max_exact)) / np.float32(math.log(REL_MAX_DISTANCE / max_exact))
                         * np.float32(nb - max_exact)).astype(np.int32)
    large = np.minimum(large, nb - 1)
    bucket = np.where(rel > 0, nb, 0) + np.where(n < max_exact, n, large)
    return np.where(np.abs(delta) <= ATTN_HALF, bucket, -1).astype(np.int32)


def _attn_kernel(rb_ref, bkt_ref, q_ref, kp_ref, kc_ref, kn_ref, vp_ref, vc_ref, vn_ref, o_ref, lse_ref,
                 q_scr, k_scr, v_scr, bias_scr, *, gi, dil, n_tiles, sub_len):
    i = pl.program_id(1)
    h = ATTN_HALF
    tile = INPROJ_TM
    per = tile // dil
    tq = n_tiles * per

    @pl.when((pl.program_id(0) == 0) & (i == 0))
    def _():
        bk = bkt_ref[...]
        for hh in range(HEADS_PER_GROUP):
            acc = jnp.full((ATTN_QB, ATTN_WIN), NEG_INF, f32)
            for b_id in range(NUM_BUCKETS):
                acc = jnp.where(bk == b_id, rb_ref[b_id * N_ATTN_HEADS + gi * HEADS_PER_GROUP + hh], acc)
            bias_scr[hh] = acc

    halo_rows = kp_ref.shape[0]
    for r in range(dil):
        prev_lo = (r + 1) * per - h if halo_rows == tile else 0
        next_lo = r * per if halo_rows == tile else 0
        for dst, prv, cur, nxt in ((k_scr, kp_ref, kc_ref, kn_ref), (v_scr, vp_ref, vc_ref, vn_ref)):
            dst[r, 0:h, :] = prv[prev_lo:prev_lo + h, :]
            for t in range(n_tiles):
                dst[r, h + t * per:h + (t + 1) * per, :] = cur[t * tile + r * per:t * tile + (r + 1) * per, :]
            dst[r, h + tq:, :] = nxt[next_lo:next_lo + h, :]
        for t in range(n_tiles):
            q_scr[r, t * per:(t + 1) * per, :] = q_ref[t * tile + r * per:t * tile + (r + 1) * per, :]

    scale = HEAD_DIM ** -0.5
    col = lax.broadcasted_iota(jnp.int32, (ATTN_QB, ATTN_WIN), 1)
    lane = lax.broadcasted_iota(jnp.int32, (ATTN_QB, LANES), 1)

    def rows(start, n):
        if dil == 1:
            return pl.ds(pl.multiple_of(start, ATTN_QB), n)
        return pl.ds(start, n, stride=dil)

    def unit(u, carry):
        r = jnp.bitwise_and(u, dil - 1)
        jb = lax.shift_right_logical(u, jnp.int32(dil.bit_length() - 1))
        start = jb * (ATTN_QB * dil) + r
        p0 = pl.multiple_of(jb * ATTN_QB, ATTN_QB)
        cols = [slice(hh * HEAD_DIM, (hh + 1) * HEAD_DIM) for hh in range(HEADS_PER_GROUP)]
        q = [q_scr[r, pl.ds(p0, ATTN_QB), cs] for cs in cols]
        kw = [k_scr[r, pl.ds(p0, ATTN_WIN), cs] for cs in cols]
        vw = [v_scr[r, pl.ds(p0, ATTN_WIN), cs] for cs in cols]
        key0 = i * tq + jb * ATTN_QB - ATTN_HALF
        valid = (col + key0 >= 0) & (col + key0 < sub_len)
        s = [lax.dot_general(q[hh], kw[hh], (((1,), (1,)), ((), ())), preferred_element_type=f32)
             for hh in range(HEADS_PER_GROUP)]
        s = [jnp.where(valid, s[hh] * scale + bias_scr[hh], NEG_INF) for hh in range(HEADS_PER_GROUP)]
        m = [jnp.max(x, axis=-1, keepdims=True) for x in s]
        p = [jnp.exp(s[hh] - m[hh]) for hh in range(HEADS_PER_GROUP)]
        den = [jnp.sum(x, axis=-1, keepdims=True) for x in p]
        pv = [jnp.dot(p[hh].astype(bf16), vw[hh], preferred_element_type=f32)
              for hh in range(HEADS_PER_GROUP)]
        lse_tile = jnp.zeros((ATTN_QB, LANES), f32)
        for hh in range(HEADS_PER_GROUP):
            o_ref[hh, rows(start, ATTN_QB), :] = pv[hh] / den[hh]
            lse_tile = jnp.where(lane == hh, m[hh] + jnp.log(den[hh]), lse_tile)
        lse_ref[rows(start, ATTN_QB), :] = lse_tile
        return carry

    lax.fori_loop(0, n_tiles * tile // ATTN_QB, unit, 0, unroll=4)


def _attn_group(qkv, rel_bias_flat, gi, dil, *, batch, seq):
    t = batch * seq
    tile = INPROJ_TM
    n_tiles = max(1, ATTN_QB * dil // tile)
    ts = n_tiles * tile
    steps = seq // ts
    tq = ts // dil
    halo = ATTN_HALF if dil == 1 else tile
    hpt = ts // halo
    hps = seq // halo

    def cur(which):
        return pl.BlockSpec((None, ts, GROUP_W), lambda b, i: (gi, b * steps + i, which))

    def prev(which):
        return pl.BlockSpec((None, halo, GROUP_W),
                            lambda b, i: (gi, b * hps + jnp.maximum(i * hpt - 1, 0), which))

    def nxt(which):
        return pl.BlockSpec((None, halo, GROUP_W),
                            lambda b, i: (gi, b * hps + jnp.minimum((i + 1) * hpt, hps - 1), which))

    bkt = jnp.asarray(_t5_bucket_table(dil))
    return pl.pallas_call(
        functools.partial(_attn_kernel, gi=gi, dil=dil, n_tiles=n_tiles, sub_len=seq // dil),
        grid=(batch, steps),
        in_specs=[pl.BlockSpec(memory_space=pltpu.SMEM),
                  pl.BlockSpec((ATTN_QB, ATTN_WIN), lambda b, i: (0, 0)),
                  cur(0), prev(1), cur(1), nxt(1), prev(2), cur(2), nxt(2)],
        out_specs=[pl.BlockSpec((HEADS_PER_GROUP, ts, LANES), lambda b, i: (0, b * steps + i, 0)),
                   pl.BlockSpec((ts, LANES), lambda b, i: (b * steps + i, 0))],
        out_shape=[jax.ShapeDtypeStruct((HEADS_PER_GROUP, t, LANES), f32),
                   jax.ShapeDtypeStruct((t, LANES), f32)],
        scratch_shapes=[pltpu.VMEM((dil, tq, GROUP_W), bf16),
                        pltpu.VMEM((dil, tq + 2 * ATTN_HALF, GROUP_W), bf16),
                        pltpu.VMEM((dil, tq + 2 * ATTN_HALF, GROUP_W), bf16),
                        pltpu.VMEM((HEADS_PER_GROUP, ATTN_QB, ATTN_WIN), f32)],
        compiler_params=_cparams(("arbitrary", "arbitrary")),
        name=f"attn_g{gi}",
    )(rel_bias_flat, bkt, qkv, qkv, qkv, qkv, qkv, qkv, qkv)


def _merge_kernel(x_ref, ca_ref, o0_ref, o1_ref, o2_ref, l0_ref, l1_ref, l2_ref, ga_ref, gb_ref,
                  wco_ref, bco_ref, wao_ref, wo_ref, gpost_ref, gate1_ref, gpre_ref, scale2_ref,
                  shift2_ref, wr_ref, br_ref, x1_ref, h2p_ref, logit_ref, *, n_sub):
    hm = x_ref.shape[0] // n_sub
    subs = [slice(k * hm, (k + 1) * hm) for k in range(n_sub)]

    branch_a = [jnp.dot(ca_ref[rs, :], wco_ref[...], preferred_element_type=f32) + bco_ref[...] for rs in subs]

    def attention_mix(rs):
        l0, l1, l2 = l0_ref[rs, :], l1_ref[rs, :], l2_ref[rs, :]
        lmax = jnp.maximum(jnp.maximum(l0, l1), l2)
        e0, e1, e2 = jnp.exp(l0 - lmax), jnp.exp(l1 - lmax), jnp.exp(l2 - lmax)
        inv = 1.0 / (e0 + e1 + e2)
        parts = []
        for hh in range(HEADS_PER_GROUP):
            ls = slice(hh, hh + 1)
            parts.append((e0[:, ls] * inv[:, ls]) * o0_ref[hh, rs, :] + (e1[:, ls] * inv[:, ls]) * o1_ref[hh, rs, :]
                         + (e2[:, ls] * inv[:, ls]) * o2_ref[hh, rs, :])
        return jnp.concatenate(parts, axis=1).astype(bf16)

    att = [attention_mix(rs) for rs in subs]
    branch_b = [jnp.dot(a, wao_ref[...], preferred_element_type=f32) for a in att]
    mix = [(_sigmoid(ga_ref[rs, :].astype(f32)) * branch_a[k]
            + _sigmoid(gb_ref[rs, :].astype(f32)) * branch_b[k]).astype(bf16) for k, rs in enumerate(subs)]
    y = [jnp.dot(m, wo_ref[...], preferred_element_type=f32) for m in mix]

    h2s = []
    for k, rs in enumerate(subs):
        yn = y[k] * lax.rsqrt(jnp.mean(y[k] * y[k], axis=-1, keepdims=True) + EPS) * gpost_ref[...]
        x1 = x_ref[rs, :] + gate1_ref[...] * yn
        x1_ref[rs, :] = x1
        hn = x1 * lax.rsqrt(jnp.mean(x1 * x1, axis=-1, keepdims=True) + EPS) * gpre_ref[...]
        h2 = hn * (1.0 + scale2_ref[...]) + shift2_ref[...]
        _store_position_major(h2p_ref.at[pl.ds(k * hm * ROW_TILE, hm * ROW_TILE), :], h2)
        h2s.append(h2.astype(bf16))
    for k, rs in enumerate(subs):
        logit_ref[rs, :] = jnp.dot(h2s[k], wr_ref[...], preferred_element_type=f32) + br_ref[...]


def _merge(x2, conv_act, os_, lses, zt, w_co, b_co, w_ao, w_o, g_post, gate1, g_pre, scale2, shift2,
           w_r, b_r, *, seq, tm, n_sub):
    t, d = x2.shape
    rows = lambda w: pl.BlockSpec((tm, w), lambda i: (i, 0))
    heads = pl.BlockSpec((HEADS_PER_GROUP, tm, LANES), lambda i: (0, i, 0))
    full = lambda a: pl.BlockSpec(a.shape, lambda i: (0,) * a.ndim)
    perb = pl.BlockSpec((None, 1, d), lambda i: ((i * tm) // seq, 0, 0))
    return pl.pallas_call(
        functools.partial(_merge_kernel, n_sub=n_sub),
        grid=(t // tm,),
        in_specs=[rows(d), rows(CONV_CH), heads, heads, heads, rows(LANES), rows(LANES), rows(LANES),
                  pl.BlockSpec((tm, d), lambda i: (i, 0)), pl.BlockSpec((tm, d), lambda i: (i, 1)),
                  full(w_co), full(b_co), full(w_ao), full(w_o), full(g_post), perb, full(g_pre),
                  perb, perb, full(w_r), full(b_r)],
        out_specs=[rows(d), pl.BlockSpec((tm * ROW_TILE, LANES), lambda i: (i, 0)), rows(ROUTE_LANES)],
        out_shape=[jax.ShapeDtypeStruct((t, d), f32),
                   jax.ShapeDtypeStruct((t * ROW_TILE, LANES), u32),
                   jax.ShapeDtypeStruct((t, ROUTE_LANES), f32)],
        compiler_params=_cparams(("arbitrary",)),
        name="merge",
    )(x2, conv_act, *os_, *lses, zt, zt, w_co, b_co, w_ao, w_o, g_post, gate1, g_pre, scale2, shift2,
      w_r, b_r)


def _route_kernel(logit_ref, rec_ref, rect_ref, cnt_ref, carry):
    i = pl.program_id(0)

    @pl.when(i == 0)
    def _():
        carry[...] = jnp.zeros_like(carry)

    lg = logit_ref[...]
    tm = lg.shape[0]
    lane = lax.broadcasted_iota(jnp.int32, lg.shape, 1).astype(f32)
    big = float(2 * ROUTE_LANES)

    def first_max(mask):
        v = jnp.max(jnp.where(mask, lg, -jnp.inf), axis=-1, keepdims=True)
        idx = jnp.min(jnp.where(mask & (lg == v), lane, big), axis=-1, keepdims=True)
        return v, idx

    gmask = lane < N_GROUPS
    gmax, gsel = first_max(gmask)
    p_g = 1.0 / jnp.sum(jnp.where(gmask, jnp.exp(lg - gmax), 0.0), axis=-1, keepdims=True)
    e_lo = N_GROUPS + EXPERTS_PER_GROUP * gsel
    emask = (lane >= e_lo) & (lane < e_lo + EXPERTS_PER_GROUP)
    v0, i0 = first_max(emask)
    v1, i1 = first_max(emask & (lane != i0))
    t1 = jnp.exp(v1 - v0)
    w0 = p_g / (1.0 + t1)
    w1 = p_g * t1 / (1.0 + t1)
    e0 = i0 - N_GROUPS
    e1 = i1 - N_GROUPS

    hit0 = lane == e0
    hit1 = lane == e1
    onehot = jnp.where(hit0 | hit1, 1.0, 0.0)
    r_i = lax.broadcasted_iota(jnp.int32, (tm, tm), 0)
    c_i = lax.broadcasted_iota(jnp.int32, (tm, tm), 1)
    tril = jnp.where(c_i < r_i, 1.0, 0.0).astype(bf16)
    before = jnp.dot(tril, onehot.astype(bf16), preferred_element_type=f32) + carry[...]
    rank0 = jnp.sum(jnp.where(hit0, before, 0.0), axis=-1, keepdims=True)
    rank1 = jnp.sum(jnp.where(hit1, before, 0.0), axis=-1, keepdims=True)
    carry[...] = carry[...] + jnp.sum(onehot, axis=0, keepdims=True)
    cnt_ref[...] = carry[...]

    rec = jnp.zeros_like(lg)
    for slot, val in enumerate((e0, e1, rank0, rank1, w0, w1)):
        rec = jnp.where(lane == slot, val, rec)
    rec_ref[...] = rec
    rect_ref[...] = jnp.transpose(rec)[0:ROUTE_FIELDS, :]


def _route(logits, *, tm):
    t = logits.shape[0]
    return pl.pallas_call(
        _route_kernel,
        grid=(t // tm,),
        in_specs=[pl.BlockSpec((tm, ROUTE_LANES), lambda i: (i, 0))],
        out_specs=[pl.BlockSpec((tm, ROUTE_LANES), lambda i: (i, 0)),
                   pl.BlockSpec((ROUTE_FIELDS, tm), lambda i: (0, i)),
                   pl.BlockSpec((1, ROUTE_LANES), lambda i: (0, 0))],
        out_shape=[jax.ShapeDtypeStruct((t, ROUTE_LANES), f32),
                   jax.ShapeDtypeStruct((ROUTE_FIELDS, t), f32),
                   jax.ShapeDtypeStruct((1, ROUTE_LANES), f32)],
        scratch_shapes=[pltpu.VMEM((1, ROUTE_LANES), f32)],
        compiler_params=_cparams(("arbitrary",)),
        name="route",
    )(logits)


DMA_UNROLL = 8


def _dispatch_kernel(zero_blk_ref, dest_ref, h_ref, xs_ref, zero_scr, sem, zsem, *, tm):
    def row_copy(r, d):
        return pltpu.make_async_copy(h_ref.at[pl.ds(pl.multiple_of(r * ROW_TILE, ROW_TILE), ROW_TILE), :],
                                     xs_ref.at[pl.ds(pl.multiple_of(d * ROW_TILE, ROW_TILE), ROW_TILE), :],
                                     sem)

    def zero_copy(blk):
        n = MOE_ROWS * ROW_TILE
        return pltpu.make_async_copy(zero_scr, xs_ref.at[pl.ds(pl.multiple_of(blk * n, n), n), :], zsem)

    @pl.when(pl.program_id(0) == 0)
    def _():
        zero_scr[...] = jnp.zeros_like(zero_scr)
        n_cand = zero_blk_ref.shape[0]

        def start(j, c):
            @pl.when(zero_blk_ref[j] >= 0)
            def _():
                zero_copy(zero_blk_ref[j]).start()
            return c

        def wait(j, c):
            @pl.when(zero_blk_ref[j] >= 0)
            def _():
                zero_copy(0).wait()
            return c

        lax.fori_loop(0, n_cand, start, 0)
        lax.fori_loop(0, n_cand, wait, 0)

    def issue(r, c):
        row_copy(r, dest_ref[TOP_K * r]).start(priority=0)
        row_copy(r, dest_ref[TOP_K * r + 1]).start(priority=1)
        return c

    lax.fori_loop(0, tm, issue, 0, unroll=DMA_UNROLL)
    for _ in range(TOP_K):
        pltpu.make_async_copy(h_ref, xs_ref.at[pl.ds(0, tm * ROW_TILE), :], sem).wait()


def _dispatch(zero_blk, dest, h2p, *, p_rows, tm):
    grid_spec = pltpu.PrefetchScalarGridSpec(
        num_scalar_prefetch=1,
        grid=(h2p.shape[0] // (tm * ROW_TILE),),
        in_specs=[pl.BlockSpec((TOP_K * tm,), lambda i, *_: (i,), memory_space=pltpu.SMEM),
                  pl.BlockSpec((tm * ROW_TILE, LANES), lambda i, *_: (i, 0))],
        out_specs=pl.BlockSpec(memory_space=pl.ANY),
        scratch_shapes=[pltpu.VMEM((MOE_ROWS * ROW_TILE, LANES), u32), pltpu.SemaphoreType.DMA(()),
                        pltpu.SemaphoreType.DMA(())],
    )
    return pl.pallas_call(
        functools.partial(_dispatch_kernel, tm=tm),
        grid_spec=grid_spec,
        out_shape=jax.ShapeDtypeStruct((p_rows * ROW_TILE, LANES), u32),
        compiler_params=_cparams(("arbitrary",)),
        name="dispatch",
    )(zero_blk, dest, h2p)


def _expert_kernel(bstart_ref, nblk_ref, tail_ref, xs_ref, w1_ref, w3_ref, w2_ref, ys_ref,
                   wf1, wf3, wf2, w1b, w3b, w2b, xbuf, ybuf, sem_in, sem_out, sem_w):
    e = pl.program_id(0)
    last = pl.num_programs(0) - 1
    ws = jnp.bitwise_and(e, 1)
    n = nblk_ref[e]
    g0 = bstart_ref[e]
    n_used = tail_ref[0]
    blk = MOE_ROWS * ROW_TILE

    def hbm_rows(ref, g):
        return ref.at[pl.ds(pl.multiple_of(g * blk, blk), blk), :]

    def in_copy(g, slot):
        return pltpu.make_async_copy(hbm_rows(xs_ref, g), xbuf.at[slot], sem_in.at[slot])

    def out_copy(g, slot):
        return pltpu.make_async_copy(ybuf.at[slot], hbm_rows(ys_ref, g), sem_out.at[slot])

    def weight_copies(ex, slot):
        return [pltpu.make_async_copy(w1_ref.at[ex], wf1.at[slot], sem_w.at[slot]),
                pltpu.make_async_copy(w3_ref.at[ex], wf3.at[slot], sem_w.at[slot]),
                pltpu.make_async_copy(w2_ref.at[ex], wf2.at[slot], sem_w.at[slot])]

    @pl.when(e == 0)
    def _():
        for cp in weight_copies(0, 0):
            cp.start(priority=1)

        @pl.when(n_used > 0)
        def _():
            in_copy(0, 0).start()

    @pl.when(e < last)
    def _():
        for cp in weight_copies(e + 1, 1 - ws):
            cp.start(priority=1)

    for cp in weight_copies(e, ws):
        cp.wait()
    w1b[...] = wf1[ws].astype(bf16)
    w3b[...] = wf3[ws].astype(bf16)
    w2b[...] = wf2[ws].astype(bf16)

    def body(b, carry):
        g = g0 + b
        slot = jnp.bitwise_and(g, 1)
        in_copy(g, slot).wait()

        @pl.when(g + 1 < n_used)
        def _():
            in_copy(g + 1, 1 - slot).start()

        @pl.when(g >= 2)
        def _():
            out_copy(g - 2, slot).wait()

        x = _load_position_major(xbuf.at[slot], MOE_ROWS).astype(bf16)
        h1 = jnp.dot(x, w1b[...], preferred_element_type=f32)
        h3 = jnp.dot(x, w3b[...], preferred_element_type=f32)
        a = (h1 * _sigmoid(h1) * h3).astype(bf16)
        _store_position_major(ybuf.at[slot], jnp.dot(a, w2b[...], preferred_element_type=f32))
        out_copy(g, slot).start()
        return carry

    lax.fori_loop(0, n, body, 0)

    @pl.when(e == last)
    def _():
        @pl.when(n_used >= 2)
        def _():
            out_copy(n_used - 2, jnp.bitwise_and(n_used, 1)).wait()

        @pl.when(n_used >= 1)
        def _():
            out_copy(n_used - 1, jnp.bitwise_and(n_used - 1, 1)).wait()

        ybuf[0] = jnp.zeros(ybuf.shape[1:], u32)
        count = tail_ref[1]

        def zero_copy(b):
            return pltpu.make_async_copy(ybuf.at[0], hbm_rows(ys_ref, n_used + b), sem_out.at[0])

        lax.fori_loop(0, count, lambda b, c: (zero_copy(b).start(), c)[1], 0)
        lax.fori_loop(0, count, lambda b, c: (zero_copy(0).wait(), c)[1], 0)


def _experts(bstart, nblk, tail, xs, w1, w3, w2):
    blk = MOE_ROWS * ROW_TILE
    n_exp, d, ff = w1.shape
    grid_spec = pltpu.PrefetchScalarGridSpec(
        num_scalar_prefetch=3,
        grid=(n_exp,),
        in_specs=[pl.BlockSpec(memory_space=pl.ANY)] * 4,
        out_specs=pl.BlockSpec(memory_space=pl.ANY),
        scratch_shapes=[pltpu.VMEM((2, d, ff), f32), pltpu.VMEM((2, d, ff), f32), pltpu.VMEM((2, ff, d), f32),
                        pltpu.VMEM((d, ff), bf16), pltpu.VMEM((d, ff), bf16), pltpu.VMEM((ff, d), bf16),
                        pltpu.VMEM((2, blk, LANES), u32), pltpu.VMEM((2, blk, LANES), u32),
                        pltpu.SemaphoreType.DMA((2,)), pltpu.SemaphoreType.DMA((2,)),
                        pltpu.SemaphoreType.DMA((2,))],
    )
    return pl.pallas_call(
        _expert_kernel,
        grid_spec=grid_spec,
        out_shape=jax.ShapeDtypeStruct(xs.shape, u32),
        compiler_params=_cparams(("arbitrary",)),
        name="experts",
    )(bstart, nblk, tail, xs, w1, w3, w2)


def _combine_kernel(dest_ref, dest_next_ref, ys_ref, rec_ref, x1_ref, gpost_ref, gate2_ref, o_ref, buf, sem,
                    *, tm):
    i = pl.program_id(0)
    slot = jnp.bitwise_and(i, 1)

    def gather(d_ref, s):
        def row_copy(d, k, r):
            return pltpu.make_async_copy(
                ys_ref.at[pl.ds(pl.multiple_of(d * ROW_TILE, ROW_TILE), ROW_TILE), :],
                buf.at[s, k, pl.ds(pl.multiple_of(r * ROW_TILE, ROW_TILE), ROW_TILE), :], sem.at[s])

        def issue(r, c):
            row_copy(d_ref[TOP_K * r], 0, r).start(priority=0)
            row_copy(d_ref[TOP_K * r + 1], 1, r).start(priority=1)
            return c

        lax.fori_loop(0, tm, issue, 0, unroll=DMA_UNROLL)

    @pl.when(i == 0)
    def _():
        gather(dest_ref, 0)

    @pl.when(i + 1 < pl.num_programs(0))
    def _():
        gather(dest_next_ref, 1 - slot)

    for k in range(TOP_K):
        pltpu.make_async_copy(ys_ref.at[pl.ds(0, tm * ROW_TILE), :], buf.at[slot, k], sem.at[slot]).wait()

    rec = rec_ref[...]
    y = (rec[:, 4:5] * _load_position_major(buf.at[slot, 0], tm)
         + rec[:, 5:6] * _load_position_major(buf.at[slot, 1], tm))
    yn = y * lax.rsqrt(jnp.mean(y * y, axis=-1, keepdims=True) + EPS) * gpost_ref[...]
    o_ref[...] = x1_ref[...] + gate2_ref[...] * yn


def _combine(dest, ys, rec, x1, g_post, gate2, *, seq, tm):
    t, d = x1.shape
    last = t // tm - 1
    return pl.pallas_call(
        functools.partial(_combine_kernel, tm=tm),
        grid=(t // tm,),
        in_specs=[pl.BlockSpec((TOP_K * tm,), lambda i: (i,), memory_space=pltpu.SMEM),
                  pl.BlockSpec((TOP_K * tm,), lambda i: (jnp.minimum(i + 1, last),), memory_space=pltpu.SMEM),
                  pl.BlockSpec(memory_space=pl.ANY),
                  pl.BlockSpec((tm, ROUTE_LANES), lambda i: (i, 0)),
                  pl.BlockSpec((tm, d), lambda i: (i, 0)),
                  pl.BlockSpec((1, d), lambda i: (0, 0)),
                  pl.BlockSpec((None, 1, d), lambda i: ((i * tm) // seq, 0, 0))],
        out_specs=pl.BlockSpec((tm, d), lambda i: (i, 0)),
        out_shape=jax.ShapeDtypeStruct((t, d), f32),
        scratch_shapes=[pltpu.VMEM((2, TOP_K, tm * ROW_TILE, LANES), u32), pltpu.SemaphoreType.DMA((2,))],
        compiler_params=_cparams(("arbitrary",)),
        name="combine",
    )(dest, dest, ys, rec, x1, g_post, gate2)


def _layer(x, c, w_ada, b_ada, g_pre_mix, g_post_mix, w_in, b_in, w_dw, b_dw, ln_conv_g, ln_conv_b,
           w_conv_out, b_conv_out, rel_bias, w_attn_out, w_out, g_pre_ffn, g_post_ffn,
           w_router_group, b_router_group, w_router_expert, b_router_expert, w1, w3, w2):
    batch, seq, d = x.shape
    t = batch * seq
    row = lambda v: v.reshape(1, -1)

    c_pad = jnp.zeros((SUBLANES, d), f32).at[:batch].set(c)
    mod = _ada(c_pad, w_ada, row(b_ada))[:batch]
    shift1, scale1, gate1, shift2, scale2, gate2 = [m.reshape(batch, 1, d) for m in jnp.split(mod, 6, axis=-1)]

    x2 = x.reshape(t, d)
    zt, qkv = _inproj(x2, row(g_pre_mix), scale1, shift1, w_in.astype(bf16), row(b_in), seq=seq, tm=INPROJ_TM)

    conv_act = _conv(zt.reshape(batch, seq, ZT_COLS), w_dw, row(b_dw), row(ln_conv_g), row(ln_conv_b),
                     tm=512).reshape(t, CONV_CH)

    os_, lses = [], []
    for gi, (_, dil) in enumerate(DILATED_GROUPS):
        o, lse = _attn_group(qkv, rel_bias.reshape(-1), gi, dil, batch=batch, seq=seq)
        os_.append(o)
        lses.append(lse)

    pad = ROUTE_LANES - N_GROUPS - N_EXPERTS
    w_r = jnp.concatenate([w_router_group, w_router_expert, jnp.zeros((d, pad), f32)], axis=1).astype(bf16)
    b_r = row(jnp.concatenate([b_router_group, b_router_expert.reshape(-1), jnp.zeros((pad,), f32)]))
    x1, h2p, logits = _merge(x2, conv_act, os_, lses, zt, w_conv_out.astype(bf16), row(b_conv_out),
                             w_attn_out.astype(bf16), w_out.astype(bf16), row(g_post_mix), gate1,
                             row(g_pre_ffn), scale2, shift2, w_r, b_r, seq=seq, tm=MERGE_TM, n_sub=MERGE_SUB)

    rec, rec_t, cnt = _route(logits, tm=512)

    counts = cnt[0, :N_EXPERTS].astype(jnp.int32)
    pcounts = (counts + MOE_ROWS - 1) // MOE_ROWS * MOE_ROWS
    pends = jnp.cumsum(pcounts)
    pstarts = pends - pcounts
    n_blocks = (t * TOP_K + N_EXPERTS * (MOE_ROWS - 1) + MOE_ROWS - 1) // MOE_ROWS
    n_used = (pends[-1:] // MOE_ROWS).astype(jnp.int32)
    tail_blocks = jnp.concatenate([n_used, n_blocks - n_used])
    eid = rec_t[0:TOP_K].astype(jnp.int32)
    rank = rec_t[TOP_K:2 * TOP_K].astype(jnp.int32)
    start_of = jnp.sum(jnp.where(eid[..., None] == jnp.arange(N_EXPERTS), pstarts, 0), axis=-1)
    dest = jnp.transpose(start_of + rank).reshape(t * TOP_K)
    tail = n_blocks - N_EXPERTS + jnp.arange(N_EXPERTS, dtype=jnp.int32)
    zero_blk = jnp.concatenate([jnp.where(pcounts > counts, pends // MOE_ROWS - 1, -1),
                                jnp.where(tail >= n_used[0], tail, -1)]).astype(jnp.int32)

    xs = _dispatch(zero_blk, dest, h2p, p_rows=n_blocks * MOE_ROWS, tm=512)
    ys = _experts((pstarts // MOE_ROWS).astype(jnp.int32), (pcounts // MOE_ROWS).astype(jnp.int32),
                  tail_blocks, xs, w1, w3, w2)
    out = _combine(dest, ys, rec, x1, row(g_post_ffn), gate2, seq=seq, tm=512)
    return out.reshape(batch, seq, d)


def kernel(x, c, w_ada, b_ada, g_pre_mix, g_post_mix, w_in, b_in, w_dw, b_dw, ln_conv_g, ln_conv_b,
           w_conv_out, b_conv_out, rel_bias, w_attn_out, w_out, g_pre_ffn, g_post_ffn,
           w_router_group, b_router_group, w_router_expert, b_router_expert, w1, w3, w2):
    depth = w_ada.shape[0]
    for l in range(depth):
        pick = (lambda a: a.reshape(a.shape[1:])) if depth == 1 else (lambda a, l=l: a[l])
        x = _layer(x, c, pick(w_ada), pick(b_ada), pick(g_pre_mix), pick(g_post_mix), pick(w_in),
                   pick(b_in), pick(w_dw), pick(b_dw), pick(ln_conv_g), pick(ln_conv_b),
                   pick(w_conv_out), pick(b_conv_out), rel_bias, pick(w_attn_out), pick(w_out),
                   pick(g_pre_ffn), pick(g_post_ffn), pick(w_router_group), pick(b_router_group),
                   pick(w_router_expert), pick(b_router_expert), pick(w1), pick(w3), pick(w2))
    return x
```

```python
import functools
import math

import numpy as np
import jax
import jax.numpy as jnp
from jax import lax
from jax.experimental import pallas as pl
from jax.experimental.pallas import tpu as pltpu

D_MODEL = 2048
CONV_CH = 1024
CONV_WIDTH = 31
N_ATTN_HEADS = 12
HEADS_PER_GROUP = 4
HEAD_DIM = 128
ATTN_WIDTH = N_ATTN_HEADS * HEAD_DIM
DILATED_GROUPS = ((128, 1), (512, 4), (2048, 16))
NUM_BUCKETS = 32
REL_MAX_DISTANCE = 1024
N_GROUPS = 8
EXPERTS_PER_GROUP = 8
N_EXPERTS = N_GROUPS * EXPERTS_PER_GROUP
TOP_K = 2
EXPERT_FF = 512
EPS = 1e-6
NEG_INF = -1e30

LANES = 128
SUBLANES = 8
V7X_VMEM_LIMIT_BYTES = 56 * 1024 * 1024

GROUP_W = HEADS_PER_GROUP * HEAD_DIM
ZT_COLS = 2 * D_MODEL + 2 * CONV_CH
COL_GLU = 2 * D_MODEL
INPROJ_TN = 3 * GROUP_W
ZT_TILES = ZT_COLS // INPROJ_TN
INPROJ_TM = 1024
ATTN_HALF = 64
ATTN_QB = 128
ATTN_WIN = ATTN_QB + 2 * ATTN_HALF
MOE_ROWS = 256
MERGE_TM = 256
MERGE_SUB = 2
ROUTE_LANES = 128
ROUTE_FIELDS = 8
PACKED_W = D_MODEL // 2
ROW_TILE = PACKED_W // LANES

bf16 = jnp.bfloat16
f32 = jnp.float32
u32 = jnp.uint32
HI_MASK = 0xFFFF0000


def _cparams(sem):
    return pltpu.CompilerParams(dimension_semantics=sem, vmem_limit_bytes=V7X_VMEM_LIMIT_BYTES)


def _sigmoid(x):
    return 1.0 / (1.0 + jnp.exp(-x))


def _pack_pairs(v):
    n = v.shape[1] // 2
    lo = lax.bitcast_convert_type(v[:, :n].astype(bf16).astype(f32), u32)
    hi = lax.bitcast_convert_type(v[:, n:].astype(bf16).astype(f32), u32)
    return (lo >> 16) | (hi & u32(HI_MASK))


def _unpack_lo(u):
    return lax.bitcast_convert_type(u << 16, f32)


def _unpack_hi(u):
    return lax.bitcast_convert_type(u & u32(HI_MASK), f32)


def _store_position_major(ref, v):
    packed = _pack_pairs(v)
    for j in range(ROW_TILE):
        ref[pl.ds(j, v.shape[0], stride=ROW_TILE), :] = packed[:, j * LANES:(j + 1) * LANES]


def _load_position_major(ref, n):
    chunks = [ref[pl.ds(j, n, stride=ROW_TILE), :] for j in range(ROW_TILE)]
    return jnp.concatenate([_unpack_lo(c) for c in chunks] + [_unpack_hi(c) for c in chunks], axis=1)


def _ada_kernel(c_ref, w_ref, b_ref, o_ref):
    c = c_ref[...]
    a = (c * _sigmoid(c)).astype(bf16)
    o_ref[...] = jnp.dot(a, w_ref[...].astype(bf16), preferred_element_type=f32) + b_ref[...]


def _ada(c_pad, w_ada, b_ada):
    rows, d = c_pad.shape
    n = w_ada.shape[1]
    tn = 1024
    return pl.pallas_call(
        _ada_kernel,
        grid=(n // tn,),
        in_specs=[pl.BlockSpec((rows, d), lambda j: (0, 0)),
                  pl.BlockSpec((d, tn), lambda j: (0, j)),
                  pl.BlockSpec((1, tn), lambda j: (0, j))],
        out_specs=pl.BlockSpec((rows, tn), lambda j: (0, j)),
        out_shape=jax.ShapeDtypeStruct((rows, n), f32),
        compiler_params=_cparams(("arbitrary",)),
        name="ada_mod",
    )(c_pad, w_ada, b_ada)


def _inproj_kernel(x_ref, g_ref, scale_ref, shift_ref, wa_ref, wb_ref, wc_ref, ba_ref, bb_ref, bc_ref,
                   zt_ref, qkv_ref, h_scr, slab_scr):
    j = pl.program_id(1)
    tm = x_ref.shape[0]

    @pl.when(j == 0)
    def _():
        x = x_ref[...]
        ms = jnp.mean(x * x, axis=-1, keepdims=True)
        h = x * lax.rsqrt(ms + EPS) * g_ref[...]
        h_scr[...] = (h * (1.0 + scale_ref[...]) + shift_ref[...]).astype(bf16)

    w_refs, b_refs = (wa_ref, wb_ref, wc_ref), (ba_ref, bb_ref, bc_ref)

    def chunk(c):
        return jnp.dot(h_scr[...], w_refs[c][...], preferred_element_type=f32) + b_refs[c][...]

    @pl.when(j < ZT_TILES)
    def _():
        for c in range(INPROJ_TN // GROUP_W):
            zt_ref[:, c * GROUP_W:(c + 1) * GROUP_W] = chunk(c).astype(bf16)

    for gi, (_, dil) in enumerate(DILATED_GROUPS):
        @pl.when(j == ZT_TILES + gi)
        def _(dil=dil):
            per = tm // dil
            for c in range(INPROJ_TN // GROUP_W):
                z = chunk(c)
                if dil == 1:
                    qkv_ref[:, c * GROUP_W:(c + 1) * GROUP_W] = z.astype(bf16)
                    continue
                slabs = range(GROUP_W // LANES)
                for s in slabs:
                    slab_scr[0, s] = z[:, s * LANES:(s + 1) * LANES]
                src, stride, first = 0, dil, lambda r: r
                if dil == 16:
                    quarter = tm // 4
                    for s in slabs:
                        for r4 in range(4):
                            slab_scr[1, s, r4 * quarter:(r4 + 1) * quarter, :] = (
                                slab_scr[0, s, pl.ds(r4, quarter, stride=4), :])
                    src, stride, first = 1, 4, lambda r: (r % 4) * quarter + r // 4
                for r in range(dil):
                    for s in slabs:
                        col = c * GROUP_W + s * LANES
                        qkv_ref[r * per:(r + 1) * per, col:col + LANES] = (
                            slab_scr[src, s, pl.ds(first(r), per, stride=stride), :].astype(bf16))


def _inproj_col_block(j, c):
    n_blocks = (2 * CONV_CH + 3 * ATTN_WIDTH + 2 * D_MODEL) // GROUP_W
    gate0 = (2 * CONV_CH + 3 * ATTN_WIDTH) // GROUP_W
    q0 = 2 * CONV_CH // GROUP_W
    per_proj = ATTN_WIDTH // GROUP_W
    token_major = lax.rem(gate0 + (INPROJ_TN // GROUP_W) * j + c, n_blocks)
    return jnp.where(j < ZT_TILES, token_major, q0 + (j - ZT_TILES) + per_proj * c)


def _inproj(x2, g, scale, shift, w, b, *, seq, tm):
    t, d = x2.shape
    n = w.shape[1]
    tn = INPROJ_TN
    n_groups = n // tn - ZT_TILES
    bmap = lambda i, j: ((i * tm) // seq, 0, 0)
    chunks = range(tn // GROUP_W)
    w_specs = [pl.BlockSpec((d, GROUP_W), lambda i, j, c=c: (0, _inproj_col_block(j, c))) for c in chunks]
    b_specs = [pl.BlockSpec((1, GROUP_W), lambda i, j, c=c: (0, _inproj_col_block(j, c))) for c in chunks]
    return pl.pallas_call(
        _inproj_kernel,
        grid=(t // tm, n // tn),
        in_specs=[pl.BlockSpec((tm, d), lambda i, j: (i, 0)),
                  pl.BlockSpec((1, d), lambda i, j: (0, 0)),
                  pl.BlockSpec((None, 1, d), bmap),
                  pl.BlockSpec((None, 1, d), bmap)] + w_specs + b_specs,
        out_specs=[pl.BlockSpec((tm, tn), lambda i, j: (i, jnp.minimum(j, ZT_TILES - 1))),
                   pl.BlockSpec((None, tm, tn), lambda i, j: (jnp.maximum(j - ZT_TILES, 0), i, 0))],
        out_shape=[jax.ShapeDtypeStruct((t, ZT_COLS), bf16),
                   jax.ShapeDtypeStruct((n_groups, t, tn), bf16)],
        scratch_shapes=[pltpu.VMEM((tm, d), bf16), pltpu.VMEM((2, GROUP_W // LANES, tm, LANES), f32)],
        compiler_params=_cparams(("arbitrary", "arbitrary")),
        name="inproj",
    )(x2, g, scale, shift, w, w, w, b, b, b)


CONV_HALO = 16
CONV_CHUNK = 32
CONV_SUB = CONV_CH // LANES
NORM_CHUNK = 64


def _conv_kernel(ap_ref, ac_ref, an_ref, gp_ref, gc_ref, gn_ref, w_ref, bdw_ref, lng_ref, lnb_ref,
                 o_ref, u_scr, y_scr, *, tm):
    i = pl.program_id(1)
    last = pl.num_programs(1) - 1

    def glu(a_ref, g_ref):
        return a_ref[...].astype(f32) * _sigmoid(g_ref[...].astype(f32))

    def put(row0, val):
        for j in range(CONV_SUB):
            u_scr[pl.ds(row0 * CONV_SUB + j, val.shape[0], stride=CONV_SUB), :] = (
                val[:, j * LANES:(j + 1) * LANES])

    put(0, jnp.where(i > 0, glu(ap_ref, gp_ref), 0.0))
    put(CONV_HALO, glu(ac_ref, gc_ref))
    put(CONV_HALO + tm, jnp.where(i < last, glu(an_ref, gn_ref), 0.0))

    def chunk(c, carry):
        r0 = c * CONV_CHUNK
        acc = jnp.zeros((CONV_CHUNK, CONV_SUB, LANES), f32) + bdw_ref[...][None]
        for k in range(CONV_WIDTH):
            start = pl.multiple_of((r0 + k + 1) * CONV_SUB, CONV_SUB)
            xk = u_scr[pl.ds(start, CONV_CHUNK * CONV_SUB), :].reshape(CONV_CHUNK, CONV_SUB, LANES)
            acc = acc + w_ref[k][None] * xk
        out0 = pl.multiple_of(r0 * CONV_SUB, CONV_CHUNK * CONV_SUB)
        y_scr[pl.ds(out0, CONV_CHUNK * CONV_SUB), :] = acc.reshape(CONV_CHUNK * CONV_SUB, LANES)
        return carry

    lax.fori_loop(0, tm // CONV_CHUNK, chunk, 0)

    def norm(c, carry):
        r0 = pl.multiple_of(c * NORM_CHUNK, NORM_CHUNK)
        acc = jnp.concatenate(
            [y_scr[pl.ds(r0 * CONV_SUB + j, NORM_CHUNK, stride=CONV_SUB), :] for j in range(CONV_SUB)],
            axis=1)
        mu = jnp.mean(acc, axis=-1, keepdims=True)
        cen = acc - mu
        var = jnp.mean(cen * cen, axis=-1, keepdims=True)
        y = cen * lax.rsqrt(var + EPS) * lng_ref[...] + lnb_ref[...]
        o_ref[pl.ds(r0, NORM_CHUNK), :] = (y * _sigmoid(y)).astype(bf16)
        return carry

    lax.fori_loop(0, tm // NORM_CHUNK, norm, 0, unroll=2)


def _conv(z3, w_dw, b_dw, ln_g, ln_b, *, tm):
    b, s, _ = z3.shape
    hb = tm // CONV_HALO
    nhb = s // CONV_HALO
    ca, cg = COL_GLU // CONV_CH, COL_GLU // CONV_CH + 1

    def halo(col, which):
        if which < 0:
            return pl.BlockSpec((None, CONV_HALO, CONV_CH),
                                lambda bb, i: (bb, jnp.maximum(i * hb - 1, 0), col))
        return pl.BlockSpec((None, CONV_HALO, CONV_CH),
                            lambda bb, i: (bb, jnp.minimum((i + 1) * hb, nhb - 1), col))

    cur = lambda col: pl.BlockSpec((None, tm, CONV_CH), lambda bb, i: (bb, i, col))
    vec = pl.BlockSpec((1, CONV_CH), lambda bb, i: (0, 0))
    return pl.pallas_call(
        functools.partial(_conv_kernel, tm=tm),
        grid=(b, s // tm),
        in_specs=[halo(ca, -1), cur(ca), halo(ca, 1), halo(cg, -1), cur(cg), halo(cg, 1),
                  pl.BlockSpec((CONV_WIDTH, CONV_SUB, LANES), lambda bb, i: (0, 0, 0)),
                  pl.BlockSpec((CONV_SUB, LANES), lambda bb, i: (0, 0)), vec, vec],
        out_specs=pl.BlockSpec((None, tm, CONV_CH), lambda bb, i: (bb, i, 0)),
        out_shape=jax.ShapeDtypeStruct((b, s, CONV_CH), bf16),
        scratch_shapes=[pltpu.VMEM(((tm + 2 * CONV_HALO) * CONV_SUB, LANES), f32),
                        pltpu.VMEM((tm * CONV_SUB, LANES), f32)],
        compiler_params=_cparams(("arbitrary", "arbitrary")),
        name="conv_branch",
    )(z3, z3, z3, z3, z3, z3, w_dw.reshape(CONV_WIDTH, CONV_SUB, LANES), b_dw.reshape(CONV_SUB, LANES),
      ln_g, ln_b)


def _t5_bucket_table(dil):
    delta = (np.arange(ATTN_WIN)[None, :] - ATTN_HALF) - np.arange(ATTN_QB)[:, None]
    rel = delta * dil
    nb = NUM_BUCKETS // 2
    max_exact = nb // 2
    n = np.abs(rel)
    nf = np.maximum(n, 1).astype(np.float32)
    large = max_exact + (np.log(nf / np.float32(max_exact)) / np.float32(math.log(REL_MAX_DISTANCE / max_exact))
                         * np.float32(nb - max_exact)).astype(np.int32)
    large = np.minimum(large, nb - 1)
    bucket = np.where(rel > 0, nb, 0) + np.where(n < max_exact, n, large)
    return np.where(np.abs(delta) <= ATTN_HALF, bucket, -1).astype(np.int32)


def _attn_kernel(rb_ref, bkt_ref, q_ref, kp_ref, kc_ref, kn_ref, vp_ref, vc_ref, vn_ref, o_ref, lse_ref,
                 q_scr, k_scr, v_scr, bias_scr, *, gi, dil, n_tiles, sub_len):
    i = pl.program_id(1)
    h = ATTN_HALF
    tile = INPROJ_TM
    per = tile // dil
    tq = n_tiles * per

    @pl.when((pl.program_id(0) == 0) & (i == 0))
    def _():
        bk = bkt_ref[...]
        for hh in range(HEADS_PER_GROUP):
            acc = jnp.full((ATTN_QB, ATTN_WIN), NEG_INF, f32)
            for b_id in range(NUM_BUCKETS):
                acc = jnp.where(bk == b_id, rb_ref[b_id * N_ATTN_HEADS + gi * HEADS_PER_GROUP + hh], acc)
            bias_scr[hh] = acc

    halo_rows = kp_ref.shape[0]
    for r in range(dil):
        prev_lo = (r + 1) * per - h if halo_rows == tile else 0
        next_lo = r * per if halo_rows == tile else 0
        for dst, prv, cur, nxt in ((k_scr, kp_ref, kc_ref, kn_ref), (v_scr, vp_ref, vc_ref, vn_ref)):
            dst[r, 0:h, :] = prv[prev_lo:prev_lo + h, :]
            for t in range(n_tiles):
                dst[r, h + t * per:h + (t + 1) * per, :] = cur[t * tile + r * per:t * tile + (r + 1) * per, :]
            dst[r, h + tq:, :] = nxt[next_lo:next_lo + h, :]
        for t in range(n_tiles):
            q_scr[r, t * per:(t + 1) * per, :] = q_ref[t * tile + r * per:t * tile + (r + 1) * per, :]

    scale = HEAD_DIM ** -0.5
    col = lax.broadcasted_iota(jnp.int32, (ATTN_QB, ATTN_WIN), 1)
    lane = lax.broadcasted_iota(jnp.int32, (ATTN_QB, LANES), 1)

    def rows(start, n):
        if dil == 1:
            return pl.ds(pl.multiple_of(start, ATTN_QB), n)
        return pl.ds(start, n, stride=dil)

    def unit(u, carry):
        r = jnp.bitwise_and(u, dil - 1)
        jb = lax.shift_right_logical(u, jnp.int32(dil.bit_length() - 1))
        start = jb * (ATTN_QB * dil) + r
        p0 = pl.multiple_of(jb * ATTN_QB, ATTN_QB)
        cols = [slice(hh * HEAD_DIM, (hh + 1) * HEAD_DIM) for hh in range(HEADS_PER_GROUP)]
        q = [q_scr[r, pl.ds(p0, ATTN_QB), cs] for cs in cols]
        kw = [k_scr[r, pl.ds(p0, ATTN_WIN), cs] for cs in cols]
        vw = [v_scr[r, pl.ds(p0, ATTN_WIN), cs] for cs in cols]
        key0 = i * tq + jb * ATTN_QB - ATTN_HALF
        valid = (col + key0 >= 0) & (col + key0 < sub_len)
        s = [lax.dot_general(q[hh], kw[hh], (((1,), (1,)), ((), ())), preferred_element_type=f32)
             for hh in range(HEADS_PER_GROUP)]
        s = [jnp.where(valid, s[hh] * scale + bias_scr[hh], NEG_INF) for hh in range(HEADS_PER_GROUP)]
        m = [jnp.max(x, axis=-1, keepdims=True) for x in s]
        p = [jnp.exp(s[hh] - m[hh]) for hh in range(HEADS_PER_GROUP)]
        den = [jnp.sum(x, axis=-1, keepdims=True) for x in p]
        pv = [jnp.dot(p[hh].astype(bf16), vw[hh], preferred_element_type=f32)
              for hh in range(HEADS_PER_GROUP)]
        lse_tile = jnp.zeros((ATTN_QB, LANES), f32)
        for hh in range(HEADS_PER_GROUP):
            o_ref[hh, rows(start, ATTN_QB), :] = pv[hh] / den[hh]
            lse_tile = jnp.where(lane == hh, m[hh] + jnp.log(den[hh]), lse_tile)
        lse_ref[rows(start, ATTN_QB), :] = lse_tile
        return carry

    lax.fori_loop(0, n_tiles * tile // ATTN_QB, unit, 0, unroll=4)


def _attn_group(qkv, rel_bias_flat, gi, dil, *, batch, seq):
    t = batch * seq
    tile = INPROJ_TM
    n_tiles = max(1, ATTN_QB * dil // tile)
    ts = n_tiles * tile
    steps = seq // ts
    tq = ts // dil
    halo = ATTN_HALF if dil == 1 else tile
    hpt = ts // halo
    hps = seq // halo

    def cur(which):
        return pl.BlockSpec((None, ts, GROUP_W), lambda b, i: (gi, b * steps + i, which))

    def prev(which):
        return pl.BlockSpec((None, halo, GROUP_W),
                            lambda b, i: (gi, b * hps + jnp.maximum(i * hpt - 1, 0), which))

    def nxt(which):
        return pl.BlockSpec((None, halo, GROUP_W),
                            lambda b, i: (gi, b * hps + jnp.minimum((i + 1) * hpt, hps - 1), which))

    bkt = jnp.asarray(_t5_bucket_table(dil))
    return pl.pallas_call(
        functools.partial(_attn_kernel, gi=gi, dil=dil, n_tiles=n_tiles, sub_len=seq // dil),
        grid=(batch, steps),
        in_specs=[pl.BlockSpec(memory_space=pltpu.SMEM),
                  pl.BlockSpec((ATTN_QB, ATTN_WIN), lambda b, i: (0, 0)),
                  cur(0), prev(1), cur(1), nxt(1), prev(2), cur(2), nxt(2)],
        out_specs=[pl.BlockSpec((HEADS_PER_GROUP, ts, LANES), lambda b, i: (0, b * steps + i, 0)),
                   pl.BlockSpec((ts, LANES), lambda b, i: (b * steps + i, 0))],
        out_shape=[jax.ShapeDtypeStruct((HEADS_PER_GROUP, t, LANES), f32),
                   jax.ShapeDtypeStruct((t, LANES), f32)],
        scratch_shapes=[pltpu.VMEM((dil, tq, GROUP_W), bf16),
                        pltpu.VMEM((dil, tq + 2 * ATTN_HALF, GROUP_W), bf16),
                        pltpu.VMEM((dil, tq + 2 * ATTN_HALF, GROUP_W), bf16),
                        pltpu.VMEM((HEADS_PER_GROUP, ATTN_QB, ATTN_WIN), f32)],
        compiler_params=_cparams(("arbitrary", "arbitrary")),
        name=f"attn_g{gi}",
    )(rel_bias_flat, bkt, qkv, qkv, qkv, qkv, qkv, qkv, qkv)


def _merge_kernel(x_ref, ca_ref, o0_ref, o1_ref, o2_ref, l0_ref, l1_ref, l2_ref, ga_ref, gb_ref,
                  wco_ref, bco_ref, wao_ref, wo_ref, gpost_ref, gate1_ref, gpre_ref, scale2_ref,
                  shift2_ref, wr_ref, br_ref, x1_ref, h2p_ref, logit_ref, *, n_sub):
    hm = x_ref.shape[0] // n_sub
    subs = [slice(k * hm, (k + 1) * hm) for k in range(n_sub)]

    branch_a = [jnp.dot(ca_ref[rs, :], wco_ref[...], preferred_element_type=f32) + bco_ref[...] for rs in subs]

    def attention_mix(rs):
        l0, l1, l2 = l0_ref[rs, :], l1_ref[rs, :], l2_ref[rs, :]
        lmax = jnp.maximum(jnp.maximum(l0, l1), l2)
        e0, e1, e2 = jnp.exp(l0 - lmax), jnp.exp(l1 - lmax), jnp.exp(l2 - lmax)
        inv = 1.0 / (e0 + e1 + e2)
        parts = []
        for hh in range(HEADS_PER_GROUP):
            ls = slice(hh, hh + 1)
            parts.append((e0[:, ls] * inv[:, ls]) * o0_ref[hh, rs, :] + (e1[:, ls] * inv[:, ls]) * o1_ref[hh, rs, :]
                         + (e2[:, ls] * inv[:, ls]) * o2_ref[hh, rs, :])
        return jnp.concatenate(parts, axis=1).astype(bf16)

    att = [attention_mix(rs) for rs in subs]
    branch_b = [jnp.dot(a, wao_ref[...], preferred_element_type=f32) for a in att]
    mix = [(_sigmoid(ga_ref[rs, :].astype(f32)) * branch_a[k]
            + _sigmoid(gb_ref[rs, :].astype(f32)) * branch_b[k]).astype(bf16) for k, rs in enumerate(subs)]
    y = [jnp.dot(m, wo_ref[...], preferred_element_type=f32) for m in mix]

    post_gain = gate1_ref[...] * gpost_ref[...]
    pre_gain = gpre_ref[...] * (1.0 + scale2_ref[...])
    h2s = []
    for k, rs in enumerate(subs):
        x1 = x_ref[rs, :] + (y[k] * lax.rsqrt(jnp.mean(y[k] * y[k], axis=-1, keepdims=True) + EPS)) * post_gain
        x1_ref[rs, :] = x1
        h2 = (x1 * lax.rsqrt(jnp.mean(x1 * x1, axis=-1, keepdims=True) + EPS)) * pre_gain + shift2_ref[...]
        _store_position_major(h2p_ref.at[pl.ds(k * hm * ROW_TILE, hm * ROW_TILE), :], h2)
        h2s.append(h2.astype(bf16))
    for k, rs in enumerate(subs):
        logit_ref[rs, :] = jnp.dot(h2s[k], wr_ref[...], preferred_element_type=f32) + br_ref[...]


def _merge(x2, conv_act, os_, lses, zt, w_co, b_co, w_ao, w_o, g_post, gate1, g_pre, scale2, shift2,
           w_r, b_r, *, seq, tm, n_sub):
    t, d = x2.shape
    rows = lambda w: pl.BlockSpec((tm, w), lambda i: (i, 0))
    heads = pl.BlockSpec((HEADS_PER_GROUP, tm, LANES), lambda i: (0, i, 0))
    full = lambda a: pl.BlockSpec(a.shape, lambda i: (0,) * a.ndim)
    perb = pl.BlockSpec((None, 1, d), lambda i: ((i * tm) // seq, 0, 0))
    return pl.pallas_call(
        functools.partial(_merge_kernel, n_sub=n_sub),
        grid=(t // tm,),
        in_specs=[rows(d), rows(CONV_CH), heads, heads, heads, rows(LANES), rows(LANES), rows(LANES),
                  pl.BlockSpec((tm, d), lambda i: (i, 0)), pl.BlockSpec((tm, d), lambda i: (i, 1)),
                  full(w_co), full(b_co), full(w_ao), full(w_o), full(g_post), perb, full(g_pre),
                  perb, perb, full(w_r), full(b_r)],
        out_specs=[rows(d), pl.BlockSpec((tm * ROW_TILE, LANES), lambda i: (i, 0)), rows(ROUTE_LANES)],
        out_shape=[jax.ShapeDtypeStruct((t, d), f32),
                   jax.ShapeDtypeStruct((t * ROW_TILE, LANES), u32),
                   jax.ShapeDtypeStruct((t, ROUTE_LANES), f32)],
        compiler_params=_cparams(("arbitrary",)),
        name="merge",
    )(x2, conv_act, *os_, *lses, zt, zt, w_co, b_co, w_ao, w_o, g_post, gate1, g_pre, scale2, shift2,
      w_r, b_r)


def _route_kernel(logit_ref, rec_ref, rect_ref, cnt_ref, carry):
    i = pl.program_id(0)

    @pl.when(i == 0)
    def _():
        carry[...] = jnp.zeros_like(carry)

    lg = logit_ref[...]
    tm = lg.shape[0]
    lane = lax.broadcasted_iota(jnp.int32, lg.shape, 1).astype(f32)
    big = float(2 * ROUTE_LANES)

    def first_max(mask):
        v = jnp.max(jnp.where(mask, lg, -jnp.inf), axis=-1, keepdims=True)
        idx = jnp.min(jnp.where(mask & (lg == v), lane, big), axis=-1, keepdims=True)
        return v, idx

    gmask = lane < N_GROUPS
    gmax, gsel = first_max(gmask)
    p_g = 1.0 / jnp.sum(jnp.where(gmask, jnp.exp(lg - gmax), 0.0), axis=-1, keepdims=True)
    e_lo = N_GROUPS + EXPERTS_PER_GROUP * gsel
    emask = (lane >= e_lo) & (lane < e_lo + EXPERTS_PER_GROUP)
    v0, i0 = first_max(emask)
    v1, i1 = first_max(emask & (lane != i0))
    t1 = jnp.exp(v1 - v0)
    w0 = p_g / (1.0 + t1)
    w1 = p_g * t1 / (1.0 + t1)
    e0 = i0 - N_GROUPS
    e1 = i1 - N_GROUPS

    hit0 = lane == e0
    hit1 = lane == e1
    onehot = jnp.where(hit0 | hit1, 1.0, 0.0)
    r_i = lax.broadcasted_iota(jnp.int32, (tm, tm), 0)
    c_i = lax.broadcasted_iota(jnp.int32, (tm, tm), 1)
    tril = jnp.where(c_i < r_i, 1.0, 0.0).astype(bf16)
    before = jnp.dot(tril, onehot.astype(bf16), preferred_element_type=f32) + carry[...]
    rank0 = jnp.sum(jnp.where(hit0, before, 0.0), axis=-1, keepdims=True)
    rank1 = jnp.sum(jnp.where(hit1, before, 0.0), axis=-1, keepdims=True)
    carry[...] = carry[...] + jnp.sum(onehot, axis=0, keepdims=True)
    cnt_ref[...] = carry[...]

    rec = jnp.zeros_like(lg)
    for slot, val in enumerate((e0, e1, rank0, rank1, w0, w1)):
        rec = jnp.where(lane == slot, val, rec)
    rec_ref[...] = rec
    rect_ref[...] = jnp.transpose(rec)[0:ROUTE_FIELDS, :]


def _route(logits, *, tm):
    t = logits.shape[0]
    return pl.pallas_call(
        _route_kernel,
        grid=(t // tm,),
        in_specs=[pl.BlockSpec((tm, ROUTE_LANES), lambda i: (i, 0))],
        out_specs=[pl.BlockSpec((tm, ROUTE_LANES), lambda i: (i, 0)),
                   pl.BlockSpec((ROUTE_FIELDS, tm), lambda i: (0, i)),
                   pl.BlockSpec((1, ROUTE_LANES), lambda i: (0, 0))],
        out_shape=[jax.ShapeDtypeStruct((t, ROUTE_LANES), f32),
                   jax.ShapeDtypeStruct((ROUTE_FIELDS, t), f32),
                   jax.ShapeDtypeStruct((1, ROUTE_LANES), f32)],
        scratch_shapes=[pltpu.VMEM((1, ROUTE_LANES), f32)],
        compiler_params=_cparams(("arbitrary",)),
        name="route",
    )(logits)


DMA_UNROLL = 8


def _dispatch_kernel(zero_blk_ref, dest_ref, h_ref, xs_ref, zero_scr, sem, zsem, *, tm):
    def row_copy(r, d):
        return pltpu.make_async_copy(h_ref.at[pl.ds(pl.multiple_of(r * ROW_TILE, ROW_TILE), ROW_TILE), :],
                                     xs_ref.at[pl.ds(pl.multiple_of(d * ROW_TILE, ROW_TILE), ROW_TILE), :],
                                     sem)

    def zero_copy(blk):
        n = MOE_ROWS * ROW_TILE
        return pltpu.make_async_copy(zero_scr, xs_ref.at[pl.ds(pl.multiple_of(blk * n, n), n), :], zsem)

    @pl.when(pl.program_id(0) == 0)
    def _():
        zero_scr[...] = jnp.zeros_like(zero_scr)
        n_cand = zero_blk_ref.shape[0]

        def start(j, c):
            @pl.when(zero_blk_ref[j] >= 0)
            def _():
                zero_copy(zero_blk_ref[j]).start()
            return c

        def wait(j, c):
            @pl.when(zero_blk_ref[j] >= 0)
            def _():
                zero_copy(0).wait()
            return c

        lax.fori_loop(0, n_cand, start, 0)
        lax.fori_loop(0, n_cand, wait, 0)

    def issue(r, c):
        row_copy(r, dest_ref[TOP_K * r]).start(priority=0)
        row_copy(r, dest_ref[TOP_K * r + 1]).start(priority=1)
        return c

    lax.fori_loop(0, tm, issue, 0, unroll=DMA_UNROLL)
    for _ in range(TOP_K):
        pltpu.make_async_copy(h_ref, xs_ref.at[pl.ds(0, tm * ROW_TILE), :], sem).wait()


def _dispatch(zero_blk, dest, h2p, *, p_rows, tm):
    grid_spec = pltpu.PrefetchScalarGridSpec(
        num_scalar_prefetch=1,
        grid=(h2p.shape[0] // (tm * ROW_TILE),),
        in_specs=[pl.BlockSpec((TOP_K * tm,), lambda i, *_: (i,), memory_space=pltpu.SMEM),
                  pl.BlockSpec((tm * ROW_TILE, LANES), lambda i, *_: (i, 0))],
        out_specs=pl.BlockSpec(memory_space=pl.ANY),
        scratch_shapes=[pltpu.VMEM((MOE_ROWS * ROW_TILE, LANES), u32), pltpu.SemaphoreType.DMA(()),
                        pltpu.SemaphoreType.DMA(())],
    )
    return pl.pallas_call(
        functools.partial(_dispatch_kernel, tm=tm),
        grid_spec=grid_spec,
        out_shape=jax.ShapeDtypeStruct((p_rows * ROW_TILE, LANES), u32),
        compiler_params=_cparams(("arbitrary",)),
        name="dispatch",
    )(zero_blk, dest, h2p)


def _expert_kernel(bstart_ref, nblk_ref, tail_ref, xs_ref, w1_ref, w3_ref, w2_ref, ys_ref,
                   wf1, wf3, wf2, w1b, w3b, w2b, xbuf, ybuf, sem_in, sem_out, sem_w):
    e = pl.program_id(0)
    last = pl.num_programs(0) - 1
    ws = jnp.bitwise_and(e, 1)
    n = nblk_ref[e]
    g0 = bstart_ref[e]
    n_used = tail_ref[0]
    blk = MOE_ROWS * ROW_TILE

    def hbm_rows(ref, g):
        return ref.at[pl.ds(pl.multiple_of(g * blk, blk), blk), :]

    def in_copy(g, slot):
        return pltpu.make_async_copy(hbm_rows(xs_ref, g), xbuf.at[slot], sem_in.at[slot])

    def out_copy(g, slot):
        return pltpu.make_async_copy(ybuf.at[slot], hbm_rows(ys_ref, g), sem_out.at[slot])

    def weight_copies(ex, slot):
        return [pltpu.make_async_copy(w1_ref.at[ex], wf1.at[slot], sem_w.at[slot]),
                pltpu.make_async_copy(w3_ref.at[ex], wf3.at[slot], sem_w.at[slot]),
                pltpu.make_async_copy(w2_ref.at[ex], wf2.at[slot], sem_w.at[slot])]

    @pl.when(e == 0)
    def _():
        for cp in weight_copies(0, 0):
            cp.start(priority=1)

        @pl.when(n_used > 0)
        def _():
            in_copy(0, 0).start()

    @pl.when(e < last)
    def _():
        for cp in weight_copies(e + 1, 1 - ws):
            cp.start(priority=1)

    for cp in weight_copies(e, ws):
        cp.wait()
    w1b[...] = wf1[ws].astype(bf16)
    w3b[...] = wf3[ws].astype(bf16)
    w2b[...] = wf2[ws].astype(bf16)

    def body(b, carry):
        g = g0 + b
        slot = jnp.bitwise_and(g, 1)
        in_copy(g, slot).wait()

        @pl.when(g + 1 < n_used)
        def _():
            in_copy(g + 1, 1 - slot).start()

        @pl.when(g >= 2)
        def _():
            out_copy(g - 2, slot).wait()

        x = _load_position_major(xbuf.at[slot], MOE_ROWS).astype(bf16)
        h1 = jnp.dot(x, w1b[...], preferred_element_type=f32)
        h3 = jnp.dot(x, w3b[...], preferred_element_type=f32)
        a = (h1 * _sigmoid(h1) * h3).astype(bf16)
        _store_position_major(ybuf.at[slot], jnp.dot(a, w2b[...], preferred_element_type=f32))
        out_copy(g, slot).start()
        return carry

    lax.fori_loop(0, n, body, 0)

    @pl.when(e == last)
    def _():
        @pl.when(n_used >= 2)
        def _():
            out_copy(n_used - 2, jnp.bitwise_and(n_used, 1)).wait()

        @pl.when(n_used >= 1)
        def _():
            out_copy(n_used - 1, jnp.bitwise_and(n_used - 1, 1)).wait()

        ybuf[0] = jnp.zeros(ybuf.shape[1:], u32)
        count = tail_ref[1]

        def zero_copy(b):
            return pltpu.make_async_copy(ybuf.at[0], hbm_rows(ys_ref, n_used + b), sem_out.at[0])

        lax.fori_loop(0, count, lambda b, c: (zero_copy(b).start(), c)[1], 0)
        lax.fori_loop(0, count, lambda b, c: (zero_copy(0).wait(), c)[1], 0)


def _experts(bstart, nblk, tail, xs, w1, w3, w2):
    blk = MOE_ROWS * ROW_TILE
    n_exp, d, ff = w1.shape
    grid_spec = pltpu.PrefetchScalarGridSpec(
        num_scalar_prefetch=3,
        grid=(n_exp,),
        in_specs=[pl.BlockSpec(memory_space=pl.ANY)] * 4,
        out_specs=pl.BlockSpec(memory_space=pl.ANY),
        scratch_shapes=[pltpu.VMEM((2, d, ff), f32), pltpu.VMEM((2, d, ff), f32), pltpu.VMEM((2, ff, d), f32),
                        pltpu.VMEM((d, ff), bf16), pltpu.VMEM((d, ff), bf16), pltpu.VMEM((ff, d), bf16),
                        pltpu.VMEM((2, blk, LANES), u32), pltpu.VMEM((2, blk, LANES), u32),
                        pltpu.SemaphoreType.DMA((2,)), pltpu.SemaphoreType.DMA((2,)),
                        pltpu.SemaphoreType.DMA((2,))],
    )
    return pl.pallas_call(
        _expert_kernel,
        grid_spec=grid_spec,
        out_shape=jax.ShapeDtypeStruct(xs.shape, u32),
        compiler_params=_cparams(("arbitrary",)),
        name="experts",
    )(bstart, nblk, tail, xs, w1, w3, w2)


def _combine_kernel(dest_ref, dest_next_ref, ys_ref, rec_ref, x1_ref, gpost_ref, gate2_ref, o_ref, buf, sem,
                    *, tm):
    i = pl.program_id(0)
    slot = jnp.bitwise_and(i, 1)

    def gather(d_ref, s):
        def row_copy(d, k, r):
            return pltpu.make_async_copy(
                ys_ref.at[pl.ds(pl.multiple_of(d * ROW_TILE, ROW_TILE), ROW_TILE), :],
                buf.at[s, k, pl.ds(pl.multiple_of(r * ROW_TILE, ROW_TILE), ROW_TILE), :], sem.at[s])

        def issue(r, c):
            row_copy(d_ref[TOP_K * r], 0, r).start(priority=0)
            row_copy(d_ref[TOP_K * r + 1], 1, r).start(priority=1)
            return c

        lax.fori_loop(0, tm, issue, 0, unroll=DMA_UNROLL)

    @pl.when(i == 0)
    def _():
        gather(dest_ref, 0)

    @pl.when(i + 1 < pl.num_programs(0))
    def _():
        gather(dest_next_ref, 1 - slot)

    for k in range(TOP_K):
        pltpu.make_async_copy(ys_ref.at[pl.ds(0, tm * ROW_TILE), :], buf.at[slot, k], sem.at[slot]).wait()

    rec = rec_ref[...]
    y = (rec[:, 4:5] * _load_position_major(buf.at[slot, 0], tm)
         + rec[:, 5:6] * _load_position_major(buf.at[slot, 1], tm))
    gain = gate2_ref[...] * gpost_ref[...]
    o_ref[...] = x1_ref[...] + (y * lax.rsqrt(jnp.mean(y * y, axis=-1, keepdims=True) + EPS)) * gain


def _combine(dest, ys, rec, x1, g_post, gate2, *, seq, tm):
    t, d = x1.shape
    last = t // tm - 1
    return pl.pallas_call(
        functools.partial(_combine_kernel, tm=tm),
        grid=(t // tm,),
        in_specs=[pl.BlockSpec((TOP_K * tm,), lambda i: (i,), memory_space=pltpu.SMEM),
                  pl.BlockSpec((TOP_K * tm,), lambda i: (jnp.minimum(i + 1, last),), memory_space=pltpu.SMEM),
                  pl.BlockSpec(memory_space=pl.ANY),
                  pl.BlockSpec((tm, ROUTE_LANES), lambda i: (i, 0)),
                  pl.BlockSpec((tm, d), lambda i: (i, 0)),
                  pl.BlockSpec((1, d), lambda i: (0, 0)),
                  pl.BlockSpec((None, 1, d), lambda i: ((i * tm) // seq, 0, 0))],
        out_specs=pl.BlockSpec((tm, d), lambda i: (i, 0)),
        out_shape=jax.ShapeDtypeStruct((t, d), f32),
        scratch_shapes=[pltpu.VMEM((2, TOP_K, tm * ROW_TILE, LANES), u32), pltpu.SemaphoreType.DMA((2,))],
        compiler_params=_cparams(("arbitrary",)),
        name="combine",
    )(dest, dest, ys, rec, x1, g_post, gate2)


def _layer(x, c, w_ada, b_ada, g_pre_mix, g_post_mix, w_in, b_in, w_dw, b_dw, ln_conv_g, ln_conv_b,
           w_conv_out, b_conv_out, rel_bias, w_attn_out, w_out, g_pre_ffn, g_post_ffn,
           w_router_group, b_router_group, w_router_expert, b_router_expert, w1, w3, w2):
    batch, seq, d = x.shape
    t = batch * seq
    row = lambda v: v.reshape(1, -1)

    c_pad = jnp.zeros((SUBLANES, d), f32).at[:batch].set(c)
    mod = _ada(c_pad, w_ada, row(b_ada))[:batch]
    shift1, scale1, gate1, shift2, scale2, gate2 = [m.reshape(batch, 1, d) for m in jnp.split(mod, 6, axis=-1)]

    x2 = x.reshape(t, d)
    zt, qkv = _inproj(x2, row(g_pre_mix), scale1, shift1, w_in.astype(bf16), row(b_in), seq=seq, tm=INPROJ_TM)

    conv_act = _conv(zt.reshape(batch, seq, ZT_COLS), w_dw, row(b_dw), row(ln_conv_g), row(ln_conv_b),
                     tm=512).reshape(t, CONV_CH)

    os_, lses = [], []
    for gi, (_, dil) in enumerate(DILATED_GROUPS):
        o, lse = _attn_group(qkv, rel_bias.reshape(-1), gi, dil, batch=batch, seq=seq)
        os_.append(o)
        lses.append(lse)

    pad = ROUTE_LANES - N_GROUPS - N_EXPERTS
    w_r = jnp.concatenate([w_router_group, w_router_expert, jnp.zeros((d, pad), f32)], axis=1).astype(bf16)
    b_r = row(jnp.concatenate([b_router_group, b_router_expert.reshape(-1), jnp.zeros((pad,), f32)]))
    x1, h2p, logits = _merge(x2, conv_act, os_, lses, zt, w_conv_out.astype(bf16), row(b_conv_out),
                             w_attn_out.astype(bf16), w_out.astype(bf16), row(g_post_mix), gate1,
                             row(g_pre_ffn), scale2, shift2, w_r, b_r, seq=seq, tm=MERGE_TM, n_sub=MERGE_SUB)

    rec, rec_t, cnt = _route(logits, tm=512)

    counts = cnt[0, :N_EXPERTS].astype(jnp.int32)
    pcounts = (counts + MOE_ROWS - 1) // MOE_ROWS * MOE_ROWS
    pends = jnp.cumsum(pcounts)
    pstarts = pends - pcounts
    n_blocks = (t * TOP_K + N_EXPERTS * (MOE_ROWS - 1) + MOE_ROWS - 1) // MOE_ROWS
    n_used = (pends[-1:] // MOE_ROWS).astype(jnp.int32)
    tail_blocks = jnp.concatenate([n_used, n_blocks - n_used])
    eid = rec_t[0:TOP_K].astype(jnp.int32)
    rank = rec_t[TOP_K:2 * TOP_K].astype(jnp.int32)
    start_of = jnp.sum(jnp.where(eid[..., None] == jnp.arange(N_EXPERTS), pstarts, 0), axis=-1)
    dest = jnp.transpose(start_of + rank).reshape(t * TOP_K)
    tail = n_blocks - N_EXPERTS + jnp.arange(N_EXPERTS, dtype=jnp.int32)
    zero_blk = jnp.concatenate([jnp.where(pcounts > counts, pends // MOE_ROWS - 1, -1),
                                jnp.where(tail >= n_used[0], tail, -1)]).astype(jnp.int32)

    xs = _dispatch(zero_blk, dest, h2p, p_rows=n_blocks * MOE_ROWS, tm=512)
    ys = _experts((pstarts // MOE_ROWS).astype(jnp.int32), (pcounts // MOE_ROWS).astype(jnp.int32),
                  tail_blocks, xs, w1, w3, w2)
    out = _combine(dest, ys, rec, x1, row(g_post_ffn), gate2, seq=seq, tm=512)
    return out.reshape(batch, seq, d)


def kernel(x, c, w_ada, b_ada, g_pre_mix, g_post_mix, w_in, b_in, w_dw, b_dw, ln_conv_g, ln_conv_b,
           w_conv_out, b_conv_out, rel_bias, w_attn_out, w_out, g_pre_ffn, g_post_ffn,
           w_router_group, b_router_group, w_router_expert, b_router_expert, w1, w3, w2):
    depth = w_ada.shape[0]
    for l in range(depth):
        pick = (lambda a: a.reshape(a.shape[1:])) if depth == 1 else (lambda a, l=l: a[l])
        x = _layer(x, c, pick(w_ada), pick(b_ada), pick(g_pre_mix), pick(g_post_mix), pick(w_in),
                   pick(b_in), pick(w_dw), pick(b_dw), pick(ln_conv_g), pick(ln_conv_b),
                   pick(w_conv_out), pick(b_conv_out), rel_bias, pick(w_attn_out), pick(w_out),
                   pick(g_pre_ffn), pick(g_post_ffn), pick(w_router_group), pick(b_router_group),
                   pick(w_router_expert), pick(b_router_expert), pick(w1), pick(w3), pick(w2))
    return x
```

```python
import functools
import math

import numpy as np
import jax
import jax.numpy as jnp
from jax import lax
from jax.experimental import pallas as pl
from jax.experimental.pallas import tpu as pltpu

D_MODEL = 2048
CONV_CH = 1024
CONV_WIDTH = 31
N_ATTN_HEADS = 12
HEADS_PER_GROUP = 4
HEAD_DIM = 128
ATTN_WIDTH = N_ATTN_HEADS * HEAD_DIM
DILATED_GROUPS = ((128, 1), (512, 4), (2048, 16))
NUM_BUCKETS = 32
REL_MAX_DISTANCE = 1024
N_GROUPS = 8
EXPERTS_PER_GROUP = 8
N_EXPERTS = N_GROUPS * EXPERTS_PER_GROUP
TOP_K = 2
EXPERT_FF = 512
EPS = 1e-6
NEG_INF = -1e30

LANES = 128
SUBLANES = 8
V7X_VMEM_LIMIT_BYTES = 56 * 1024 * 1024

GROUP_W = HEADS_PER_GROUP * HEAD_DIM
ZT_COLS = 2 * D_MODEL + 2 * CONV_CH
COL_GLU = 2 * D_MODEL
INPROJ_TN = 3 * GROUP_W
ZT_TILES = ZT_COLS // INPROJ_TN
INPROJ_TM = 1024
ATTN_HALF = 64
ATTN_QB = 128
ATTN_WIN = ATTN_QB + 2 * ATTN_HALF
MOE_ROWS = 256
MERGE_TM = 256
MERGE_SUB = 2
ROUTE_LANES = 128
ROUTE_FIELDS = 8
PACKED_W = D_MODEL // 2
ROW_TILE = PACKED_W // LANES

bf16 = jnp.bfloat16
f32 = jnp.float32
u32 = jnp.uint32
HI_MASK = 0xFFFF0000


def _cparams(sem):
    return pltpu.CompilerParams(dimension_semantics=sem, vmem_limit_bytes=V7X_VMEM_LIMIT_BYTES)


def _sigmoid(x):
    return 1.0 / (1.0 + jnp.exp(-x))


def _pack_pairs(v):
    n = v.shape[1] // 2
    lo = lax.bitcast_convert_type(v[:, :n].astype(bf16).astype(f32), u32)
    hi = lax.bitcast_convert_type(v[:, n:].astype(bf16).astype(f32), u32)
    return (lo >> 16) | (hi & u32(HI_MASK))


def _unpack_lo(u):
    return lax.bitcast_convert_type(u << 16, f32)


def _unpack_hi(u):
    return lax.bitcast_convert_type(u & u32(HI_MASK), f32)


def _store_position_major(ref, v):
    packed = _pack_pairs(v)
    for j in range(ROW_TILE):
        ref[pl.ds(j, v.shape[0], stride=ROW_TILE), :] = packed[:, j * LANES:(j + 1) * LANES]


def _load_position_major(ref, n):
    chunks = [ref[pl.ds(j, n, stride=ROW_TILE), :] for j in range(ROW_TILE)]
    return jnp.concatenate([_unpack_lo(c) for c in chunks] + [_unpack_hi(c) for c in chunks], axis=1)


def _ada_kernel(c_ref, w_ref, b_ref, o_ref):
    c = c_ref[...]
    a = (c * _sigmoid(c)).astype(bf16)
    o_ref[...] = jnp.dot(a, w_ref[...].astype(bf16), preferred_element_type=f32) + b_ref[...]


def _ada(c_pad, w_ada, b_ada):
    rows, d = c_pad.shape
    n = w_ada.shape[1]
    tn = 1024
    return pl.pallas_call(
        _ada_kernel,
        grid=(n // tn,),
        in_specs=[pl.BlockSpec((rows, d), lambda j: (0, 0)),
                  pl.BlockSpec((d, tn), lambda j: (0, j)),
                  pl.BlockSpec((1, tn), lambda j: (0, j))],
        out_specs=pl.BlockSpec((rows, tn), lambda j: (0, j)),
        out_shape=jax.ShapeDtypeStruct((rows, n), f32),
        compiler_params=_cparams(("arbitrary",)),
        name="ada_mod",
    )(c_pad, w_ada, b_ada)


def _inproj_kernel(x_ref, g_ref, scale_ref, shift_ref, wa_ref, wb_ref, wc_ref, ba_ref, bb_ref, bc_ref,
                   zt_ref, qkv_ref, h_scr, slab_scr):
    j = pl.program_id(1)
    tm = x_ref.shape[0]

    @pl.when(j == 0)
    def _():
        x = x_ref[...]
        ms = jnp.mean(x * x, axis=-1, keepdims=True)
        h = x * lax.rsqrt(ms + EPS) * g_ref[...]
        h_scr[...] = (h * (1.0 + scale_ref[...]) + shift_ref[...]).astype(bf16)

    w_refs, b_refs = (wa_ref, wb_ref, wc_ref), (ba_ref, bb_ref, bc_ref)

    def chunk(c):
        return jnp.dot(h_scr[...], w_refs[c][...], preferred_element_type=f32) + b_refs[c][...]

    @pl.when(j < ZT_TILES)
    def _():
        for c in range(INPROJ_TN // GROUP_W):
            zt_ref[:, c * GROUP_W:(c + 1) * GROUP_W] = chunk(c).astype(bf16)

    for gi, (_, dil) in enumerate(DILATED_GROUPS):
        @pl.when(j == ZT_TILES + gi)
        def _(dil=dil):
            per = tm // dil
            for c in range(INPROJ_TN // GROUP_W):
                z = chunk(c)
                if dil == 1:
                    qkv_ref[:, c * GROUP_W:(c + 1) * GROUP_W] = z.astype(bf16)
                    continue
                slabs = range(GROUP_W // LANES)
                for s in slabs:
                    slab_scr[0, s] = z[:, s * LANES:(s + 1) * LANES]
                src, stride, first = 0, dil, lambda r: r
                if dil == 16:
                    quarter = tm // 4
                    for s in slabs:
                        for r4 in range(4):
                            slab_scr[1, s, r4 * quarter:(r4 + 1) * quarter, :] = (
                                slab_scr[0, s, pl.ds(r4, quarter, stride=4), :])
                    src, stride, first = 1, 4, lambda r: (r % 4) * quarter + r // 4
                for r in range(dil):
                    for s in slabs:
                        col = c * GROUP_W + s * LANES
                        qkv_ref[r * per:(r + 1) * per, col:col + LANES] = (
                            slab_scr[src, s, pl.ds(first(r), per, stride=stride), :].astype(bf16))


def _inproj_col_block(j, c):
    n_blocks = (2 * CONV_CH + 3 * ATTN_WIDTH + 2 * D_MODEL) // GROUP_W
    gate0 = (2 * CONV_CH + 3 * ATTN_WIDTH) // GROUP_W
    q0 = 2 * CONV_CH // GROUP_W
    per_proj = ATTN_WIDTH // GROUP_W
    token_major = lax.rem(gate0 + (INPROJ_TN // GROUP_W) * j + c, n_blocks)
    return jnp.where(j < ZT_TILES, token_major, q0 + (j - ZT_TILES) + per_proj * c)


def _inproj(x2, g, scale, shift, w, b, *, seq, tm):
    t, d = x2.shape
    n = w.shape[1]
    tn = INPROJ_TN
    n_groups = n // tn - ZT_TILES
    bmap = lambda i, j: ((i * tm) // seq, 0, 0)
    chunks = range(tn // GROUP_W)
    w_specs = [pl.BlockSpec((d, GROUP_W), lambda i, j, c=c: (0, _inproj_col_block(j, c))) for c in chunks]
    b_specs = [pl.BlockSpec((1, GROUP_W), lambda i, j, c=c: (0, _inproj_col_block(j, c))) for c in chunks]
    return pl.pallas_call(
        _inproj_kernel,
        grid=(t // tm, n // tn),
        in_specs=[pl.BlockSpec((tm, d), lambda i, j: (i, 0)),
                  pl.BlockSpec((1, d), lambda i, j: (0, 0)),
                  pl.BlockSpec((None, 1, d), bmap),
                  pl.BlockSpec((None, 1, d), bmap)] + w_specs + b_specs,
        out_specs=[pl.BlockSpec((tm, tn), lambda i, j: (i, jnp.minimum(j, ZT_TILES - 1))),
                   pl.BlockSpec((None, tm, tn), lambda i, j: (jnp.maximum(j - ZT_TILES, 0), i, 0))],
        out_shape=[jax.ShapeDtypeStruct((t, ZT_COLS), bf16),
                   jax.ShapeDtypeStruct((n_groups, t, tn), bf16)],
        scratch_shapes=[pltpu.VMEM((tm, d), bf16), pltpu.VMEM((2, GROUP_W // LANES, tm, LANES), f32)],
        compiler_params=_cparams(("arbitrary", "arbitrary")),
        name="inproj",
    )(x2, g, scale, shift, w, w, w, b, b, b)


CONV_HALO = 16
CONV_CHUNK = 32
CONV_SUB = CONV_CH // LANES
NORM_CHUNK = 64


def _conv_kernel(ap_ref, ac_ref, an_ref, gp_ref, gc_ref, gn_ref, w_ref, bdw_ref, lng_ref, lnb_ref,
                 o_ref, u_scr, y_scr, *, tm):
    i = pl.program_id(1)
    last = pl.num_programs(1) - 1

    def glu(a_ref, g_ref):
        return a_ref[...].astype(f32) * _sigmoid(g_ref[...].astype(f32))

    def put(row0, val):
        for j in range(CONV_SUB):
            u_scr[pl.ds(row0 * CONV_SUB + j, val.shape[0], stride=CONV_SUB), :] = (
                val[:, j * LANES:(j + 1) * LANES])

    put(0, jnp.where(i > 0, glu(ap_ref, gp_ref), 0.0))
    put(CONV_HALO, glu(ac_ref, gc_ref))
    put(CONV_HALO + tm, jnp.where(i < last, glu(an_ref, gn_ref), 0.0))

    def chunk(c, carry):
        r0 = c * CONV_CHUNK
        acc = jnp.zeros((CONV_CHUNK, CONV_SUB, LANES), f32) + bdw_ref[...][None]
        for k in range(CONV_WIDTH):
            start = pl.multiple_of((r0 + k + 1) * CONV_SUB, CONV_SUB)
            xk = u_scr[pl.ds(start, CONV_CHUNK * CONV_SUB), :].reshape(CONV_CHUNK, CONV_SUB, LANES)
            acc = acc + w_ref[k][None] * xk
        out0 = pl.multiple_of(r0 * CONV_SUB, CONV_CHUNK * CONV_SUB)
        y_scr[pl.ds(out0, CONV_CHUNK * CONV_SUB), :] = acc.reshape(CONV_CHUNK * CONV_SUB, LANES)
        return carry

    lax.fori_loop(0, tm // CONV_CHUNK, chunk, 0)

    def norm(c, carry):
        r0 = pl.multiple_of(c * NORM_CHUNK, NORM_CHUNK)
        acc = jnp.concatenate(
            [y_scr[pl.ds(r0 * CONV_SUB + j, NORM_CHUNK, stride=CONV_SUB), :] for j in range(CONV_SUB)],
            axis=1)
        mu = jnp.mean(acc, axis=-1, keepdims=True)
        cen = acc - mu
        var = jnp.mean(cen * cen, axis=-1, keepdims=True)
        y = cen * lax.rsqrt(var + EPS) * lng_ref[...] + lnb_ref[...]
        o_ref[pl.ds(r0, NORM_CHUNK), :] = (y * _sigmoid(y)).astype(bf16)
        return carry

    lax.fori_loop(0, tm // NORM_CHUNK, norm, 0, unroll=2)


def _conv(z3, w_dw, b_dw, ln_g, ln_b, *, tm):
    b, s, _ = z3.shape
    hb = tm // CONV_HALO
    nhb = s // CONV_HALO
    ca, cg = COL_GLU // CONV_CH, COL_GLU // CONV_CH + 1

    def halo(col, which):
        if which < 0:
            return pl.BlockSpec((None, CONV_HALO, CONV_CH),
                                lambda bb, i: (bb, jnp.maximum(i * hb - 1, 0), col))
        return pl.BlockSpec((None, CONV_HALO, CONV_CH),
                            lambda bb, i: (bb, jnp.minimum((i + 1) * hb, nhb - 1), col))

    cur = lambda col: pl.BlockSpec((None, tm, CONV_CH), lambda bb, i: (bb, i, col))
    vec = pl.BlockSpec((1, CONV_CH), lambda bb, i: (0, 0))
    return pl.pallas_call(
        functools.partial(_conv_kernel, tm=tm),
        grid=(b, s // tm),
        in_specs=[halo(ca, -1), cur(ca), halo(ca, 1), halo(cg, -1), cur(cg), halo(cg, 1),
                  pl.BlockSpec((CONV_WIDTH, CONV_SUB, LANES), lambda bb, i: (0, 0, 0)),
                  pl.BlockSpec((CONV_SUB, LANES), lambda bb, i: (0, 0)), vec, vec],
        out_specs=pl.BlockSpec((None, tm, CONV_CH), lambda bb, i: (bb, i, 0)),
        out_shape=jax.ShapeDtypeStruct((b, s, CONV_CH), bf16),
        scratch_shapes=[pltpu.VMEM(((tm + 2 * CONV_HALO) * CONV_SUB, LANES), f32),
                        pltpu.VMEM((tm * CONV_SUB, LANES), f32)],
        compiler_params=_cparams(("arbitrary", "arbitrary")),
        name="conv_branch",
    )(z3, z3, z3, z3, z3, z3, w_dw.reshape(CONV_WIDTH, CONV_SUB, LANES), b_dw.reshape(CONV_SUB, LANES),
      ln_g, ln_b)


def _t5_bucket_table(dil):
    delta = (np.arange(ATTN_WIN)[None, :] - ATTN_HALF) - np.arange(ATTN_QB)[:, None]
    rel = delta * dil
    nb = NUM_BUCKETS // 2
    max_exact = nb // 2
    n = np.abs(rel)
    nf = np.maximum(n, 1).astype(np.float32)
    large = max_exact + (np.log(nf / np.float32(max_exact)) / np.float32(math.log(REL_MAX_DISTANCE / max_exact))
                         * np.float32(nb - max_exact)).astype(np.int32)
    large = np.minimum(large, nb - 1)
    bucket = np.where(rel > 0, nb, 0) + np.where(n < max_exact, n, large)
    return np.where(np.abs(delta) <= ATTN_HALF, bucket, -1).astype(np.int32)


def _attn_kernel(rb_ref, bkt_ref, q_ref, kp_ref, kc_ref, kn_ref, vp_ref, vc_ref, vn_ref, o_ref, lse_ref,
                 q_scr, k_scr, v_scr, bias_scr, *, gi, dil, n_tiles, sub_len):
    i = pl.program_id(1)
    h = ATTN_HALF
    tile = INPROJ_TM
    per = tile // dil
    tq = n_tiles * per

    @pl.when((pl.program_id(0) == 0) & (i == 0))
    def _():
        bk = bkt_ref[...]
        for hh in range(HEADS_PER_GROUP):
            acc = jnp.full((ATTN_QB, ATTN_WIN), NEG_INF, f32)
            for b_id in range(NUM_BUCKETS):
                acc = jnp.where(bk == b_id, rb_ref[b_id * N_ATTN_HEADS + gi * HEADS_PER_GROUP + hh], acc)
            bias_scr[hh] = acc

    halo_rows = kp_ref.shape[0]
    for r in range(dil):
        prev_lo = (r + 1) * per - h if halo_rows == tile else 0
        next_lo = r * per if halo_rows == tile else 0
        for dst, prv, cur, nxt in ((k_scr, kp_ref, kc_ref, kn_ref), (v_scr, vp_ref, vc_ref, vn_ref)):
            dst[r, 0:h, :] = prv[prev_lo:prev_lo + h, :]
            for t in range(n_tiles):
                dst[r, h + t * per:h + (t + 1) * per, :] = cur[t * tile + r * per:t * tile + (r + 1) * per, :]
            dst[r, h + tq:, :] = nxt[next_lo:next_lo + h, :]
        for t in range(n_tiles):
            q_scr[r, t * per:(t + 1) * per, :] = q_ref[t * tile + r * per:t * tile + (r + 1) * per, :]

    scale = HEAD_DIM ** -0.5
    col = lax.broadcasted_iota(jnp.int32, (ATTN_QB, ATTN_WIN), 1)
    lane = lax.broadcasted_iota(jnp.int32, (ATTN_QB, LANES), 1)

    def rows(start, n):
        if dil == 1:
            return pl.ds(pl.multiple_of(start, ATTN_QB), n)
        return pl.ds(start, n, stride=dil)

    def unit(u, carry):
        r = jnp.bitwise_and(u, dil - 1)
        jb = lax.shift_right_logical(u, jnp.int32(dil.bit_length() - 1))
        start = jb * (ATTN_QB * dil) + r
        p0 = pl.multiple_of(jb * ATTN_QB, ATTN_QB)
        cols = [slice(hh * HEAD_DIM, (hh + 1) * HEAD_DIM) for hh in range(HEADS_PER_GROUP)]
        q = [q_scr[r, pl.ds(p0, ATTN_QB), cs] for cs in cols]
        kw = [k_scr[r, pl.ds(p0, ATTN_WIN), cs] for cs in cols]
        vw = [v_scr[r, pl.ds(p0, ATTN_WIN), cs] for cs in cols]
        key0 = i * tq + jb * ATTN_QB - ATTN_HALF
        valid = (col + key0 >= 0) & (col + key0 < sub_len)
        s = [lax.dot_general(q[hh], kw[hh], (((1,), (1,)), ((), ())), preferred_element_type=f32)
             for hh in range(HEADS_PER_GROUP)]
        s = [jnp.where(valid, s[hh] * scale + bias_scr[hh], NEG_INF) for hh in range(HEADS_PER_GROUP)]
        m = [jnp.max(x, axis=-1, keepdims=True) for x in s]
        p = [jnp.exp(s[hh] - m[hh]) for hh in range(HEADS_PER_GROUP)]
        den = [jnp.sum(x, axis=-1, keepdims=True) for x in p]
        pv = [jnp.dot(p[hh].astype(bf16), vw[hh], preferred_element_type=f32)
              for hh in range(HEADS_PER_GROUP)]
        lse_tile = jnp.zeros((ATTN_QB, LANES), f32)
        for hh in range(HEADS_PER_GROUP):
            o_ref[hh, rows(start, ATTN_QB), :] = pv[hh] / den[hh]
            lse_tile = jnp.where(lane == hh, m[hh] + jnp.log(den[hh]), lse_tile)
        lse_ref[rows(start, ATTN_QB), :] = lse_tile
        return carry

    lax.fori_loop(0, n_tiles * tile // ATTN_QB, unit, 0, unroll=4)


def _attn_group(qkv, rel_bias_flat, gi, dil, *, batch, seq):
    t = batch * seq
    tile = INPROJ_TM
    n_tiles = max(1, ATTN_QB * dil // tile)
    ts = n_tiles * tile
    steps = seq // ts
    tq = ts // dil
    halo = ATTN_HALF if dil == 1 else tile
    hpt = ts // halo
    hps = seq // halo

    def cur(which):
        return pl.BlockSpec((None, ts, GROUP_W), lambda b, i: (gi, b * steps + i, which))

    def prev(which):
        return pl.BlockSpec((None, halo, GROUP_W),
                            lambda b, i: (gi, b * hps + jnp.maximum(i * hpt - 1, 0), which))

    def nxt(which):
        return pl.BlockSpec((None, halo, GROUP_W),
                            lambda b, i: (gi, b * hps + jnp.minimum((i + 1) * hpt, hps - 1), which))

    bkt = jnp.asarray(_t5_bucket_table(dil))
    return pl.pallas_call(
        functools.partial(_attn_kernel, gi=gi, dil=dil, n_tiles=n_tiles, sub_len=seq // dil),
        grid=(batch, steps),
        in_specs=[pl.BlockSpec(memory_space=pltpu.SMEM),
                  pl.BlockSpec((ATTN_QB, ATTN_WIN), lambda b, i: (0, 0)),
                  cur(0), prev(1), cur(1), nxt(1), prev(2), cur(2), nxt(2)],
        out_specs=[pl.BlockSpec((HEADS_PER_GROUP, ts, LANES), lambda b, i: (0, b * steps + i, 0)),
                   pl.BlockSpec((ts, LANES), lambda b, i: (b * steps + i, 0))],
        out_shape=[jax.ShapeDtypeStruct((HEADS_PER_GROUP, t, LANES), f32),
                   jax.ShapeDtypeStruct((t, LANES), f32)],
        scratch_shapes=[pltpu.VMEM((dil, tq, GROUP_W), bf16),
                        pltpu.VMEM((dil, tq + 2 * ATTN_HALF, GROUP_W), bf16),
                        pltpu.VMEM((dil, tq + 2 * ATTN_HALF, GROUP_W), bf16),
                        pltpu.VMEM((HEADS_PER_GROUP, ATTN_QB, ATTN_WIN), f32)],
        compiler_params=_cparams(("arbitrary", "arbitrary")),
        name=f"attn_g{gi}",
    )(rel_bias_flat, bkt, qkv, qkv, qkv, qkv, qkv, qkv, qkv)


def _merge_kernel(x_ref, ca_ref, o0_ref, o1_ref, o2_ref, l0_ref, l1_ref, l2_ref, ga_ref, gb_ref,
                  wco_ref, bco_ref, wao_ref, wo_ref, gpost_ref, gate1_ref, gpre_ref, scale2_ref,
                  shift2_ref, wr_ref, br_ref, x1_ref, h2p_ref, logit_ref, *, n_sub):
    hm = x_ref.shape[0] // n_sub
    subs = [slice(k * hm, (k + 1) * hm) for k in range(n_sub)]

    branch_a = [jnp.dot(ca_ref[rs, :], wco_ref[...], preferred_element_type=f32) + bco_ref[...] for rs in subs]

    def attention_mix(rs):
        l0, l1, l2 = l0_ref[rs, :], l1_ref[rs, :], l2_ref[rs, :]
        lmax = jnp.maximum(jnp.maximum(l0, l1), l2)
        e0, e1, e2 = jnp.exp(l0 - lmax), jnp.exp(l1 - lmax), jnp.exp(l2 - lmax)
        inv = 1.0 / (e0 + e1 + e2)
        parts = []
        for hh in range(HEADS_PER_GROUP):
            ls = slice(hh, hh + 1)
            parts.append((e0[:, ls] * inv[:, ls]) * o0_ref[hh, rs, :] + (e1[:, ls] * inv[:, ls]) * o1_ref[hh, rs, :]
                         + (e2[:, ls] * inv[:, ls]) * o2_ref[hh, rs, :])
        return jnp.concatenate(parts, axis=1).astype(bf16)

    att = [attention_mix(rs) for rs in subs]
    branch_b = [jnp.dot(a, wao_ref[...], preferred_element_type=f32) for a in att]
    mix = [(_sigmoid(ga_ref[rs, :].astype(f32)) * branch_a[k]
            + _sigmoid(gb_ref[rs, :].astype(f32)) * branch_b[k]).astype(bf16) for k, rs in enumerate(subs)]
    y = [jnp.dot(m, wo_ref[...], preferred_element_type=f32) for m in mix]

    post_gain = gate1_ref[...] * gpost_ref[...]
    pre_gain = gpre_ref[...] * (1.0 + scale2_ref[...])
    h2s = []
    for k, rs in enumerate(subs):
        x1 = x_ref[rs, :] + (y[k] * lax.rsqrt(jnp.mean(y[k] * y[k], axis=-1, keepdims=True) + EPS)) * post_gain
        x1_ref[rs, :] = x1
        h2 = (x1 * lax.rsqrt(jnp.mean(x1 * x1, axis=-1, keepdims=True) + EPS)) * pre_gain + shift2_ref[...]
        _store_position_major(h2p_ref.at[pl.ds(k * hm * ROW_TILE, hm * ROW_TILE), :], h2)
        h2s.append(h2.astype(bf16))
    for k, rs in enumerate(subs):
        logit_ref[rs, :] = jnp.dot(h2s[k], wr_ref[...], preferred_element_type=f32) + br_ref[...]


def _merge(x2, conv_act, os_, lses, zt, w_co, b_co, w_ao, w_o, g_post, gate1, g_pre, scale2, shift2,
           w_r, b_r, *, seq, tm, n_sub):
    t, d = x2.shape
    rows = lambda w: pl.BlockSpec((tm, w), lambda i: (i, 0))
    heads = pl.BlockSpec((HEADS_PER_GROUP, tm, LANES), lambda i: (0, i, 0))
    full = lambda a: pl.BlockSpec(a.shape, lambda i: (0,) * a.ndim)
    perb = pl.BlockSpec((None, 1, d), lambda i: ((i * tm) // seq, 0, 0))
    return pl.pallas_call(
        functools.partial(_merge_kernel, n_sub=n_sub),
        grid=(t // tm,),
        in_specs=[rows(d), rows(CONV_CH), heads, heads, heads, rows(LANES), rows(LANES), rows(LANES),
                  pl.BlockSpec((tm, d), lambda i: (i, 0)), pl.BlockSpec((tm, d), lambda i: (i, 1)),
                  full(w_co), full(b_co), full(w_ao), full(w_o), full(g_post), perb, full(g_pre),
                  perb, perb, full(w_r), full(b_r)],
        out_specs=[rows(d), pl.BlockSpec((tm * ROW_TILE, LANES), lambda i: (i, 0)), rows(ROUTE_LANES)],
        out_shape=[jax.ShapeDtypeStruct((t, d), f32),
                   jax.ShapeDtypeStruct((t * ROW_TILE, LANES), u32),
                   jax.ShapeDtypeStruct((t, ROUTE_LANES), f32)],
        compiler_params=_cparams(("arbitrary",)),
        name="merge",
    )(x2, conv_act, *os_, *lses, zt, zt, w_co, b_co, w_ao, w_o, g_post, gate1, g_pre, scale2, shift2,
      w_r, b_r)


def _route_kernel(logit_ref, rec_ref, rect_ref, cnt_ref, carry):
    i = pl.program_id(0)

    @pl.when(i == 0)
    def _():
        carry[...] = jnp.zeros_like(carry)

    lg = logit_ref[...]
    tm = lg.shape[0]
    lane = lax.broadcasted_iota(jnp.int32, lg.shape, 1).astype(f32)
    big = float(2 * ROUTE_LANES)

    def first_max(mask):
        v = jnp.max(jnp.where(mask, lg, -jnp.inf), axis=-1, keepdims=True)
        idx = jnp.min(jnp.where(mask & (lg == v), lane, big), axis=-1, keepdims=True)
        return v, idx

    gmask = lane < N_GROUPS
    gmax, gsel = first_max(gmask)
    p_g = 1.0 / jnp.sum(jnp.where(gmask, jnp.exp(lg - gmax), 0.0), axis=-1, keepdims=True)
    e_lo = N_GROUPS + EXPERTS_PER_GROUP * gsel
    emask = (lane >= e_lo) & (lane < e_lo + EXPERTS_PER_GROUP)
    v0, i0 = first_max(emask)
    v1, i1 = first_max(emask & (lane != i0))
    t1 = jnp.exp(v1 - v0)
    w0 = p_g / (1.0 + t1)
    w1 = p_g * t1 / (1.0 + t1)
    e0 = i0 - N_GROUPS
    e1 = i1 - N_GROUPS

    hit0 = lane == e0
    hit1 = lane == e1
    onehot = jnp.where(hit0 | hit1, 1.0, 0.0)
    r_i = lax.broadcasted_iota(jnp.int32, (tm, tm), 0)
    c_i = lax.broadcasted_iota(jnp.int32, (tm, tm), 1)
    tril = jnp.where(c_i < r_i, 1.0, 0.0).astype(bf16)
    before = jnp.dot(tril, onehot.astype(bf16), preferred_element_type=f32) + carry[...]
    rank0 = jnp.sum(jnp.where(hit0, before, 0.0), axis=-1, keepdims=True)
    rank1 = jnp.sum(jnp.where(hit1, before, 0.0), axis=-1, keepdims=True)
    carry[...] = carry[...] + jnp.sum(onehot, axis=0, keepdims=True)
    cnt_ref[...] = carry[...]

    rec = jnp.zeros_like(lg)
    for slot, val in enumerate((e0, e1, rank0, rank1, w0, w1)):
        rec = jnp.where(lane == slot, val, rec)
    rec_ref[...] = rec
    rect_ref[...] = jnp.transpose(rec)[0:ROUTE_FIELDS, :]


def _route(logits, *, tm):
    t = logits.shape[0]
    return pl.pallas_call(
        _route_kernel,
        grid=(t // tm,),
        in_specs=[pl.BlockSpec((tm, ROUTE_LANES), lambda i: (i, 0))],
        out_specs=[pl.BlockSpec((tm, ROUTE_LANES), lambda i: (i, 0)),
                   pl.BlockSpec((ROUTE_FIELDS, tm), lambda i: (0, i)),
                   pl.BlockSpec((1, ROUTE_LANES), lambda i: (0, 0))],
        out_shape=[jax.ShapeDtypeStruct((t, ROUTE_LANES), f32),
                   jax.ShapeDtypeStruct((ROUTE_FIELDS, t), f32),
                   jax.ShapeDtypeStruct((1, ROUTE_LANES), f32)],
        scratch_shapes=[pltpu.VMEM((1, ROUTE_LANES), f32)],
        compiler_params=_cparams(("arbitrary",)),
        name="route",
    )(logits)


DMA_UNROLL = 8


def _dispatch_kernel(zero_blk_ref, dest_ref, h_ref, xs_ref, zero_scr, sem, zsem, *, tm):
    def row_copy(r, d):
        return pltpu.make_async_copy(h_ref.at[pl.ds(pl.multiple_of(r * ROW_TILE, ROW_TILE), ROW_TILE), :],
                                     xs_ref.at[pl.ds(pl.multiple_of(d * ROW_TILE, ROW_TILE), ROW_TILE), :],
                                     sem)

    def zero_copy(blk):
        n = MOE_ROWS * ROW_TILE
        return pltpu.make_async_copy(zero_scr, xs_ref.at[pl.ds(pl.multiple_of(blk * n, n), n), :], zsem)

    @pl.when(pl.program_id(0) == 0)
    def _():
        zero_scr[...] = jnp.zeros_like(zero_scr)
        n_cand = zero_blk_ref.shape[0]

        def start(j, c):
            @pl.when(zero_blk_ref[j] >= 0)
            def _():
                zero_copy(zero_blk_ref[j]).start()
            return c

        def wait(j, c):
            @pl.when(zero_blk_ref[j] >= 0)
            def _():
                zero_copy(0).wait()
            return c

        lax.fori_loop(0, n_cand, start, 0)
        lax.fori_loop(0, n_cand, wait, 0)

    def issue(r, c):
        row_copy(r, dest_ref[TOP_K * r]).start(priority=0)
        row_copy(r, dest_ref[TOP_K * r + 1]).start(priority=1)
        return c

    lax.fori_loop(0, tm, issue, 0, unroll=DMA_UNROLL)
    for _ in range(TOP_K):
        pltpu.make_async_copy(h_ref, xs_ref.at[pl.ds(0, tm * ROW_TILE), :], sem).wait()


def _dispatch(zero_blk, dest, h2p, *, p_rows, tm):
    grid_spec = pltpu.PrefetchScalarGridSpec(
        num_scalar_prefetch=1,
        grid=(h2p.shape[0] // (tm * ROW_TILE),),
        in_specs=[pl.BlockSpec((TOP_K * tm,), lambda i, *_: (i,), memory_space=pltpu.SMEM),
                  pl.BlockSpec((tm * ROW_TILE, LANES), lambda i, *_: (i, 0))],
        out_specs=pl.BlockSpec(memory_space=pl.ANY),
        scratch_shapes=[pltpu.VMEM((MOE_ROWS * ROW_TILE, LANES), u32), pltpu.SemaphoreType.DMA(()),
                        pltpu.SemaphoreType.DMA(())],
    )
    return pl.pallas_call(
        functools.partial(_dispatch_kernel, tm=tm),
        grid_spec=grid_spec,
        out_shape=jax.ShapeDtypeStruct((p_rows * ROW_TILE, LANES), u32),
        compiler_params=_cparams(("arbitrary",)),
        name="dispatch",
    )(zero_blk, dest, h2p)


def _expert_kernel(bstart_ref, nblk_ref, tail_ref, cnt_ref, xs_ref, w1_ref, w3_ref, w2_ref, ys_ref,
                   wf1, wf3, wf2, w1b, w3b, w2b, xbuf, ybuf, sem_in, sem_out, sem_w):
    e = pl.program_id(0)
    last = pl.num_programs(0) - 1
    ws = jnp.bitwise_and(e, 1)
    n = nblk_ref[e]
    g0 = bstart_ref[e]
    n_used = tail_ref[0]
    blk = MOE_ROWS * ROW_TILE

    def hbm_rows(ref, g):
        return ref.at[pl.ds(pl.multiple_of(g * blk, blk), blk), :]

    def in_copy(g, slot):
        return pltpu.make_async_copy(hbm_rows(xs_ref, g), xbuf.at[slot], sem_in.at[slot])

    def out_copy(g, slot):
        return pltpu.make_async_copy(ybuf.at[slot], hbm_rows(ys_ref, g), sem_out.at[slot])

    def weight_copies(ex, slot):
        return [pltpu.make_async_copy(w1_ref.at[ex], wf1.at[slot], sem_w.at[slot]),
                pltpu.make_async_copy(w3_ref.at[ex], wf3.at[slot], sem_w.at[slot]),
                pltpu.make_async_copy(w2_ref.at[ex], wf2.at[slot], sem_w.at[slot])]

    @pl.when(e == 0)
    def _():
        for cp in weight_copies(0, 0):
            cp.start(priority=1)

        @pl.when(n_used > 0)
        def _():
            in_copy(0, 0).start()

    @pl.when(e < last)
    def _():
        for cp in weight_copies(e + 1, 1 - ws):
            cp.start(priority=1)

    for cp in weight_copies(e, ws):
        cp.wait()
    w1b[...] = wf1[ws].astype(bf16)
    w3b[...] = wf3[ws].astype(bf16)
    w2b[...] = wf2[ws].astype(bf16)

    def body(b, carry):
        g = g0 + b
        slot = jnp.bitwise_and(g, 1)
        in_copy(g, slot).wait()

        @pl.when(g + 1 < n_used)
        def _():
            in_copy(g + 1, 1 - slot).start()

        @pl.when(g >= 2)
        def _():
            out_copy(g - 2, slot).wait()

        def swiglu(rows):
            x = _load_position_major(xbuf.at[slot, pl.ds(0, rows * ROW_TILE)], rows).astype(bf16)
            h1 = jnp.dot(x, w1b[...], preferred_element_type=f32)
            h3 = jnp.dot(x, w3b[...], preferred_element_type=f32)
            a = (h1 * _sigmoid(h1) * h3).astype(bf16)
            _store_position_major(ybuf.at[slot, pl.ds(0, rows * ROW_TILE)],
                                  jnp.dot(a, w2b[...], preferred_element_type=f32))

        half = MOE_ROWS // 2
        real_rows = cnt_ref[e] - b * MOE_ROWS

        @pl.when(real_rows > half)
        def _():
            swiglu(MOE_ROWS)

        @pl.when(real_rows <= half)
        def _():
            swiglu(half)
            ybuf[slot, half * ROW_TILE:, :] = jnp.zeros((half * ROW_TILE, LANES), u32)

        out_copy(g, slot).start()
        return carry

    lax.fori_loop(0, n, body, 0)

    @pl.when(e == last)
    def _():
        @pl.when(n_used >= 2)
        def _():
            out_copy(n_used - 2, jnp.bitwise_and(n_used, 1)).wait()

        @pl.when(n_used >= 1)
        def _():
            out_copy(n_used - 1, jnp.bitwise_and(n_used - 1, 1)).wait()

        ybuf[0] = jnp.zeros(ybuf.shape[1:], u32)
        count = tail_ref[1]

        def zero_copy(b):
            return pltpu.make_async_copy(ybuf.at[0], hbm_rows(ys_ref, n_used + b), sem_out.at[0])

        lax.fori_loop(0, count, lambda b, c: (zero_copy(b).start(), c)[1], 0)
        lax.fori_loop(0, count, lambda b, c: (zero_copy(0).wait(), c)[1], 0)


def _experts(bstart, nblk, tail, cnt, xs, w1, w3, w2):
    blk = MOE_ROWS * ROW_TILE
    n_exp, d, ff = w1.shape
    grid_spec = pltpu.PrefetchScalarGridSpec(
        num_scalar_prefetch=4,
        grid=(n_exp,),
        in_specs=[pl.BlockSpec(memory_space=pl.ANY)] * 4,
        out_specs=pl.BlockSpec(memory_space=pl.ANY),
        scratch_shapes=[pltpu.VMEM((2, d, ff), f32), pltpu.VMEM((2, d, ff), f32), pltpu.VMEM((2, ff, d), f32),
                        pltpu.VMEM((d, ff), bf16), pltpu.VMEM((d, ff), bf16), pltpu.VMEM((ff, d), bf16),
                        pltpu.VMEM((2, blk, LANES), u32), pltpu.VMEM((2, blk, LANES), u32),
                        pltpu.SemaphoreType.DMA((2,)), pltpu.SemaphoreType.DMA((2,)),
                        pltpu.SemaphoreType.DMA((2,))],
    )
    return pl.pallas_call(
        _expert_kernel,
        grid_spec=grid_spec,
        out_shape=jax.ShapeDtypeStruct(xs.shape, u32),
        compiler_params=_cparams(("arbitrary",)),
        name="experts",
    )(bstart, nblk, tail, cnt, xs, w1, w3, w2)


def _combine_kernel(dest_ref, dest_next_ref, ys_ref, rec_ref, x1_ref, gpost_ref, gate2_ref, o_ref, buf, sem,
                    *, tm):
    i = pl.program_id(0)
    slot = jnp.bitwise_and(i, 1)

    def gather(d_ref, s):
        def row_copy(d, k, r):
            return pltpu.make_async_copy(
                ys_ref.at[pl.ds(pl.multiple_of(d * ROW_TILE, ROW_TILE), ROW_TILE), :],
                buf.at[s, k, pl.ds(pl.multiple_of(r * ROW_TILE, ROW_TILE), ROW_TILE), :], sem.at[s])

        def issue(r, c):
            row_copy(d_ref[TOP_K * r], 0, r).start(priority=0)
            row_copy(d_ref[TOP_K * r + 1], 1, r).start(priority=1)
            return c

        lax.fori_loop(0, tm, issue, 0, unroll=DMA_UNROLL)

    @pl.when(i == 0)
    def _():
        gather(dest_ref, 0)

    @pl.when(i + 1 < pl.num_programs(0))
    def _():
        gather(dest_next_ref, 1 - slot)

    for k in range(TOP_K):
        pltpu.make_async_copy(ys_ref.at[pl.ds(0, tm * ROW_TILE), :], buf.at[slot, k], sem.at[slot]).wait()

    rec = rec_ref[...]
    y = (rec[:, 4:5] * _load_position_major(buf.at[slot, 0], tm)
         + rec[:, 5:6] * _load_position_major(buf.at[slot, 1], tm))
    gain = gate2_ref[...] * gpost_ref[...]
    o_ref[...] = x1_ref[...] + (y * lax.rsqrt(jnp.mean(y * y, axis=-1, keepdims=True) + EPS)) * gain


def _combine(dest, ys, rec, x1, g_post, gate2, *, seq, tm):
    t, d = x1.shape
    last = t // tm - 1
    return pl.pallas_call(
        functools.partial(_combine_kernel, tm=tm),
        grid=(t // tm,),
        in_specs=[pl.BlockSpec((TOP_K * tm,), lambda i: (i,), memory_space=pltpu.SMEM),
                  pl.BlockSpec((TOP_K * tm,), lambda i: (jnp.minimum(i + 1, last),), memory_space=pltpu.SMEM),
                  pl.BlockSpec(memory_space=pl.ANY),
                  pl.BlockSpec((tm, ROUTE_LANES), lambda i: (i, 0)),
                  pl.BlockSpec((tm, d), lambda i: (i, 0)),
                  pl.BlockSpec((1, d), lambda i: (0, 0)),
                  pl.BlockSpec((None, 1, d), lambda i: ((i * tm) // seq, 0, 0))],
        out_specs=pl.BlockSpec((tm, d), lambda i: (i, 0)),
        out_shape=jax.ShapeDtypeStruct((t, d), f32),
        scratch_shapes=[pltpu.VMEM((2, TOP_K, tm * ROW_TILE, LANES), u32), pltpu.SemaphoreType.DMA((2,))],
        compiler_params=_cparams(("arbitrary",)),
        name="combine",
    )(dest, dest, ys, rec, x1, g_post, gate2)


def _layer(x, c, w_ada, b_ada, g_pre_mix, g_post_mix, w_in, b_in, w_dw, b_dw, ln_conv_g, ln_conv_b,
           w_conv_out, b_conv_out, rel_bias, w_attn_out, w_out, g_pre_ffn, g_post_ffn,
           w_router_group, b_router_group, w_router_expert, b_router_expert, w1, w3, w2):
    batch, seq, d = x.shape
    t = batch * seq
    row = lambda v: v.reshape(1, -1)

    c_pad = jnp.zeros((SUBLANES, d), f32).at[:batch].set(c)
    mod = _ada(c_pad, w_ada, row(b_ada))[:batch]
    shift1, scale1, gate1, shift2, scale2, gate2 = [m.reshape(batch, 1, d) for m in jnp.split(mod, 6, axis=-1)]

    x2 = x.reshape(t, d)
    zt, qkv = _inproj(x2, row(g_pre_mix), scale1, shift1, w_in.astype(bf16), row(b_in), seq=seq, tm=INPROJ_TM)

    conv_act = _conv(zt.reshape(batch, seq, ZT_COLS), w_dw, row(b_dw), row(ln_conv_g), row(ln_conv_b),
                     tm=512).reshape(t, CONV_CH)

    os_, lses = [], []
    for gi, (_, dil) in enumerate(DILATED_GROUPS):
        o, lse = _attn_group(qkv, rel_bias.reshape(-1), gi, dil, batch=batch, seq=seq)
        os_.append(o)
        lses.append(lse)

    pad = ROUTE_LANES - N_GROUPS - N_EXPERTS
    w_r = jnp.concatenate([w_router_group, w_router_expert, jnp.zeros((d, pad), f32)], axis=1).astype(bf16)
    b_r = row(jnp.concatenate([b_router_group, b_router_expert.reshape(-1), jnp.zeros((pad,), f32)]))
    x1, h2p, logits = _merge(x2, conv_act, os_, lses, zt, w_conv_out.astype(bf16), row(b_conv_out),
                             w_attn_out.astype(bf16), w_out.astype(bf16), row(g_post_mix), gate1,
                             row(g_pre_ffn), scale2, shift2, w_r, b_r, seq=seq, tm=MERGE_TM, n_sub=MERGE_SUB)

    rec, rec_t, cnt = _route(logits, tm=512)

    counts = cnt[0, :N_EXPERTS].astype(jnp.int32)
    pcounts = (counts + MOE_ROWS - 1) // MOE_ROWS * MOE_ROWS
    pends = jnp.cumsum(pcounts)
    pstarts = pends - pcounts
    n_blocks = (t * TOP_K + N_EXPERTS * (MOE_ROWS - 1) + MOE_ROWS - 1) // MOE_ROWS
    n_used = (pends[-1:] // MOE_ROWS).astype(jnp.int32)
    tail_blocks = jnp.concatenate([n_used, n_blocks - n_used])
    eid = rec_t[0:TOP_K].astype(jnp.int32)
    rank = rec_t[TOP_K:2 * TOP_K].astype(jnp.int32)
    start_of = jnp.sum(jnp.where(eid[..., None] == jnp.arange(N_EXPERTS), pstarts, 0), axis=-1)
    dest = jnp.transpose(start_of + rank).reshape(t * TOP_K)
    tail = n_blocks - N_EXPERTS + jnp.arange(N_EXPERTS, dtype=jnp.int32)
    zero_blk = jnp.concatenate([jnp.where(pcounts > counts, pends // MOE_ROWS - 1, -1),
                                jnp.where(tail >= n_used[0], tail, -1)]).astype(jnp.int32)

    xs = _dispatch(zero_blk, dest, h2p, p_rows=n_blocks * MOE_ROWS, tm=512)
    ys = _experts((pstarts // MOE_ROWS).astype(jnp.int32), (pcounts // MOE_ROWS).astype(jnp.int32),
                  tail_blocks, counts, xs, w1, w3, w2)
    out = _combine(dest, ys, rec, x1, row(g_post_ffn), gate2, seq=seq, tm=512)
    return out.reshape(batch, seq, d)


def kernel(x, c, w_ada, b_ada, g_pre_mix, g_post_mix, w_in, b_in, w_dw, b_dw, ln_conv_g, ln_conv_b,
           w_conv_out, b_conv_out, rel_bias, w_attn_out, w_out, g_pre_ffn, g_post_ffn,
           w_router_group, b_router_group, w_router_expert, b_router_expert, w1, w3, w2):
    depth = w_ada.shape[0]
    for l in range(depth):
        pick = (lambda a: a.reshape(a.shape[1:])) if depth == 1 else (lambda a, l=l: a[l])
        x = _layer(x, c, pick(w_ada), pick(b_ada), pick(g_pre_mix), pick(g_post_mix), pick(w_in),
                   pick(b_in), pick(w_dw), pick(b_dw), pick(ln_conv_g), pick(ln_conv_b),
                   pick(w_conv_out), pick(b_conv_out), rel_bias, pick(w_attn_out), pick(w_out),
                   pick(g_pre_ffn), pick(g_post_ffn), pick(w_router_group), pick(b_router_group),
                   pick(w_router_expert), pick(b_router_expert), pick(w1), pick(w3), pick(w2))
    return x
```

```python
import functools
import math

import numpy as np
import jax
import jax.numpy as jnp
from jax import lax
from jax.experimental import pallas as pl
from jax.experimental.pallas import tpu as pltpu

D_MODEL = 2048
CONV_CH = 1024
CONV_WIDTH = 31
N_ATTN_HEADS = 12
HEADS_PER_GROUP = 4
HEAD_DIM = 128
ATTN_WIDTH = N_ATTN_HEADS * HEAD_DIM
DILATED_GROUPS = ((128, 1), (512, 4), (2048, 16))
NUM_BUCKETS = 32
REL_MAX_DISTANCE = 1024
N_GROUPS = 8
EXPERTS_PER_GROUP = 8
N_EXPERTS = N_GROUPS * EXPERTS_PER_GROUP
TOP_K = 2
EXPERT_FF = 512
EPS = 1e-6
NEG_INF = -1e30

LANES = 128
SUBLANES = 8
SUBLANES_BF16 = 16
V7X_VMEM_LIMIT_BYTES = 56 * 1024 * 1024

GROUP_W = HEADS_PER_GROUP * HEAD_DIM
ZT_COLS = 2 * D_MODEL + 2 * CONV_CH
COL_GLU = 2 * D_MODEL
INPROJ_TN = 3 * GROUP_W
ZT_TILES = ZT_COLS // INPROJ_TN
INPROJ_TM = 1024
ATTN_HALF = 64
ATTN_QB = 128
ATTN_WIN = ATTN_QB + 2 * ATTN_HALF
MOE_ROWS = 256
MERGE_TM = 256
MERGE_SUB = 2
ROUTE_LANES = 128
ROUTE_FIELDS = 8
PACKED_W = D_MODEL // 2
ROW_TILE = PACKED_W // LANES

bf16 = jnp.bfloat16
f32 = jnp.float32
u32 = jnp.uint32
HI_MASK = 0xFFFF0000


def _cparams(sem):
    return pltpu.CompilerParams(dimension_semantics=sem, vmem_limit_bytes=V7X_VMEM_LIMIT_BYTES)


def _sigmoid(x):
    return 1.0 / (1.0 + jnp.exp(-x))


def _pack_pairs(v):
    n = v.shape[1] // 2
    lo = lax.bitcast_convert_type(v[:, :n].astype(bf16).astype(f32), u32)
    hi = lax.bitcast_convert_type(v[:, n:].astype(bf16).astype(f32), u32)
    return (lo >> 16) | (hi & u32(HI_MASK))


def _unpack_lo(u):
    return lax.bitcast_convert_type(u << 16, f32)


def _unpack_hi(u):
    return lax.bitcast_convert_type(u & u32(HI_MASK), f32)


def _store_position_major(ref, v):
    packed = _pack_pairs(v)
    for j in range(ROW_TILE):
        ref[pl.ds(j, v.shape[0], stride=ROW_TILE), :] = packed[:, j * LANES:(j + 1) * LANES]


def _load_position_major(ref, n):
    chunks = [ref[pl.ds(j, n, stride=ROW_TILE), :] for j in range(ROW_TILE)]
    return jnp.concatenate([_unpack_lo(c) for c in chunks] + [_unpack_hi(c) for c in chunks], axis=1)


def _ada_kernel(c_ref, w_ref, b_ref, o_ref):
    c = c_ref[...]
    a = (c * _sigmoid(c)).astype(bf16)
    o_ref[...] = jnp.dot(a, w_ref[...].astype(bf16), preferred_element_type=f32) + b_ref[...]


def _ada(c_pad, w_ada, b_ada):
    rows, d = c_pad.shape
    n = w_ada.shape[1]
    tn = 1024
    return pl.pallas_call(
        _ada_kernel,
        grid=(n // tn,),
        in_specs=[pl.BlockSpec((rows, d), lambda j: (0, 0)),
                  pl.BlockSpec((d, tn), lambda j: (0, j)),
                  pl.BlockSpec((1, tn), lambda j: (0, j))],
        out_specs=pl.BlockSpec((rows, tn), lambda j: (0, j)),
        out_shape=jax.ShapeDtypeStruct((rows, n), f32),
        compiler_params=_cparams(("arbitrary",)),
        name="ada_mod",
    )(c_pad, w_ada, b_ada)


def _inproj_kernel(x_ref, g_ref, scale0_ref, shift0_ref, scale_ref, shift_ref,
                   wa_ref, wb_ref, wc_ref, ba_ref, bb_ref, bc_ref, zt_ref, qkv_ref, h_scr, x_scr, slab_scr, xsem):
    i, j = pl.program_id(0), pl.program_id(1)
    tm = x_scr.shape[0]
    slot = jnp.bitwise_and(i, 1)
    nxt_tile = jnp.minimum(i + 1, pl.num_programs(0) - 1)

    def x_copy(tile):
        return pltpu.make_async_copy(x_ref.at[pl.ds(pl.multiple_of(tile * tm, tm), tm), :], x_scr, xsem)

    def prenorm(rows, sc_ref, sh_ref, dst_slot):
        x = x_scr[rows, :]
        ms = jnp.mean(x * x, axis=-1, keepdims=True)
        h = x * lax.rsqrt(ms + EPS) * g_ref[...]
        h_scr[dst_slot, rows, :] = (h * (1.0 + sc_ref[...]) + sh_ref[...]).astype(bf16)

    @pl.when(j == 0)
    def _():
        @pl.when(i == 0)
        def _():
            x_copy(0).start()
            x_copy(0).wait()
            prenorm(slice(None), scale0_ref, shift0_ref, 0)

        x_copy(nxt_tile).start()

    @pl.when(j == ZT_TILES)
    def _():
        x_copy(nxt_tile).wait()

    w_refs, b_refs = (wa_ref, wb_ref, wc_ref), (ba_ref, bb_ref, bc_ref)

    def chunk(c):
        return jnp.dot(h_scr[slot], w_refs[c][...], preferred_element_type=f32) + b_refs[c][...]

    n_parts = len(DILATED_GROUPS)
    part_rows = [(tm // n_parts // SUBLANES_BF16) * SUBLANES_BF16] * (n_parts - 1)
    part_rows.append(tm - sum(part_rows))
    part_start = [sum(part_rows[:p]) for p in range(n_parts)]

    @pl.when(j < ZT_TILES)
    def _():
        for c in range(INPROJ_TN // GROUP_W):
            zt_ref[:, c * GROUP_W:(c + 1) * GROUP_W] = chunk(c).astype(bf16)

    for gi, (_, dil) in enumerate(DILATED_GROUPS):
        @pl.when(j == ZT_TILES + gi)
        def _(gi=gi, dil=dil):
            prenorm(slice(part_start[gi], part_start[gi] + part_rows[gi]), scale_ref, shift_ref, 1 - slot)
            per = tm // dil
            for c in range(INPROJ_TN // GROUP_W):
                z = chunk(c)
                if dil == 1:
                    qkv_ref[:, c * GROUP_W:(c + 1) * GROUP_W] = z.astype(bf16)
                    continue
                slabs = range(GROUP_W // LANES)
                for s in slabs:
                    slab_scr[0, s] = z[:, s * LANES:(s + 1) * LANES]
                src, stride, first = 0, dil, lambda r: r
                if dil == 16:
                    quarter = tm // 4
                    for s in slabs:
                        for r4 in range(4):
                            slab_scr[1, s, r4 * quarter:(r4 + 1) * quarter, :] = (
                                slab_scr[0, s, pl.ds(r4, quarter, stride=4), :])
                    src, stride, first = 1, 4, lambda r: (r % 4) * quarter + r // 4
                for r in range(dil):
                    for s in slabs:
                        col = c * GROUP_W + s * LANES
                        qkv_ref[r * per:(r + 1) * per, col:col + LANES] = (
                            slab_scr[src, s, pl.ds(first(r), per, stride=stride), :].astype(bf16))


def _inproj_col_block(j, c):
    n_blocks = (2 * CONV_CH + 3 * ATTN_WIDTH + 2 * D_MODEL) // GROUP_W
    gate0 = (2 * CONV_CH + 3 * ATTN_WIDTH) // GROUP_W
    q0 = 2 * CONV_CH // GROUP_W
    per_proj = ATTN_WIDTH // GROUP_W
    token_major = lax.rem(gate0 + (INPROJ_TN // GROUP_W) * j + c, n_blocks)
    return jnp.where(j < ZT_TILES, token_major, q0 + (j - ZT_TILES) + per_proj * c)


def _inproj(x2, g, scale, shift, w, b, *, seq, tm):
    t, d = x2.shape
    n = w.shape[1]
    tn = INPROJ_TN
    n_groups = n // tn - ZT_TILES
    last = t // tm - 1
    first_map = lambda i, j: (0, 0, 0)
    next_map = lambda i, j: ((jnp.minimum(i + 1, last) * tm) // seq, 0, 0)
    chunks = range(tn // GROUP_W)
    w_specs = [pl.BlockSpec((d, GROUP_W), lambda i, j, c=c: (0, _inproj_col_block(j, c))) for c in chunks]
    b_specs = [pl.BlockSpec((1, GROUP_W), lambda i, j, c=c: (0, _inproj_col_block(j, c))) for c in chunks]
    return pl.pallas_call(
        _inproj_kernel,
        grid=(t // tm, n // tn),
        in_specs=[pl.BlockSpec(memory_space=pl.ANY),
                  pl.BlockSpec((1, d), lambda i, j: (0, 0)),
                  pl.BlockSpec((None, 1, d), first_map), pl.BlockSpec((None, 1, d), first_map),
                  pl.BlockSpec((None, 1, d), next_map), pl.BlockSpec((None, 1, d), next_map)]
                 + w_specs + b_specs,
        out_specs=[pl.BlockSpec((tm, tn), lambda i, j: (i, jnp.minimum(j, ZT_TILES - 1))),
                   pl.BlockSpec((None, tm, tn), lambda i, j: (jnp.maximum(j - ZT_TILES, 0), i, 0))],
        out_shape=[jax.ShapeDtypeStruct((t, ZT_COLS), bf16),
                   jax.ShapeDtypeStruct((n_groups, t, tn), bf16)],
        scratch_shapes=[pltpu.VMEM((2, tm, d), bf16), pltpu.VMEM((tm, d), f32),
                        pltpu.VMEM((2, GROUP_W // LANES, tm, LANES), f32), pltpu.SemaphoreType.DMA(())],
        compiler_params=_cparams(("arbitrary", "arbitrary")),
        name="inproj",
    )(x2, g, scale, shift, scale, shift, w, w, w, b, b, b)


CONV_HALO = 16
CONV_CHUNK = 32
CONV_SUB = CONV_CH // LANES
NORM_CHUNK = 64


def _conv_kernel(ap_ref, ac_ref, an_ref, gp_ref, gc_ref, gn_ref, w_ref, bdw_ref, lng_ref, lnb_ref,
                 o_ref, u_scr, y_scr, *, tm):
    i = pl.program_id(1)
    last = pl.num_programs(1) - 1

    def glu(a_ref, g_ref):
        return a_ref[...].astype(f32) * _sigmoid(g_ref[...].astype(f32))

    def put(row0, val):
        for j in range(CONV_SUB):
            u_scr[pl.ds(row0 * CONV_SUB + j, val.shape[0], stride=CONV_SUB), :] = (
                val[:, j * LANES:(j + 1) * LANES])

    put(0, jnp.where(i > 0, glu(ap_ref, gp_ref), 0.0))
    put(CONV_HALO, glu(ac_ref, gc_ref))
    put(CONV_HALO + tm, jnp.where(i < last, glu(an_ref, gn_ref), 0.0))

    def chunk(c, carry):
        r0 = c * CONV_CHUNK
        acc = jnp.zeros((CONV_CHUNK, CONV_SUB, LANES), f32) + bdw_ref[...][None]
        for k in range(CONV_WIDTH):
            start = pl.multiple_of((r0 + k + 1) * CONV_SUB, CONV_SUB)
            xk = u_scr[pl.ds(start, CONV_CHUNK * CONV_SUB), :].reshape(CONV_CHUNK, CONV_SUB, LANES)
            acc = acc + w_ref[k][None] * xk
        out0 = pl.multiple_of(r0 * CONV_SUB, CONV_CHUNK * CONV_SUB)
        y_scr[pl.ds(out0, CONV_CHUNK * CONV_SUB), :] = acc.reshape(CONV_CHUNK * CONV_SUB, LANES)
        return carry

    lax.fori_loop(0, tm // CONV_CHUNK, chunk, 0)

    def norm(c, carry):
        r0 = pl.multiple_of(c * NORM_CHUNK, NORM_CHUNK)
        acc = jnp.concatenate(
            [y_scr[pl.ds(r0 * CONV_SUB + j, NORM_CHUNK, stride=CONV_SUB), :] for j in range(CONV_SUB)],
            axis=1)
        mu = jnp.mean(acc, axis=-1, keepdims=True)
        cen = acc - mu
        var = jnp.mean(cen * cen, axis=-1, keepdims=True)
        y = cen * lax.rsqrt(var + EPS) * lng_ref[...] + lnb_ref[...]
        o_ref[pl.ds(r0, NORM_CHUNK), :] = (y * _sigmoid(y)).astype(bf16)
        return carry

    lax.fori_loop(0, tm // NORM_CHUNK, norm, 0, unroll=2)


def _conv(z3, w_dw, b_dw, ln_g, ln_b, *, tm):
    b, s, _ = z3.shape
    hb = tm // CONV_HALO
    nhb = s // CONV_HALO
    ca, cg = COL_GLU // CONV_CH, COL_GLU // CONV_CH + 1

    def halo(col, which):
        if which < 0:
            return pl.BlockSpec((None, CONV_HALO, CONV_CH),
                                lambda bb, i: (bb, jnp.maximum(i * hb - 1, 0), col))
        return pl.BlockSpec((None, CONV_HALO, CONV_CH),
                            lambda bb, i: (bb, jnp.minimum((i + 1) * hb, nhb - 1), col))

    cur = lambda col: pl.BlockSpec((None, tm, CONV_CH), lambda bb, i: (bb, i, col))
    vec = pl.BlockSpec((1, CONV_CH), lambda bb, i: (0, 0))
    return pl.pallas_call(
        functools.partial(_conv_kernel, tm=tm),
        grid=(b, s // tm),
        in_specs=[halo(ca, -1), cur(ca), halo(ca, 1), halo(cg, -1), cur(cg), halo(cg, 1),
                  pl.BlockSpec((CONV_WIDTH, CONV_SUB, LANES), lambda bb, i: (0, 0, 0)),
                  pl.BlockSpec((CONV_SUB, LANES), lambda bb, i: (0, 0)), vec, vec],
        out_specs=pl.BlockSpec((None, tm, CONV_CH), lambda bb, i: (bb, i, 0)),
        out_shape=jax.ShapeDtypeStruct((b, s, CONV_CH), bf16),
        scratch_shapes=[pltpu.VMEM(((tm + 2 * CONV_HALO) * CONV_SUB, LANES), f32),
                        pltpu.VMEM((tm * CONV_SUB, LANES), f32)],
        compiler_params=_cparams(("arbitrary", "arbitrary")),
        name="conv_branch",
    )(z3, z3, z3, z3, z3, z3, w_dw.reshape(CONV_WIDTH, CONV_SUB, LANES), b_dw.reshape(CONV_SUB, LANES),
      ln_g, ln_b)


def _t5_bucket_table(dil):
    delta = (np.arange(ATTN_WIN)[None, :] - ATTN_HALF) - np.arange(ATTN_QB)[:, None]
    rel = delta * dil
    nb = NUM_BUCKETS // 2
    max_exact = nb // 2
    n = np.abs(rel)
    nf = np.maximum(n, 1).astype(np.float32)
    large = max_exact + (np.log(nf / np.float32(max_exact)) / np.float32(math.log(REL_MAX_DISTANCE / max_exact))
                         * np.float32(nb - max_exact)).astype(np.int32)
    large = np.minimum(large, nb - 1)
    bucket = np.where(rel > 0, nb, 0) + np.where(n < max_exact, n, large)
    return np.where(np.abs(delta) <= ATTN_HALF, bucket, -1).astype(np.int32)


def _attn_kernel(rb_ref, bkt_ref, q_ref, kp_ref, kc_ref, kn_ref, vp_ref, vc_ref, vn_ref, o_ref, lse_ref,
                 q_scr, k_scr, v_scr, bias_scr, *, gi, dil, n_tiles, sub_len):
    i = pl.program_id(1)
    h = ATTN_HALF
    tile = INPROJ_TM
    per = tile // dil
    tq = n_tiles * per

    @pl.when((pl.program_id(0) == 0) & (i == 0))
    def _():
        bk = bkt_ref[...]
        for hh in range(HEADS_PER_GROUP):
            acc = jnp.full((ATTN_QB, ATTN_WIN), NEG_INF, f32)
            for b_id in range(NUM_BUCKETS):
                acc = jnp.where(bk == b_id, rb_ref[b_id * N_ATTN_HEADS + gi * HEADS_PER_GROUP + hh], acc)
            bias_scr[hh] = acc

    halo_rows = kp_ref.shape[0]
    for r in range(dil):
        prev_lo = (r + 1) * per - h if halo_rows == tile else 0
        next_lo = r * per if halo_rows == tile else 0
        for dst, prv, cur, nxt in ((k_scr, kp_ref, kc_ref, kn_ref), (v_scr, vp_ref, vc_ref, vn_ref)):
            dst[r, 0:h, :] = prv[prev_lo:prev_lo + h, :]
            for t in range(n_tiles):
                dst[r, h + t * per:h + (t + 1) * per, :] = cur[t * tile + r * per:t * tile + (r + 1) * per, :]
            dst[r, h + tq:, :] = nxt[next_lo:next_lo + h, :]
        for t in range(n_tiles):
            q_scr[r, t * per:(t + 1) * per, :] = q_ref[t * tile + r * per:t * tile + (r + 1) * per, :]

    scale = HEAD_DIM ** -0.5
    col = lax.broadcasted_iota(jnp.int32, (ATTN_QB, ATTN_WIN), 1)
    lane = lax.broadcasted_iota(jnp.int32, (ATTN_QB, LANES), 1)

    def rows(start, n):
        if dil == 1:
            return pl.ds(pl.multiple_of(start, ATTN_QB), n)
        return pl.ds(start, n, stride=dil)

    def unit(u, carry):
        r = jnp.bitwise_and(u, dil - 1)
        jb = lax.shift_right_logical(u, jnp.int32(dil.bit_length() - 1))
        start = jb * (ATTN_QB * dil) + r
        p0 = pl.multiple_of(jb * ATTN_QB, ATTN_QB)
        cols = [slice(hh * HEAD_DIM, (hh + 1) * HEAD_DIM) for hh in range(HEADS_PER_GROUP)]
        q = [q_scr[r, pl.ds(p0, ATTN_QB), cs] for cs in cols]
        kw = [k_scr[r, pl.ds(p0, ATTN_WIN), cs] for cs in cols]
        vw = [v_scr[r, pl.ds(p0, ATTN_WIN), cs] for cs in cols]
        key0 = i * tq + jb * ATTN_QB - ATTN_HALF
        valid = (col + key0 >= 0) & (col + key0 < sub_len)
        s = [lax.dot_general(q[hh], kw[hh], (((1,), (1,)), ((), ())), preferred_element_type=f32)
             for hh in range(HEADS_PER_GROUP)]
        s = [jnp.where(valid, s[hh] * scale + bias_scr[hh], NEG_INF) for hh in range(HEADS_PER_GROUP)]
        m = [jnp.max(x, axis=-1, keepdims=True) for x in s]
        p = [jnp.exp(s[hh] - m[hh]) for hh in range(HEADS_PER_GROUP)]
        den = [jnp.sum(x, axis=-1, keepdims=True) for x in p]
        pv = [jnp.dot(p[hh].astype(bf16), vw[hh], preferred_element_type=f32)
              for hh in range(HEADS_PER_GROUP)]
        lse_tile = jnp.zeros((ATTN_QB, LANES), f32)
        for hh in range(HEADS_PER_GROUP):
            o_ref[hh, rows(start, ATTN_QB), :] = pv[hh] / den[hh]
            lse_tile = jnp.where(lane == hh, m[hh] + jnp.log(den[hh]), lse_tile)
        lse_ref[rows(start, ATTN_QB), :] = lse_tile
        return carry

    lax.fori_loop(0, n_tiles * tile // ATTN_QB, unit, 0, unroll=4)


def _attn_group(qkv, rel_bias_flat, gi, dil, *, batch, seq):
    t = batch * seq
    tile = INPROJ_TM
    n_tiles = max(1, ATTN_QB * dil // tile)
    ts = n_tiles * tile
    steps = seq // ts
    tq = ts // dil
    halo = ATTN_HALF if dil == 1 else tile
    hpt = ts // halo
    hps = seq // halo

    def cur(which):
        return pl.BlockSpec((None, ts, GROUP_W), lambda b, i: (gi, b * steps + i, which))

    def prev(which):
        return pl.BlockSpec((None, halo, GROUP_W),
                            lambda b, i: (gi, b * hps + jnp.maximum(i * hpt - 1, 0), which))

    def nxt(which):
        return pl.BlockSpec((None, halo, GROUP_W),
                            lambda b, i: (gi, b * hps + jnp.minimum((i + 1) * hpt, hps - 1), which))

    bkt = jnp.asarray(_t5_bucket_table(dil))
    return pl.pallas_call(
        functools.partial(_attn_kernel, gi=gi, dil=dil, n_tiles=n_tiles, sub_len=seq // dil),
        grid=(batch, steps),
        in_specs=[pl.BlockSpec(memory_space=pltpu.SMEM),
                  pl.BlockSpec((ATTN_QB, ATTN_WIN), lambda b, i: (0, 0)),
                  cur(0), prev(1), cur(1), nxt(1), prev(2), cur(2), nxt(2)],
        out_specs=[pl.BlockSpec((HEADS_PER_GROUP, ts, LANES), lambda b, i: (0, b * steps + i, 0)),
                   pl.BlockSpec((ts, LANES), lambda b, i: (b * steps + i, 0))],
        out_shape=[jax.ShapeDtypeStruct((HEADS_PER_GROUP, t, LANES), f32),
                   jax.ShapeDtypeStruct((t, LANES), f32)],
        scratch_shapes=[pltpu.VMEM((dil, tq, GROUP_W), bf16),
                        pltpu.VMEM((dil, tq + 2 * ATTN_HALF, GROUP_W), bf16),
                        pltpu.VMEM((dil, tq + 2 * ATTN_HALF, GROUP_W), bf16),
                        pltpu.VMEM((HEADS_PER_GROUP, ATTN_QB, ATTN_WIN), f32)],
        compiler_params=_cparams(("arbitrary", "arbitrary")),
        name=f"attn_g{gi}",
    )(rel_bias_flat, bkt, qkv, qkv, qkv, qkv, qkv, qkv, qkv)


def _merge_kernel(x_ref, ca_ref, o0_ref, o1_ref, o2_ref, l0_ref, l1_ref, l2_ref, ga_ref, gb_ref,
                  wco_ref, bco_ref, wao_ref, wo_ref, gpost_ref, gate1_ref, gpre_ref, scale2_ref,
                  shift2_ref, wr_ref, br_ref, x1_ref, h2p_ref, logit_ref, *, n_sub):
    hm = x_ref.shape[0] // n_sub
    subs = [slice(k * hm, (k + 1) * hm) for k in range(n_sub)]

    branch_a = [jnp.dot(ca_ref[rs, :], wco_ref[...], preferred_element_type=f32) + bco_ref[...] for rs in subs]

    def attention_mix(rs):
        l0, l1, l2 = l0_ref[rs, :], l1_ref[rs, :], l2_ref[rs, :]
        lmax = jnp.maximum(jnp.maximum(l0, l1), l2)
        e0, e1, e2 = jnp.exp(l0 - lmax), jnp.exp(l1 - lmax), jnp.exp(l2 - lmax)
        inv = 1.0 / (e0 + e1 + e2)
        parts = []
        for hh in range(HEADS_PER_GROUP):
            ls = slice(hh, hh + 1)
            parts.append((e0[:, ls] * inv[:, ls]) * o0_ref[hh, rs, :] + (e1[:, ls] * inv[:, ls]) * o1_ref[hh, rs, :]
                         + (e2[:, ls] * inv[:, ls]) * o2_ref[hh, rs, :])
        return jnp.concatenate(parts, axis=1).astype(bf16)

    att = [attention_mix(rs) for rs in subs]
    branch_b = [jnp.dot(a, wao_ref[...], preferred_element_type=f32) for a in att]
    mix = [(_sigmoid(ga_ref[rs, :].astype(f32)) * branch_a[k]
            + _sigmoid(gb_ref[rs, :].astype(f32)) * branch_b[k]).astype(bf16) for k, rs in enumerate(subs)]
    y = [jnp.dot(m, wo_ref[...], preferred_element_type=f32) for m in mix]

    post_gain = gate1_ref[...] * gpost_ref[...]
    pre_gain = gpre_ref[...] * (1.0 + scale2_ref[...])
    h2s = []
    for k, rs in enumerate(subs):
        x1 = x_ref[rs, :] + (y[k] * lax.rsqrt(jnp.mean(y[k] * y[k], axis=-1, keepdims=True) + EPS)) * post_gain
        x1_ref[rs, :] = x1
        h2 = (x1 * lax.rsqrt(jnp.mean(x1 * x1, axis=-1, keepdims=True) + EPS)) * pre_gain + shift2_ref[...]
        _store_position_major(h2p_ref.at[pl.ds(k * hm * ROW_TILE, hm * ROW_TILE), :], h2)
        h2s.append(h2.astype(bf16))
    for k, rs in enumerate(subs):
        logit_ref[rs, :] = jnp.dot(h2s[k], wr_ref[...], preferred_element_type=f32) + br_ref[...]


def _merge(x2, conv_act, os_, lses, zt, w_co, b_co, w_ao, w_o, g_post, gate1, g_pre, scale2, shift2,
           w_r, b_r, *, seq, tm, n_sub):
    t, d = x2.shape
    rows = lambda w: pl.BlockSpec((tm, w), lambda i: (i, 0))
    heads = pl.BlockSpec((HEADS_PER_GROUP, tm, LANES), lambda i: (0, i, 0))
    full = lambda a: pl.BlockSpec(a.shape, lambda i: (0,) * a.ndim)
    perb = pl.BlockSpec((None, 1, d), lambda i: ((i * tm) // seq, 0, 0))
    return pl.pallas_call(
        functools.partial(_merge_kernel, n_sub=n_sub),
        grid=(t // tm,),
        in_specs=[rows(d), rows(CONV_CH), heads, heads, heads, rows(LANES), rows(LANES), rows(LANES),
                  pl.BlockSpec((tm, d), lambda i: (i, 0)), pl.BlockSpec((tm, d), lambda i: (i, 1)),
                  full(w_co), full(b_co), full(w_ao), full(w_o), full(g_post), perb, full(g_pre),
                  perb, perb, full(w_r), full(b_r)],
        out_specs=[rows(d), pl.BlockSpec((tm * ROW_TILE, LANES), lambda i: (i, 0)), rows(ROUTE_LANES)],
        out_shape=[jax.ShapeDtypeStruct((t, d), f32),
                   jax.ShapeDtypeStruct((t * ROW_TILE, LANES), u32),
                   jax.ShapeDtypeStruct((t, ROUTE_LANES), f32)],
        compiler_params=_cparams(("arbitrary",)),
        name="merge",
    )(x2, conv_act, *os_, *lses, zt, zt, w_co, b_co, w_ao, w_o, g_post, gate1, g_pre, scale2, shift2,
      w_r, b_r)


def _route_kernel(logit_ref, rec_ref, rect_ref, cnt_ref, carry):
    i = pl.program_id(0)

    @pl.when(i == 0)
    def _():
        carry[...] = jnp.zeros_like(carry)

    lg = logit_ref[...]
    tm = lg.shape[0]
    lane = lax.broadcasted_iota(jnp.int32, lg.shape, 1).astype(f32)
    big = float(2 * ROUTE_LANES)

    def first_max(mask):
        v = jnp.max(jnp.where(mask, lg, -jnp.inf), axis=-1, keepdims=True)
        idx = jnp.min(jnp.where(mask & (lg == v), lane, big), axis=-1, keepdims=True)
        return v, idx

    gmask = lane < N_GROUPS
    gmax, gsel = first_max(gmask)
    p_g = 1.0 / jnp.sum(jnp.where(gmask, jnp.exp(lg - gmax), 0.0), axis=-1, keepdims=True)
    e_lo = N_GROUPS + EXPERTS_PER_GROUP * gsel
    emask = (lane >= e_lo) & (lane < e_lo + EXPERTS_PER_GROUP)
    v0, i0 = first_max(emask)
    v1, i1 = first_max(emask & (lane != i0))
    t1 = jnp.exp(v1 - v0)
    w0 = p_g / (1.0 + t1)
    w1 = p_g * t1 / (1.0 + t1)
    e0 = i0 - N_GROUPS
    e1 = i1 - N_GROUPS

    hit0 = lane == e0
    hit1 = lane == e1
    onehot = jnp.where(hit0 | hit1, 1.0, 0.0)
    r_i = lax.broadcasted_iota(jnp.int32, (tm, tm), 0)
    c_i = lax.broadcasted_iota(jnp.int32, (tm, tm), 1)
    tril = jnp.where(c_i < r_i, 1.0, 0.0).astype(bf16)
    before = jnp.dot(tril, onehot.astype(bf16), preferred_element_type=f32) + carry[...]
    rank0 = jnp.sum(jnp.where(hit0, before, 0.0), axis=-1, keepdims=True)
    rank1 = jnp.sum(jnp.where(hit1, before, 0.0), axis=-1, keepdims=True)
    carry[...] = carry[...] + jnp.sum(onehot, axis=0, keepdims=True)
    cnt_ref[...] = carry[...]

    rec = jnp.zeros_like(lg)
    for slot, val in enumerate((e0, e1, rank0, rank1, w0, w1)):
        rec = jnp.where(lane == slot, val, rec)
    rec_ref[...] = rec
    rect_ref[...] = jnp.transpose(rec)[0:ROUTE_FIELDS, :]


def _route(logits, *, tm):
    t = logits.shape[0]
    return pl.pallas_call(
        _route_kernel,
        grid=(t // tm,),
        in_specs=[pl.BlockSpec((tm, ROUTE_LANES), lambda i: (i, 0))],
        out_specs=[pl.BlockSpec((tm, ROUTE_LANES), lambda i: (i, 0)),
                   pl.BlockSpec((ROUTE_FIELDS, tm), lambda i: (0, i)),
                   pl.BlockSpec((1, ROUTE_LANES), lambda i: (0, 0))],
        out_shape=[jax.ShapeDtypeStruct((t, ROUTE_LANES), f32),
                   jax.ShapeDtypeStruct((ROUTE_FIELDS, t), f32),
                   jax.ShapeDtypeStruct((1, ROUTE_LANES), f32)],
        scratch_shapes=[pltpu.VMEM((1, ROUTE_LANES), f32)],
        compiler_params=_cparams(("arbitrary",)),
        name="route",
    )(logits)


DMA_UNROLL = 8


def _dispatch_kernel(zero_blk_ref, dest_ref, h_ref, xs_ref, zero_scr, sem, zsem, *, tm):
    def row_copy(r, d):
        return pltpu.make_async_copy(h_ref.at[pl.ds(pl.multiple_of(r * ROW_TILE, ROW_TILE), ROW_TILE), :],
                                     xs_ref.at[pl.ds(pl.multiple_of(d * ROW_TILE, ROW_TILE), ROW_TILE), :],
                                     sem)

    def zero_copy(blk):
        n = MOE_ROWS * ROW_TILE
        return pltpu.make_async_copy(zero_scr, xs_ref.at[pl.ds(pl.multiple_of(blk * n, n), n), :], zsem)

    @pl.when(pl.program_id(0) == 0)
    def _():
        zero_scr[...] = jnp.zeros_like(zero_scr)
        n_cand = zero_blk_ref.shape[0]

        def start(j, c):
            @pl.when(zero_blk_ref[j] >= 0)
            def _():
                zero_copy(zero_blk_ref[j]).start()
            return c

        def wait(j, c):
            @pl.when(zero_blk_ref[j] >= 0)
            def _():
                zero_copy(0).wait()
            return c

        lax.fori_loop(0, n_cand, start, 0)
        lax.fori_loop(0, n_cand, wait, 0)

    def issue(r, c):
        row_copy(r, dest_ref[TOP_K * r]).start(priority=0)
        row_copy(r, dest_ref[TOP_K * r + 1]).start(priority=1)
        return c

    lax.fori_loop(0, tm, issue, 0, unroll=DMA_UNROLL)
    for _ in range(TOP_K):
        pltpu.make_async_copy(h_ref, xs_ref.at[pl.ds(0, tm * ROW_TILE), :], sem).wait()


def _dispatch(zero_blk, dest, h2p, *, p_rows, tm):
    grid_spec = pltpu.PrefetchScalarGridSpec(
        num_scalar_prefetch=1,
        grid=(h2p.shape[0] // (tm * ROW_TILE),),
        in_specs=[pl.BlockSpec((TOP_K * tm,), lambda i, *_: (i,), memory_space=pltpu.SMEM),
                  pl.BlockSpec((tm * ROW_TILE, LANES), lambda i, *_: (i, 0))],
        out_specs=pl.BlockSpec(memory_space=pl.ANY),
        scratch_shapes=[pltpu.VMEM((MOE_ROWS * ROW_TILE, LANES), u32), pltpu.SemaphoreType.DMA(()),
                        pltpu.SemaphoreType.DMA(())],
    )
    return pl.pallas_call(
        functools.partial(_dispatch_kernel, tm=tm),
        grid_spec=grid_spec,
        out_shape=jax.ShapeDtypeStruct((p_rows * ROW_TILE, LANES), u32),
        compiler_params=_cparams(("arbitrary",)),
        name="dispatch",
    )(zero_blk, dest, h2p)


def _expert_kernel(bstart_ref, nblk_ref, tail_ref, cnt_ref, xs_ref, w1_ref, w3_ref, w2_ref, ys_ref,
                   wf1, wf3, wf2, w1b, w3b, w2b, xbuf, ybuf, sem_in, sem_out, sem_w):
    e = pl.program_id(0)
    last = pl.num_programs(0) - 1
    ws = jnp.bitwise_and(e, 1)
    n = nblk_ref[e]
    g0 = bstart_ref[e]
    n_used = tail_ref[0]
    blk = MOE_ROWS * ROW_TILE

    def hbm_rows(ref, g):
        return ref.at[pl.ds(pl.multiple_of(g * blk, blk), blk), :]

    def in_copy(g, slot):
        return pltpu.make_async_copy(hbm_rows(xs_ref, g), xbuf.at[slot], sem_in.at[slot])

    def out_copy(g, slot):
        return pltpu.make_async_copy(ybuf.at[slot], hbm_rows(ys_ref, g), sem_out.at[slot])

    def weight_copies(ex, slot):
        return [pltpu.make_async_copy(w1_ref.at[ex], wf1.at[slot], sem_w.at[slot]),
                pltpu.make_async_copy(w3_ref.at[ex], wf3.at[slot], sem_w.at[slot]),
                pltpu.make_async_copy(w2_ref.at[ex], wf2.at[slot], sem_w.at[slot])]

    @pl.when(e == 0)
    def _():
        for cp in weight_copies(0, 0):
            cp.start(priority=1)

        @pl.when(n_used > 0)
        def _():
            in_copy(0, 0).start()

    @pl.when(e < last)
    def _():
        for cp in weight_copies(e + 1, 1 - ws):
            cp.start(priority=1)

    for cp in weight_copies(e, ws):
        cp.wait()
    w1b[...] = wf1[ws].astype(bf16)
    w3b[...] = wf3[ws].astype(bf16)
    w2b[...] = wf2[ws].astype(bf16)

    def body(b, carry):
        g = g0 + b
        slot = jnp.bitwise_and(g, 1)
        in_copy(g, slot).wait()

        @pl.when(g + 1 < n_used)
        def _():
            in_copy(g + 1, 1 - slot).start()

        @pl.when(g >= 2)
        def _():
            out_copy(g - 2, slot).wait()

        def swiglu(rows):
            x = _load_position_major(xbuf.at[slot, pl.ds(0, rows * ROW_TILE)], rows).astype(bf16)
            h1 = jnp.dot(x, w1b[...], preferred_element_type=f32)
            h3 = jnp.dot(x, w3b[...], preferred_element_type=f32)
            a = (h1 * _sigmoid(h1) * h3).astype(bf16)
            _store_position_major(ybuf.at[slot, pl.ds(0, rows * ROW_TILE)],
                                  jnp.dot(a, w2b[...], preferred_element_type=f32))

        half = MOE_ROWS // 2
        real_rows = cnt_ref[e] - b * MOE_ROWS

        @pl.when(real_rows > half)
        def _():
            swiglu(MOE_ROWS)

        @pl.when(real_rows <= half)
        def _():
            swiglu(half)
            ybuf[slot, half * ROW_TILE:, :] = jnp.zeros((half * ROW_TILE, LANES), u32)

        out_copy(g, slot).start()
        return carry

    lax.fori_loop(0, n, body, 0)

    @pl.when(e == last)
    def _():
        @pl.when(n_used >= 2)
        def _():
            out_copy(n_used - 2, jnp.bitwise_and(n_used, 1)).wait()

        @pl.when(n_used >= 1)
        def _():
            out_copy(n_used - 1, jnp.bitwise_and(n_used - 1, 1)).wait()

        ybuf[0] = jnp.zeros(ybuf.shape[1:], u32)
        count = tail_ref[1]

        def zero_copy(b):
            return pltpu.make_async_copy(ybuf.at[0], hbm_rows(ys_ref, n_used + b), sem_out.at[0])

        lax.fori_loop(0, count, lambda b, c: (zero_copy(b).start(), c)[1], 0)
        lax.fori_loop(0, count, lambda b, c: (zero_copy(0).wait(), c)[1], 0)


def _experts(bstart, nblk, tail, cnt, xs, w1, w3, w2):
    blk = MOE_ROWS * ROW_TILE
    n_exp, d, ff = w1.shape
    grid_spec = pltpu.PrefetchScalarGridSpec(
        num_scalar_prefetch=4,
        grid=(n_exp,),
        in_specs=[pl.BlockSpec(memory_space=pl.ANY)] * 4,
        out_specs=pl.BlockSpec(memory_space=pl.ANY),
        scratch_shapes=[pltpu.VMEM((2, d, ff), f32), pltpu.VMEM((2, d, ff), f32), pltpu.VMEM((2, ff, d), f32),
                        pltpu.VMEM((d, ff), bf16), pltpu.VMEM((d, ff), bf16), pltpu.VMEM((ff, d), bf16),
                        pltpu.VMEM((2, blk, LANES), u32), pltpu.VMEM((2, blk, LANES), u32),
                        pltpu.SemaphoreType.DMA((2,)), pltpu.SemaphoreType.DMA((2,)),
                        pltpu.SemaphoreType.DMA((2,))],
    )
    return pl.pallas_call(
        _expert_kernel,
        grid_spec=grid_spec,
        out_shape=jax.ShapeDtypeStruct(xs.shape, u32),
        compiler_params=_cparams(("arbitrary",)),
        name="experts",
    )(bstart, nblk, tail, cnt, xs, w1, w3, w2)


def _combine_kernel(dest_ref, dest_next_ref, ys_ref, rec_ref, x1_ref, gpost_ref, gate2_ref, o_ref, buf, sem,
                    *, tm):
    i = pl.program_id(0)
    slot = jnp.bitwise_and(i, 1)

    def gather(d_ref, s):
        def row_copy(d, k, r):
            return pltpu.make_async_copy(
                ys_ref.at[pl.ds(pl.multiple_of(d * ROW_TILE, ROW_TILE), ROW_TILE), :],
                buf.at[s, k, pl.ds(pl.multiple_of(r * ROW_TILE, ROW_TILE), ROW_TILE), :], sem.at[s])

        def issue(r, c):
            row_copy(d_ref[TOP_K * r], 0, r).start(priority=0)
            row_copy(d_ref[TOP_K * r + 1], 1, r).start(priority=1)
            return c

        lax.fori_loop(0, tm, issue, 0, unroll=DMA_UNROLL)

    @pl.when(i == 0)
    def _():
        gather(dest_ref, 0)

    @pl.when(i + 1 < pl.num_programs(0))
    def _():
        gather(dest_next_ref, 1 - slot)

    for k in range(TOP_K):
        pltpu.make_async_copy(ys_ref.at[pl.ds(0, tm * ROW_TILE), :], buf.at[slot, k], sem.at[slot]).wait()

    rec = rec_ref[...]
    y = (rec[:, 4:5] * _load_position_major(buf.at[slot, 0], tm)
         + rec[:, 5:6] * _load_position_major(buf.at[slot, 1], tm))
    gain = gate2_ref[...] * gpost_ref[...]
    o_ref[...] = x1_ref[...] + (y * lax.rsqrt(jnp.mean(y * y, axis=-1, keepdims=True) + EPS)) * gain


def _combine(dest, ys, rec, x1, g_post, gate2, *, seq, tm):
    t, d = x1.shape
    last = t // tm - 1
    return pl.pallas_call(
        functools.partial(_combine_kernel, tm=tm),
        grid=(t // tm,),
        in_specs=[pl.BlockSpec((TOP_K * tm,), lambda i: (i,), memory_space=pltpu.SMEM),
                  pl.BlockSpec((TOP_K * tm,), lambda i: (jnp.minimum(i + 1, last),), memory_space=pltpu.SMEM),
                  pl.BlockSpec(memory_space=pl.ANY),
                  pl.BlockSpec((tm, ROUTE_LANES), lambda i: (i, 0)),
                  pl.BlockSpec((tm, d), lambda i: (i, 0)),
                  pl.BlockSpec((1, d), lambda i: (0, 0)),
                  pl.BlockSpec((None, 1, d), lambda i: ((i * tm) // seq, 0, 0))],
        out_specs=pl.BlockSpec((tm, d), lambda i: (i, 0)),
        out_shape=jax.ShapeDtypeStruct((t, d), f32),
        scratch_shapes=[pltpu.VMEM((2, TOP_K, tm * ROW_TILE, LANES), u32), pltpu.SemaphoreType.DMA((2,))],
        compiler_params=_cparams(("arbitrary",)),
        name="combine",
    )(dest, dest, ys, rec, x1, g_post, gate2)


def _layer(x, c, w_ada, b_ada, g_pre_mix, g_post_mix, w_in, b_in, w_dw, b_dw, ln_conv_g, ln_conv_b,
           w_conv_out, b_conv_out, rel_bias, w_attn_out, w_out, g_pre_ffn, g_post_ffn,
           w_router_group, b_router_group, w_router_expert, b_router_expert, w1, w3, w2):
    batch, seq, d = x.shape
    t = batch * seq
    row = lambda v: v.reshape(1, -1)

    c_pad = jnp.zeros((SUBLANES, d), f32).at[:batch].set(c)
    mod = _ada(c_pad, w_ada, row(b_ada))[:batch]
    shift1, scale1, gate1, shift2, scale2, gate2 = [m.reshape(batch, 1, d) for m in jnp.split(mod, 6, axis=-1)]

    x2 = x.reshape(t, d)
    zt, qkv = _inproj(x2, row(g_pre_mix), scale1, shift1, w_in.astype(bf16), row(b_in), seq=seq, tm=INPROJ_TM)

    conv_act = _conv(zt.reshape(batch, seq, ZT_COLS), w_dw, row(b_dw), row(ln_conv_g), row(ln_conv_b),
                     tm=512).reshape(t, CONV_CH)

    os_, lses = [], []
    for gi, (_, dil) in enumerate(DILATED_GROUPS):
        o, lse = _attn_group(qkv, rel_bias.reshape(-1), gi, dil, batch=batch, seq=seq)
        os_.append(o)
        lses.append(lse)

    pad = ROUTE_LANES - N_GROUPS - N_EXPERTS
    w_r = jnp.concatenate([w_router_group, w_router_expert, jnp.zeros((d, pad), f32)], axis=1).astype(bf16)
    b_r = row(jnp.concatenate([b_router_group, b_router_expert.reshape(-1), jnp.zeros((pad,), f32)]))
    x1, h2p, logits = _merge(x2, conv_act, os_, lses, zt, w_conv_out.astype(bf16), row(b_conv_out),
                             w_attn_out.astype(bf16), w_out.astype(bf16), row(g_post_mix), gate1,
                             row(g_pre_ffn), scale2, shift2, w_r, b_r, seq=seq, tm=MERGE_TM, n_sub=MERGE_SUB)

    rec, rec_t, cnt = _route(logits, tm=512)

    counts = cnt[0, :N_EXPERTS].astype(jnp.int32)
    pcounts = (counts + MOE_ROWS - 1) // MOE_ROWS * MOE_ROWS
    pends = jnp.cumsum(pcounts)
    pstarts = pends - pcounts
    n_blocks = (t * TOP_K + N_EXPERTS * (MOE_ROWS - 1) + MOE_ROWS - 1) // MOE_ROWS
    n_used = (pends[-1:] // MOE_ROWS).astype(jnp.int32)
    tail_blocks = jnp.concatenate([n_used, n_blocks - n_used])
    eid = rec_t[0:TOP_K].astype(jnp.int32)
    rank = rec_t[TOP_K:2 * TOP_K].astype(jnp.int32)
    start_of = jnp.sum(jnp.where(eid[..., None] == jnp.arange(N_EXPERTS), pstarts, 0), axis=-1)
    dest = jnp.transpose(start_of + rank).reshape(t * TOP_K)
    tail = n_blocks - N_EXPERTS + jnp.arange(N_EXPERTS, dtype=jnp.int32)
    zero_blk = jnp.concatenate([jnp.where(pcounts > counts, pends // MOE_ROWS - 1, -1),
                                jnp.where(tail >= n_used[0], tail, -1)]).astype(jnp.int32)

    xs = _dispatch(zero_blk, dest, h2p, p_rows=n_blocks * MOE_ROWS, tm=512)
    ys = _experts((pstarts // MOE_ROWS).astype(jnp.int32), (pcounts // MOE_ROWS).astype(jnp.int32),
                  tail_blocks, counts, xs, w1, w3, w2)
    out = _combine(dest, ys, rec, x1, row(g_post_ffn), gate2, seq=seq, tm=512)
    return out.reshape(batch, seq, d)


def kernel(x, c, w_ada, b_ada, g_pre_mix, g_post_mix, w_in, b_in, w_dw, b_dw, ln_conv_g, ln_conv_b,
           w_conv_out, b_conv_out, rel_bias, w_attn_out, w_out, g_pre_ffn, g_post_ffn,
           w_router_group, b_router_group, w_router_expert, b_router_expert, w1, w3, w2):
    depth = w_ada.shape[0]
    for l in range(depth):
        pick = (lambda a: a.reshape(a.shape[1:])) if depth == 1 else (lambda a, l=l: a[l])
        x = _layer(x, c, pick(w_ada), pick(b_ada), pick(g_pre_mix), pick(g_post_mix), pick(w_in),
                   pick(b_in), pick(w_dw), pick(b_dw), pick(ln_conv_g), pick(ln_conv_b),
                   pick(w_conv_out), pick(b_conv_out), rel_bias, pick(w_attn_out), pick(w_out),
                   pick(g_pre_ffn), pick(g_post_ffn), pick(w_router_group), pick(b_router_group),
                   pick(w_router_expert), pick(b_router_expert), pick(w1), pick(w3), pick(w2))
    return x
```

```python
import functools
import math

import numpy as np
import jax
import jax.numpy as jnp
from jax import lax
from jax.experimental import pallas as pl
from jax.experimental.pallas import tpu as pltpu

D_MODEL = 2048
CONV_CH = 1024
CONV_WIDTH = 31
N_ATTN_HEADS = 12
HEADS_PER_GROUP = 4
HEAD_DIM = 128
ATTN_WIDTH = N_ATTN_HEADS * HEAD_DIM
DILATED_GROUPS = ((128, 1), (512, 4), (2048, 16))
NUM_BUCKETS = 32
REL_MAX_DISTANCE = 1024
N_GROUPS = 8
EXPERTS_PER_GROUP = 8
N_EXPERTS = N_GROUPS * EXPERTS_PER_GROUP
TOP_K = 2
EXPERT_FF = 512
EPS = 1e-6
NEG_INF = -1e30

LANES = 128
SUBLANES = 8
SUBLANES_BF16 = 16
V7X_VMEM_LIMIT_BYTES = 56 * 1024 * 1024

GROUP_W = HEADS_PER_GROUP * HEAD_DIM
ZT_COLS = 2 * D_MODEL + 2 * CONV_CH
COL_GLU = 2 * D_MODEL
INPROJ_TN = 3 * GROUP_W
ZT_TILES = ZT_COLS // INPROJ_TN
INPROJ_TM = 1024
ATTN_HALF = 64
ATTN_QB = 128
ATTN_WIN = ATTN_QB + 2 * ATTN_HALF
MOE_ROWS = 256
MERGE_TM = 256
MERGE_SUB = 2
ROUTE_LANES = 128
ROUTE_FIELDS = 8
PACKED_W = D_MODEL // 2
ROW_TILE = PACKED_W // LANES

bf16 = jnp.bfloat16
f32 = jnp.float32
u32 = jnp.uint32
HI_MASK = 0xFFFF0000


def _cparams(sem):
    return pltpu.CompilerParams(dimension_semantics=sem, vmem_limit_bytes=V7X_VMEM_LIMIT_BYTES)


def _sigmoid(x):
    return 1.0 / (1.0 + jnp.exp(-x))


def _pack_pairs(v):
    n = v.shape[1] // 2
    lo = lax.bitcast_convert_type(v[:, :n].astype(bf16).astype(f32), u32)
    hi = lax.bitcast_convert_type(v[:, n:].astype(bf16).astype(f32), u32)
    return (lo >> 16) | (hi & u32(HI_MASK))


def _unpack_lo(u):
    return lax.bitcast_convert_type(u << 16, f32)


def _unpack_hi(u):
    return lax.bitcast_convert_type(u & u32(HI_MASK), f32)


def _store_position_major(ref, v):
    packed = _pack_pairs(v)
    for j in range(ROW_TILE):
        ref[pl.ds(j, v.shape[0], stride=ROW_TILE), :] = packed[:, j * LANES:(j + 1) * LANES]


def _load_position_major(ref, n):
    chunks = [ref[pl.ds(j, n, stride=ROW_TILE), :] for j in range(ROW_TILE)]
    return jnp.concatenate([_unpack_lo(c) for c in chunks] + [_unpack_hi(c) for c in chunks], axis=1)


def _ada_kernel(c_ref, w_ref, b_ref, o_ref):
    c = c_ref[...]
    a = (c * _sigmoid(c)).astype(bf16)
    o_ref[...] = jnp.dot(a, w_ref[...].astype(bf16), preferred_element_type=f32) + b_ref[...]


def _ada(c_pad, w_ada, b_ada):
    rows, d = c_pad.shape
    n = w_ada.shape[1]
    tn = 1024
    return pl.pallas_call(
        _ada_kernel,
        grid=(n // tn,),
        in_specs=[pl.BlockSpec((rows, d), lambda j: (0, 0)),
                  pl.BlockSpec((d, tn), lambda j: (0, j)),
                  pl.BlockSpec((1, tn), lambda j: (0, j))],
        out_specs=pl.BlockSpec((rows, tn), lambda j: (0, j)),
        out_shape=jax.ShapeDtypeStruct((rows, n), f32),
        compiler_params=_cparams(("arbitrary",)),
        name="ada_mod",
    )(c_pad, w_ada, b_ada)


def _inproj_kernel(x_ref, g_ref, scale0_ref, shift0_ref, scale_ref, shift_ref,
                   wa_ref, wb_ref, wc_ref, ba_ref, bb_ref, bc_ref, zt_ref, qkv_ref, h_scr, x_scr, slab_scr, xsem):
    i, j = pl.program_id(0), pl.program_id(1)
    tm = x_scr.shape[0]
    slot = jnp.bitwise_and(i, 1)
    nxt_tile = jnp.minimum(i + 1, pl.num_programs(0) - 1)

    def x_copy(tile):
        return pltpu.make_async_copy(x_ref.at[pl.ds(pl.multiple_of(tile * tm, tm), tm), :], x_scr, xsem)

    def prenorm(rows, sc_ref, sh_ref, dst_slot):
        x = x_scr[rows, :]
        ms = jnp.mean(x * x, axis=-1, keepdims=True)
        h = x * lax.rsqrt(ms + EPS) * g_ref[...]
        h_scr[dst_slot, rows, :] = (h * (1.0 + sc_ref[...]) + sh_ref[...]).astype(bf16)

    @pl.when(j == 0)
    def _():
        @pl.when(i == 0)
        def _():
            x_copy(0).start()
            x_copy(0).wait()
            prenorm(slice(None), scale0_ref, shift0_ref, 0)

        x_copy(nxt_tile).start()

    @pl.when(j == ZT_TILES)
    def _():
        x_copy(nxt_tile).wait()

    w_refs, b_refs = (wa_ref, wb_ref, wc_ref), (ba_ref, bb_ref, bc_ref)

    def chunk(c):
        return jnp.dot(h_scr[slot], w_refs[c][...], preferred_element_type=f32) + b_refs[c][...]

    n_parts = len(DILATED_GROUPS)
    part_rows = [(tm // n_parts // SUBLANES_BF16) * SUBLANES_BF16] * (n_parts - 1)
    part_rows.append(tm - sum(part_rows))
    part_start = [sum(part_rows[:p]) for p in range(n_parts)]

    @pl.when(j < ZT_TILES)
    def _():
        for c in range(INPROJ_TN // GROUP_W):
            zt_ref[:, c * GROUP_W:(c + 1) * GROUP_W] = chunk(c).astype(bf16)

    for gi, (_, dil) in enumerate(DILATED_GROUPS):
        @pl.when(j == ZT_TILES + gi)
        def _(gi=gi, dil=dil):
            prenorm(slice(part_start[gi], part_start[gi] + part_rows[gi]), scale_ref, shift_ref, 1 - slot)
            per = tm // dil
            for c in range(INPROJ_TN // GROUP_W):
                z = chunk(c)
                if dil == 1:
                    qkv_ref[:, c * GROUP_W:(c + 1) * GROUP_W] = z.astype(bf16)
                    continue
                slabs = range(GROUP_W // LANES)
                for s in slabs:
                    slab_scr[0, s] = z[:, s * LANES:(s + 1) * LANES]
                src, stride, first = 0, dil, lambda r: r
                if dil == 16:
                    quarter = tm // 4
                    for s in slabs:
                        for r4 in range(4):
                            slab_scr[1, s, r4 * quarter:(r4 + 1) * quarter, :] = (
                                slab_scr[0, s, pl.ds(r4, quarter, stride=4), :])
                    src, stride, first = 1, 4, lambda r: (r % 4) * quarter + r // 4
                for r in range(dil):
                    for s in slabs:
                        col = c * GROUP_W + s * LANES
                        qkv_ref[r * per:(r + 1) * per, col:col + LANES] = (
                            slab_scr[src, s, pl.ds(first(r), per, stride=stride), :].astype(bf16))


def _inproj_col_block(j, c):
    n_blocks = (2 * CONV_CH + 3 * ATTN_WIDTH + 2 * D_MODEL) // GROUP_W
    gate0 = (2 * CONV_CH + 3 * ATTN_WIDTH) // GROUP_W
    q0 = 2 * CONV_CH // GROUP_W
    per_proj = ATTN_WIDTH // GROUP_W
    token_major = lax.rem(gate0 + (INPROJ_TN // GROUP_W) * j + c, n_blocks)
    return jnp.where(j < ZT_TILES, token_major, q0 + (j - ZT_TILES) + per_proj * c)


def _inproj(x2, g, scale, shift, w, b, *, seq, tm):
    t, d = x2.shape
    n = w.shape[1]
    tn = INPROJ_TN
    n_groups = n // tn - ZT_TILES
    last = t // tm - 1
    first_map = lambda i, j: (0, 0, 0)
    next_map = lambda i, j: ((jnp.minimum(i + 1, last) * tm) // seq, 0, 0)
    chunks = range(tn // GROUP_W)
    w_specs = [pl.BlockSpec((d, GROUP_W), lambda i, j, c=c: (0, _inproj_col_block(j, c))) for c in chunks]
    b_specs = [pl.BlockSpec((1, GROUP_W), lambda i, j, c=c: (0, _inproj_col_block(j, c))) for c in chunks]
    return pl.pallas_call(
        _inproj_kernel,
        grid=(t // tm, n // tn),
        in_specs=[pl.BlockSpec(memory_space=pl.ANY),
                  pl.BlockSpec((1, d), lambda i, j: (0, 0)),
                  pl.BlockSpec((None, 1, d), first_map), pl.BlockSpec((None, 1, d), first_map),
                  pl.BlockSpec((None, 1, d), next_map), pl.BlockSpec((None, 1, d), next_map)]
                 + w_specs + b_specs,
        out_specs=[pl.BlockSpec((tm, tn), lambda i, j: (i, jnp.minimum(j, ZT_TILES - 1))),
                   pl.BlockSpec((None, tm, tn), lambda i, j: (jnp.maximum(j - ZT_TILES, 0), i, 0))],
        out_shape=[jax.ShapeDtypeStruct((t, ZT_COLS), bf16),
                   jax.ShapeDtypeStruct((n_groups, t, tn), bf16)],
        scratch_shapes=[pltpu.VMEM((2, tm, d), bf16), pltpu.VMEM((tm, d), f32),
                        pltpu.VMEM((2, GROUP_W // LANES, tm, LANES), f32), pltpu.SemaphoreType.DMA(())],
        compiler_params=_cparams(("arbitrary", "arbitrary")),
        name="inproj",
    )(x2, g, scale, shift, scale, shift, w, w, w, b, b, b)


CONV_HALO = 16
CONV_CHUNK = 32
CONV_SUB = CONV_CH // LANES
NORM_CHUNK = 64


def _conv_kernel(ap_ref, ac_ref, an_ref, gp_ref, gc_ref, gn_ref, w_ref, bdw_ref, lng_ref, lnb_ref,
                 o_ref, u_scr, y_scr, *, tm):
    i = pl.program_id(1)
    last = pl.num_programs(1) - 1

    def glu(a_ref, g_ref):
        return a_ref[...].astype(f32) * _sigmoid(g_ref[...].astype(f32))

    def put(row0, val):
        for j in range(CONV_SUB):
            u_scr[pl.ds(row0 * CONV_SUB + j, val.shape[0], stride=CONV_SUB), :] = (
                val[:, j * LANES:(j + 1) * LANES])

    put(0, jnp.where(i > 0, glu(ap_ref, gp_ref), 0.0))
    put(CONV_HALO, glu(ac_ref, gc_ref))
    put(CONV_HALO + tm, jnp.where(i < last, glu(an_ref, gn_ref), 0.0))

    def chunk(c, carry):
        r0 = c * CONV_CHUNK
        acc = jnp.zeros((CONV_CHUNK, CONV_SUB, LANES), f32) + bdw_ref[...][None]
        for k in range(CONV_WIDTH):
            start = pl.multiple_of((r0 + k + 1) * CONV_SUB, CONV_SUB)
            xk = u_scr[pl.ds(start, CONV_CHUNK * CONV_SUB), :].reshape(CONV_CHUNK, CONV_SUB, LANES)
            acc = acc + w_ref[k][None] * xk
        out0 = pl.multiple_of(r0 * CONV_SUB, CONV_CHUNK * CONV_SUB)
        y_scr[pl.ds(out0, CONV_CHUNK * CONV_SUB), :] = acc.reshape(CONV_CHUNK * CONV_SUB, LANES)
        return carry

    lax.fori_loop(0, tm // CONV_CHUNK, chunk, 0)

    def norm(c, carry):
        r0 = pl.multiple_of(c * NORM_CHUNK, NORM_CHUNK)
        acc = jnp.concatenate(
            [y_scr[pl.ds(r0 * CONV_SUB + j, NORM_CHUNK, stride=CONV_SUB), :] for j in range(CONV_SUB)],
            axis=1)
        mu = jnp.mean(acc, axis=-1, keepdims=True)
        cen = acc - mu
        var = jnp.mean(cen * cen, axis=-1, keepdims=True)
        y = cen * lax.rsqrt(var + EPS) * lng_ref[...] + lnb_ref[...]
        o_ref[pl.ds(r0, NORM_CHUNK), :] = (y * _sigmoid(y)).astype(bf16)
        return carry

    lax.fori_loop(0, tm // NORM_CHUNK, norm, 0, unroll=2)


def _conv(z3, w_dw, b_dw, ln_g, ln_b, *, tm):
    b, s, _ = z3.shape
    hb = tm // CONV_HALO
    nhb = s // CONV_HALO
    ca, cg = COL_GLU // CONV_CH, COL_GLU // CONV_CH + 1

    def halo(col, which):
        if which < 0:
            return pl.BlockSpec((None, CONV_HALO, CONV_CH),
                                lambda bb, i: (bb, jnp.maximum(i * hb - 1, 0), col))
        return pl.BlockSpec((None, CONV_HALO, CONV_CH),
                            lambda bb, i: (bb, jnp.minimum((i + 1) * hb, nhb - 1), col))

    cur = lambda col: pl.BlockSpec((None, tm, CONV_CH), lambda bb, i: (bb, i, col))
    vec = pl.BlockSpec((1, CONV_CH), lambda bb, i: (0, 0))
    return pl.pallas_call(
        functools.partial(_conv_kernel, tm=tm),
        grid=(b, s // tm),
        in_specs=[halo(ca, -1), cur(ca), halo(ca, 1), halo(cg, -1), cur(cg), halo(cg, 1),
                  pl.BlockSpec((CONV_WIDTH, CONV_SUB, LANES), lambda bb, i: (0, 0, 0)),
                  pl.BlockSpec((CONV_SUB, LANES), lambda bb, i: (0, 0)), vec, vec],
        out_specs=pl.BlockSpec((None, tm, CONV_CH), lambda bb, i: (bb, i, 0)),
        out_shape=jax.ShapeDtypeStruct((b, s, CONV_CH), bf16),
        scratch_shapes=[pltpu.VMEM(((tm + 2 * CONV_HALO) * CONV_SUB, LANES), f32),
                        pltpu.VMEM((tm * CONV_SUB, LANES), f32)],
        compiler_params=_cparams(("arbitrary", "arbitrary")),
        name="conv_branch",
    )(z3, z3, z3, z3, z3, z3, w_dw.reshape(CONV_WIDTH, CONV_SUB, LANES), b_dw.reshape(CONV_SUB, LANES),
      ln_g, ln_b)


def _t5_bucket_table(dil):
    delta = (np.arange(ATTN_WIN)[None, :] - ATTN_HALF) - np.arange(ATTN_QB)[:, None]
    rel = delta * dil
    nb = NUM_BUCKETS // 2
    max_exact = nb // 2
    n = np.abs(rel)
    nf = np.maximum(n, 1).astype(np.float32)
    large = max_exact + (np.log(nf / np.float32(max_exact)) / np.float32(math.log(REL_MAX_DISTANCE / max_exact))
                         * np.float32(nb - max_exact)).astype(np.int32)
    large = np.minimum(large, nb - 1)
    bucket = np.where(rel > 0, nb, 0) + np.where(n < max_exact, n, large)
    return np.where(np.abs(delta) <= ATTN_HALF, bucket, -1).astype(np.int32)


def _attn_kernel(rb_ref, bkt_ref, q_ref, kp_ref, kc_ref, kn_ref, vp_ref, vc_ref, vn_ref, o_ref, lse_ref,
                 q_scr, k_scr, v_scr, bias_scr, *, gi, dil, n_tiles, sub_len):
    i = pl.program_id(1)
    h = ATTN_HALF
    tile = INPROJ_TM
    per = tile // dil
    tq = n_tiles * per

    @pl.when((pl.program_id(0) == 0) & (i == 0))
    def _():
        bk = bkt_ref[...]
        for hh in range(HEADS_PER_GROUP):
            acc = jnp.full((ATTN_QB, ATTN_WIN), NEG_INF, f32)
            for b_id in range(NUM_BUCKETS):
                acc = jnp.where(bk == b_id, rb_ref[b_id * N_ATTN_HEADS + gi * HEADS_PER_GROUP + hh], acc)
            bias_scr[hh] = acc

    halo_rows = kp_ref.shape[0]
    for r in range(dil):
        prev_lo = (r + 1) * per - h if halo_rows == tile else 0
        next_lo = r * per if halo_rows == tile else 0
        for dst, prv, cur, nxt in ((k_scr, kp_ref, kc_ref, kn_ref), (v_scr, vp_ref, vc_ref, vn_ref)):
            dst[r, 0:h, :] = prv[prev_lo:prev_lo + h, :]
            for t in range(n_tiles):
                dst[r, h + t * per:h + (t + 1) * per, :] = cur[t * tile + r * per:t * tile + (r + 1) * per, :]
            dst[r, h + tq:, :] = nxt[next_lo:next_lo + h, :]
        for t in range(n_tiles):
            q_scr[r, t * per:(t + 1) * per, :] = q_ref[t * tile + r * per:t * tile + (r + 1) * per, :]

    scale = HEAD_DIM ** -0.5
    col = lax.broadcasted_iota(jnp.int32, (ATTN_QB, ATTN_WIN), 1)
    lane = lax.broadcasted_iota(jnp.int32, (ATTN_QB, LANES), 1)

    def rows(start, n):
        if dil == 1:
            return pl.ds(pl.multiple_of(start, ATTN_QB), n)
        return pl.ds(start, n, stride=dil)

    def unit(u, carry):
        r = jnp.bitwise_and(u, dil - 1)
        jb = lax.shift_right_logical(u, jnp.int32(dil.bit_length() - 1))
        start = jb * (ATTN_QB * dil) + r
        p0 = pl.multiple_of(jb * ATTN_QB, ATTN_QB)
        cols = [slice(hh * HEAD_DIM, (hh + 1) * HEAD_DIM) for hh in range(HEADS_PER_GROUP)]
        q = [q_scr[r, pl.ds(p0, ATTN_QB), cs] for cs in cols]
        kw = [k_scr[r, pl.ds(p0, ATTN_WIN), cs] for cs in cols]
        vw = [v_scr[r, pl.ds(p0, ATTN_WIN), cs] for cs in cols]
        key0 = i * tq + jb * ATTN_QB - ATTN_HALF
        valid = (col + key0 >= 0) & (col + key0 < sub_len)
        s = [lax.dot_general(q[hh], kw[hh], (((1,), (1,)), ((), ())), preferred_element_type=f32)
             for hh in range(HEADS_PER_GROUP)]
        s = [jnp.where(valid, s[hh] * scale + bias_scr[hh], NEG_INF) for hh in range(HEADS_PER_GROUP)]
        m = [jnp.max(x, axis=-1, keepdims=True) for x in s]
        p = [jnp.exp(s[hh] - m[hh]) for hh in range(HEADS_PER_GROUP)]
        den = [jnp.sum(x, axis=-1, keepdims=True) for x in p]
        pv = [jnp.dot(p[hh].astype(bf16), vw[hh], preferred_element_type=f32)
              for hh in range(HEADS_PER_GROUP)]
        lse_tile = jnp.zeros((ATTN_QB, LANES), f32)
        for hh in range(HEADS_PER_GROUP):
            o_ref[hh, rows(start, ATTN_QB), :] = pv[hh] / den[hh]
            lse_tile = jnp.where(lane == hh, m[hh] + jnp.log(den[hh]), lse_tile)
        lse_ref[rows(start, ATTN_QB), :] = lse_tile
        return carry

    lax.fori_loop(0, n_tiles * tile // ATTN_QB, unit, 0, unroll=4)


def _attn_group(qkv, rel_bias_flat, gi, dil, *, batch, seq):
    t = batch * seq
    tile = INPROJ_TM
    n_tiles = max(1, ATTN_QB * dil // tile)
    ts = n_tiles * tile
    steps = seq // ts
    tq = ts // dil
    halo = ATTN_HALF if dil == 1 else tile
    hpt = ts // halo
    hps = seq // halo

    def cur(which):
        return pl.BlockSpec((None, ts, GROUP_W), lambda b, i: (gi, b * steps + i, which))

    def prev(which):
        return pl.BlockSpec((None, halo, GROUP_W),
                            lambda b, i: (gi, b * hps + jnp.maximum(i * hpt - 1, 0), which))

    def nxt(which):
        return pl.BlockSpec((None, halo, GROUP_W),
                            lambda b, i: (gi, b * hps + jnp.minimum((i + 1) * hpt, hps - 1), which))

    bkt = jnp.asarray(_t5_bucket_table(dil))
    return pl.pallas_call(
        functools.partial(_attn_kernel, gi=gi, dil=dil, n_tiles=n_tiles, sub_len=seq // dil),
        grid=(batch, steps),
        in_specs=[pl.BlockSpec(memory_space=pltpu.SMEM),
                  pl.BlockSpec((ATTN_QB, ATTN_WIN), lambda b, i: (0, 0)),
                  cur(0), prev(1), cur(1), nxt(1), prev(2), cur(2), nxt(2)],
        out_specs=[pl.BlockSpec((HEADS_PER_GROUP, ts, LANES), lambda b, i: (0, b * steps + i, 0)),
                   pl.BlockSpec((ts, LANES), lambda b, i: (b * steps + i, 0))],
        out_shape=[jax.ShapeDtypeStruct((HEADS_PER_GROUP, t, LANES), f32),
                   jax.ShapeDtypeStruct((t, LANES), f32)],
        scratch_shapes=[pltpu.VMEM((dil, tq, GROUP_W), bf16),
                        pltpu.VMEM((dil, tq + 2 * ATTN_HALF, GROUP_W), bf16),
                        pltpu.VMEM((dil, tq + 2 * ATTN_HALF, GROUP_W), bf16),
                        pltpu.VMEM((HEADS_PER_GROUP, ATTN_QB, ATTN_WIN), f32)],
        compiler_params=_cparams(("arbitrary", "arbitrary")),
        name=f"attn_g{gi}",
    )(rel_bias_flat, bkt, qkv, qkv, qkv, qkv, qkv, qkv, qkv)


def _merge_kernel(x_ref, ca_ref, o0_ref, o1_ref, o2_ref, l0_ref, l1_ref, l2_ref, ga_ref, gb_ref,
                  wco_ref, bco_ref, wao_ref, wo_ref, gpost_ref, gate1_ref, gpre_ref, scale2_ref,
                  shift2_ref, wr_ref, br_ref, x1_ref, h2p_ref, logit_ref, *, n_sub):
    hm = x_ref.shape[0] // n_sub
    subs = [slice(k * hm, (k + 1) * hm) for k in range(n_sub)]

    branch_a = [jnp.dot(ca_ref[rs, :], wco_ref[...], preferred_element_type=f32) + bco_ref[...] for rs in subs]

    def attention_mix(rs):
        l0, l1, l2 = l0_ref[rs, :], l1_ref[rs, :], l2_ref[rs, :]
        lmax = jnp.maximum(jnp.maximum(l0, l1), l2)
        e0, e1, e2 = jnp.exp(l0 - lmax), jnp.exp(l1 - lmax), jnp.exp(l2 - lmax)
        inv = 1.0 / (e0 + e1 + e2)
        parts = []
        for hh in range(HEADS_PER_GROUP):
            ls = slice(hh, hh + 1)
            parts.append((e0[:, ls] * inv[:, ls]) * o0_ref[hh, rs, :] + (e1[:, ls] * inv[:, ls]) * o1_ref[hh, rs, :]
                         + (e2[:, ls] * inv[:, ls]) * o2_ref[hh, rs, :])
        return jnp.concatenate(parts, axis=1).astype(bf16)

    att = [attention_mix(rs) for rs in subs]
    branch_b = [jnp.dot(a, wao_ref[...], preferred_element_type=f32) for a in att]
    mix = [(_sigmoid(ga_ref[rs, :].astype(f32)) * branch_a[k]
            + _sigmoid(gb_ref[rs, :].astype(f32)) * branch_b[k]).astype(bf16) for k, rs in enumerate(subs)]
    y = [jnp.dot(m, wo_ref[...], preferred_element_type=f32) for m in mix]

    post_gain = gate1_ref[...] * gpost_ref[...]
    pre_gain = gpre_ref[...] * (1.0 + scale2_ref[...])
    h2s = []
    for k, rs in enumerate(subs):
        x1 = x_ref[rs, :] + (y[k] * lax.rsqrt(jnp.mean(y[k] * y[k], axis=-1, keepdims=True) + EPS)) * post_gain
        x1_ref[rs, :] = x1
        h2 = (x1 * lax.rsqrt(jnp.mean(x1 * x1, axis=-1, keepdims=True) + EPS)) * pre_gain + shift2_ref[...]
        _store_position_major(h2p_ref.at[pl.ds(k * hm * ROW_TILE, hm * ROW_TILE), :], h2)
        h2s.append(h2.astype(bf16))
    for k, rs in enumerate(subs):
        logit_ref[rs, :] = jnp.dot(h2s[k], wr_ref[...], preferred_element_type=f32) + br_ref[...]


def _merge(x2, conv_act, os_, lses, zt, w_co, b_co, w_ao, w_o, g_post, gate1, g_pre, scale2, shift2,
           w_r, b_r, *, seq, tm, n_sub):
    t, d = x2.shape
    rows = lambda w: pl.BlockSpec((tm, w), lambda i: (i, 0))
    heads = pl.BlockSpec((HEADS_PER_GROUP, tm, LANES), lambda i: (0, i, 0))
    full = lambda a: pl.BlockSpec(a.shape, lambda i: (0,) * a.ndim)
    perb = pl.BlockSpec((None, 1, d), lambda i: ((i * tm) // seq, 0, 0))
    return pl.pallas_call(
        functools.partial(_merge_kernel, n_sub=n_sub),
        grid=(t // tm,),
        in_specs=[rows(d), rows(CONV_CH), heads, heads, heads, rows(LANES), rows(LANES), rows(LANES),
                  pl.BlockSpec((tm, d), lambda i: (i, 0)), pl.BlockSpec((tm, d), lambda i: (i, 1)),
                  full(w_co), full(b_co), full(w_ao), full(w_o), full(g_post), perb, full(g_pre),
                  perb, perb, full(w_r), full(b_r)],
        out_specs=[rows(d), pl.BlockSpec((tm * ROW_TILE, LANES), lambda i: (i, 0)), rows(ROUTE_LANES)],
        out_shape=[jax.ShapeDtypeStruct((t, d), f32),
                   jax.ShapeDtypeStruct((t * ROW_TILE, LANES), u32),
                   jax.ShapeDtypeStruct((t, ROUTE_LANES), f32)],
        compiler_params=_cparams(("arbitrary",)),
        name="merge",
    )(x2, conv_act, *os_, *lses, zt, zt, w_co, b_co, w_ao, w_o, g_post, gate1, g_pre, scale2, shift2,
      w_r, b_r)


def _route_kernel(logit_ref, rec_ref, rect_ref, cnt_ref, carry):
    i = pl.program_id(0)

    @pl.when(i == 0)
    def _():
        carry[...] = jnp.zeros_like(carry)

    lg = logit_ref[...]
    tm = lg.shape[0]
    lane = lax.broadcasted_iota(jnp.int32, lg.shape, 1).astype(f32)
    big = float(2 * ROUTE_LANES)

    def first_max(mask):
        v = jnp.max(jnp.where(mask, lg, -jnp.inf), axis=-1, keepdims=True)
        idx = jnp.min(jnp.where(mask & (lg == v), lane, big), axis=-1, keepdims=True)
        return v, idx

    gmask = lane < N_GROUPS
    gmax, gsel = first_max(gmask)
    p_g = 1.0 / jnp.sum(jnp.where(gmask, jnp.exp(lg - gmax), 0.0), axis=-1, keepdims=True)
    e_lo = N_GROUPS + EXPERTS_PER_GROUP * gsel
    emask = (lane >= e_lo) & (lane < e_lo + EXPERTS_PER_GROUP)
    v0, i0 = first_max(emask)
    v1, i1 = first_max(emask & (lane != i0))
    t1 = jnp.exp(v1 - v0)
    w0 = p_g / (1.0 + t1)
    w1 = p_g * t1 / (1.0 + t1)
    e0 = i0 - N_GROUPS
    e1 = i1 - N_GROUPS

    hit0 = lane == e0
    hit1 = lane == e1
    onehot = jnp.where(hit0 | hit1, 1.0, 0.0)
    r_i = lax.broadcasted_iota(jnp.int32, (tm, tm), 0)
    c_i = lax.broadcasted_iota(jnp.int32, (tm, tm), 1)
    tril = jnp.where(c_i < r_i, 1.0, 0.0).astype(bf16)
    before = jnp.dot(tril, onehot.astype(bf16), preferred_element_type=f32) + carry[...]
    rank0 = jnp.sum(jnp.where(hit0, before, 0.0), axis=-1, keepdims=True)
    rank1 = jnp.sum(jnp.where(hit1, before, 0.0), axis=-1, keepdims=True)
    carry[...] = carry[...] + jnp.sum(onehot, axis=0, keepdims=True)
    cnt_ref[...] = carry[...]

    rec = jnp.zeros_like(lg)
    for slot, val in enumerate((e0, e1, rank0, rank1, w0, w1)):
        rec = jnp.where(lane == slot, val, rec)
    rec_ref[...] = rec
    rect_ref[...] = jnp.transpose(rec)[0:ROUTE_FIELDS, :]


def _route(logits, *, tm):
    t = logits.shape[0]
    return pl.pallas_call(
        _route_kernel,
        grid=(t // tm,),
        in_specs=[pl.BlockSpec((tm, ROUTE_LANES), lambda i: (i, 0))],
        out_specs=[pl.BlockSpec((tm, ROUTE_LANES), lambda i: (i, 0)),
                   pl.BlockSpec((ROUTE_FIELDS, tm), lambda i: (0, i)),
                   pl.BlockSpec((1, ROUTE_LANES), lambda i: (0, 0))],
        out_shape=[jax.ShapeDtypeStruct((t, ROUTE_LANES), f32),
                   jax.ShapeDtypeStruct((ROUTE_FIELDS, t), f32),
                   jax.ShapeDtypeStruct((1, ROUTE_LANES), f32)],
        scratch_shapes=[pltpu.VMEM((1, ROUTE_LANES), f32)],
        compiler_params=_cparams(("arbitrary",)),
        name="route",
    )(logits)


DMA_UNROLL = 8


def _dispatch_kernel(zero_blk_ref, dest_ref, h_ref, xs_ref, zero_scr, sem, zsem, *, tm):
    def row_copy(r, d):
        return pltpu.make_async_copy(h_ref.at[pl.ds(pl.multiple_of(r * ROW_TILE, ROW_TILE), ROW_TILE), :],
                                     xs_ref.at[pl.ds(pl.multiple_of(d * ROW_TILE, ROW_TILE), ROW_TILE), :],
                                     sem)

    def zero_copy(blk):
        n = MOE_ROWS * ROW_TILE
        return pltpu.make_async_copy(zero_scr, xs_ref.at[pl.ds(pl.multiple_of(blk * n, n), n), :], zsem)

    @pl.when(pl.program_id(0) == 0)
    def _():
        zero_scr[...] = jnp.zeros_like(zero_scr)
        n_cand = zero_blk_ref.shape[0]

        def start(j, c):
            @pl.when(zero_blk_ref[j] >= 0)
            def _():
                zero_copy(zero_blk_ref[j]).start()
            return c

        def wait(j, c):
            @pl.when(zero_blk_ref[j] >= 0)
            def _():
                zero_copy(0).wait()
            return c

        lax.fori_loop(0, n_cand, start, 0)
        lax.fori_loop(0, n_cand, wait, 0)

    def issue(r, c):
        row_copy(r, dest_ref[TOP_K * r]).start(priority=0)
        row_copy(r, dest_ref[TOP_K * r + 1]).start(priority=1)
        return c

    lax.fori_loop(0, tm, issue, 0, unroll=DMA_UNROLL)
    for _ in range(TOP_K):
        pltpu.make_async_copy(h_ref, xs_ref.at[pl.ds(0, tm * ROW_TILE), :], sem).wait()


def _dispatch(zero_blk, dest, h2p, *, p_rows, tm):
    grid_spec = pltpu.PrefetchScalarGridSpec(
        num_scalar_prefetch=1,
        grid=(h2p.shape[0] // (tm * ROW_TILE),),
        in_specs=[pl.BlockSpec((TOP_K * tm,), lambda i, *_: (i,), memory_space=pltpu.SMEM),
                  pl.BlockSpec((tm * ROW_TILE, LANES), lambda i, *_: (i, 0))],
        out_specs=pl.BlockSpec(memory_space=pl.ANY),
        scratch_shapes=[pltpu.VMEM((MOE_ROWS * ROW_TILE, LANES), u32), pltpu.SemaphoreType.DMA(()),
                        pltpu.SemaphoreType.DMA(())],
    )
    return pl.pallas_call(
        functools.partial(_dispatch_kernel, tm=tm),
        grid_spec=grid_spec,
        out_shape=jax.ShapeDtypeStruct((p_rows * ROW_TILE, LANES), u32),
        compiler_params=_cparams(("arbitrary",)),
        name="dispatch",
    )(zero_blk, dest, h2p)


def _expert_kernel(bstart_ref, nblk_ref, tail_ref, cnt_ref, xs_ref, w1_ref, w3_ref, w2_ref, ys_ref,
                   wf1, wf3, wf2, w1b, w3b, w2b, xbuf, ybuf, sem_in, sem_out, sem_w):
    e = pl.program_id(0)
    last = pl.num_programs(0) - 1
    ws = jnp.bitwise_and(e, 1)
    n = nblk_ref[e]
    g0 = bstart_ref[e]
    n_used = tail_ref[0]
    blk = MOE_ROWS * ROW_TILE

    def hbm_rows(ref, g):
        return ref.at[pl.ds(pl.multiple_of(g * blk, blk), blk), :]

    def in_copy(g, slot):
        return pltpu.make_async_copy(hbm_rows(xs_ref, g), xbuf.at[slot], sem_in.at[slot])

    def out_copy(g, slot):
        return pltpu.make_async_copy(ybuf.at[slot], hbm_rows(ys_ref, g), sem_out.at[slot])

    def weight_copies(ex, slot):
        return [pltpu.make_async_copy(w1_ref.at[ex], wf1.at[slot], sem_w.at[slot]),
                pltpu.make_async_copy(w3_ref.at[ex], wf3.at[slot], sem_w.at[slot]),
                pltpu.make_async_copy(w2_ref.at[ex], wf2.at[slot], sem_w.at[slot])]

    @pl.when(e == 0)
    def _():
        for cp in weight_copies(0, 0):
            cp.start(priority=1)

        @pl.when(n_used > 0)
        def _():
            in_copy(0, 0).start()

    @pl.when(e < last)
    def _():
        for cp in weight_copies(e + 1, 1 - ws):
            cp.start(priority=1)

    for cp in weight_copies(e, ws):
        cp.wait()

    def cast(dst, src):
        dst[...] = src[ws].astype(bf16)

    @pl.when(n == 0)
    def _():
        cast(w1b, wf1)
        cast(w3b, wf3)
        cast(w2b, wf2)

    def block(b, cast_weights):
        g = g0 + b
        slot = jnp.bitwise_and(g, 1)
        in_copy(g, slot).wait()

        @pl.when(g + 1 < n_used)
        def _():
            in_copy(g + 1, 1 - slot).start()

        @pl.when(g >= 2)
        def _():
            out_copy(g - 2, slot).wait()

        def swiglu(rows):
            x = _load_position_major(xbuf.at[slot, pl.ds(0, rows * ROW_TILE)], rows).astype(bf16)
            if cast_weights:
                cast(w1b, wf1)
            h1 = jnp.dot(x, w1b[...], preferred_element_type=f32)
            if cast_weights:
                cast(w3b, wf3)
            h3 = jnp.dot(x, w3b[...], preferred_element_type=f32)
            a = (h1 * _sigmoid(h1) * h3).astype(bf16)
            if cast_weights:
                cast(w2b, wf2)
            _store_position_major(ybuf.at[slot, pl.ds(0, rows * ROW_TILE)],
                                  jnp.dot(a, w2b[...], preferred_element_type=f32))

        half = MOE_ROWS // 2
        real_rows = cnt_ref[e] - b * MOE_ROWS

        @pl.when(real_rows > half)
        def _():
            swiglu(MOE_ROWS)

        @pl.when(real_rows <= half)
        def _():
            swiglu(half)
            ybuf[slot, half * ROW_TILE:, :] = jnp.zeros((half * ROW_TILE, LANES), u32)

        out_copy(g, slot).start()

    @pl.when(n > 0)
    def _():
        block(0, True)

    lax.fori_loop(1, n, lambda b, c: (block(b, False), c)[1], 0)

    @pl.when(e == last)
    def _():
        @pl.when(n_used >= 2)
        def _():
            out_copy(n_used - 2, jnp.bitwise_and(n_used, 1)).wait()

        @pl.when(n_used >= 1)
        def _():
            out_copy(n_used - 1, jnp.bitwise_and(n_used - 1, 1)).wait()

        ybuf[0] = jnp.zeros(ybuf.shape[1:], u32)
        count = tail_ref[1]

        def zero_copy(b):
            return pltpu.make_async_copy(ybuf.at[0], hbm_rows(ys_ref, n_used + b), sem_out.at[0])

        lax.fori_loop(0, count, lambda b, c: (zero_copy(b).start(), c)[1], 0)
        lax.fori_loop(0, count, lambda b, c: (zero_copy(0).wait(), c)[1], 0)


def _experts(bstart, nblk, tail, cnt, xs, w1, w3, w2):
    blk = MOE_ROWS * ROW_TILE
    n_exp, d, ff = w1.shape
    grid_spec = pltpu.PrefetchScalarGridSpec(
        num_scalar_prefetch=4,
        grid=(n_exp,),
        in_specs=[pl.BlockSpec(memory_space=pl.ANY)] * 4,
        out_specs=pl.BlockSpec(memory_space=pl.ANY),
        scratch_shapes=[pltpu.VMEM((2, d, ff), f32), pltpu.VMEM((2, d, ff), f32), pltpu.VMEM((2, ff, d), f32),
                        pltpu.VMEM((d, ff), bf16), pltpu.VMEM((d, ff), bf16), pltpu.VMEM((ff, d), bf16),
                        pltpu.VMEM((2, blk, LANES), u32), pltpu.VMEM((2, blk, LANES), u32),
                        pltpu.SemaphoreType.DMA((2,)), pltpu.SemaphoreType.DMA((2,)),
                        pltpu.SemaphoreType.DMA((2,))],
    )
    return pl.pallas_call(
        _expert_kernel,
        grid_spec=grid_spec,
        out_shape=jax.ShapeDtypeStruct(xs.shape, u32),
        compiler_params=_cparams(("arbitrary",)),
        name="experts",
    )(bstart, nblk, tail, cnt, xs, w1, w3, w2)


def _combine_kernel(dest_ref, dest_next_ref, ys_ref, rec_ref, x1_ref, gpost_ref, gate2_ref, o_ref, buf, sem,
                    *, tm):
    i = pl.program_id(0)
    slot = jnp.bitwise_and(i, 1)

    def gather(d_ref, s):
        def row_copy(d, k, r):
            return pltpu.make_async_copy(
                ys_ref.at[pl.ds(pl.multiple_of(d * ROW_TILE, ROW_TILE), ROW_TILE), :],
                buf.at[s, k, pl.ds(pl.multiple_of(r * ROW_TILE, ROW_TILE), ROW_TILE), :], sem.at[s])

        def issue(r, c):
            row_copy(d_ref[TOP_K * r], 0, r).start(priority=0)
            row_copy(d_ref[TOP_K * r + 1], 1, r).start(priority=1)
            return c

        lax.fori_loop(0, tm, issue, 0, unroll=DMA_UNROLL)

    @pl.when(i == 0)
    def _():
        gather(dest_ref, 0)

    @pl.when(i + 1 < pl.num_programs(0))
    def _():
        gather(dest_next_ref, 1 - slot)

    for k in range(TOP_K):
        pltpu.make_async_copy(ys_ref.at[pl.ds(0, tm * ROW_TILE), :], buf.at[slot, k], sem.at[slot]).wait()

    rec = rec_ref[...]
    y = (rec[:, 4:5] * _load_position_major(buf.at[slot, 0], tm)
         + rec[:, 5:6] * _load_position_major(buf.at[slot, 1], tm))
    gain = gate2_ref[...] * gpost_ref[...]
    o_ref[...] = x1_ref[...] + (y * lax.rsqrt(jnp.mean(y * y, axis=-1, keepdims=True) + EPS)) * gain


def _combine(dest, ys, rec, x1, g_post, gate2, *, seq, tm):
    t, d = x1.shape
    last = t // tm - 1
    return pl.pallas_call(
        functools.partial(_combine_kernel, tm=tm),
        grid=(t // tm,),
        in_specs=[pl.BlockSpec((TOP_K * tm,), lambda i: (i,), memory_space=pltpu.SMEM),
                  pl.BlockSpec((TOP_K * tm,), lambda i: (jnp.minimum(i + 1, last),), memory_space=pltpu.SMEM),
                  pl.BlockSpec(memory_space=pl.ANY),
                  pl.BlockSpec((tm, ROUTE_LANES), lambda i: (i, 0)),
                  pl.BlockSpec((tm, d), lambda i: (i, 0)),
                  pl.BlockSpec((1, d), lambda i: (0, 0)),
                  pl.BlockSpec((None, 1, d), lambda i: ((i * tm) // seq, 0, 0))],
        out_specs=pl.BlockSpec((tm, d), lambda i: (i, 0)),
        out_shape=jax.ShapeDtypeStruct((t, d), f32),
        scratch_shapes=[pltpu.VMEM((2, TOP_K, tm * ROW_TILE, LANES), u32), pltpu.SemaphoreType.DMA((2,))],
        compiler_params=_cparams(("arbitrary",)),
        name="combine",
    )(dest, dest, ys, rec, x1, g_post, gate2)


def _layer(x, c, w_ada, b_ada, g_pre_mix, g_post_mix, w_in, b_in, w_dw, b_dw, ln_conv_g, ln_conv_b,
           w_conv_out, b_conv_out, rel_bias, w_attn_out, w_out, g_pre_ffn, g_post_ffn,
           w_router_group, b_router_group, w_router_expert, b_router_expert, w1, w3, w2):
    batch, seq, d = x.shape
    t = batch * seq
    row = lambda v: v.reshape(1, -1)

    c_pad = jnp.zeros((SUBLANES, d), f32).at[:batch].set(c)
    mod = _ada(c_pad, w_ada, row(b_ada))[:batch]
    shift1, scale1, gate1, shift2, scale2, gate2 = [m.reshape(batch, 1, d) for m in jnp.split(mod, 6, axis=-1)]

    x2 = x.reshape(t, d)
    zt, qkv = _inproj(x2, row(g_pre_mix), scale1, shift1, w_in.astype(bf16), row(b_in), seq=seq, tm=INPROJ_TM)

    conv_act = _conv(zt.reshape(batch, seq, ZT_COLS), w_dw, row(b_dw), row(ln_conv_g), row(ln_conv_b),
                     tm=512).reshape(t, CONV_CH)

    os_, lses = [], []
    for gi, (_, dil) in enumerate(DILATED_GROUPS):
        o, lse = _attn_group(qkv, rel_bias.reshape(-1), gi, dil, batch=batch, seq=seq)
        os_.append(o)
        lses.append(lse)

    pad = ROUTE_LANES - N_GROUPS - N_EXPERTS
    w_r = jnp.concatenate([w_router_group, w_router_expert, jnp.zeros((d, pad), f32)], axis=1).astype(bf16)
    b_r = row(jnp.concatenate([b_router_group, b_router_expert.reshape(-1), jnp.zeros((pad,), f32)]))
    x1, h2p, logits = _merge(x2, conv_act, os_, lses, zt, w_conv_out.astype(bf16), row(b_conv_out),
                             w_attn_out.astype(bf16), w_out.astype(bf16), row(g_post_mix), gate1,
                             row(g_pre_ffn), scale2, shift2, w_r, b_r, seq=seq, tm=MERGE_TM, n_sub=MERGE_SUB)

    rec, rec_t, cnt = _route(logits, tm=512)

    counts = cnt[0, :N_EXPERTS].astype(jnp.int32)
    pcounts = (counts + MOE_ROWS - 1) // MOE_ROWS * MOE_ROWS
    pends = jnp.cumsum(pcounts)
    pstarts = pends - pcounts
    n_blocks = (t * TOP_K + N_EXPERTS * (MOE_ROWS - 1) + MOE_ROWS - 1) // MOE_ROWS
    n_used = (pends[-1:] // MOE_ROWS).astype(jnp.int32)
    tail_blocks = jnp.concatenate([n_used, n_blocks - n_used])
    eid = rec_t[0:TOP_K].astype(jnp.int32)
    rank = rec_t[TOP_K:2 * TOP_K].astype(jnp.int32)
    start_of = jnp.sum(jnp.where(eid[..., None] == jnp.arange(N_EXPERTS), pstarts, 0), axis=-1)
    dest = jnp.transpose(start_of + rank).reshape(t * TOP_K)
    tail = n_blocks - N_EXPERTS + jnp.arange(N_EXPERTS, dtype=jnp.int32)
    zero_blk = jnp.concatenate([jnp.where(pcounts > counts, pends // MOE_ROWS - 1, -1),
                                jnp.where(tail >= n_used[0], tail, -1)]).astype(jnp.int32)

    xs = _dispatch(zero_blk, dest, h2p, p_rows=n_blocks * MOE_ROWS, tm=512)
    ys = _experts((pstarts // MOE_ROWS).astype(jnp.int32), (pcounts // MOE_ROWS).astype(jnp.int32),
                  tail_blocks, counts, xs, w1, w3, w2)
    out = _combine(dest, ys, rec, x1, row(g_post_ffn), gate2, seq=seq, tm=512)
    return out.reshape(batch, seq, d)


def kernel(x, c, w_ada, b_ada, g_pre_mix, g_post_mix, w_in, b_in, w_dw, b_dw, ln_conv_g, ln_conv_b,
           w_conv_out, b_conv_out, rel_bias, w_attn_out, w_out, g_pre_ffn, g_post_ffn,
           w_router_group, b_router_group, w_router_expert, b_router_expert, w1, w3, w2):
    depth = w_ada.shape[0]
    for l in range(depth):
        pick = (lambda a: a.reshape(a.shape[1:])) if depth == 1 else (lambda a, l=l: a[l])
        x = _layer(x, c, pick(w_ada), pick(b_ada), pick(g_pre_mix), pick(g_post_mix), pick(w_in),
                   pick(b_in), pick(w_dw), pick(b_dw), pick(ln_conv_g), pick(ln_conv_b),
                   pick(w_conv_out), pick(b_conv_out), rel_bias, pick(w_attn_out), pick(w_out),
                   pick(g_pre_ffn), pick(g_post_ffn), pick(w_router_group), pick(b_router_group),
                   pick(w_router_expert), pick(b_router_expert), pick(w1), pick(w3), pick(w2))
    return x
```

```python
import functools
import math

import numpy as np
import jax
import jax.numpy as jnp
from jax import lax
from jax.experimental import pallas as pl
from jax.experimental.pallas import tpu as pltpu

D_MODEL = 2048
CONV_CH = 1024
CONV_WIDTH = 31
N_ATTN_HEADS = 12
HEADS_PER_GROUP = 4
HEAD_DIM = 128
ATTN_WIDTH = N_ATTN_HEADS * HEAD_DIM
DILATED_GROUPS = ((128, 1), (512, 4), (2048, 16))
NUM_BUCKETS = 32
REL_MAX_DISTANCE = 1024
N_GROUPS = 8
EXPERTS_PER_GROUP = 8
N_EXPERTS = N_GROUPS * EXPERTS_PER_GROUP
TOP_K = 2
EXPERT_FF = 512
EPS = 1e-6
NEG_INF = -1e30

LANES = 128
SUBLANES = 8
SUBLANES_BF16 = 16
V7X_VMEM_LIMIT_BYTES = 56 * 1024 * 1024

GROUP_W = HEADS_PER_GROUP * HEAD_DIM
ZT_COLS = 2 * D_MODEL + 2 * CONV_CH
COL_GLU = 2 * D_MODEL
INPROJ_TN = 3 * GROUP_W
ZT_TILES = ZT_COLS // INPROJ_TN
INPROJ_TM = 1024
ATTN_HALF = 64
ATTN_QB = 128
ATTN_WIN = ATTN_QB + 2 * ATTN_HALF
MOE_ROWS = 256
MERGE_TM = 256
MERGE_SUB = 2
ROUTE_LANES = 128
ROUTE_FIELDS = 8
PACKED_W = D_MODEL // 2
ROW_TILE = PACKED_W // LANES

bf16 = jnp.bfloat16
f32 = jnp.float32
u32 = jnp.uint32
HI_MASK = 0xFFFF0000


def _cparams(sem):
    return pltpu.CompilerParams(dimension_semantics=sem, vmem_limit_bytes=V7X_VMEM_LIMIT_BYTES)


def _sigmoid(x):
    return 1.0 / (1.0 + jnp.exp(-x))


def _pack_pairs(v):
    n = v.shape[1] // 2
    lo = lax.bitcast_convert_type(v[:, :n].astype(bf16).astype(f32), u32)
    hi = lax.bitcast_convert_type(v[:, n:].astype(bf16).astype(f32), u32)
    return (lo >> 16) | (hi & u32(HI_MASK))


def _unpack_lo(u):
    return lax.bitcast_convert_type(u << 16, f32)


def _unpack_hi(u):
    return lax.bitcast_convert_type(u & u32(HI_MASK), f32)


def _store_position_major(ref, v):
    packed = _pack_pairs(v)
    for j in range(ROW_TILE):
        ref[pl.ds(j, v.shape[0], stride=ROW_TILE), :] = packed[:, j * LANES:(j + 1) * LANES]


def _load_position_major(ref, n):
    chunks = [ref[pl.ds(j, n, stride=ROW_TILE), :] for j in range(ROW_TILE)]
    return jnp.concatenate([_unpack_lo(c) for c in chunks] + [_unpack_hi(c) for c in chunks], axis=1)


def _ada_kernel(c_ref, w_ref, b_ref, o_ref):
    c = c_ref[...]
    a = (c * _sigmoid(c)).astype(bf16)
    o_ref[...] = jnp.dot(a, w_ref[...].astype(bf16), preferred_element_type=f32) + b_ref[...]


def _ada(c_pad, w_ada, b_ada):
    rows, d = c_pad.shape
    n = w_ada.shape[1]
    tn = 1024
    return pl.pallas_call(
        _ada_kernel,
        grid=(n // tn,),
        in_specs=[pl.BlockSpec((rows, d), lambda j: (0, 0)),
                  pl.BlockSpec((d, tn), lambda j: (0, j)),
                  pl.BlockSpec((1, tn), lambda j: (0, j))],
        out_specs=pl.BlockSpec((rows, tn), lambda j: (0, j)),
        out_shape=jax.ShapeDtypeStruct((rows, n), f32),
        compiler_params=_cparams(("arbitrary",)),
        name="ada_mod",
    )(c_pad, w_ada, b_ada)


def _inproj_kernel(x_ref, g_ref, scale0_ref, shift0_ref, scale_ref, shift_ref,
                   wa_ref, wb_ref, wc_ref, ba_ref, bb_ref, bc_ref, zt_ref, qkv_ref, h_scr, x_scr, slab_scr, xsem):
    i, j = pl.program_id(0), pl.program_id(1)
    tm = x_scr.shape[0]
    slot = jnp.bitwise_and(i, 1)
    nxt_tile = jnp.minimum(i + 1, pl.num_programs(0) - 1)

    def x_copy(tile):
        return pltpu.make_async_copy(x_ref.at[pl.ds(pl.multiple_of(tile * tm, tm), tm), :], x_scr, xsem)

    def prenorm(rows, sc_ref, sh_ref, dst_slot):
        x = x_scr[rows, :]
        ms = jnp.mean(x * x, axis=-1, keepdims=True)
        h = x * lax.rsqrt(ms + EPS) * g_ref[...]
        h_scr[dst_slot, rows, :] = (h * (1.0 + sc_ref[...]) + sh_ref[...]).astype(bf16)

    @pl.when(j == 0)
    def _():
        @pl.when(i == 0)
        def _():
            x_copy(0).start()
            x_copy(0).wait()
            prenorm(slice(None), scale0_ref, shift0_ref, 0)

        x_copy(nxt_tile).start()

    @pl.when(j == ZT_TILES)
    def _():
        x_copy(nxt_tile).wait()

    w_refs, b_refs = (wa_ref, wb_ref, wc_ref), (ba_ref, bb_ref, bc_ref)

    def chunk(c):
        return jnp.dot(h_scr[slot], w_refs[c][...], preferred_element_type=f32) + b_refs[c][...]

    n_parts = len(DILATED_GROUPS)
    part_rows = [(tm // n_parts // SUBLANES_BF16) * SUBLANES_BF16] * (n_parts - 1)
    part_rows.append(tm - sum(part_rows))
    part_start = [sum(part_rows[:p]) for p in range(n_parts)]

    @pl.when(j < ZT_TILES)
    def _():
        for c in range(INPROJ_TN // GROUP_W):
            zt_ref[:, c * GROUP_W:(c + 1) * GROUP_W] = chunk(c).astype(bf16)

    for gi, (_, dil) in enumerate(DILATED_GROUPS):
        @pl.when(j == ZT_TILES + gi)
        def _(gi=gi, dil=dil):
            prenorm(slice(part_start[gi], part_start[gi] + part_rows[gi]), scale_ref, shift_ref, 1 - slot)
            per = tm // dil
            for c in range(INPROJ_TN // GROUP_W):
                z = chunk(c)
                if dil == 1:
                    qkv_ref[:, c * GROUP_W:(c + 1) * GROUP_W] = z.astype(bf16)
                    continue
                slabs = range(GROUP_W // LANES)
                for s in slabs:
                    slab_scr[0, s] = z[:, s * LANES:(s + 1) * LANES]
                src, stride, first = 0, dil, lambda r: r
                if dil == 16:
                    quarter = tm // 4
                    for s in slabs:
                        for r4 in range(4):
                            slab_scr[1, s, r4 * quarter:(r4 + 1) * quarter, :] = (
                                slab_scr[0, s, pl.ds(r4, quarter, stride=4), :])
                    src, stride, first = 1, 4, lambda r: (r % 4) * quarter + r // 4
                for r in range(dil):
                    for s in slabs:
                        col = c * GROUP_W + s * LANES
                        qkv_ref[r * per:(r + 1) * per, col:col + LANES] = (
                            slab_scr[src, s, pl.ds(first(r), per, stride=stride), :].astype(bf16))


def _inproj_col_block(j, c):
    n_blocks = (2 * CONV_CH + 3 * ATTN_WIDTH + 2 * D_MODEL) // GROUP_W
    gate0 = (2 * CONV_CH + 3 * ATTN_WIDTH) // GROUP_W
    q0 = 2 * CONV_CH // GROUP_W
    per_proj = ATTN_WIDTH // GROUP_W
    token_major = lax.rem(gate0 + (INPROJ_TN // GROUP_W) * j + c, n_blocks)
    return jnp.where(j < ZT_TILES, token_major, q0 + (j - ZT_TILES) + per_proj * c)


def _inproj(x2, g, scale, shift, w, b, *, seq, tm):
    t, d = x2.shape
    n = w.shape[1]
    tn = INPROJ_TN
    n_groups = n // tn - ZT_TILES
    last = t // tm - 1
    first_map = lambda i, j: (0, 0, 0)
    next_map = lambda i, j: ((jnp.minimum(i + 1, last) * tm) // seq, 0, 0)
    chunks = range(tn // GROUP_W)
    w_specs = [pl.BlockSpec((d, GROUP_W), lambda i, j, c=c: (0, _inproj_col_block(j, c))) for c in chunks]
    b_specs = [pl.BlockSpec((1, GROUP_W), lambda i, j, c=c: (0, _inproj_col_block(j, c))) for c in chunks]
    return pl.pallas_call(
        _inproj_kernel,
        grid=(t // tm, n // tn),
        in_specs=[pl.BlockSpec(memory_space=pl.ANY),
                  pl.BlockSpec((1, d), lambda i, j: (0, 0)),
                  pl.BlockSpec((None, 1, d), first_map), pl.BlockSpec((None, 1, d), first_map),
                  pl.BlockSpec((None, 1, d), next_map), pl.BlockSpec((None, 1, d), next_map)]
                 + w_specs + b_specs,
        out_specs=[pl.BlockSpec((tm, tn), lambda i, j: (i, jnp.minimum(j, ZT_TILES - 1))),
                   pl.BlockSpec((None, tm, tn), lambda i, j: (jnp.maximum(j - ZT_TILES, 0), i, 0))],
        out_shape=[jax.ShapeDtypeStruct((t, ZT_COLS), bf16),
                   jax.ShapeDtypeStruct((n_groups, t, tn), bf16)],
        scratch_shapes=[pltpu.VMEM((2, tm, d), bf16), pltpu.VMEM((tm, d), f32),
                        pltpu.VMEM((2, GROUP_W // LANES, tm, LANES), f32), pltpu.SemaphoreType.DMA(())],
        compiler_params=_cparams(("arbitrary", "arbitrary")),
        name="inproj",
    )(x2, g, scale, shift, scale, shift, w, w, w, b, b, b)


CONV_HALO = 16
CONV_CHUNK = 32
CONV_SUB = CONV_CH // LANES
NORM_CHUNK = 64


def _conv_kernel(ap_ref, ac_ref, an_ref, gp_ref, gc_ref, gn_ref, w_ref, bdw_ref, lng_ref, lnb_ref,
                 o_ref, u_scr, y_scr, *, tm):
    i = pl.program_id(1)
    last = pl.num_programs(1) - 1

    def glu(a_ref, g_ref):
        return a_ref[...].astype(f32) * _sigmoid(g_ref[...].astype(f32))

    def put(row0, val):
        for j in range(CONV_SUB):
            u_scr[pl.ds(row0 * CONV_SUB + j, val.shape[0], stride=CONV_SUB), :] = (
                val[:, j * LANES:(j + 1) * LANES])

    put(0, jnp.where(i > 0, glu(ap_ref, gp_ref), 0.0))
    put(CONV_HALO, glu(ac_ref, gc_ref))
    put(CONV_HALO + tm, jnp.where(i < last, glu(an_ref, gn_ref), 0.0))

    def chunk(c, carry):
        r0 = c * CONV_CHUNK
        acc = jnp.zeros((CONV_CHUNK, CONV_SUB, LANES), f32) + bdw_ref[...][None]
        for k in range(CONV_WIDTH):
            start = pl.multiple_of((r0 + k + 1) * CONV_SUB, CONV_SUB)
            xk = u_scr[pl.ds(start, CONV_CHUNK * CONV_SUB), :].reshape(CONV_CHUNK, CONV_SUB, LANES)
            acc = acc + w_ref[k][None] * xk
        out0 = pl.multiple_of(r0 * CONV_SUB, CONV_CHUNK * CONV_SUB)
        y_scr[pl.ds(out0, CONV_CHUNK * CONV_SUB), :] = acc.reshape(CONV_CHUNK * CONV_SUB, LANES)
        return carry

    lax.fori_loop(0, tm // CONV_CHUNK, chunk, 0)

    def norm(c, carry):
        r0 = pl.multiple_of(c * NORM_CHUNK, NORM_CHUNK)
        acc = jnp.concatenate(
            [y_scr[pl.ds(r0 * CONV_SUB + j, NORM_CHUNK, stride=CONV_SUB), :] for j in range(CONV_SUB)],
            axis=1)
        mu = jnp.mean(acc, axis=-1, keepdims=True)
        cen = acc - mu
        var = jnp.mean(cen * cen, axis=-1, keepdims=True)
        y = cen * lax.rsqrt(var + EPS) * lng_ref[...] + lnb_ref[...]
        o_ref[pl.ds(r0, NORM_CHUNK), :] = (y * _sigmoid(y)).astype(bf16)
        return carry

    lax.fori_loop(0, tm // NORM_CHUNK, norm, 0, unroll=2)


def _conv(z3, w_dw, b_dw, ln_g, ln_b, *, tm):
    b, s, _ = z3.shape
    hb = tm // CONV_HALO
    nhb = s // CONV_HALO
    ca, cg = COL_GLU // CONV_CH, COL_GLU // CONV_CH + 1

    def halo(col, which):
        if which < 0:
            return pl.BlockSpec((None, CONV_HALO, CONV_CH),
                                lambda bb, i: (bb, jnp.maximum(i * hb - 1, 0), col))
        return pl.BlockSpec((None, CONV_HALO, CONV_CH),
                            lambda bb, i: (bb, jnp.minimum((i + 1) * hb, nhb - 1), col))

    cur = lambda col: pl.BlockSpec((None, tm, CONV_CH), lambda bb, i: (bb, i, col))
    vec = pl.BlockSpec((1, CONV_CH), lambda bb, i: (0, 0))
    return pl.pallas_call(
        functools.partial(_conv_kernel, tm=tm),
        grid=(b, s // tm),
        in_specs=[halo(ca, -1), cur(ca), halo(ca, 1), halo(cg, -1), cur(cg), halo(cg, 1),
                  pl.BlockSpec((CONV_WIDTH, CONV_SUB, LANES), lambda bb, i: (0, 0, 0)),
                  pl.BlockSpec((CONV_SUB, LANES), lambda bb, i: (0, 0)), vec, vec],
        out_specs=pl.BlockSpec((None, tm, CONV_CH), lambda bb, i: (bb, i, 0)),
        out_shape=jax.ShapeDtypeStruct((b, s, CONV_CH), bf16),
        scratch_shapes=[pltpu.VMEM(((tm + 2 * CONV_HALO) * CONV_SUB, LANES), f32),
                        pltpu.VMEM((tm * CONV_SUB, LANES), f32)],
        compiler_params=_cparams(("arbitrary", "arbitrary")),
        name="conv_branch",
    )(z3, z3, z3, z3, z3, z3, w_dw.reshape(CONV_WIDTH, CONV_SUB, LANES), b_dw.reshape(CONV_SUB, LANES),
      ln_g, ln_b)


def _t5_bucket_table(dil):
    delta = (np.arange(ATTN_WIN)[None, :] - ATTN_HALF) - np.arange(ATTN_QB)[:, None]
    rel = delta * dil
    nb = NUM_BUCKETS // 2
    max_exact = nb // 2
    n = np.abs(rel)
    nf = np.maximum(n, 1).astype(np.float32)
    large = max_exact + (np.log(nf / np.float32(max_exact)) / np.float32(math.log(REL_MAX_DISTANCE / max_exact))
                         * np.float32(nb - max_exact)).astype(np.int32)
    large = np.minimum(large, nb - 1)
    bucket = np.where(rel > 0, nb, 0) + np.where(n < max_exact, n, large)
    return np.where(np.abs(delta) <= ATTN_HALF, bucket, -1).astype(np.int32)


def _attn_kernel(rb_ref, bkt_ref, q_ref, kp_ref, kc_ref, kn_ref, vp_ref, vc_ref, vn_ref, o_ref, lse_ref,
                 q_scr, k_scr, v_scr, bias_scr, *, gi, dil, n_tiles, sub_len):
    i = pl.program_id(1)
    h = ATTN_HALF
    tile = INPROJ_TM
    per = tile // dil
    tq = n_tiles * per

    @pl.when((pl.program_id(0) == 0) & (i == 0))
    def _():
        bk = bkt_ref[...]
        for hh in range(HEADS_PER_GROUP):
            acc = jnp.full((ATTN_QB, ATTN_WIN), NEG_INF, f32)
            for b_id in range(NUM_BUCKETS):
                acc = jnp.where(bk == b_id, rb_ref[b_id * N_ATTN_HEADS + gi * HEADS_PER_GROUP + hh], acc)
            bias_scr[hh] = acc

    halo_rows = kp_ref.shape[0]
    for r in range(dil):
        prev_lo = (r + 1) * per - h if halo_rows == tile else 0
        next_lo = r * per if halo_rows == tile else 0
        for dst, prv, cur, nxt in ((k_scr, kp_ref, kc_ref, kn_ref), (v_scr, vp_ref, vc_ref, vn_ref)):
            dst[r, 0:h, :] = prv[prev_lo:prev_lo + h, :]
            for t in range(n_tiles):
                dst[r, h + t * per:h + (t + 1) * per, :] = cur[t * tile + r * per:t * tile + (r + 1) * per, :]
            dst[r, h + tq:, :] = nxt[next_lo:next_lo + h, :]
        for t in range(n_tiles):
            q_scr[r, t * per:(t + 1) * per, :] = q_ref[t * tile + r * per:t * tile + (r + 1) * per, :]

    scale = HEAD_DIM ** -0.5
    col = lax.broadcasted_iota(jnp.int32, (ATTN_QB, ATTN_WIN), 1)
    lane = lax.broadcasted_iota(jnp.int32, (ATTN_QB, LANES), 1)

    def rows(start, n):
        if dil == 1:
            return pl.ds(pl.multiple_of(start, ATTN_QB), n)
        return pl.ds(start, n, stride=dil)

    def unit(u, carry):
        r = jnp.bitwise_and(u, dil - 1)
        jb = lax.shift_right_logical(u, jnp.int32(dil.bit_length() - 1))
        start = jb * (ATTN_QB * dil) + r
        p0 = pl.multiple_of(jb * ATTN_QB, ATTN_QB)
        cols = [slice(hh * HEAD_DIM, (hh + 1) * HEAD_DIM) for hh in range(HEADS_PER_GROUP)]
        q = [q_scr[r, pl.ds(p0, ATTN_QB), cs] for cs in cols]
        kw = [k_scr[r, pl.ds(p0, ATTN_WIN), cs] for cs in cols]
        vw = [v_scr[r, pl.ds(p0, ATTN_WIN), cs] for cs in cols]
        key0 = i * tq + jb * ATTN_QB - ATTN_HALF
        valid = (col + key0 >= 0) & (col + key0 < sub_len)
        s = [lax.dot_general(q[hh], kw[hh], (((1,), (1,)), ((), ())), preferred_element_type=f32)
             for hh in range(HEADS_PER_GROUP)]
        s = [jnp.where(valid, s[hh] * scale + bias_scr[hh], NEG_INF) for hh in range(HEADS_PER_GROUP)]
        m = [jnp.max(x, axis=-1, keepdims=True) for x in s]
        p = [jnp.exp(s[hh] - m[hh]) for hh in range(HEADS_PER_GROUP)]
        den = [jnp.sum(x, axis=-1, keepdims=True) for x in p]
        pv = [jnp.dot(p[hh].astype(bf16), vw[hh], preferred_element_type=f32)
              for hh in range(HEADS_PER_GROUP)]
        lse_tile = jnp.zeros((ATTN_QB, LANES), f32)
        for hh in range(HEADS_PER_GROUP):
            o_ref[hh, rows(start, ATTN_QB), :] = pv[hh] / den[hh]
            lse_tile = jnp.where(lane == hh, m[hh] + jnp.log(den[hh]), lse_tile)
        lse_ref[rows(start, ATTN_QB), :] = lse_tile
        return carry

    lax.fori_loop(0, n_tiles * tile // ATTN_QB, unit, 0, unroll=4)


def _attn_group(qkv, rel_bias_flat, gi, dil, *, batch, seq):
    t = batch * seq
    tile = INPROJ_TM
    n_tiles = max(1, ATTN_QB * dil // tile)
    ts = n_tiles * tile
    steps = seq // ts
    tq = ts // dil
    halo = ATTN_HALF if dil == 1 else tile
    hpt = ts // halo
    hps = seq // halo

    def cur(which):
        return pl.BlockSpec((None, ts, GROUP_W), lambda b, i: (gi, b * steps + i, which))

    def prev(which):
        return pl.BlockSpec((None, halo, GROUP_W),
                            lambda b, i: (gi, b * hps + jnp.maximum(i * hpt - 1, 0), which))

    def nxt(which):
        return pl.BlockSpec((None, halo, GROUP_W),
                            lambda b, i: (gi, b * hps + jnp.minimum((i + 1) * hpt, hps - 1), which))

    bkt = jnp.asarray(_t5_bucket_table(dil))
    return pl.pallas_call(
        functools.partial(_attn_kernel, gi=gi, dil=dil, n_tiles=n_tiles, sub_len=seq // dil),
        grid=(batch, steps),
        in_specs=[pl.BlockSpec(memory_space=pltpu.SMEM),
                  pl.BlockSpec((ATTN_QB, ATTN_WIN), lambda b, i: (0, 0)),
                  cur(0), prev(1), cur(1), nxt(1), prev(2), cur(2), nxt(2)],
        out_specs=[pl.BlockSpec((HEADS_PER_GROUP, ts, LANES), lambda b, i: (0, b * steps + i, 0)),
                   pl.BlockSpec((ts, LANES), lambda b, i: (b * steps + i, 0))],
        out_shape=[jax.ShapeDtypeStruct((HEADS_PER_GROUP, t, LANES), f32),
                   jax.ShapeDtypeStruct((t, LANES), f32)],
        scratch_shapes=[pltpu.VMEM((dil, tq, GROUP_W), bf16),
                        pltpu.VMEM((dil, tq + 2 * ATTN_HALF, GROUP_W), bf16),
                        pltpu.VMEM((dil, tq + 2 * ATTN_HALF, GROUP_W), bf16),
                        pltpu.VMEM((HEADS_PER_GROUP, ATTN_QB, ATTN_WIN), f32)],
        compiler_params=_cparams(("arbitrary", "arbitrary")),
        name=f"attn_g{gi}",
    )(rel_bias_flat, bkt, qkv, qkv, qkv, qkv, qkv, qkv, qkv)


def _merge_kernel(x_ref, ca_ref, o0_ref, o1_ref, o2_ref, l0_ref, l1_ref, l2_ref, ga_ref, gb_ref,
                  wco_ref, bco_ref, wao_ref, wo_ref, gpost_ref, gate1_ref, gpre_ref, scale2_ref,
                  shift2_ref, wr_ref, br_ref, x1_ref, h2p_ref, logit_ref, *, n_sub):
    hm = x_ref.shape[0] // n_sub
    subs = [slice(k * hm, (k + 1) * hm) for k in range(n_sub)]

    branch_a = [jnp.dot(ca_ref[rs, :], wco_ref[...], preferred_element_type=f32) + bco_ref[...] for rs in subs]

    def attention_mix(rs):
        l0, l1, l2 = l0_ref[rs, :], l1_ref[rs, :], l2_ref[rs, :]
        lmax = jnp.maximum(jnp.maximum(l0, l1), l2)
        e0, e1, e2 = jnp.exp(l0 - lmax), jnp.exp(l1 - lmax), jnp.exp(l2 - lmax)
        inv = 1.0 / (e0 + e1 + e2)
        parts = []
        for hh in range(HEADS_PER_GROUP):
            ls = slice(hh, hh + 1)
            parts.append((e0[:, ls] * inv[:, ls]) * o0_ref[hh, rs, :] + (e1[:, ls] * inv[:, ls]) * o1_ref[hh, rs, :]
                         + (e2[:, ls] * inv[:, ls]) * o2_ref[hh, rs, :])
        return jnp.concatenate(parts, axis=1).astype(bf16)

    att = [attention_mix(rs) for rs in subs]
    branch_b = [jnp.dot(a, wao_ref[...], preferred_element_type=f32) for a in att]
    mix = [(_sigmoid(ga_ref[rs, :].astype(f32)) * branch_a[k]
            + _sigmoid(gb_ref[rs, :].astype(f32)) * branch_b[k]).astype(bf16) for k, rs in enumerate(subs)]
    y = [jnp.dot(m, wo_ref[...], preferred_element_type=f32) for m in mix]

    post_gain = gate1_ref[...] * gpost_ref[...]
    pre_gain = gpre_ref[...] * (1.0 + scale2_ref[...])
    h2s = []
    for k, rs in enumerate(subs):
        x1 = x_ref[rs, :] + (y[k] * lax.rsqrt(jnp.mean(y[k] * y[k], axis=-1, keepdims=True) + EPS)) * post_gain
        x1_ref[rs, :] = x1
        h2 = (x1 * lax.rsqrt(jnp.mean(x1 * x1, axis=-1, keepdims=True) + EPS)) * pre_gain + shift2_ref[...]
        _store_position_major(h2p_ref.at[pl.ds(k * hm * ROW_TILE, hm * ROW_TILE), :], h2)
        h2s.append(h2.astype(bf16))
    for k, rs in enumerate(subs):
        logit_ref[rs, :] = jnp.dot(h2s[k], wr_ref[...], preferred_element_type=f32) + br_ref[...]


def _merge(x2, conv_act, os_, lses, zt, w_co, b_co, w_ao, w_o, g_post, gate1, g_pre, scale2, shift2,
           w_r, b_r, *, seq, tm, n_sub):
    t, d = x2.shape
    rows = lambda w: pl.BlockSpec((tm, w), lambda i: (i, 0))
    heads = pl.BlockSpec((HEADS_PER_GROUP, tm, LANES), lambda i: (0, i, 0))
    full = lambda a: pl.BlockSpec(a.shape, lambda i: (0,) * a.ndim)
    perb = pl.BlockSpec((None, 1, d), lambda i: ((i * tm) // seq, 0, 0))
    return pl.pallas_call(
        functools.partial(_merge_kernel, n_sub=n_sub),
        grid=(t // tm,),
        in_specs=[rows(d), rows(CONV_CH), heads, heads, heads, rows(LANES), rows(LANES), rows(LANES),
                  pl.BlockSpec((tm, d), lambda i: (i, 0)), pl.BlockSpec((tm, d), lambda i: (i, 1)),
                  full(w_co), full(b_co), full(w_ao), full(w_o), full(g_post), perb, full(g_pre),
                  perb, perb, full(w_r), full(b_r)],
        out_specs=[rows(d), pl.BlockSpec((tm * ROW_TILE, LANES), lambda i: (i, 0)), rows(ROUTE_LANES)],
        out_shape=[jax.ShapeDtypeStruct((t, d), f32),
                   jax.ShapeDtypeStruct((t * ROW_TILE, LANES), u32),
                   jax.ShapeDtypeStruct((t, ROUTE_LANES), f32)],
        compiler_params=_cparams(("arbitrary",)),
        name="merge",
    )(x2, conv_act, *os_, *lses, zt, zt, w_co, b_co, w_ao, w_o, g_post, gate1, g_pre, scale2, shift2,
      w_r, b_r)


def _route_kernel(logit_ref, rec_ref, rect_ref, cnt_ref, carry):
    i = pl.program_id(0)

    @pl.when(i == 0)
    def _():
        carry[...] = jnp.zeros_like(carry)

    lg = logit_ref[...]
    tm = lg.shape[0]
    lane = lax.broadcasted_iota(jnp.int32, lg.shape, 1).astype(f32)
    big = float(2 * ROUTE_LANES)

    def first_max(mask):
        v = jnp.max(jnp.where(mask, lg, -jnp.inf), axis=-1, keepdims=True)
        idx = jnp.min(jnp.where(mask & (lg == v), lane, big), axis=-1, keepdims=True)
        return v, idx

    gmask = lane < N_GROUPS
    gmax, gsel = first_max(gmask)
    p_g = 1.0 / jnp.sum(jnp.where(gmask, jnp.exp(lg - gmax), 0.0), axis=-1, keepdims=True)
    e_lo = N_GROUPS + EXPERTS_PER_GROUP * gsel
    emask = (lane >= e_lo) & (lane < e_lo + EXPERTS_PER_GROUP)
    v0, i0 = first_max(emask)
    v1, i1 = first_max(emask & (lane != i0))
    t1 = jnp.exp(v1 - v0)
    w0 = p_g / (1.0 + t1)
    w1 = p_g * t1 / (1.0 + t1)
    e0 = i0 - N_GROUPS
    e1 = i1 - N_GROUPS

    hit0 = lane == e0
    hit1 = lane == e1
    onehot = jnp.where(hit0 | hit1, 1.0, 0.0)
    r_i = lax.broadcasted_iota(jnp.int32, (tm, tm), 0)
    c_i = lax.broadcasted_iota(jnp.int32, (tm, tm), 1)
    tril = jnp.where(c_i < r_i, 1.0, 0.0).astype(bf16)
    before = jnp.dot(tril, onehot.astype(bf16), preferred_element_type=f32) + carry[...]
    rank0 = jnp.sum(jnp.where(hit0, before, 0.0), axis=-1, keepdims=True)
    rank1 = jnp.sum(jnp.where(hit1, before, 0.0), axis=-1, keepdims=True)
    carry[...] = carry[...] + jnp.sum(onehot, axis=0, keepdims=True)
    cnt_ref[...] = carry[...]

    rec = jnp.zeros_like(lg)
    for slot, val in enumerate((e0, e1, rank0, rank1, w0, w1)):
        rec = jnp.where(lane == slot, val, rec)
    rec_ref[...] = rec
    rect_ref[...] = jnp.transpose(rec)[0:ROUTE_FIELDS, :]


def _route(logits, *, tm):
    t = logits.shape[0]
    return pl.pallas_call(
        _route_kernel,
        grid=(t // tm,),
        in_specs=[pl.BlockSpec((tm, ROUTE_LANES), lambda i: (i, 0))],
        out_specs=[pl.BlockSpec((tm, ROUTE_LANES), lambda i: (i, 0)),
                   pl.BlockSpec((ROUTE_FIELDS, tm), lambda i: (0, i)),
                   pl.BlockSpec((1, ROUTE_LANES), lambda i: (0, 0))],
        out_shape=[jax.ShapeDtypeStruct((t, ROUTE_LANES), f32),
                   jax.ShapeDtypeStruct((ROUTE_FIELDS, t), f32),
                   jax.ShapeDtypeStruct((1, ROUTE_LANES), f32)],
        scratch_shapes=[pltpu.VMEM((1, ROUTE_LANES), f32)],
        compiler_params=_cparams(("arbitrary",)),
        name="route",
    )(logits)


DMA_UNROLL = 8


def _dispatch_kernel(zero_blk_ref, dest_ref, h_ref, xs_ref, zero_scr, sem, zsem, *, tm):
    def row_copy(r, d):
        return pltpu.make_async_copy(h_ref.at[pl.ds(pl.multiple_of(r * ROW_TILE, ROW_TILE), ROW_TILE), :],
                                     xs_ref.at[pl.ds(pl.multiple_of(d * ROW_TILE, ROW_TILE), ROW_TILE), :],
                                     sem)

    def zero_copy(blk):
        n = MOE_ROWS * ROW_TILE
        return pltpu.make_async_copy(zero_scr, xs_ref.at[pl.ds(pl.multiple_of(blk * n, n), n), :], zsem)

    @pl.when(pl.program_id(0) == 0)
    def _():
        zero_scr[...] = jnp.zeros_like(zero_scr)
        n_cand = zero_blk_ref.shape[0]

        def start(j, c):
            @pl.when(zero_blk_ref[j] >= 0)
            def _():
                zero_copy(zero_blk_ref[j]).start()
            return c

        def wait(j, c):
            @pl.when(zero_blk_ref[j] >= 0)
            def _():
                zero_copy(0).wait()
            return c

        lax.fori_loop(0, n_cand, start, 0)
        lax.fori_loop(0, n_cand, wait, 0)

    def issue(r, c):
        row_copy(r, dest_ref[TOP_K * r]).start(priority=0)
        row_copy(r, dest_ref[TOP_K * r + 1]).start(priority=1)
        return c

    lax.fori_loop(0, tm, issue, 0, unroll=DMA_UNROLL)
    for _ in range(TOP_K):
        pltpu.make_async_copy(h_ref, xs_ref.at[pl.ds(0, tm * ROW_TILE), :], sem).wait()


def _dispatch(zero_blk, dest, h2p, *, p_rows, tm):
    grid_spec = pltpu.PrefetchScalarGridSpec(
        num_scalar_prefetch=1,
        grid=(h2p.shape[0] // (tm * ROW_TILE),),
        in_specs=[pl.BlockSpec((TOP_K * tm,), lambda i, *_: (i,), memory_space=pltpu.SMEM),
                  pl.BlockSpec((tm * ROW_TILE, LANES), lambda i, *_: (i, 0))],
        out_specs=pl.BlockSpec(memory_space=pl.ANY),
        scratch_shapes=[pltpu.VMEM((MOE_ROWS * ROW_TILE, LANES), u32), pltpu.SemaphoreType.DMA(()),
                        pltpu.SemaphoreType.DMA(())],
    )
    return pl.pallas_call(
        functools.partial(_dispatch_kernel, tm=tm),
        grid_spec=grid_spec,
        out_shape=jax.ShapeDtypeStruct((p_rows * ROW_TILE, LANES), u32),
        compiler_params=_cparams(("arbitrary",)),
        name="dispatch",
    )(zero_blk, dest, h2p)


def _expert_kernel(bstart_ref, nblk_ref, tail_ref, cnt_ref, xs_ref, w1_ref, w3_ref, w2_ref, ys_ref,
                   wf1, wf3, wf2, w1b, w3b, w2b, xbuf, ybuf, sem_in, sem_out, sem_w):
    e = pl.program_id(0)
    last = pl.num_programs(0) - 1
    ws = jnp.bitwise_and(e, 1)
    n = nblk_ref[e]
    g0 = bstart_ref[e]
    n_used = tail_ref[0]
    blk = MOE_ROWS * ROW_TILE

    def hbm_rows(ref, g):
        return ref.at[pl.ds(pl.multiple_of(g * blk, blk), blk), :]

    def in_copy(g, slot):
        return pltpu.make_async_copy(hbm_rows(xs_ref, g), xbuf.at[slot], sem_in.at[slot])

    def out_copy(g, slot):
        return pltpu.make_async_copy(ybuf.at[slot], hbm_rows(ys_ref, g), sem_out.at[slot])

    def weight_copies(ex, slot):
        return [pltpu.make_async_copy(w1_ref.at[ex], wf1.at[slot], sem_w.at[slot]),
                pltpu.make_async_copy(w3_ref.at[ex], wf3.at[slot], sem_w.at[slot]),
                pltpu.make_async_copy(w2_ref.at[ex], wf2.at[slot], sem_w.at[slot])]

    @pl.when(e == 0)
    def _():
        for cp in weight_copies(0, 0):
            cp.start(priority=1)

        @pl.when(n_used > 0)
        def _():
            in_copy(0, 0).start()

    @pl.when(e < last)
    def _():
        for cp in weight_copies(e + 1, 1 - ws):
            cp.start(priority=1)

    for cp in weight_copies(e, ws):
        cp.wait()

    def cast(dst, src):
        dst[...] = src[ws].astype(bf16)

    @pl.when(n == 0)
    def _():
        cast(w1b, wf1)
        cast(w3b, wf3)
        cast(w2b, wf2)

    def block(b, cast_weights):
        g = g0 + b
        slot = jnp.bitwise_and(g, 1)
        in_copy(g, slot).wait()

        @pl.when(g + 1 < n_used)
        def _():
            in_copy(g + 1, 1 - slot).start()

        @pl.when(g >= 2)
        def _():
            out_copy(g - 2, slot).wait()

        def swiglu(rows):
            x = _load_position_major(xbuf.at[slot, pl.ds(0, rows * ROW_TILE)], rows).astype(bf16)
            if cast_weights:
                cast(w1b, wf1)
            h1 = jnp.dot(x, w1b[...], preferred_element_type=f32)
            if cast_weights:
                cast(w3b, wf3)
            h3 = jnp.dot(x, w3b[...], preferred_element_type=f32)
            a = (h1 * _sigmoid(h1) * h3).astype(bf16)
            if cast_weights:
                cast(w2b, wf2)
            _store_position_major(ybuf.at[slot, pl.ds(0, rows * ROW_TILE)],
                                  jnp.dot(a, w2b[...], preferred_element_type=f32))

        half = MOE_ROWS // 2
        real_rows = cnt_ref[e] - b * MOE_ROWS

        @pl.when(real_rows > half)
        def _():
            swiglu(MOE_ROWS)

        @pl.when(real_rows <= half)
        def _():
            swiglu(half)
            ybuf[slot, half * ROW_TILE:, :] = jnp.zeros((half * ROW_TILE, LANES), u32)

        out_copy(g, slot).start()

    @pl.when(n > 0)
    def _():
        block(0, True)

    lax.fori_loop(1, n, lambda b, c: (block(b, False), c)[1], 0)

    @pl.when(e == last)
    def _():
        @pl.when(n_used >= 2)
        def _():
            out_copy(n_used - 2, jnp.bitwise_and(n_used, 1)).wait()

        @pl.when(n_used >= 1)
        def _():
            out_copy(n_used - 1, jnp.bitwise_and(n_used - 1, 1)).wait()

        ybuf[0] = jnp.zeros(ybuf.shape[1:], u32)
        count = tail_ref[1]

        def zero_copy(b):
            return pltpu.make_async_copy(ybuf.at[0], hbm_rows(ys_ref, n_used + b), sem_out.at[0])

        lax.fori_loop(0, count, lambda b, c: (zero_copy(b).start(), c)[1], 0)
        lax.fori_loop(0, count, lambda b, c: (zero_copy(0).wait(), c)[1], 0)


def _experts(bstart, nblk, tail, cnt, xs, w1, w3, w2):
    blk = MOE_ROWS * ROW_TILE
    n_exp, d, ff = w1.shape
    grid_spec = pltpu.PrefetchScalarGridSpec(
        num_scalar_prefetch=4,
        grid=(n_exp,),
        in_specs=[pl.BlockSpec(memory_space=pl.ANY)] * 4,
        out_specs=pl.BlockSpec(memory_space=pl.ANY),
        scratch_shapes=[pltpu.VMEM((2, d, ff), f32), pltpu.VMEM((2, d, ff), f32), pltpu.VMEM((2, ff, d), f32),
                        pltpu.VMEM((d, ff), bf16), pltpu.VMEM((d, ff), bf16), pltpu.VMEM((ff, d), bf16),
                        pltpu.VMEM((2, blk, LANES), u32), pltpu.VMEM((2, blk, LANES), u32),
                        pltpu.SemaphoreType.DMA((2,)), pltpu.SemaphoreType.DMA((2,)),
                        pltpu.SemaphoreType.DMA((2,))],
    )
    return pl.pallas_call(
        _expert_kernel,
        grid_spec=grid_spec,
        out_shape=jax.ShapeDtypeStruct(xs.shape, u32),
        compiler_params=_cparams(("arbitrary",)),
        name="experts",
    )(bstart, nblk, tail, cnt, xs, w1, w3, w2)


COMBINE_RING = 3
COMBINE_CHUNK = 64


def _combine_kernel(dest_ref, dest1_ref, dest2_ref, ys_ref, rec_ref, x1_ref, gpost_ref, gate2_ref, o_ref,
                    buf, sem, *, tm):
    i = pl.program_id(0)
    last = pl.num_programs(0) - 1
    ring = COMBINE_RING

    def row_copy(d_ref, s, k, r):
        d = d_ref[TOP_K * r + k]
        return pltpu.make_async_copy(
            ys_ref.at[pl.ds(pl.multiple_of(d * ROW_TILE, ROW_TILE), ROW_TILE), :],
            buf.at[s, k, pl.ds(r * ROW_TILE, ROW_TILE), :], sem.at[s])

    def wait_tile(s):
        for k in range(TOP_K):
            pltpu.make_async_copy(ys_ref.at[pl.ds(0, tm * ROW_TILE), :], buf.at[s, k], sem.at[s]).wait()

    @pl.when(i == 0)
    def _():
        def issue(r, c):
            for k in range(TOP_K):
                row_copy(dest_ref, 0, k, r).start(priority=k)
                row_copy(dest1_ref, 1, k, r).start(priority=k)
            return c

        lax.fori_loop(0, tm, issue, 0, unroll=DMA_UNROLL)

    slot = lax.rem(i, ring)
    ahead = lax.rem(i + 2, ring)
    wait_tile(slot)

    gain = gate2_ref[...] * gpost_ref[...]
    for c in range(tm // COMBINE_CHUNK):
        rows = slice(c * COMBINE_CHUNK, (c + 1) * COMBINE_CHUNK)
        words = pl.ds(c * COMBINE_CHUNK * ROW_TILE, COMBINE_CHUNK * ROW_TILE)
        w = rec_ref[rows, :]
        y = (w[:, 4:5] * _load_position_major(buf.at[slot, 0, words], COMBINE_CHUNK)
             + w[:, 5:6] * _load_position_major(buf.at[slot, 1, words], COMBINE_CHUNK))
        o_ref[rows, :] = x1_ref[rows, :] + (y * lax.rsqrt(jnp.mean(y * y, axis=-1, keepdims=True) + EPS)) * gain
        for r in range(c * COMBINE_CHUNK, (c + 1) * COMBINE_CHUNK):
            for k in range(TOP_K):
                row_copy(dest2_ref, ahead, k, r).start(priority=k)

    @pl.when(i == last)
    def _():
        wait_tile(lax.rem(i + 1, ring))
        wait_tile(ahead)


def _combine(dest, ys, rec, x1, g_post, gate2, *, seq, tm):
    t, d = x1.shape
    last = t // tm - 1
    ahead = lambda n: pl.BlockSpec((TOP_K * tm,), lambda i: (jnp.minimum(i + n, last),), memory_space=pltpu.SMEM)
    return pl.pallas_call(
        functools.partial(_combine_kernel, tm=tm),
        grid=(t // tm,),
        in_specs=[ahead(0), ahead(1), ahead(2),
                  pl.BlockSpec(memory_space=pl.ANY),
                  pl.BlockSpec((tm, ROUTE_LANES), lambda i: (i, 0)),
                  pl.BlockSpec((tm, d), lambda i: (i, 0)),
                  pl.BlockSpec((1, d), lambda i: (0, 0)),
                  pl.BlockSpec((None, 1, d), lambda i: ((i * tm) // seq, 0, 0))],
        out_specs=pl.BlockSpec((tm, d), lambda i: (i, 0)),
        out_shape=jax.ShapeDtypeStruct((t, d), f32),
        scratch_shapes=[pltpu.VMEM((COMBINE_RING, TOP_K, tm * ROW_TILE, LANES), u32),
                        pltpu.SemaphoreType.DMA((COMBINE_RING,))],
        compiler_params=_cparams(("arbitrary",)),
        name="combine",
    )(dest, dest, dest, ys, rec, x1, g_post, gate2)


def _layer(x, c, w_ada, b_ada, g_pre_mix, g_post_mix, w_in, b_in, w_dw, b_dw, ln_conv_g, ln_conv_b,
           w_conv_out, b_conv_out, rel_bias, w_attn_out, w_out, g_pre_ffn, g_post_ffn,
           w_router_group, b_router_group, w_router_expert, b_router_expert, w1, w3, w2):
    batch, seq, d = x.shape
    t = batch * seq
    row = lambda v: v.reshape(1, -1)

    c_pad = jnp.zeros((SUBLANES, d), f32).at[:batch].set(c)
    mod = _ada(c_pad, w_ada, row(b_ada))[:batch]
    shift1, scale1, gate1, shift2, scale2, gate2 = [m.reshape(batch, 1, d) for m in jnp.split(mod, 6, axis=-1)]

    x2 = x.reshape(t, d)
    zt, qkv = _inproj(x2, row(g_pre_mix), scale1, shift1, w_in.astype(bf16), row(b_in), seq=seq, tm=INPROJ_TM)

    conv_act = _conv(zt.reshape(batch, seq, ZT_COLS), w_dw, row(b_dw), row(ln_conv_g), row(ln_conv_b),
                     tm=512).reshape(t, CONV_CH)

    os_, lses = [], []
    for gi, (_, dil) in enumerate(DILATED_GROUPS):
        o, lse = _attn_group(qkv, rel_bias.reshape(-1), gi, dil, batch=batch, seq=seq)
        os_.append(o)
        lses.append(lse)

    pad = ROUTE_LANES - N_GROUPS - N_EXPERTS
    w_r = jnp.concatenate([w_router_group, w_router_expert, jnp.zeros((d, pad), f32)], axis=1).astype(bf16)
    b_r = row(jnp.concatenate([b_router_group, b_router_expert.reshape(-1), jnp.zeros((pad,), f32)]))
    x1, h2p, logits = _merge(x2, conv_act, os_, lses, zt, w_conv_out.astype(bf16), row(b_conv_out),
                             w_attn_out.astype(bf16), w_out.astype(bf16), row(g_post_mix), gate1,
                             row(g_pre_ffn), scale2, shift2, w_r, b_r, seq=seq, tm=MERGE_TM, n_sub=MERGE_SUB)

    rec, rec_t, cnt = _route(logits, tm=512)

    counts = cnt[0, :N_EXPERTS].astype(jnp.int32)
    pcounts = (counts + MOE_ROWS - 1) // MOE_ROWS * MOE_ROWS
    pends = jnp.cumsum(pcounts)
    pstarts = pends - pcounts
    n_blocks = (t * TOP_K + N_EXPERTS * (MOE_ROWS - 1) + MOE_ROWS - 1) // MOE_ROWS
    n_used = (pends[-1:] // MOE_ROWS).astype(jnp.int32)
    tail_blocks = jnp.concatenate([n_used, n_blocks - n_used])
    eid = rec_t[0:TOP_K].astype(jnp.int32)
    rank = rec_t[TOP_K:2 * TOP_K].astype(jnp.int32)
    start_of = jnp.sum(jnp.where(eid[..., None] == jnp.arange(N_EXPERTS), pstarts, 0), axis=-1)
    dest = jnp.transpose(start_of + rank).reshape(t * TOP_K)
    tail = n_blocks - N_EXPERTS + jnp.arange(N_EXPERTS, dtype=jnp.int32)
    zero_blk = jnp.concatenate([jnp.where(pcounts > counts, pends // MOE_ROWS - 1, -1),
                                jnp.where(tail >= n_used[0], tail, -1)]).astype(jnp.int32)

    xs = _dispatch(zero_blk, dest, h2p, p_rows=n_blocks * MOE_ROWS, tm=512)
    ys = _experts((pstarts // MOE_ROWS).astype(jnp.int32), (pcounts // MOE_ROWS).astype(jnp.int32),
                  tail_blocks, counts, xs, w1, w3, w2)
    out = _combine(dest, ys, rec, x1, row(g_post_ffn), gate2, seq=seq, tm=512)
    return out.reshape(batch, seq, d)


def kernel(x, c, w_ada, b_ada, g_pre_mix, g_post_mix, w_in, b_in, w_dw, b_dw, ln_conv_g, ln_conv_b,
           w_conv_out, b_conv_out, rel_bias, w_attn_out, w_out, g_pre_ffn, g_post_ffn,
           w_router_group, b_router_group, w_router_expert, b_router_expert, w1, w3, w2):
    depth = w_ada.shape[0]
    for l in range(depth):
        pick = (lambda a: a.reshape(a.shape[1:])) if depth == 1 else (lambda a, l=l: a[l])
        x = _layer(x, c, pick(w_ada), pick(b_ada), pick(g_pre_mix), pick(g_post_mix), pick(w_in),
                   pick(b_in), pick(w_dw), pick(b_dw), pick(ln_conv_g), pick(ln_conv_b),
                   pick(w_conv_out), pick(b_conv_out), rel_bias, pick(w_attn_out), pick(w_out),
                   pick(g_pre_ffn), pick(g_post_ffn), pick(w_router_group), pick(b_router_group),
                   pick(w_router_expert), pick(b_router_expert), pick(w1), pick(w3), pick(w2))
    return x
```

```python
import functools
import math

import numpy as np
import jax
import jax.numpy as jnp
from jax import lax
from jax.experimental import pallas as pl
from jax.experimental.pallas import tpu as pltpu

D_MODEL = 2048
CONV_CH = 1024
CONV_WIDTH = 31
N_ATTN_HEADS = 12
HEADS_PER_GROUP = 4
HEAD_DIM = 128
ATTN_WIDTH = N_ATTN_HEADS * HEAD_DIM
DILATED_GROUPS = ((128, 1), (512, 4), (2048, 16))
NUM_BUCKETS = 32
REL_MAX_DISTANCE = 1024
N_GROUPS = 8
EXPERTS_PER_GROUP = 8
N_EXPERTS = N_GROUPS * EXPERTS_PER_GROUP
TOP_K = 2
EXPERT_FF = 512
EPS = 1e-6
NEG_INF = -1e30

LANES = 128
SUBLANES = 8
SUBLANES_BF16 = 16
V7X_VMEM_LIMIT_BYTES = 56 * 1024 * 1024

GROUP_W = HEADS_PER_GROUP * HEAD_DIM
ZT_COLS = 2 * D_MODEL + 2 * CONV_CH
COL_GLU = 2 * D_MODEL
INPROJ_TN = 3 * GROUP_W
ZT_TILES = ZT_COLS // INPROJ_TN
INPROJ_TM = 1024
ATTN_HALF = 64
ATTN_QB = 128
ATTN_WIN = ATTN_QB + 2 * ATTN_HALF
MOE_ROWS = 256
MERGE_TM = 256
MERGE_SUB = 2
ROUTE_LANES = 128
ROUTE_FIELDS = 8
PACKED_W = D_MODEL // 2
ROW_TILE = PACKED_W // LANES

bf16 = jnp.bfloat16
f32 = jnp.float32
u32 = jnp.uint32
HI_MASK = 0xFFFF0000


def _cparams(sem):
    return pltpu.CompilerParams(dimension_semantics=sem, vmem_limit_bytes=V7X_VMEM_LIMIT_BYTES)


def _sigmoid(x):
    return 1.0 / (1.0 + jnp.exp(-x))


def _pack_pairs(v):
    n = v.shape[1] // 2
    lo = lax.bitcast_convert_type(v[:, :n].astype(bf16).astype(f32), u32)
    hi = lax.bitcast_convert_type(v[:, n:].astype(bf16).astype(f32), u32)
    return (lo >> 16) | (hi & u32(HI_MASK))


def _unpack_lo(u):
    return lax.bitcast_convert_type(u << 16, f32)


def _unpack_hi(u):
    return lax.bitcast_convert_type(u & u32(HI_MASK), f32)


def _store_position_major(ref, v):
    packed = _pack_pairs(v)
    for j in range(ROW_TILE):
        ref[pl.ds(j, v.shape[0], stride=ROW_TILE), :] = packed[:, j * LANES:(j + 1) * LANES]


def _load_position_major(ref, n):
    chunks = [ref[pl.ds(j, n, stride=ROW_TILE), :] for j in range(ROW_TILE)]
    return jnp.concatenate([_unpack_lo(c) for c in chunks] + [_unpack_hi(c) for c in chunks], axis=1)


def _ada_kernel(c_ref, w_ref, b_ref, o_ref):
    c = c_ref[...]
    a = (c * _sigmoid(c)).astype(bf16)
    o_ref[...] = jnp.dot(a, w_ref[...].astype(bf16), preferred_element_type=f32) + b_ref[...]


def _ada(c_pad, w_ada, b_ada):
    rows, d = c_pad.shape
    n = w_ada.shape[1]
    tn = 1024
    return pl.pallas_call(
        _ada_kernel,
        grid=(n // tn,),
        in_specs=[pl.BlockSpec((rows, d), lambda j: (0, 0)),
                  pl.BlockSpec((d, tn), lambda j: (0, j)),
                  pl.BlockSpec((1, tn), lambda j: (0, j))],
        out_specs=pl.BlockSpec((rows, tn), lambda j: (0, j)),
        out_shape=jax.ShapeDtypeStruct((rows, n), f32),
        compiler_params=_cparams(("arbitrary",)),
        name="ada_mod",
    )(c_pad, w_ada, b_ada)


def _inproj_kernel(x_ref, g_ref, scale0_ref, shift0_ref, scale_ref, shift_ref,
                   wa_ref, wb_ref, wc_ref, ba_ref, bb_ref, bc_ref, zt_ref, qkv_ref, h_scr, x_scr, slab_scr, xsem):
    i, j = pl.program_id(0), pl.program_id(1)
    tm = x_scr.shape[0]
    slot = jnp.bitwise_and(i, 1)
    nxt_tile = jnp.minimum(i + 1, pl.num_programs(0) - 1)

    def x_copy(tile):
        return pltpu.make_async_copy(x_ref.at[pl.ds(pl.multiple_of(tile * tm, tm), tm), :], x_scr, xsem)

    def prenorm(rows, sc_ref, sh_ref, dst_slot):
        x = x_scr[rows, :]
        ms = jnp.mean(x * x, axis=-1, keepdims=True)
        h = x * lax.rsqrt(ms + EPS) * g_ref[...]
        h_scr[dst_slot, rows, :] = (h * (1.0 + sc_ref[...]) + sh_ref[...]).astype(bf16)

    @pl.when(j == 0)
    def _():
        @pl.when(i == 0)
        def _():
            x_copy(0).start()
            x_copy(0).wait()
            prenorm(slice(None), scale0_ref, shift0_ref, 0)

        x_copy(nxt_tile).start()

    @pl.when(j == ZT_TILES)
    def _():
        x_copy(nxt_tile).wait()

    w_refs, b_refs = (wa_ref, wb_ref, wc_ref), (ba_ref, bb_ref, bc_ref)

    def chunk(c):
        return jnp.dot(h_scr[slot], w_refs[c][...], preferred_element_type=f32) + b_refs[c][...]

    n_parts = len(DILATED_GROUPS)
    part_rows = [(tm // n_parts // SUBLANES_BF16) * SUBLANES_BF16] * (n_parts - 1)
    part_rows.append(tm - sum(part_rows))
    part_start = [sum(part_rows[:p]) for p in range(n_parts)]

    @pl.when(j < ZT_TILES)
    def _():
        for c in range(INPROJ_TN // GROUP_W):
            zt_ref[:, c * GROUP_W:(c + 1) * GROUP_W] = chunk(c).astype(bf16)

    for gi, (_, dil) in enumerate(DILATED_GROUPS):
        @pl.when(j == ZT_TILES + gi)
        def _(gi=gi, dil=dil):
            prenorm(slice(part_start[gi], part_start[gi] + part_rows[gi]), scale_ref, shift_ref, 1 - slot)
            per = tm // dil
            for c in range(INPROJ_TN // GROUP_W):
                z = chunk(c)
                if dil == 1:
                    qkv_ref[:, c * GROUP_W:(c + 1) * GROUP_W] = z.astype(bf16)
                    continue
                slabs = range(GROUP_W // LANES)
                for s in slabs:
                    slab_scr[0, s] = z[:, s * LANES:(s + 1) * LANES]
                src, stride, first = 0, dil, lambda r: r
                if dil == 16:
                    quarter = tm // 4
                    for s in slabs:
                        for r4 in range(4):
                            slab_scr[1, s, r4 * quarter:(r4 + 1) * quarter, :] = (
                                slab_scr[0, s, pl.ds(r4, quarter, stride=4), :])
                    src, stride, first = 1, 4, lambda r: (r % 4) * quarter + r // 4
                for r in range(dil):
                    for s in slabs:
                        col = c * GROUP_W + s * LANES
                        qkv_ref[r * per:(r + 1) * per, col:col + LANES] = (
                            slab_scr[src, s, pl.ds(first(r), per, stride=stride), :].astype(bf16))


def _inproj_col_block(j, c):
    n_blocks = (2 * CONV_CH + 3 * ATTN_WIDTH + 2 * D_MODEL) // GROUP_W
    gate0 = (2 * CONV_CH + 3 * ATTN_WIDTH) // GROUP_W
    q0 = 2 * CONV_CH // GROUP_W
    per_proj = ATTN_WIDTH // GROUP_W
    token_major = lax.rem(gate0 + (INPROJ_TN // GROUP_W) * j + c, n_blocks)
    return jnp.where(j < ZT_TILES, token_major, q0 + (j - ZT_TILES) + per_proj * c)


def _inproj(x2, g, scale, shift, w, b, *, seq, tm):
    t, d = x2.shape
    n = w.shape[1]
    tn = INPROJ_TN
    n_groups = n // tn - ZT_TILES
    last = t // tm - 1
    first_map = lambda i, j: (0, 0, 0)
    next_map = lambda i, j: ((jnp.minimum(i + 1, last) * tm) // seq, 0, 0)
    chunks = range(tn // GROUP_W)
    w_specs = [pl.BlockSpec((d, GROUP_W), lambda i, j, c=c: (0, _inproj_col_block(j, c))) for c in chunks]
    b_specs = [pl.BlockSpec((1, GROUP_W), lambda i, j, c=c: (0, _inproj_col_block(j, c))) for c in chunks]
    return pl.pallas_call(
        _inproj_kernel,
        grid=(t // tm, n // tn),
        in_specs=[pl.BlockSpec(memory_space=pl.ANY),
                  pl.BlockSpec((1, d), lambda i, j: (0, 0)),
                  pl.BlockSpec((None, 1, d), first_map), pl.BlockSpec((None, 1, d), first_map),
                  pl.BlockSpec((None, 1, d), next_map), pl.BlockSpec((None, 1, d), next_map)]
                 + w_specs + b_specs,
        out_specs=[pl.BlockSpec((tm, tn), lambda i, j: (i, jnp.minimum(j, ZT_TILES - 1))),
                   pl.BlockSpec((None, tm, tn), lambda i, j: (jnp.maximum(j - ZT_TILES, 0), i, 0))],
        out_shape=[jax.ShapeDtypeStruct((t, ZT_COLS), bf16),
                   jax.ShapeDtypeStruct((n_groups, t, tn), bf16)],
        scratch_shapes=[pltpu.VMEM((2, tm, d), bf16), pltpu.VMEM((tm, d), f32),
                        pltpu.VMEM((2, GROUP_W // LANES, tm, LANES), f32), pltpu.SemaphoreType.DMA(())],
        compiler_params=_cparams(("arbitrary", "arbitrary")),
        name="inproj",
    )(x2, g, scale, shift, scale, shift, w, w, w, b, b, b)


CONV_HALO = 16
CONV_CHUNK = 32
CONV_SUB = CONV_CH // LANES
NORM_CHUNK = 64


def _conv_kernel(ap_ref, ac_ref, an_ref, gp_ref, gc_ref, gn_ref, w_ref, bdw_ref, lng_ref, lnb_ref,
                 o_ref, u_scr, y_scr, *, tm):
    i = pl.program_id(1)
    last = pl.num_programs(1) - 1

    def glu(a_ref, g_ref):
        return a_ref[...].astype(f32) * _sigmoid(g_ref[...].astype(f32))

    def put(row0, val):
        for j in range(CONV_SUB):
            u_scr[pl.ds(row0 * CONV_SUB + j, val.shape[0], stride=CONV_SUB), :] = (
                val[:, j * LANES:(j + 1) * LANES])

    put(0, jnp.where(i > 0, glu(ap_ref, gp_ref), 0.0))
    put(CONV_HALO, glu(ac_ref, gc_ref))
    put(CONV_HALO + tm, jnp.where(i < last, glu(an_ref, gn_ref), 0.0))

    def chunk(c, carry):
        r0 = c * CONV_CHUNK
        acc = jnp.zeros((CONV_CHUNK, CONV_SUB, LANES), f32) + bdw_ref[...][None]
        for k in range(CONV_WIDTH):
            start = pl.multiple_of((r0 + k + 1) * CONV_SUB, CONV_SUB)
            xk = u_scr[pl.ds(start, CONV_CHUNK * CONV_SUB), :].reshape(CONV_CHUNK, CONV_SUB, LANES)
            acc = acc + w_ref[k][None] * xk
        out0 = pl.multiple_of(r0 * CONV_SUB, CONV_CHUNK * CONV_SUB)
        y_scr[pl.ds(out0, CONV_CHUNK * CONV_SUB), :] = acc.reshape(CONV_CHUNK * CONV_SUB, LANES)
        return carry

    lax.fori_loop(0, tm // CONV_CHUNK, chunk, 0)

    def norm(c, carry):
        r0 = pl.multiple_of(c * NORM_CHUNK, NORM_CHUNK)
        acc = jnp.concatenate(
            [y_scr[pl.ds(r0 * CONV_SUB + j, NORM_CHUNK, stride=CONV_SUB), :] for j in range(CONV_SUB)],
            axis=1)
        mu = jnp.mean(acc, axis=-1, keepdims=True)
        cen = acc - mu
        var = jnp.mean(cen * cen, axis=-1, keepdims=True)
        y = cen * lax.rsqrt(var + EPS) * lng_ref[...] + lnb_ref[...]
        o_ref[pl.ds(r0, NORM_CHUNK), :] = (y * _sigmoid(y)).astype(bf16)
        return carry

    lax.fori_loop(0, tm // NORM_CHUNK, norm, 0, unroll=2)


def _conv(z3, w_dw, b_dw, ln_g, ln_b, *, tm):
    b, s, _ = z3.shape
    hb = tm // CONV_HALO
    nhb = s // CONV_HALO
    ca, cg = COL_GLU // CONV_CH, COL_GLU // CONV_CH + 1

    def halo(col, which):
        if which < 0:
            return pl.BlockSpec((None, CONV_HALO, CONV_CH),
                                lambda bb, i: (bb, jnp.maximum(i * hb - 1, 0), col))
        return pl.BlockSpec((None, CONV_HALO, CONV_CH),
                            lambda bb, i: (bb, jnp.minimum((i + 1) * hb, nhb - 1), col))

    cur = lambda col: pl.BlockSpec((None, tm, CONV_CH), lambda bb, i: (bb, i, col))
    vec = pl.BlockSpec((1, CONV_CH), lambda bb, i: (0, 0))
    return pl.pallas_call(
        functools.partial(_conv_kernel, tm=tm),
        grid=(b, s // tm),
        in_specs=[halo(ca, -1), cur(ca), halo(ca, 1), halo(cg, -1), cur(cg), halo(cg, 1),
                  pl.BlockSpec((CONV_WIDTH, CONV_SUB, LANES), lambda bb, i: (0, 0, 0)),
                  pl.BlockSpec((CONV_SUB, LANES), lambda bb, i: (0, 0)), vec, vec],
        out_specs=pl.BlockSpec((None, tm, CONV_CH), lambda bb, i: (bb, i, 0)),
        out_shape=jax.ShapeDtypeStruct((b, s, CONV_CH), bf16),
        scratch_shapes=[pltpu.VMEM(((tm + 2 * CONV_HALO) * CONV_SUB, LANES), f32),
                        pltpu.VMEM((tm * CONV_SUB, LANES), f32)],
        compiler_params=_cparams(("arbitrary", "arbitrary")),
        name="conv_branch",
    )(z3, z3, z3, z3, z3, z3, w_dw.reshape(CONV_WIDTH, CONV_SUB, LANES), b_dw.reshape(CONV_SUB, LANES),
      ln_g, ln_b)


def _t5_bucket_table(dil):
    delta = (np.arange(ATTN_WIN)[None, :] - ATTN_HALF) - np.arange(ATTN_QB)[:, None]
    rel = delta * dil
    nb = NUM_BUCKETS // 2
    max_exact = nb // 2
    n = np.abs(rel)
    nf = np.maximum(n, 1).astype(np.float32)
    large = max_exact + (np.log(nf / np.float32(max_exact)) / np.float32(math.log(REL_MAX_DISTANCE / max_exact))
                         * np.float32(nb - max_exact)).astype(np.int32)
    large = np.minimum(large, nb - 1)
    bucket = np.where(rel > 0, nb, 0) + np.where(n < max_exact, n, large)
    return np.where(np.abs(delta) <= ATTN_HALF, bucket, -1).astype(np.int32)


def _attn_kernel(rb_ref, bkt_ref, q_ref, kp_ref, kc_ref, kn_ref, vp_ref, vc_ref, vn_ref, o_ref, lse_ref,
                 q_scr, k_scr, v_scr, bias_scr, *, gi, dil, n_tiles, sub_len):
    i = pl.program_id(1)
    h = ATTN_HALF
    tile = INPROJ_TM
    per = tile // dil
    tq = n_tiles * per

    @pl.when((pl.program_id(0) == 0) & (i == 0))
    def _():
        bk = bkt_ref[...]
        for hh in range(HEADS_PER_GROUP):
            acc = jnp.full((ATTN_QB, ATTN_WIN), NEG_INF, f32)
            for b_id in range(NUM_BUCKETS):
                acc = jnp.where(bk == b_id, rb_ref[b_id * N_ATTN_HEADS + gi * HEADS_PER_GROUP + hh], acc)
            bias_scr[hh] = acc

    halo_rows = kp_ref.shape[0]
    for r in range(dil):
        prev_lo = (r + 1) * per - h if halo_rows == tile else 0
        next_lo = r * per if halo_rows == tile else 0
        for dst, prv, cur, nxt in ((k_scr, kp_ref, kc_ref, kn_ref), (v_scr, vp_ref, vc_ref, vn_ref)):
            dst[r, 0:h, :] = prv[prev_lo:prev_lo + h, :]
            for t in range(n_tiles):
                dst[r, h + t * per:h + (t + 1) * per, :] = cur[t * tile + r * per:t * tile + (r + 1) * per, :]
            dst[r, h + tq:, :] = nxt[next_lo:next_lo + h, :]
        for t in range(n_tiles):
            q_scr[r, t * per:(t + 1) * per, :] = q_ref[t * tile + r * per:t * tile + (r + 1) * per, :]

    scale = HEAD_DIM ** -0.5
    col = lax.broadcasted_iota(jnp.int32, (ATTN_QB, ATTN_WIN), 1)
    lane = lax.broadcasted_iota(jnp.int32, (ATTN_QB, LANES), 1)

    def rows(start, n):
        if dil == 1:
            return pl.ds(pl.multiple_of(start, ATTN_QB), n)
        return pl.ds(start, n, stride=dil)

    def unit(u, carry):
        r = jnp.bitwise_and(u, dil - 1)
        jb = lax.shift_right_logical(u, jnp.int32(dil.bit_length() - 1))
        start = jb * (ATTN_QB * dil) + r
        p0 = pl.multiple_of(jb * ATTN_QB, ATTN_QB)
        cols = [slice(hh * HEAD_DIM, (hh + 1) * HEAD_DIM) for hh in range(HEADS_PER_GROUP)]
        q = [q_scr[r, pl.ds(p0, ATTN_QB), cs] for cs in cols]
        kw = [k_scr[r, pl.ds(p0, ATTN_WIN), cs] for cs in cols]
        vw = [v_scr[r, pl.ds(p0, ATTN_WIN), cs] for cs in cols]
        key0 = i * tq + jb * ATTN_QB - ATTN_HALF
        valid = (col + key0 >= 0) & (col + key0 < sub_len)
        s = [lax.dot_general(q[hh], kw[hh], (((1,), (1,)), ((), ())), preferred_element_type=f32)
             for hh in range(HEADS_PER_GROUP)]
        s = [jnp.where(valid, s[hh] * scale + bias_scr[hh], NEG_INF) for hh in range(HEADS_PER_GROUP)]
        m = [jnp.max(x, axis=-1, keepdims=True) for x in s]
        p = [jnp.exp(s[hh] - m[hh]) for hh in range(HEADS_PER_GROUP)]
        den = [jnp.sum(x, axis=-1, keepdims=True) for x in p]
        pv = [jnp.dot(p[hh].astype(bf16), vw[hh], preferred_element_type=f32)
              for hh in range(HEADS_PER_GROUP)]
        lse_tile = jnp.zeros((ATTN_QB, LANES), f32)
        for hh in range(HEADS_PER_GROUP):
            o_ref[hh, rows(start, ATTN_QB), :] = pv[hh] / den[hh]
            lse_tile = jnp.where(lane == hh, m[hh] + jnp.log(den[hh]), lse_tile)
        lse_ref[rows(start, ATTN_QB), :] = lse_tile
        return carry

    lax.fori_loop(0, n_tiles * tile // ATTN_QB, unit, 0, unroll=4)


def _attn_group(qkv, rel_bias_flat, gi, dil, *, batch, seq):
    t = batch * seq
    tile = INPROJ_TM
    n_tiles = max(1, ATTN_QB * dil // tile)
    ts = n_tiles * tile
    steps = seq // ts
    tq = ts // dil
    halo = ATTN_HALF if dil == 1 else tile
    hpt = ts // halo
    hps = seq // halo

    def cur(which):
        return pl.BlockSpec((None, ts, GROUP_W), lambda b, i: (gi, b * steps + i, which))

    def prev(which):
        return pl.BlockSpec((None, halo, GROUP_W),
                            lambda b, i: (gi, b * hps + jnp.maximum(i * hpt - 1, 0), which))

    def nxt(which):
        return pl.BlockSpec((None, halo, GROUP_W),
                            lambda b, i: (gi, b * hps + jnp.minimum((i + 1) * hpt, hps - 1), which))

    bkt = jnp.asarray(_t5_bucket_table(dil))
    return pl.pallas_call(
        functools.partial(_attn_kernel, gi=gi, dil=dil, n_tiles=n_tiles, sub_len=seq // dil),
        grid=(batch, steps),
        in_specs=[pl.BlockSpec(memory_space=pltpu.SMEM),
                  pl.BlockSpec((ATTN_QB, ATTN_WIN), lambda b, i: (0, 0)),
                  cur(0), prev(1), cur(1), nxt(1), prev(2), cur(2), nxt(2)],
        out_specs=[pl.BlockSpec((HEADS_PER_GROUP, ts, LANES), lambda b, i: (0, b * steps + i, 0)),
                   pl.BlockSpec((ts, LANES), lambda b, i: (b * steps + i, 0))],
        out_shape=[jax.ShapeDtypeStruct((HEADS_PER_GROUP, t, LANES), f32),
                   jax.ShapeDtypeStruct((t, LANES), f32)],
        scratch_shapes=[pltpu.VMEM((dil, tq, GROUP_W), bf16),
                        pltpu.VMEM((dil, tq + 2 * ATTN_HALF, GROUP_W), bf16),
                        pltpu.VMEM((dil, tq + 2 * ATTN_HALF, GROUP_W), bf16),
                        pltpu.VMEM((HEADS_PER_GROUP, ATTN_QB, ATTN_WIN), f32)],
        compiler_params=_cparams(("arbitrary", "arbitrary")),
        name=f"attn_g{gi}",
    )(rel_bias_flat, bkt, qkv, qkv, qkv, qkv, qkv, qkv, qkv)


def _merge_kernel(x_ref, ca_ref, o0_ref, o1_ref, o2_ref, l0_ref, l1_ref, l2_ref, ga_ref, gb_ref,
                  wco_ref, bco_ref, wao_ref, wo_ref, gpost_ref, gate1_ref, gpre_ref, scale2_ref,
                  shift2_ref, wr_ref, br_ref, x1_ref, h2p_ref, logit_ref, *, n_sub):
    hm = x_ref.shape[0] // n_sub
    subs = [slice(k * hm, (k + 1) * hm) for k in range(n_sub)]

    branch_a = [jnp.dot(ca_ref[rs, :], wco_ref[...], preferred_element_type=f32) + bco_ref[...] for rs in subs]

    def attention_mix(rs):
        l0, l1, l2 = l0_ref[rs, :], l1_ref[rs, :], l2_ref[rs, :]
        lmax = jnp.maximum(jnp.maximum(l0, l1), l2)
        e0, e1, e2 = jnp.exp(l0 - lmax), jnp.exp(l1 - lmax), jnp.exp(l2 - lmax)
        inv = 1.0 / (e0 + e1 + e2)
        parts = []
        for hh in range(HEADS_PER_GROUP):
            ls = slice(hh, hh + 1)
            parts.append((e0[:, ls] * inv[:, ls]) * o0_ref[hh, rs, :] + (e1[:, ls] * inv[:, ls]) * o1_ref[hh, rs, :]
                         + (e2[:, ls] * inv[:, ls]) * o2_ref[hh, rs, :])
        return jnp.concatenate(parts, axis=1).astype(bf16)

    att = [attention_mix(rs) for rs in subs]
    branch_b = [jnp.dot(a, wao_ref[...], preferred_element_type=f32) for a in att]
    mix = [(_sigmoid(ga_ref[rs, :].astype(f32)) * branch_a[k]
            + _sigmoid(gb_ref[rs, :].astype(f32)) * branch_b[k]).astype(bf16) for k, rs in enumerate(subs)]
    y = [jnp.dot(m, wo_ref[...], preferred_element_type=f32) for m in mix]

    post_gain = gate1_ref[...] * gpost_ref[...]
    pre_gain = gpre_ref[...] * (1.0 + scale2_ref[...])
    h2s = []
    for k, rs in enumerate(subs):
        x1 = x_ref[rs, :] + (y[k] * lax.rsqrt(jnp.mean(y[k] * y[k], axis=-1, keepdims=True) + EPS)) * post_gain
        x1_ref[rs, :] = x1
        h2 = (x1 * lax.rsqrt(jnp.mean(x1 * x1, axis=-1, keepdims=True) + EPS)) * pre_gain + shift2_ref[...]
        _store_position_major(h2p_ref.at[pl.ds(k * hm * ROW_TILE, hm * ROW_TILE), :], h2)
        h2s.append(h2.astype(bf16))
    for k, rs in enumerate(subs):
        logit_ref[rs, :] = jnp.dot(h2s[k], wr_ref[...], preferred_element_type=f32) + br_ref[...]


def _merge(x2, conv_act, os_, lses, zt, w_co, b_co, w_ao, w_o, g_post, gate1, g_pre, scale2, shift2,
           w_r, b_r, *, seq, tm, n_sub):
    t, d = x2.shape
    rows = lambda w: pl.BlockSpec((tm, w), lambda i: (i, 0))
    heads = pl.BlockSpec((HEADS_PER_GROUP, tm, LANES), lambda i: (0, i, 0))
    full = lambda a: pl.BlockSpec(a.shape, lambda i: (0,) * a.ndim)
    perb = pl.BlockSpec((None, 1, d), lambda i: ((i * tm) // seq, 0, 0))
    return pl.pallas_call(
        functools.partial(_merge_kernel, n_sub=n_sub),
        grid=(t // tm,),
        in_specs=[rows(d), rows(CONV_CH), heads, heads, heads, rows(LANES), rows(LANES), rows(LANES),
                  pl.BlockSpec((tm, d), lambda i: (i, 0)), pl.BlockSpec((tm, d), lambda i: (i, 1)),
                  full(w_co), full(b_co), full(w_ao), full(w_o), full(g_post), perb, full(g_pre),
                  perb, perb, full(w_r), full(b_r)],
        out_specs=[rows(d), pl.BlockSpec((tm * ROW_TILE, LANES), lambda i: (i, 0)), rows(ROUTE_LANES)],
        out_shape=[jax.ShapeDtypeStruct((t, d), f32),
                   jax.ShapeDtypeStruct((t * ROW_TILE, LANES), u32),
                   jax.ShapeDtypeStruct((t, ROUTE_LANES), f32)],
        compiler_params=_cparams(("arbitrary",)),
        name="merge",
    )(x2, conv_act, *os_, *lses, zt, zt, w_co, b_co, w_ao, w_o, g_post, gate1, g_pre, scale2, shift2,
      w_r, b_r)


def _route_kernel(logit_ref, rec_ref, rect_ref, cnt_ref, carry):
    i = pl.program_id(0)

    @pl.when(i == 0)
    def _():
        carry[...] = jnp.zeros_like(carry)

    lg = logit_ref[...]
    tm = lg.shape[0]
    lane = lax.broadcasted_iota(jnp.int32, lg.shape, 1).astype(f32)
    big = float(2 * ROUTE_LANES)

    def first_max(mask):
        v = jnp.max(jnp.where(mask, lg, -jnp.inf), axis=-1, keepdims=True)
        idx = jnp.min(jnp.where(mask & (lg == v), lane, big), axis=-1, keepdims=True)
        return v, idx

    gmask = lane < N_GROUPS
    gmax, gsel = first_max(gmask)
    p_g = 1.0 / jnp.sum(jnp.where(gmask, jnp.exp(lg - gmax), 0.0), axis=-1, keepdims=True)
    e_lo = N_GROUPS + EXPERTS_PER_GROUP * gsel
    emask = (lane >= e_lo) & (lane < e_lo + EXPERTS_PER_GROUP)
    v0, i0 = first_max(emask)
    v1, i1 = first_max(emask & (lane != i0))
    t1 = jnp.exp(v1 - v0)
    w0 = p_g / (1.0 + t1)
    w1 = p_g * t1 / (1.0 + t1)
    e0 = i0 - N_GROUPS
    e1 = i1 - N_GROUPS

    hit0 = lane == e0
    hit1 = lane == e1
    onehot = jnp.where(hit0 | hit1, 1.0, 0.0)
    r_i = lax.broadcasted_iota(jnp.int32, (tm, tm), 0)
    c_i = lax.broadcasted_iota(jnp.int32, (tm, tm), 1)
    tril = jnp.where(c_i < r_i, 1.0, 0.0).astype(bf16)
    before = jnp.dot(tril, onehot.astype(bf16), preferred_element_type=f32) + carry[...]
    rank0 = jnp.sum(jnp.where(hit0, before, 0.0), axis=-1, keepdims=True)
    rank1 = jnp.sum(jnp.where(hit1, before, 0.0), axis=-1, keepdims=True)
    carry[...] = carry[...] + jnp.sum(onehot, axis=0, keepdims=True)
    cnt_ref[...] = carry[...]

    rec = jnp.zeros_like(lg)
    for slot, val in enumerate((e0, e1, rank0, rank1, w0, w1)):
        rec = jnp.where(lane == slot, val, rec)
    rec_ref[...] = rec
    rect_ref[...] = jnp.transpose(rec)[0:ROUTE_FIELDS, :]


def _route(logits, *, tm):
    t = logits.shape[0]
    return pl.pallas_call(
        _route_kernel,
        grid=(t // tm,),
        in_specs=[pl.BlockSpec((tm, ROUTE_LANES), lambda i: (i, 0))],
        out_specs=[pl.BlockSpec((tm, ROUTE_LANES), lambda i: (i, 0)),
                   pl.BlockSpec((ROUTE_FIELDS, tm), lambda i: (0, i)),
                   pl.BlockSpec((1, ROUTE_LANES), lambda i: (0, 0))],
        out_shape=[jax.ShapeDtypeStruct((t, ROUTE_LANES), f32),
                   jax.ShapeDtypeStruct((ROUTE_FIELDS, t), f32),
                   jax.ShapeDtypeStruct((1, ROUTE_LANES), f32)],
        scratch_shapes=[pltpu.VMEM((1, ROUTE_LANES), f32)],
        compiler_params=_cparams(("arbitrary",)),
        name="route",
    )(logits)


DMA_UNROLL = 8


def _dispatch_kernel(zero_blk_ref, dest_ref, h_ref, xs_ref, zero_scr, sem, zsem, *, tm):
    def row_copy(r, d):
        return pltpu.make_async_copy(h_ref.at[pl.ds(pl.multiple_of(r * ROW_TILE, ROW_TILE), ROW_TILE), :],
                                     xs_ref.at[pl.ds(pl.multiple_of(d * ROW_TILE, ROW_TILE), ROW_TILE), :],
                                     sem)

    def zero_copy(blk, zs):
        n = MOE_ROWS * ROW_TILE
        return pltpu.make_async_copy(zero_scr, xs_ref.at[pl.ds(pl.multiple_of(blk * n, n), n), :], zsem.at[zs])

    def start(j, zs):
        @pl.when(zero_blk_ref[j] >= 0)
        def _():
            zero_copy(zero_blk_ref[j], zs).start()

    def wait(j, zs):
        @pl.when(zero_blk_ref[j] >= 0)
        def _():
            zero_copy(0, zs).wait()

    @pl.when(pl.program_id(0) == 0)
    def _():
        zero_scr[...] = jnp.zeros_like(zero_scr)
        lax.fori_loop(0, 2 * N_EXPERTS, lambda j, c: (start(j, jnp.where(j < N_EXPERTS, 0, 1)), c)[1], 0)
        lax.fori_loop(0, N_EXPERTS, lambda j, c: (wait(j, 0), c)[1], 0)

    @pl.when(pl.program_id(0) == pl.num_programs(0) - 1)
    def _():
        lax.fori_loop(N_EXPERTS, 2 * N_EXPERTS, lambda j, c: (wait(j, 1), c)[1], 0)

    def issue(r, c):
        row_copy(r, dest_ref[TOP_K * r]).start(priority=0)
        row_copy(r, dest_ref[TOP_K * r + 1]).start(priority=1)
        return c

    lax.fori_loop(0, tm, issue, 0, unroll=DMA_UNROLL)
    for _ in range(TOP_K):
        pltpu.make_async_copy(h_ref, xs_ref.at[pl.ds(0, tm * ROW_TILE), :], sem).wait()


def _dispatch(zero_blk, dest, h2p, *, p_rows, tm):
    grid_spec = pltpu.PrefetchScalarGridSpec(
        num_scalar_prefetch=1,
        grid=(h2p.shape[0] // (tm * ROW_TILE),),
        in_specs=[pl.BlockSpec((TOP_K * tm,), lambda i, *_: (i,), memory_space=pltpu.SMEM),
                  pl.BlockSpec((tm * ROW_TILE, LANES), lambda i, *_: (i, 0))],
        out_specs=pl.BlockSpec(memory_space=pl.ANY),
        scratch_shapes=[pltpu.VMEM((MOE_ROWS * ROW_TILE, LANES), u32), pltpu.SemaphoreType.DMA(()),
                        pltpu.SemaphoreType.DMA((2,))],
    )
    return pl.pallas_call(
        functools.partial(_dispatch_kernel, tm=tm),
        grid_spec=grid_spec,
        out_shape=jax.ShapeDtypeStruct((p_rows * ROW_TILE, LANES), u32),
        compiler_params=_cparams(("arbitrary",)),
        name="dispatch",
    )(zero_blk, dest, h2p)


def _expert_kernel(bstart_ref, nblk_ref, tail_ref, cnt_ref, xs_ref, w1_ref, w3_ref, w2_ref, ys_ref,
                   wf1, wf3, wf2, w1b, w3b, w2b, xbuf, ybuf, sem_in, sem_out, sem_w):
    e = pl.program_id(0)
    last = pl.num_programs(0) - 1
    ws = jnp.bitwise_and(e, 1)
    n = nblk_ref[e]
    g0 = bstart_ref[e]
    n_used = tail_ref[0]
    blk = MOE_ROWS * ROW_TILE

    def hbm_rows(ref, g):
        return ref.at[pl.ds(pl.multiple_of(g * blk, blk), blk), :]

    def in_copy(g, slot):
        return pltpu.make_async_copy(hbm_rows(xs_ref, g), xbuf.at[slot], sem_in.at[slot])

    def out_copy(g, slot):
        return pltpu.make_async_copy(ybuf.at[slot], hbm_rows(ys_ref, g), sem_out.at[slot])

    def weight_copies(ex, slot):
        return [pltpu.make_async_copy(w1_ref.at[ex], wf1.at[slot], sem_w.at[slot]),
                pltpu.make_async_copy(w3_ref.at[ex], wf3.at[slot], sem_w.at[slot]),
                pltpu.make_async_copy(w2_ref.at[ex], wf2.at[slot], sem_w.at[slot])]

    @pl.when(e == 0)
    def _():
        for cp in weight_copies(0, 0):
            cp.start(priority=1)

        @pl.when(n_used > 0)
        def _():
            in_copy(0, 0).start()

    @pl.when(e < last)
    def _():
        for cp in weight_copies(e + 1, 1 - ws):
            cp.start(priority=1)

    for cp in weight_copies(e, ws):
        cp.wait()

    def cast(dst, src):
        dst[...] = src[ws].astype(bf16)

    @pl.when(n == 0)
    def _():
        cast(w1b, wf1)
        cast(w3b, wf3)
        cast(w2b, wf2)

    def block(b, cast_weights):
        g = g0 + b
        slot = jnp.bitwise_and(g, 1)
        in_copy(g, slot).wait()

        @pl.when(g + 1 < n_used)
        def _():
            in_copy(g + 1, 1 - slot).start()

        @pl.when(g >= 2)
        def _():
            out_copy(g - 2, slot).wait()

        def swiglu(rows):
            x = _load_position_major(xbuf.at[slot, pl.ds(0, rows * ROW_TILE)], rows).astype(bf16)
            if cast_weights:
                cast(w1b, wf1)
            h1 = jnp.dot(x, w1b[...], preferred_element_type=f32)
            if cast_weights:
                cast(w3b, wf3)
            h3 = jnp.dot(x, w3b[...], preferred_element_type=f32)
            a = (h1 * _sigmoid(h1) * h3).astype(bf16)
            if cast_weights:
                cast(w2b, wf2)
            _store_position_major(ybuf.at[slot, pl.ds(0, rows * ROW_TILE)],
                                  jnp.dot(a, w2b[...], preferred_element_type=f32))

        half = MOE_ROWS // 2
        real_rows = cnt_ref[e] - b * MOE_ROWS

        @pl.when(real_rows > half)
        def _():
            swiglu(MOE_ROWS)

        @pl.when(real_rows <= half)
        def _():
            swiglu(half)
            ybuf[slot, half * ROW_TILE:, :] = jnp.zeros((half * ROW_TILE, LANES), u32)

        out_copy(g, slot).start()

    @pl.when(n > 0)
    def _():
        block(0, True)

    lax.fori_loop(1, n, lambda b, c: (block(b, False), c)[1], 0)

    @pl.when(e == last)
    def _():
        @pl.when(n_used >= 2)
        def _():
            out_copy(n_used - 2, jnp.bitwise_and(n_used, 1)).wait()

        @pl.when(n_used >= 1)
        def _():
            out_copy(n_used - 1, jnp.bitwise_and(n_used - 1, 1)).wait()

        ybuf[0] = jnp.zeros(ybuf.shape[1:], u32)
        count = tail_ref[1]

        def zero_copy(b):
            return pltpu.make_async_copy(ybuf.at[0], hbm_rows(ys_ref, n_used + b), sem_out.at[0])

        lax.fori_loop(0, count, lambda b, c: (zero_copy(b).start(), c)[1], 0)
        lax.fori_loop(0, count, lambda b, c: (zero_copy(0).wait(), c)[1], 0)


def _experts(bstart, nblk, tail, cnt, xs, w1, w3, w2):
    blk = MOE_ROWS * ROW_TILE
    n_exp, d, ff = w1.shape
    grid_spec = pltpu.PrefetchScalarGridSpec(
        num_scalar_prefetch=4,
        grid=(n_exp,),
        in_specs=[pl.BlockSpec(memory_space=pl.ANY)] * 4,
        out_specs=pl.BlockSpec(memory_space=pl.ANY),
        scratch_shapes=[pltpu.VMEM((2, d, ff), f32), pltpu.VMEM((2, d, ff), f32), pltpu.VMEM((2, ff, d), f32),
                        pltpu.VMEM((d, ff), bf16), pltpu.VMEM((d, ff), bf16), pltpu.VMEM((ff, d), bf16),
                        pltpu.VMEM((2, blk, LANES), u32), pltpu.VMEM((2, blk, LANES), u32),
                        pltpu.SemaphoreType.DMA((2,)), pltpu.SemaphoreType.DMA((2,)),
                        pltpu.SemaphoreType.DMA((2,))],
    )
    return pl.pallas_call(
        _expert_kernel,
        grid_spec=grid_spec,
        out_shape=jax.ShapeDtypeStruct(xs.shape, u32),
        compiler_params=_cparams(("arbitrary",)),
        name="experts",
    )(bstart, nblk, tail, cnt, xs, w1, w3, w2)


COMBINE_RING = 3
COMBINE_CHUNK = 32


def _combine_kernel(dest_ref, dest1_ref, dest2_ref, ys_ref, rec_ref, x1_ref, gpost_ref, gate2_ref, o_ref,
                    buf, sem, *, tm):
    i = pl.program_id(0)
    last = pl.num_programs(0) - 1
    ring = COMBINE_RING

    def row_copy(d_ref, s, k, r):
        d = d_ref[TOP_K * r + k]
        return pltpu.make_async_copy(
            ys_ref.at[pl.ds(pl.multiple_of(d * ROW_TILE, ROW_TILE), ROW_TILE), :],
            buf.at[s, k, pl.ds(r * ROW_TILE, ROW_TILE), :], sem.at[s])

    def wait_tile(s):
        for k in range(TOP_K):
            pltpu.make_async_copy(ys_ref.at[pl.ds(0, tm * ROW_TILE), :], buf.at[s, k], sem.at[s]).wait()

    @pl.when(i == 0)
    def _():
        def issue(r, c):
            for k in range(TOP_K):
                row_copy(dest_ref, 0, k, r).start(priority=k)
                row_copy(dest1_ref, 1, k, r).start(priority=k)
            return c

        lax.fori_loop(0, tm, issue, 0, unroll=DMA_UNROLL)

    slot = lax.rem(i, ring)
    ahead = lax.rem(i + 2, ring)
    wait_tile(slot)

    gain = gate2_ref[...] * gpost_ref[...]
    for c in range(tm // COMBINE_CHUNK):
        rows = slice(c * COMBINE_CHUNK, (c + 1) * COMBINE_CHUNK)
        words = pl.ds(c * COMBINE_CHUNK * ROW_TILE, COMBINE_CHUNK * ROW_TILE)
        w = rec_ref[rows, :]
        y = (w[:, 4:5] * _load_position_major(buf.at[slot, 0, words], COMBINE_CHUNK)
             + w[:, 5:6] * _load_position_major(buf.at[slot, 1, words], COMBINE_CHUNK))
        o_ref[rows, :] = x1_ref[rows, :] + (y * lax.rsqrt(jnp.mean(y * y, axis=-1, keepdims=True) + EPS)) * gain
        for r in range(c * COMBINE_CHUNK, (c + 1) * COMBINE_CHUNK):
            for k in range(TOP_K):
                row_copy(dest2_ref, ahead, k, r).start(priority=k)

    @pl.when(i == last)
    def _():
        wait_tile(lax.rem(i + 1, ring))
        wait_tile(ahead)


def _combine(dest, ys, rec, x1, g_post, gate2, *, seq, tm):
    t, d = x1.shape
    last = t // tm - 1
    ahead = lambda n: pl.BlockSpec((TOP_K * tm,), lambda i: (jnp.minimum(i + n, last),), memory_space=pltpu.SMEM)
    return pl.pallas_call(
        functools.partial(_combine_kernel, tm=tm),
        grid=(t // tm,),
        in_specs=[ahead(0), ahead(1), ahead(2),
                  pl.BlockSpec(memory_space=pl.ANY),
                  pl.BlockSpec((tm, ROUTE_LANES), lambda i: (i, 0)),
                  pl.BlockSpec((tm, d), lambda i: (i, 0)),
                  pl.BlockSpec((1, d), lambda i: (0, 0)),
                  pl.BlockSpec((None, 1, d), lambda i: ((i * tm) // seq, 0, 0))],
        out_specs=pl.BlockSpec((tm, d), lambda i: (i, 0)),
        out_shape=jax.ShapeDtypeStruct((t, d), f32),
        scratch_shapes=[pltpu.VMEM((COMBINE_RING, TOP_K, tm * ROW_TILE, LANES), u32),
                        pltpu.SemaphoreType.DMA((COMBINE_RING,))],
        compiler_params=_cparams(("arbitrary",)),
        name="combine",
    )(dest, dest, dest, ys, rec, x1, g_post, gate2)


def _layer(x, c, w_ada, b_ada, g_pre_mix, g_post_mix, w_in, b_in, w_dw, b_dw, ln_conv_g, ln_conv_b,
           w_conv_out, b_conv_out, rel_bias, w_attn_out, w_out, g_pre_ffn, g_post_ffn,
           w_router_group, b_router_group, w_router_expert, b_router_expert, w1, w3, w2):
    batch, seq, d = x.shape
    t = batch * seq
    row = lambda v: v.reshape(1, -1)

    c_pad = jnp.zeros((SUBLANES, d), f32).at[:batch].set(c)
    mod = _ada(c_pad, w_ada, row(b_ada))[:batch]
    shift1, scale1, gate1, shift2, scale2, gate2 = [m.reshape(batch, 1, d) for m in jnp.split(mod, 6, axis=-1)]

    x2 = x.reshape(t, d)
    zt, qkv = _inproj(x2, row(g_pre_mix), scale1, shift1, w_in.astype(bf16), row(b_in), seq=seq, tm=INPROJ_TM)

    conv_act = _conv(zt.reshape(batch, seq, ZT_COLS), w_dw, row(b_dw), row(ln_conv_g), row(ln_conv_b),
                     tm=512).reshape(t, CONV_CH)

    os_, lses = [], []
    for gi, (_, dil) in enumerate(DILATED_GROUPS):
        o, lse = _attn_group(qkv, rel_bias.reshape(-1), gi, dil, batch=batch, seq=seq)
        os_.append(o)
        lses.append(lse)

    pad = ROUTE_LANES - N_GROUPS - N_EXPERTS
    w_r = jnp.concatenate([w_router_group, w_router_expert, jnp.zeros((d, pad), f32)], axis=1).astype(bf16)
    b_r = row(jnp.concatenate([b_router_group, b_router_expert.reshape(-1), jnp.zeros((pad,), f32)]))
    x1, h2p, logits = _merge(x2, conv_act, os_, lses, zt, w_conv_out.astype(bf16), row(b_conv_out),
                             w_attn_out.astype(bf16), w_out.astype(bf16), row(g_post_mix), gate1,
                             row(g_pre_ffn), scale2, shift2, w_r, b_r, seq=seq, tm=MERGE_TM, n_sub=MERGE_SUB)

    rec, rec_t, cnt = _route(logits, tm=512)

    counts = cnt[0, :N_EXPERTS].astype(jnp.int32)
    pcounts = (counts + MOE_ROWS - 1) // MOE_ROWS * MOE_ROWS
    pends = jnp.cumsum(pcounts)
    pstarts = pends - pcounts
    n_blocks = (t * TOP_K + N_EXPERTS * (MOE_ROWS - 1) + MOE_ROWS - 1) // MOE_ROWS
    n_used = (pends[-1:] // MOE_ROWS).astype(jnp.int32)
    tail_blocks = jnp.concatenate([n_used, n_blocks - n_used])
    eid = rec_t[0:TOP_K].astype(jnp.int32)
    rank = rec_t[TOP_K:2 * TOP_K].astype(jnp.int32)
    start_of = jnp.sum(jnp.where(eid[..., None] == jnp.arange(N_EXPERTS), pstarts, 0), axis=-1)
    dest = jnp.transpose(start_of + rank).reshape(t * TOP_K)
    tail = n_blocks - N_EXPERTS + jnp.arange(N_EXPERTS, dtype=jnp.int32)
    zero_blk = jnp.concatenate([jnp.where(pcounts > counts, pends // MOE_ROWS - 1, -1),
                                jnp.where(tail >= n_used[0], tail, -1)]).astype(jnp.int32)

    xs = _dispatch(zero_blk, dest, h2p, p_rows=n_blocks * MOE_ROWS, tm=512)
    ys = _experts((pstarts // MOE_ROWS).astype(jnp.int32), (pcounts // MOE_ROWS).astype(jnp.int32),
                  tail_blocks, counts, xs, w1, w3, w2)
    out = _combine(dest, ys, rec, x1, row(g_post_ffn), gate2, seq=seq, tm=512)
    return out.reshape(batch, seq, d)


def kernel(x, c, w_ada, b_ada, g_pre_mix, g_post_mix, w_in, b_in, w_dw, b_dw, ln_conv_g, ln_conv_b,
           w_conv_out, b_conv_out, rel_bias, w_attn_out, w_out, g_pre_ffn, g_post_ffn,
           w_router_group, b_router_group, w_router_expert, b_router_expert, w1, w3, w2):
    depth = w_ada.shape[0]
    for l in range(depth):
        pick = (lambda a: a.reshape(a.shape[1:])) if depth == 1 else (lambda a, l=l: a[l])
        x = _layer(x, c, pick(w_ada), pick(b_ada), pick(g_pre_mix), pick(g_post_mix), pick(w_in),
                   pick(b_in), pick(w_dw), pick(b_dw), pick(ln_conv_g), pick(ln_conv_b),
                   pick(w_conv_out), pick(b_conv_out), rel_bias, pick(w_attn_out), pick(w_out),
                   pick(g_pre_ffn), pick(g_post_ffn), pick(w_router_group), pick(b_router_group),
                   pick(w_router_expert), pick(b_router_expert), pick(w1), pick(w3), pick(w2))
    return x
```

```python
import functools
import math

import numpy as np
import jax
import jax.numpy as jnp
from jax import lax
from jax.experimental import pallas as pl
from jax.experimental.pallas import tpu as pltpu

D_MODEL = 2048
CONV_CH = 1024
CONV_WIDTH = 31
N_ATTN_HEADS = 12
HEADS_PER_GROUP = 4
HEAD_DIM = 128
ATTN_WIDTH = N_ATTN_HEADS * HEAD_DIM
DILATED_GROUPS = ((128, 1), (512, 4), (2048, 16))
NUM_BUCKETS = 32
REL_MAX_DISTANCE = 1024
N_GROUPS = 8
EXPERTS_PER_GROUP = 8
N_EXPERTS = N_GROUPS * EXPERTS_PER_GROUP
TOP_K = 2
EXPERT_FF = 512
EPS = 1e-6
NEG_INF = -1e30

LANES = 128
SUBLANES = 8
SUBLANES_BF16 = 16
V7X_VMEM_LIMIT_BYTES = 56 * 1024 * 1024

GROUP_W = HEADS_PER_GROUP * HEAD_DIM
ZT_COLS = 2 * D_MODEL + 2 * CONV_CH
COL_GLU = 2 * D_MODEL
INPROJ_TN = 3 * GROUP_W
ZT_TILES = ZT_COLS // INPROJ_TN
INPROJ_TM = 1024
ATTN_HALF = 64
ATTN_QB = 128
ATTN_WIN = ATTN_QB + 2 * ATTN_HALF
MOE_ROWS = 256
MERGE_TM = 256
MERGE_SUB = 2
ROUTE_LANES = 128
ROUTE_FIELDS = 8
PACKED_W = D_MODEL // 2
ROW_TILE = PACKED_W // LANES

bf16 = jnp.bfloat16
f32 = jnp.float32
u32 = jnp.uint32
HI_MASK = 0xFFFF0000


def _cparams(sem):
    return pltpu.CompilerParams(dimension_semantics=sem, vmem_limit_bytes=V7X_VMEM_LIMIT_BYTES)


def _sigmoid(x):
    return 1.0 / (1.0 + jnp.exp(-x))


def _pack_pairs(v):
    n = v.shape[1] // 2
    lo = lax.bitcast_convert_type(v[:, :n].astype(bf16).astype(f32), u32)
    hi = lax.bitcast_convert_type(v[:, n:].astype(bf16).astype(f32), u32)
    return (lo >> 16) | (hi & u32(HI_MASK))


def _unpack_lo(u):
    return lax.bitcast_convert_type(u << 16, f32)


def _unpack_hi(u):
    return lax.bitcast_convert_type(u & u32(HI_MASK), f32)


def _store_position_major(ref, v):
    packed = _pack_pairs(v)
    for j in range(ROW_TILE):
        ref[pl.ds(j, v.shape[0], stride=ROW_TILE), :] = packed[:, j * LANES:(j + 1) * LANES]


def _load_position_major(ref, n):
    chunks = [ref[pl.ds(j, n, stride=ROW_TILE), :] for j in range(ROW_TILE)]
    return jnp.concatenate([_unpack_lo(c) for c in chunks] + [_unpack_hi(c) for c in chunks], axis=1)


def _ada_kernel(c_ref, w_ref, b_ref, o_ref):
    c = c_ref[...]
    a = (c * _sigmoid(c)).astype(bf16)
    o_ref[...] = jnp.dot(a, w_ref[...].astype(bf16), preferred_element_type=f32) + b_ref[...]


def _ada(c_pad, w_ada, b_ada):
    rows, d = c_pad.shape
    n = w_ada.shape[1]
    tn = 1024
    return pl.pallas_call(
        _ada_kernel,
        grid=(n // tn,),
        in_specs=[pl.BlockSpec((rows, d), lambda j: (0, 0)),
                  pl.BlockSpec((d, tn), lambda j: (0, j)),
                  pl.BlockSpec((1, tn), lambda j: (0, j))],
        out_specs=pl.BlockSpec((rows, tn), lambda j: (0, j)),
        out_shape=jax.ShapeDtypeStruct((rows, n), f32),
        compiler_params=_cparams(("arbitrary",)),
        name="ada_mod",
    )(c_pad, w_ada, b_ada)


def _inproj_kernel(x_ref, g_ref, scale0_ref, shift0_ref, scale_ref, shift_ref,
                   wa_ref, wb_ref, wc_ref, ba_ref, bb_ref, bc_ref, zt_ref, qkv_ref, h_scr, x_scr, slab_scr, xsem):
    i, j = pl.program_id(0), pl.program_id(1)
    tm = x_scr.shape[0]
    slot = jnp.bitwise_and(i, 1)
    nxt_tile = jnp.minimum(i + 1, pl.num_programs(0) - 1)

    def x_copy(tile):
        return pltpu.make_async_copy(x_ref.at[pl.ds(pl.multiple_of(tile * tm, tm), tm), :], x_scr, xsem)

    def prenorm(rows, sc_ref, sh_ref, dst_slot):
        x = x_scr[rows, :]
        ms = jnp.mean(x * x, axis=-1, keepdims=True)
        h = x * lax.rsqrt(ms + EPS) * g_ref[...]
        h_scr[dst_slot, rows, :] = (h * (1.0 + sc_ref[...]) + sh_ref[...]).astype(bf16)

    @pl.when(j == 0)
    def _():
        @pl.when(i == 0)
        def _():
            x_copy(0).start()
            x_copy(0).wait()
            prenorm(slice(None), scale0_ref, shift0_ref, 0)

        x_copy(nxt_tile).start()

    @pl.when(j == ZT_TILES)
    def _():
        x_copy(nxt_tile).wait()

    w_refs, b_refs = (wa_ref, wb_ref, wc_ref), (ba_ref, bb_ref, bc_ref)

    def chunk(c):
        return jnp.dot(h_scr[slot], w_refs[c][...], preferred_element_type=f32) + b_refs[c][...]

    n_parts = len(DILATED_GROUPS)
    part_rows = [(tm // n_parts // SUBLANES_BF16) * SUBLANES_BF16] * (n_parts - 1)
    part_rows.append(tm - sum(part_rows))
    part_start = [sum(part_rows[:p]) for p in range(n_parts)]

    @pl.when(j < ZT_TILES)
    def _():
        for c in range(INPROJ_TN // GROUP_W):
            zt_ref[:, c * GROUP_W:(c + 1) * GROUP_W] = chunk(c).astype(bf16)

    for gi, (_, dil) in enumerate(DILATED_GROUPS):
        @pl.when(j == ZT_TILES + gi)
        def _(gi=gi, dil=dil):
            prenorm(slice(part_start[gi], part_start[gi] + part_rows[gi]), scale_ref, shift_ref, 1 - slot)
            per = tm // dil
            for c in range(INPROJ_TN // GROUP_W):
                z = chunk(c)
                if dil == 1:
                    qkv_ref[:, c * GROUP_W:(c + 1) * GROUP_W] = z.astype(bf16)
                    continue
                slabs = range(GROUP_W // LANES)
                for s in slabs:
                    slab_scr[0, s] = z[:, s * LANES:(s + 1) * LANES]
                src, stride, first = 0, dil, lambda r: r
                if dil == 16:
                    quarter = tm // 4
                    for s in slabs:
                        for r4 in range(4):
                            slab_scr[1, s, r4 * quarter:(r4 + 1) * quarter, :] = (
                                slab_scr[0, s, pl.ds(r4, quarter, stride=4), :])
                    src, stride, first = 1, 4, lambda r: (r % 4) * quarter + r // 4
                for r in range(dil):
                    for s in slabs:
                        col = c * GROUP_W + s * LANES
                        qkv_ref[r * per:(r + 1) * per, col:col + LANES] = (
                            slab_scr[src, s, pl.ds(first(r), per, stride=stride), :].astype(bf16))


def _inproj_col_block(j, c):
    n_blocks = (2 * CONV_CH + 3 * ATTN_WIDTH + 2 * D_MODEL) // GROUP_W
    gate0 = (2 * CONV_CH + 3 * ATTN_WIDTH) // GROUP_W
    q0 = 2 * CONV_CH // GROUP_W
    per_proj = ATTN_WIDTH // GROUP_W
    token_major = lax.rem(gate0 + (INPROJ_TN // GROUP_W) * j + c, n_blocks)
    return jnp.where(j < ZT_TILES, token_major, q0 + (j - ZT_TILES) + per_proj * c)


def _inproj(x2, g, scale, shift, w, b, *, seq, tm):
    t, d = x2.shape
    n = w.shape[1]
    tn = INPROJ_TN
    n_groups = n // tn - ZT_TILES
    last = t // tm - 1
    first_map = lambda i, j: (0, 0, 0)
    next_map = lambda i, j: ((jnp.minimum(i + 1, last) * tm) // seq, 0, 0)
    chunks = range(tn // GROUP_W)
    w_specs = [pl.BlockSpec((d, GROUP_W), lambda i, j, c=c: (0, _inproj_col_block(j, c))) for c in chunks]
    b_specs = [pl.BlockSpec((1, GROUP_W), lambda i, j, c=c: (0, _inproj_col_block(j, c))) for c in chunks]
    return pl.pallas_call(
        _inproj_kernel,
        grid=(t // tm, n // tn),
        in_specs=[pl.BlockSpec(memory_space=pl.ANY),
                  pl.BlockSpec((1, d), lambda i, j: (0, 0)),
                  pl.BlockSpec((None, 1, d), first_map), pl.BlockSpec((None, 1, d), first_map),
                  pl.BlockSpec((None, 1, d), next_map), pl.BlockSpec((None, 1, d), next_map)]
                 + w_specs + b_specs,
        out_specs=[pl.BlockSpec((tm, tn), lambda i, j: (i, jnp.minimum(j, ZT_TILES - 1))),
                   pl.BlockSpec((None, tm, tn), lambda i, j: (jnp.maximum(j - ZT_TILES, 0), i, 0))],
        out_shape=[jax.ShapeDtypeStruct((t, ZT_COLS), bf16),
                   jax.ShapeDtypeStruct((n_groups, t, tn), bf16)],
        scratch_shapes=[pltpu.VMEM((2, tm, d), bf16), pltpu.VMEM((tm, d), f32),
                        pltpu.VMEM((2, GROUP_W // LANES, tm, LANES), f32), pltpu.SemaphoreType.DMA(())],
        compiler_params=_cparams(("arbitrary", "arbitrary")),
        name="inproj",
    )(x2, g, scale, shift, scale, shift, w, w, w, b, b, b)


CONV_HALO = 16
CONV_CHUNK = 32
CONV_SUB = CONV_CH // LANES
NORM_CHUNK = 64


def _conv_kernel(ap_ref, ac_ref, an_ref, gp_ref, gc_ref, gn_ref, w_ref, bdw_ref, lng_ref, lnb_ref,
                 o_ref, u_scr, y_scr, *, tm):
    i = pl.program_id(1)
    last = pl.num_programs(1) - 1

    def glu(a_ref, g_ref):
        return a_ref[...].astype(f32) * _sigmoid(g_ref[...].astype(f32))

    def put(row0, val):
        for j in range(CONV_SUB):
            u_scr[pl.ds(row0 * CONV_SUB + j, val.shape[0], stride=CONV_SUB), :] = (
                val[:, j * LANES:(j + 1) * LANES])

    put(0, jnp.where(i > 0, glu(ap_ref, gp_ref), 0.0))
    put(CONV_HALO, glu(ac_ref, gc_ref))
    put(CONV_HALO + tm, jnp.where(i < last, glu(an_ref, gn_ref), 0.0))

    def chunk(c, carry):
        r0 = c * CONV_CHUNK
        acc = jnp.zeros((CONV_CHUNK, CONV_SUB, LANES), f32) + bdw_ref[...][None]
        for k in range(CONV_WIDTH):
            start = pl.multiple_of((r0 + k + 1) * CONV_SUB, CONV_SUB)
            xk = u_scr[pl.ds(start, CONV_CHUNK * CONV_SUB), :].reshape(CONV_CHUNK, CONV_SUB, LANES)
            acc = acc + w_ref[k][None] * xk
        out0 = pl.multiple_of(r0 * CONV_SUB, CONV_CHUNK * CONV_SUB)
        y_scr[pl.ds(out0, CONV_CHUNK * CONV_SUB), :] = acc.reshape(CONV_CHUNK * CONV_SUB, LANES)
        return carry

    lax.fori_loop(0, tm // CONV_CHUNK, chunk, 0)

    def norm(c, carry):
        r0 = pl.multiple_of(c * NORM_CHUNK, NORM_CHUNK)
        acc = jnp.concatenate(
            [y_scr[pl.ds(r0 * CONV_SUB + j, NORM_CHUNK, stride=CONV_SUB), :] for j in range(CONV_SUB)],
            axis=1)
        mu = jnp.mean(acc, axis=-1, keepdims=True)
        cen = acc - mu
        var = jnp.mean(cen * cen, axis=-1, keepdims=True)
        y = cen * lax.rsqrt(var + EPS) * lng_ref[...] + lnb_ref[...]
        o_ref[pl.ds(r0, NORM_CHUNK), :] = (y * _sigmoid(y)).astype(bf16)
        return carry

    lax.fori_loop(0, tm // NORM_CHUNK, norm, 0, unroll=4)


def _conv(z3, w_dw, b_dw, ln_g, ln_b, *, tm):
    b, s, _ = z3.shape
    hb = tm // CONV_HALO
    nhb = s // CONV_HALO
    ca, cg = COL_GLU // CONV_CH, COL_GLU // CONV_CH + 1

    def halo(col, which):
        if which < 0:
            return pl.BlockSpec((None, CONV_HALO, CONV_CH),
                                lambda bb, i: (bb, jnp.maximum(i * hb - 1, 0), col))
        return pl.BlockSpec((None, CONV_HALO, CONV_CH),
                            lambda bb, i: (bb, jnp.minimum((i + 1) * hb, nhb - 1), col))

    cur = lambda col: pl.BlockSpec((None, tm, CONV_CH), lambda bb, i: (bb, i, col))
    vec = pl.BlockSpec((1, CONV_CH), lambda bb, i: (0, 0))
    return pl.pallas_call(
        functools.partial(_conv_kernel, tm=tm),
        grid=(b, s // tm),
        in_specs=[halo(ca, -1), cur(ca), halo(ca, 1), halo(cg, -1), cur(cg), halo(cg, 1),
                  pl.BlockSpec((CONV_WIDTH, CONV_SUB, LANES), lambda bb, i: (0, 0, 0)),
                  pl.BlockSpec((CONV_SUB, LANES), lambda bb, i: (0, 0)), vec, vec],
        out_specs=pl.BlockSpec((None, tm, CONV_CH), lambda bb, i: (bb, i, 0)),
        out_shape=jax.ShapeDtypeStruct((b, s, CONV_CH), bf16),
        scratch_shapes=[pltpu.VMEM(((tm + 2 * CONV_HALO) * CONV_SUB, LANES), f32),
                        pltpu.VMEM((tm * CONV_SUB, LANES), f32)],
        compiler_params=_cparams(("arbitrary", "arbitrary")),
        name="conv_branch",
    )(z3, z3, z3, z3, z3, z3, w_dw.reshape(CONV_WIDTH, CONV_SUB, LANES), b_dw.reshape(CONV_SUB, LANES),
      ln_g, ln_b)


def _t5_bucket_table(dil):
    delta = (np.arange(ATTN_WIN)[None, :] - ATTN_HALF) - np.arange(ATTN_QB)[:, None]
    rel = delta * dil
    nb = NUM_BUCKETS // 2
    max_exact = nb // 2
    n = np.abs(rel)
    nf = np.maximum(n, 1).astype(np.float32)
    large = max_exact + (np.log(nf / np.float32(max_exact)) / np.float32(math.log(REL_MAX_DISTANCE / max_exact))
                         * np.float32(nb - max_exact)).astype(np.int32)
    large = np.minimum(large, nb - 1)
    bucket = np.where(rel > 0, nb, 0) + np.where(n < max_exact, n, large)
    return np.where(np.abs(delta) <= ATTN_HALF, bucket, -1).astype(np.int32)


def _attn_kernel(rb_ref, bkt_ref, q_ref, kp_ref, kc_ref, kn_ref, vp_ref, vc_ref, vn_ref, o_ref, lse_ref,
                 q_scr, k_scr, v_scr, bias_scr, *, gi, dil, n_tiles, sub_len):
    i = pl.program_id(1)
    h = ATTN_HALF
    tile = INPROJ_TM
    per = tile // dil
    tq = n_tiles * per

    @pl.when((pl.program_id(0) == 0) & (i == 0))
    def _():
        bk = bkt_ref[...]
        for hh in range(HEADS_PER_GROUP):
            acc = jnp.full((ATTN_QB, ATTN_WIN), NEG_INF, f32)
            for b_id in range(NUM_BUCKETS):
                acc = jnp.where(bk == b_id, rb_ref[b_id * N_ATTN_HEADS + gi * HEADS_PER_GROUP + hh], acc)
            bias_scr[hh] = acc

    halo_rows = kp_ref.shape[0]
    for r in range(dil):
        prev_lo = (r + 1) * per - h if halo_rows == tile else 0
        next_lo = r * per if halo_rows == tile else 0
        for dst, prv, cur, nxt in ((k_scr, kp_ref, kc_ref, kn_ref), (v_scr, vp_ref, vc_ref, vn_ref)):
            dst[r, 0:h, :] = prv[prev_lo:prev_lo + h, :]
            for t in range(n_tiles):
                dst[r, h + t * per:h + (t + 1) * per, :] = cur[t * tile + r * per:t * tile + (r + 1) * per, :]
            dst[r, h + tq:, :] = nxt[next_lo:next_lo + h, :]
        for t in range(n_tiles):
            q_scr[r, t * per:(t + 1) * per, :] = q_ref[t * tile + r * per:t * tile + (r + 1) * per, :]

    scale = HEAD_DIM ** -0.5
    col = lax.broadcasted_iota(jnp.int32, (ATTN_QB, ATTN_WIN), 1)
    lane = lax.broadcasted_iota(jnp.int32, (ATTN_QB, LANES), 1)

    def rows(start, n):
        if dil == 1:
            return pl.ds(pl.multiple_of(start, ATTN_QB), n)
        return pl.ds(start, n, stride=dil)

    def unit(u, carry):
        r = jnp.bitwise_and(u, dil - 1)
        jb = lax.shift_right_logical(u, jnp.int32(dil.bit_length() - 1))
        start = jb * (ATTN_QB * dil) + r
        p0 = pl.multiple_of(jb * ATTN_QB, ATTN_QB)
        cols = [slice(hh * HEAD_DIM, (hh + 1) * HEAD_DIM) for hh in range(HEADS_PER_GROUP)]
        q = [q_scr[r, pl.ds(p0, ATTN_QB), cs] for cs in cols]
        kw = [k_scr[r, pl.ds(p0, ATTN_WIN), cs] for cs in cols]
        vw = [v_scr[r, pl.ds(p0, ATTN_WIN), cs] for cs in cols]
        key0 = i * tq + jb * ATTN_QB - ATTN_HALF
        valid = (col + key0 >= 0) & (col + key0 < sub_len)
        s = [lax.dot_general(q[hh], kw[hh], (((1,), (1,)), ((), ())), preferred_element_type=f32)
             for hh in range(HEADS_PER_GROUP)]
        s = [jnp.where(valid, s[hh] * scale + bias_scr[hh], NEG_INF) for hh in range(HEADS_PER_GROUP)]
        m = [jnp.max(x, axis=-1, keepdims=True) for x in s]
        p = [jnp.exp(s[hh] - m[hh]) for hh in range(HEADS_PER_GROUP)]
        den = [jnp.sum(x, axis=-1, keepdims=True) for x in p]
        pv = [jnp.dot(p[hh].astype(bf16), vw[hh], preferred_element_type=f32)
              for hh in range(HEADS_PER_GROUP)]
        lse_tile = jnp.zeros((ATTN_QB, LANES), f32)
        for hh in range(HEADS_PER_GROUP):
            o_ref[hh, rows(start, ATTN_QB), :] = pv[hh] / den[hh]
            lse_tile = jnp.where(lane == hh, m[hh] + jnp.log(den[hh]), lse_tile)
        lse_ref[rows(start, ATTN_QB), :] = lse_tile
        return carry

    lax.fori_loop(0, n_tiles * tile // ATTN_QB, unit, 0, unroll=8)


def _attn_group(qkv, rel_bias_flat, gi, dil, *, batch, seq):
    t = batch * seq
    tile = INPROJ_TM
    n_tiles = max(1, ATTN_QB * dil // tile)
    ts = n_tiles * tile
    steps = seq // ts
    tq = ts // dil
    halo = ATTN_HALF if dil == 1 else tile
    hpt = ts // halo
    hps = seq // halo

    def cur(which):
        return pl.BlockSpec((None, ts, GROUP_W), lambda b, i: (gi, b * steps + i, which))

    def prev(which):
        return pl.BlockSpec((None, halo, GROUP_W),
                            lambda b, i: (gi, b * hps + jnp.maximum(i * hpt - 1, 0), which))

    def nxt(which):
        return pl.BlockSpec((None, halo, GROUP_W),
                            lambda b, i: (gi, b * hps + jnp.minimum((i + 1) * hpt, hps - 1), which))

    bkt = jnp.asarray(_t5_bucket_table(dil))
    return pl.pallas_call(
        functools.partial(_attn_kernel, gi=gi, dil=dil, n_tiles=n_tiles, sub_len=seq // dil),
        grid=(batch, steps),
        in_specs=[pl.BlockSpec(memory_space=pltpu.SMEM),
                  pl.BlockSpec((ATTN_QB, ATTN_WIN), lambda b, i: (0, 0)),
                  cur(0), prev(1), cur(1), nxt(1), prev(2), cur(2), nxt(2)],
        out_specs=[pl.BlockSpec((HEADS_PER_GROUP, ts, LANES), lambda b, i: (0, b * steps + i, 0)),
                   pl.BlockSpec((ts, LANES), lambda b, i: (b * steps + i, 0))],
        out_shape=[jax.ShapeDtypeStruct((HEADS_PER_GROUP, t, LANES), f32),
                   jax.ShapeDtypeStruct((t, LANES), f32)],
        scratch_shapes=[pltpu.VMEM((dil, tq, GROUP_W), bf16),
                        pltpu.VMEM((dil, tq + 2 * ATTN_HALF, GROUP_W), bf16),
                        pltpu.VMEM((dil, tq + 2 * ATTN_HALF, GROUP_W), bf16),
                        pltpu.VMEM((HEADS_PER_GROUP, ATTN_QB, ATTN_WIN), f32)],
        compiler_params=_cparams(("arbitrary", "arbitrary")),
        name=f"attn_g{gi}",
    )(rel_bias_flat, bkt, qkv, qkv, qkv, qkv, qkv, qkv, qkv)


def _merge_kernel(x_ref, ca_ref, o0_ref, o1_ref, o2_ref, l0_ref, l1_ref, l2_ref, ga_ref, gb_ref,
                  wco_ref, bco_ref, wao_ref, wo_ref, gpost_ref, gate1_ref, gpre_ref, scale2_ref,
                  shift2_ref, wr_ref, br_ref, x1_ref, h2p_ref, logit_ref, *, n_sub):
    hm = x_ref.shape[0] // n_sub
    subs = [slice(k * hm, (k + 1) * hm) for k in range(n_sub)]

    branch_a = [jnp.dot(ca_ref[rs, :], wco_ref[...], preferred_element_type=f32) + bco_ref[...] for rs in subs]

    def attention_mix(rs):
        l0, l1, l2 = l0_ref[rs, :], l1_ref[rs, :], l2_ref[rs, :]
        lmax = jnp.maximum(jnp.maximum(l0, l1), l2)
        e0, e1, e2 = jnp.exp(l0 - lmax), jnp.exp(l1 - lmax), jnp.exp(l2 - lmax)
        inv = 1.0 / (e0 + e1 + e2)
        parts = []
        for hh in range(HEADS_PER_GROUP):
            ls = slice(hh, hh + 1)
            parts.append((e0[:, ls] * inv[:, ls]) * o0_ref[hh, rs, :] + (e1[:, ls] * inv[:, ls]) * o1_ref[hh, rs, :]
                         + (e2[:, ls] * inv[:, ls]) * o2_ref[hh, rs, :])
        return jnp.concatenate(parts, axis=1).astype(bf16)

    att = [attention_mix(rs) for rs in subs]
    branch_b = [jnp.dot(a, wao_ref[...], preferred_element_type=f32) for a in att]
    mix = [(_sigmoid(ga_ref[rs, :].astype(f32)) * branch_a[k]
            + _sigmoid(gb_ref[rs, :].astype(f32)) * branch_b[k]).astype(bf16) for k, rs in enumerate(subs)]
    y = [jnp.dot(m, wo_ref[...], preferred_element_type=f32) for m in mix]

    post_gain = gate1_ref[...] * gpost_ref[...]
    pre_gain = gpre_ref[...] * (1.0 + scale2_ref[...])
    h2s = []
    for k, rs in enumerate(subs):
        x1 = x_ref[rs, :] + (y[k] * lax.rsqrt(jnp.mean(y[k] * y[k], axis=-1, keepdims=True) + EPS)) * post_gain
        x1_ref[rs, :] = x1
        h2 = (x1 * lax.rsqrt(jnp.mean(x1 * x1, axis=-1, keepdims=True) + EPS)) * pre_gain + shift2_ref[...]
        _store_position_major(h2p_ref.at[pl.ds(k * hm * ROW_TILE, hm * ROW_TILE), :], h2)
        h2s.append(h2.astype(bf16))
    for k, rs in enumerate(subs):
        logit_ref[rs, :] = jnp.dot(h2s[k], wr_ref[...], preferred_element_type=f32) + br_ref[...]


def _merge(x2, conv_act, os_, lses, zt, w_co, b_co, w_ao, w_o, g_post, gate1, g_pre, scale2, shift2,
           w_r, b_r, *, seq, tm, n_sub):
    t, d = x2.shape
    rows = lambda w: pl.BlockSpec((tm, w), lambda i: (i, 0))
    heads = pl.BlockSpec((HEADS_PER_GROUP, tm, LANES), lambda i: (0, i, 0))
    full = lambda a: pl.BlockSpec(a.shape, lambda i: (0,) * a.ndim)
    perb = pl.BlockSpec((None, 1, d), lambda i: ((i * tm) // seq, 0, 0))
    return pl.pallas_call(
        functools.partial(_merge_kernel, n_sub=n_sub),
        grid=(t // tm,),
        in_specs=[rows(d), rows(CONV_CH), heads, heads, heads, rows(LANES), rows(LANES), rows(LANES),
                  pl.BlockSpec((tm, d), lambda i: (i, 0)), pl.BlockSpec((tm, d), lambda i: (i, 1)),
                  full(w_co), full(b_co), full(w_ao), full(w_o), full(g_post), perb, full(g_pre),
                  perb, perb, full(w_r), full(b_r)],
        out_specs=[rows(d), pl.BlockSpec((tm * ROW_TILE, LANES), lambda i: (i, 0)), rows(ROUTE_LANES)],
        out_shape=[jax.ShapeDtypeStruct((t, d), f32),
                   jax.ShapeDtypeStruct((t * ROW_TILE, LANES), u32),
                   jax.ShapeDtypeStruct((t, ROUTE_LANES), f32)],
        compiler_params=_cparams(("arbitrary",)),
        name="merge",
    )(x2, conv_act, *os_, *lses, zt, zt, w_co, b_co, w_ao, w_o, g_post, gate1, g_pre, scale2, shift2,
      w_r, b_r)


def _route_kernel(logit_ref, rec_ref, rect_ref, cnt_ref, carry):
    i = pl.program_id(0)

    @pl.when(i == 0)
    def _():
        carry[...] = jnp.zeros_like(carry)

    lg = logit_ref[...]
    tm = lg.shape[0]
    lane = lax.broadcasted_iota(jnp.int32, lg.shape, 1).astype(f32)
    big = float(2 * ROUTE_LANES)

    def first_max(mask):
        v = jnp.max(jnp.where(mask, lg, -jnp.inf), axis=-1, keepdims=True)
        idx = jnp.min(jnp.where(mask & (lg == v), lane, big), axis=-1, keepdims=True)
        return v, idx

    gmask = lane < N_GROUPS
    gmax, gsel = first_max(gmask)
    p_g = 1.0 / jnp.sum(jnp.where(gmask, jnp.exp(lg - gmax), 0.0), axis=-1, keepdims=True)
    e_lo = N_GROUPS + EXPERTS_PER_GROUP * gsel
    emask = (lane >= e_lo) & (lane < e_lo + EXPERTS_PER_GROUP)
    v0, i0 = first_max(emask)
    v1, i1 = first_max(emask & (lane != i0))
    t1 = jnp.exp(v1 - v0)
    w0 = p_g / (1.0 + t1)
    w1 = p_g * t1 / (1.0 + t1)
    e0 = i0 - N_GROUPS
    e1 = i1 - N_GROUPS

    hit0 = lane == e0
    hit1 = lane == e1
    onehot = jnp.where(hit0 | hit1, 1.0, 0.0)
    r_i = lax.broadcasted_iota(jnp.int32, (tm, tm), 0)
    c_i = lax.broadcasted_iota(jnp.int32, (tm, tm), 1)
    tril = jnp.where(c_i < r_i, 1.0, 0.0).astype(bf16)
    before = jnp.dot(tril, onehot.astype(bf16), preferred_element_type=f32) + carry[...]
    rank0 = jnp.sum(jnp.where(hit0, before, 0.0), axis=-1, keepdims=True)
    rank1 = jnp.sum(jnp.where(hit1, before, 0.0), axis=-1, keepdims=True)
    carry[...] = carry[...] + jnp.sum(onehot, axis=0, keepdims=True)
    cnt_ref[...] = carry[...]

    rec = jnp.zeros_like(lg)
    for slot, val in enumerate((e0, e1, rank0, rank1, w0, w1)):
        rec = jnp.where(lane == slot, val, rec)
    rec_ref[...] = rec
    rect_ref[...] = jnp.transpose(rec)[0:ROUTE_FIELDS, :]


def _route(logits, *, tm):
    t = logits.shape[0]
    return pl.pallas_call(
        _route_kernel,
        grid=(t // tm,),
        in_specs=[pl.BlockSpec((tm, ROUTE_LANES), lambda i: (i, 0))],
        out_specs=[pl.BlockSpec((tm, ROUTE_LANES), lambda i: (i, 0)),
                   pl.BlockSpec((ROUTE_FIELDS, tm), lambda i: (0, i)),
                   pl.BlockSpec((1, ROUTE_LANES), lambda i: (0, 0))],
        out_shape=[jax.ShapeDtypeStruct((t, ROUTE_LANES), f32),
                   jax.ShapeDtypeStruct((ROUTE_FIELDS, t), f32),
                   jax.ShapeDtypeStruct((1, ROUTE_LANES), f32)],
        scratch_shapes=[pltpu.VMEM((1, ROUTE_LANES), f32)],
        compiler_params=_cparams(("arbitrary",)),
        name="route",
    )(logits)


DMA_UNROLL = 8


def _dispatch_kernel(zero_blk_ref, dest_ref, h_ref, xs_ref, zero_scr, sem, zsem, *, tm):
    def row_copy(r, d):
        return pltpu.make_async_copy(h_ref.at[pl.ds(pl.multiple_of(r * ROW_TILE, ROW_TILE), ROW_TILE), :],
                                     xs_ref.at[pl.ds(pl.multiple_of(d * ROW_TILE, ROW_TILE), ROW_TILE), :],
                                     sem)

    def zero_copy(blk, zs):
        n = MOE_ROWS * ROW_TILE
        return pltpu.make_async_copy(zero_scr, xs_ref.at[pl.ds(pl.multiple_of(blk * n, n), n), :], zsem.at[zs])

    def start(j, zs):
        @pl.when(zero_blk_ref[j] >= 0)
        def _():
            zero_copy(zero_blk_ref[j], zs).start()

    def wait(j, zs):
        @pl.when(zero_blk_ref[j] >= 0)
        def _():
            zero_copy(0, zs).wait()

    @pl.when(pl.program_id(0) == 0)
    def _():
        zero_scr[...] = jnp.zeros_like(zero_scr)
        lax.fori_loop(0, 2 * N_EXPERTS, lambda j, c: (start(j, jnp.where(j < N_EXPERTS, 0, 1)), c)[1], 0)
        lax.fori_loop(0, N_EXPERTS, lambda j, c: (wait(j, 0), c)[1], 0)

    @pl.when(pl.program_id(0) == pl.num_programs(0) - 1)
    def _():
        lax.fori_loop(N_EXPERTS, 2 * N_EXPERTS, lambda j, c: (wait(j, 1), c)[1], 0)

    def issue(r, c):
        row_copy(r, dest_ref[TOP_K * r]).start(priority=0)
        row_copy(r, dest_ref[TOP_K * r + 1]).start(priority=1)
        return c

    lax.fori_loop(0, tm, issue, 0, unroll=DMA_UNROLL)
    for _ in range(TOP_K):
        pltpu.make_async_copy(h_ref, xs_ref.at[pl.ds(0, tm * ROW_TILE), :], sem).wait()


def _dispatch(zero_blk, dest, h2p, *, p_rows, tm):
    grid_spec = pltpu.PrefetchScalarGridSpec(
        num_scalar_prefetch=1,
        grid=(h2p.shape[0] // (tm * ROW_TILE),),
        in_specs=[pl.BlockSpec((TOP_K * tm,), lambda i, *_: (i,), memory_space=pltpu.SMEM),
                  pl.BlockSpec((tm * ROW_TILE, LANES), lambda i, *_: (i, 0))],
        out_specs=pl.BlockSpec(memory_space=pl.ANY),
        scratch_shapes=[pltpu.VMEM((MOE_ROWS * ROW_TILE, LANES), u32), pltpu.SemaphoreType.DMA(()),
                        pltpu.SemaphoreType.DMA((2,))],
    )
    return pl.pallas_call(
        functools.partial(_dispatch_kernel, tm=tm),
        grid_spec=grid_spec,
        out_shape=jax.ShapeDtypeStruct((p_rows * ROW_TILE, LANES), u32),
        compiler_params=_cparams(("arbitrary",)),
        name="dispatch",
    )(zero_blk, dest, h2p)


def _expert_kernel(bstart_ref, nblk_ref, tail_ref, cnt_ref, xs_ref, w1_ref, w3_ref, w2_ref, ys_ref,
                   wf1, wf3, wf2, w1b, w3b, w2b, xbuf, ybuf, sem_in, sem_out, sem_w):
    e = pl.program_id(0)
    last = pl.num_programs(0) - 1
    ws = jnp.bitwise_and(e, 1)
    n = nblk_ref[e]
    g0 = bstart_ref[e]
    n_used = tail_ref[0]
    blk = MOE_ROWS * ROW_TILE

    def hbm_rows(ref, g):
        return ref.at[pl.ds(pl.multiple_of(g * blk, blk), blk), :]

    def in_copy(g, slot):
        return pltpu.make_async_copy(hbm_rows(xs_ref, g), xbuf.at[slot], sem_in.at[slot])

    def out_copy(g, slot):
        return pltpu.make_async_copy(ybuf.at[slot], hbm_rows(ys_ref, g), sem_out.at[slot])

    def weight_copies(ex, slot):
        return [pltpu.make_async_copy(w1_ref.at[ex], wf1.at[slot], sem_w.at[slot]),
                pltpu.make_async_copy(w3_ref.at[ex], wf3.at[slot], sem_w.at[slot]),
                pltpu.make_async_copy(w2_ref.at[ex], wf2.at[slot], sem_w.at[slot])]

    @pl.when(e == 0)
    def _():
        for cp in weight_copies(0, 0):
            cp.start(priority=1)

        @pl.when(n_used > 0)
        def _():
            in_copy(0, 0).start()

    @pl.when(e < last)
    def _():
        for cp in weight_copies(e + 1, 1 - ws):
            cp.start(priority=1)

    for cp in weight_copies(e, ws):
        cp.wait()

    def cast(dst, src):
        dst[...] = src[ws].astype(bf16)

    @pl.when(n == 0)
    def _():
        cast(w1b, wf1)
        cast(w3b, wf3)
        cast(w2b, wf2)

    def block(b, cast_weights):
        g = g0 + b
        slot = jnp.bitwise_and(g, 1)
        in_copy(g, slot).wait()

        @pl.when(g + 1 < n_used)
        def _():
            in_copy(g + 1, 1 - slot).start()

        @pl.when(g >= 2)
        def _():
            out_copy(g - 2, slot).wait()

        def swiglu(rows):
            x = _load_position_major(xbuf.at[slot, pl.ds(0, rows * ROW_TILE)], rows).astype(bf16)
            if cast_weights:
                cast(w1b, wf1)
            h1 = jnp.dot(x, w1b[...], preferred_element_type=f32)
            if cast_weights:
                cast(w3b, wf3)
            h3 = jnp.dot(x, w3b[...], preferred_element_type=f32)
            a = (h1 * _sigmoid(h1) * h3).astype(bf16)
            if cast_weights:
                cast(w2b, wf2)
            _store_position_major(ybuf.at[slot, pl.ds(0, rows * ROW_TILE)],
                                  jnp.dot(a, w2b[...], preferred_element_type=f32))

        half = MOE_ROWS // 2
        real_rows = cnt_ref[e] - b * MOE_ROWS

        @pl.when(real_rows > half)
        def _():
            swiglu(MOE_ROWS)

        @pl.when(real_rows <= half)
        def _():
            swiglu(half)
            ybuf[slot, half * ROW_TILE:, :] = jnp.zeros((half * ROW_TILE, LANES), u32)

        out_copy(g, slot).start()

    @pl.when(n > 0)
    def _():
        block(0, True)

    lax.fori_loop(1, n, lambda b, c: (block(b, False), c)[1], 0)

    @pl.when(e == last)
    def _():
        @pl.when(n_used >= 2)
        def _():
            out_copy(n_used - 2, jnp.bitwise_and(n_used, 1)).wait()

        @pl.when(n_used >= 1)
        def _():
            out_copy(n_used - 1, jnp.bitwise_and(n_used - 1, 1)).wait()

        ybuf[0] = jnp.zeros(ybuf.shape[1:], u32)
        count = tail_ref[1]

        def zero_copy(b):
            return pltpu.make_async_copy(ybuf.at[0], hbm_rows(ys_ref, n_used + b), sem_out.at[0])

        lax.fori_loop(0, count, lambda b, c: (zero_copy(b).start(), c)[1], 0)
        lax.fori_loop(0, count, lambda b, c: (zero_copy(0).wait(), c)[1], 0)


def _experts(bstart, nblk, tail, cnt, xs, w1, w3, w2):
    blk = MOE_ROWS * ROW_TILE
    n_exp, d, ff = w1.shape
    grid_spec = pltpu.PrefetchScalarGridSpec(
        num_scalar_prefetch=4,
        grid=(n_exp,),
        in_specs=[pl.BlockSpec(memory_space=pl.ANY)] * 4,
        out_specs=pl.BlockSpec(memory_space=pl.ANY),
        scratch_shapes=[pltpu.VMEM((2, d, ff), f32), pltpu.VMEM((2, d, ff), f32), pltpu.VMEM((2, ff, d), f32),
                        pltpu.VMEM((d, ff), bf16), pltpu.VMEM((d, ff), bf16), pltpu.VMEM((ff, d), bf16),
                        pltpu.VMEM((2, blk, LANES), u32), pltpu.VMEM((2, blk, LANES), u32),
                        pltpu.SemaphoreType.DMA((2,)), pltpu.SemaphoreType.DMA((2,)),
                        pltpu.SemaphoreType.DMA((2,))],
    )
    return pl.pallas_call(
        _expert_kernel,
        grid_spec=grid_spec,
        out_shape=jax.ShapeDtypeStruct(xs.shape, u32),
        compiler_params=_cparams(("arbitrary",)),
        name="experts",
    )(bstart, nblk, tail, cnt, xs, w1, w3, w2)


COMBINE_RING = 3
COMBINE_CHUNK = 32


def _combine_kernel(dest_ref, dest1_ref, dest2_ref, ys_ref, rec_ref, x1_ref, gpost_ref, gate2_ref, o_ref,
                    buf, sem, *, tm):
    i = pl.program_id(0)
    last = pl.num_programs(0) - 1
    ring = COMBINE_RING

    def row_copy(d_ref, s, k, r):
        d = d_ref[TOP_K * r + k]
        return pltpu.make_async_copy(
            ys_ref.at[pl.ds(pl.multiple_of(d * ROW_TILE, ROW_TILE), ROW_TILE), :],
            buf.at[s, k, pl.ds(r * ROW_TILE, ROW_TILE), :], sem.at[s])

    def wait_tile(s):
        for k in range(TOP_K):
            pltpu.make_async_copy(ys_ref.at[pl.ds(0, tm * ROW_TILE), :], buf.at[s, k], sem.at[s]).wait()

    @pl.when(i == 0)
    def _():
        def issue(r, c):
            for k in range(TOP_K):
                row_copy(dest_ref, 0, k, r).start(priority=k)
                row_copy(dest1_ref, 1, k, r).start(priority=k)
            return c

        lax.fori_loop(0, tm, issue, 0, unroll=DMA_UNROLL)

    slot = lax.rem(i, ring)
    ahead = lax.rem(i + 2, ring)
    wait_tile(slot)

    gain = gate2_ref[...] * gpost_ref[...]
    for c in range(tm // COMBINE_CHUNK):
        rows = slice(c * COMBINE_CHUNK, (c + 1) * COMBINE_CHUNK)
        words = pl.ds(c * COMBINE_CHUNK * ROW_TILE, COMBINE_CHUNK * ROW_TILE)
        w = rec_ref[rows, :]
        y = (w[:, 4:5] * _load_position_major(buf.at[slot, 0, words], COMBINE_CHUNK)
             + w[:, 5:6] * _load_position_major(buf.at[slot, 1, words], COMBINE_CHUNK))
        o_ref[rows, :] = x1_ref[rows, :] + (y * lax.rsqrt(jnp.mean(y * y, axis=-1, keepdims=True) + EPS)) * gain
        for r in range(c * COMBINE_CHUNK, (c + 1) * COMBINE_CHUNK):
            for k in range(TOP_K):
                row_copy(dest2_ref, ahead, k, r).start(priority=k)

    @pl.when(i == last)
    def _():
        wait_tile(lax.rem(i + 1, ring))
        wait_tile(ahead)


def _combine(dest, ys, rec, x1, g_post, gate2, *, seq, tm):
    t, d = x1.shape
    last = t // tm - 1
    ahead = lambda n: pl.BlockSpec((TOP_K * tm,), lambda i: (jnp.minimum(i + n, last),), memory_space=pltpu.SMEM)
    return pl.pallas_call(
        functools.partial(_combine_kernel, tm=tm),
        grid=(t // tm,),
        in_specs=[ahead(0), ahead(1), ahead(2),
                  pl.BlockSpec(memory_space=pl.ANY),
                  pl.BlockSpec((tm, ROUTE_LANES), lambda i: (i, 0)),
                  pl.BlockSpec((tm, d), lambda i: (i, 0)),
                  pl.BlockSpec((1, d), lambda i: (0, 0)),
                  pl.BlockSpec((None, 1, d), lambda i: ((i * tm) // seq, 0, 0))],
        out_specs=pl.BlockSpec((tm, d), lambda i: (i, 0)),
        out_shape=jax.ShapeDtypeStruct((t, d), f32),
        scratch_shapes=[pltpu.VMEM((COMBINE_RING, TOP_K, tm * ROW_TILE, LANES), u32),
                        pltpu.SemaphoreType.DMA((COMBINE_RING,))],
        compiler_params=_cparams(("arbitrary",)),
        name="combine",
    )(dest, dest, dest, ys, rec, x1, g_post, gate2)


def _layer(x, c, w_ada, b_ada, g_pre_mix, g_post_mix, w_in, b_in, w_dw, b_dw, ln_conv_g, ln_conv_b,
           w_conv_out, b_conv_out, rel_bias, w_attn_out, w_out, g_pre_ffn, g_post_ffn,
           w_router_group, b_router_group, w_router_expert, b_router_expert, w1, w3, w2):
    batch, seq, d = x.shape
    t = batch * seq
    row = lambda v: v.reshape(1, -1)

    c_pad = jnp.zeros((SUBLANES, d), f32).at[:batch].set(c)
    mod = _ada(c_pad, w_ada, row(b_ada))[:batch]
    shift1, scale1, gate1, shift2, scale2, gate2 = [m.reshape(batch, 1, d) for m in jnp.split(mod, 6, axis=-1)]

    x2 = x.reshape(t, d)
    zt, qkv = _inproj(x2, row(g_pre_mix), scale1, shift1, w_in.astype(bf16), row(b_in), seq=seq, tm=INPROJ_TM)

    conv_act = _conv(zt.reshape(batch, seq, ZT_COLS), w_dw, row(b_dw), row(ln_conv_g), row(ln_conv_b),
                     tm=512).reshape(t, CONV_CH)

    os_, lses = [], []
    for gi, (_, dil) in enumerate(DILATED_GROUPS):
        o, lse = _attn_group(qkv, rel_bias.reshape(-1), gi, dil, batch=batch, seq=seq)
        os_.append(o)
        lses.append(lse)

    pad = ROUTE_LANES - N_GROUPS - N_EXPERTS
    w_r = jnp.concatenate([w_router_group, w_router_expert, jnp.zeros((d, pad), f32)], axis=1).astype(bf16)
    b_r = row(jnp.concatenate([b_router_group, b_router_expert.reshape(-1), jnp.zeros((pad,), f32)]))
    x1, h2p, logits = _merge(x2, conv_act, os_, lses, zt, w_conv_out.astype(bf16), row(b_conv_out),
                             w_attn_out.astype(bf16), w_out.astype(bf16), row(g_post_mix), gate1,
                             row(g_pre_ffn), scale2, shift2, w_r, b_r, seq=seq, tm=MERGE_TM, n_sub=MERGE_SUB)

    rec, rec_t, cnt = _route(logits, tm=512)

    counts = cnt[0, :N_EXPERTS].astype(jnp.int32)
    pcounts = (counts + MOE_ROWS - 1) // MOE_ROWS * MOE_ROWS
    pends = jnp.cumsum(pcounts)
    pstarts = pends - pcounts
    n_blocks = (t * TOP_K + N_EXPERTS * (MOE_ROWS - 1) + MOE_ROWS - 1) // MOE_ROWS
    n_used = (pends[-1:] // MOE_ROWS).astype(jnp.int32)
    tail_blocks = jnp.concatenate([n_used, n_blocks - n_used])
    eid = rec_t[0:TOP_K].astype(jnp.int32)
    rank = rec_t[TOP_K:2 * TOP_K].astype(jnp.int32)
    start_of = jnp.sum(jnp.where(eid[..., None] == jnp.arange(N_EXPERTS), pstarts, 0), axis=-1)
    dest = jnp.transpose(start_of + rank).reshape(t * TOP_K)
    tail = n_blocks - N_EXPERTS + jnp.arange(N_EXPERTS, dtype=jnp.int32)
    zero_blk = jnp.concatenate([jnp.where(pcounts > counts, pends // MOE_ROWS - 1, -1),
                                jnp.where(tail >= n_used[0], tail, -1)]).astype(jnp.int32)

    xs = _dispatch(zero_blk, dest, h2p, p_rows=n_blocks * MOE_ROWS, tm=512)
    ys = _experts((pstarts // MOE_ROWS).astype(jnp.int32), (pcounts // MOE_ROWS).astype(jnp.int32),
                  tail_blocks, counts, xs, w1, w3, w2)
    out = _combine(dest, ys, rec, x1, row(g_post_ffn), gate2, seq=seq, tm=512)
    return out.reshape(batch, seq, d)


def kernel(x, c, w_ada, b_ada, g_pre_mix, g_post_mix, w_in, b_in, w_dw, b_dw, ln_conv_g, ln_conv_b,
           w_conv_out, b_conv_out, rel_bias, w_attn_out, w_out, g_pre_ffn, g_post_ffn,
           w_router_group, b_router_group, w_router_expert, b_router_expert, w1, w3, w2):
    depth = w_ada.shape[0]
    for l in range(depth):
        pick = (lambda a: a.reshape(a.shape[1:])) if depth == 1 else (lambda a, l=l: a[l])
        x = _layer(x, c, pick(w_ada), pick(b_ada), pick(g_pre_mix), pick(g_post_mix), pick(w_in),
                   pick(b_in), pick(w_dw), pick(b_dw), pick(ln_conv_g), pick(ln_conv_b),
                   pick(w_conv_out), pick(b_conv_out), rel_bias, pick(w_attn_out), pick(w_out),
                   pick(g_pre_ffn), pick(g_post_ffn), pick(w_router_group), pick(b_router_group),
                   pick(w_router_expert), pick(b_router_expert), pick(w1), pick(w3), pick(w2))
    return x
```

```python
import functools
import math

import numpy as np
import jax
import jax.numpy as jnp
from jax import lax
from jax.experimental import pallas as pl
from jax.experimental.pallas import tpu as pltpu

D_MODEL = 2048
CONV_CH = 1024
CONV_WIDTH = 31
N_ATTN_HEADS = 12
HEADS_PER_GROUP = 4
HEAD_DIM = 128
ATTN_WIDTH = N_ATTN_HEADS * HEAD_DIM
DILATED_GROUPS = ((128, 1), (512, 4), (2048, 16))
NUM_BUCKETS = 32
REL_MAX_DISTANCE = 1024
N_GROUPS = 8
EXPERTS_PER_GROUP = 8
N_EXPERTS = N_GROUPS * EXPERTS_PER_GROUP
TOP_K = 2
EXPERT_FF = 512
EPS = 1e-6
NEG_INF = -1e30

LANES = 128
SUBLANES = 8
SUBLANES_BF16 = 16
V7X_VMEM_LIMIT_BYTES = 56 * 1024 * 1024

GROUP_W = HEADS_PER_GROUP * HEAD_DIM
ZT_COLS = 2 * D_MODEL + 2 * CONV_CH
COL_GLU = 2 * D_MODEL
INPROJ_TN = 3 * GROUP_W
ZT_TILES = ZT_COLS // INPROJ_TN
INPROJ_TM = 1024
ATTN_HALF = 64
ATTN_QB = 128
ATTN_WIN = ATTN_QB + 2 * ATTN_HALF
MOE_ROWS = 256
MERGE_TM = 256
MERGE_SUB = 2
ROUTE_LANES = 128
ROUTE_FIELDS = 8
PACKED_W = D_MODEL // 2
ROW_TILE = PACKED_W // LANES

bf16 = jnp.bfloat16
f32 = jnp.float32
u32 = jnp.uint32
HI_MASK = 0xFFFF0000


def _cparams(sem):
    return pltpu.CompilerParams(dimension_semantics=sem, vmem_limit_bytes=V7X_VMEM_LIMIT_BYTES)


def _sigmoid(x):
    return 1.0 / (1.0 + jnp.exp(-x))


def _pack_pairs(v):
    n = v.shape[1] // 2
    lo = lax.bitcast_convert_type(v[:, :n].astype(bf16).astype(f32), u32)
    hi = lax.bitcast_convert_type(v[:, n:].astype(bf16).astype(f32), u32)
    return (lo >> 16) | (hi & u32(HI_MASK))


def _unpack_lo(u):
    return lax.bitcast_convert_type(u << 16, f32)


def _unpack_hi(u):
    return lax.bitcast_convert_type(u & u32(HI_MASK), f32)


def _store_position_major(ref, v):
    packed = _pack_pairs(v)
    for j in range(ROW_TILE):
        ref[pl.ds(j, v.shape[0], stride=ROW_TILE), :] = packed[:, j * LANES:(j + 1) * LANES]


def _load_position_major(ref, n):
    chunks = [ref[pl.ds(j, n, stride=ROW_TILE), :] for j in range(ROW_TILE)]
    return jnp.concatenate([_unpack_lo(c) for c in chunks] + [_unpack_hi(c) for c in chunks], axis=1)


def _ada_kernel(c_ref, w_ref, b_ref, o_ref):
    c = c_ref[...]
    a = (c * _sigmoid(c)).astype(bf16)
    o_ref[...] = jnp.dot(a, w_ref[...].astype(bf16), preferred_element_type=f32) + b_ref[...]


def _ada(c_pad, w_ada, b_ada):
    rows, d = c_pad.shape
    n = w_ada.shape[1]
    tn = 1024
    return pl.pallas_call(
        _ada_kernel,
        grid=(n // tn,),
        in_specs=[pl.BlockSpec((rows, d), lambda j: (0, 0)),
                  pl.BlockSpec((d, tn), lambda j: (0, j)),
                  pl.BlockSpec((1, tn), lambda j: (0, j))],
        out_specs=pl.BlockSpec((rows, tn), lambda j: (0, j)),
        out_shape=jax.ShapeDtypeStruct((rows, n), f32),
        compiler_params=_cparams(("arbitrary",)),
        name="ada_mod",
    )(c_pad, w_ada, b_ada)


def _inproj_kernel(x_ref, g_ref, scale0_ref, shift0_ref, scale_ref, shift_ref,
                   wa_ref, wb_ref, wc_ref, ba_ref, bb_ref, bc_ref, zt_ref, qkv_ref, h_scr, x_scr, slab_scr, xsem):
    i, j = pl.program_id(0), pl.program_id(1)
    tm = x_scr.shape[0]
    slot = jnp.bitwise_and(i, 1)
    nxt_tile = jnp.minimum(i + 1, pl.num_programs(0) - 1)

    def x_copy(tile):
        return pltpu.make_async_copy(x_ref.at[pl.ds(pl.multiple_of(tile * tm, tm), tm), :], x_scr, xsem)

    def prenorm(rows, sc_ref, sh_ref, dst_slot):
        x = x_scr[rows, :]
        ms = jnp.mean(x * x, axis=-1, keepdims=True)
        h = x * lax.rsqrt(ms + EPS) * g_ref[...]
        h_scr[dst_slot, rows, :] = (h * (1.0 + sc_ref[...]) + sh_ref[...]).astype(bf16)

    @pl.when(j == 0)
    def _():
        @pl.when(i == 0)
        def _():
            x_copy(0).start()
            x_copy(0).wait()
            prenorm(slice(None), scale0_ref, shift0_ref, 0)

        x_copy(nxt_tile).start()

    @pl.when(j == ZT_TILES)
    def _():
        x_copy(nxt_tile).wait()

    w_refs, b_refs = (wa_ref, wb_ref, wc_ref), (ba_ref, bb_ref, bc_ref)

    def chunk(c):
        return jnp.dot(h_scr[slot], w_refs[c][...], preferred_element_type=f32) + b_refs[c][...]

    n_parts = len(DILATED_GROUPS)
    part_rows = [(tm // n_parts // SUBLANES_BF16) * SUBLANES_BF16] * (n_parts - 1)
    part_rows.append(tm - sum(part_rows))
    part_start = [sum(part_rows[:p]) for p in range(n_parts)]

    @pl.when(j < ZT_TILES)
    def _():
        for c in range(INPROJ_TN // GROUP_W):
            zt_ref[:, c * GROUP_W:(c + 1) * GROUP_W] = chunk(c).astype(bf16)

    for gi, (_, dil) in enumerate(DILATED_GROUPS):
        @pl.when(j == ZT_TILES + gi)
        def _(gi=gi, dil=dil):
            prenorm(slice(part_start[gi], part_start[gi] + part_rows[gi]), scale_ref, shift_ref, 1 - slot)
            per = tm // dil
            for c in range(INPROJ_TN // GROUP_W):
                z = chunk(c)
                if dil == 1:
                    qkv_ref[:, c * GROUP_W:(c + 1) * GROUP_W] = z.astype(bf16)
                    continue
                slabs = range(GROUP_W // LANES)
                for s in slabs:
                    slab_scr[0, s] = z[:, s * LANES:(s + 1) * LANES]
                src, stride, first = 0, dil, lambda r: r
                if dil == 16:
                    quarter = tm // 4
                    for s in slabs:
                        for r4 in range(4):
                            slab_scr[1, s, r4 * quarter:(r4 + 1) * quarter, :] = (
                                slab_scr[0, s, pl.ds(r4, quarter, stride=4), :])
                    src, stride, first = 1, 4, lambda r: (r % 4) * quarter + r // 4
                for r in range(dil):
                    for s in slabs:
                        col = c * GROUP_W + s * LANES
                        qkv_ref[r * per:(r + 1) * per, col:col + LANES] = (
                            slab_scr[src, s, pl.ds(first(r), per, stride=stride), :].astype(bf16))


def _inproj_col_block(j, c):
    n_blocks = (2 * CONV_CH + 3 * ATTN_WIDTH + 2 * D_MODEL) // GROUP_W
    gate0 = (2 * CONV_CH + 3 * ATTN_WIDTH) // GROUP_W
    q0 = 2 * CONV_CH // GROUP_W
    per_proj = ATTN_WIDTH // GROUP_W
    token_major = lax.rem(gate0 + (INPROJ_TN // GROUP_W) * j + c, n_blocks)
    return jnp.where(j < ZT_TILES, token_major, q0 + (j - ZT_TILES) + per_proj * c)


def _inproj(x2, g, scale, shift, w, b, *, seq, tm):
    t, d = x2.shape
    n = w.shape[1]
    tn = INPROJ_TN
    n_groups = n // tn - ZT_TILES
    last = t // tm - 1
    first_map = lambda i, j: (0, 0, 0)
    next_map = lambda i, j: ((jnp.minimum(i + 1, last) * tm) // seq, 0, 0)
    chunks = range(tn // GROUP_W)
    w_specs = [pl.BlockSpec((d, GROUP_W), lambda i, j, c=c: (0, _inproj_col_block(j, c))) for c in chunks]
    b_specs = [pl.BlockSpec((1, GROUP_W), lambda i, j, c=c: (0, _inproj_col_block(j, c))) for c in chunks]
    return pl.pallas_call(
        _inproj_kernel,
        grid=(t // tm, n // tn),
        in_specs=[pl.BlockSpec(memory_space=pl.ANY),
                  pl.BlockSpec((1, d), lambda i, j: (0, 0)),
                  pl.BlockSpec((None, 1, d), first_map), pl.BlockSpec((None, 1, d), first_map),
                  pl.BlockSpec((None, 1, d), next_map), pl.BlockSpec((None, 1, d), next_map)]
                 + w_specs + b_specs,
        out_specs=[pl.BlockSpec((tm, tn), lambda i, j: (i, jnp.minimum(j, ZT_TILES - 1))),
                   pl.BlockSpec((None, tm, tn), lambda i, j: (jnp.maximum(j - ZT_TILES, 0), i, 0))],
        out_shape=[jax.ShapeDtypeStruct((t, ZT_COLS), bf16),
                   jax.ShapeDtypeStruct((n_groups, t, tn), bf16)],
        scratch_shapes=[pltpu.VMEM((2, tm, d), bf16), pltpu.VMEM((tm, d), f32),
                        pltpu.VMEM((2, GROUP_W // LANES, tm, LANES), f32), pltpu.SemaphoreType.DMA(())],
        compiler_params=_cparams(("arbitrary", "arbitrary")),
        name="inproj",
    )(x2, g, scale, shift, scale, shift, w, w, w, b, b, b)


CONV_HALO = 16
CONV_CHUNK = 32
CONV_SUB = CONV_CH // LANES
NORM_CHUNK = 64


def _conv_kernel(ap_ref, ac_ref, an_ref, gp_ref, gc_ref, gn_ref, w_ref, bdw_ref, lng_ref, lnb_ref,
                 o_ref, u_scr, y_scr, *, tm):
    i = pl.program_id(1)
    last = pl.num_programs(1) - 1

    def glu(a_ref, g_ref):
        return a_ref[...].astype(f32) * _sigmoid(g_ref[...].astype(f32))

    def put(row0, val):
        for j in range(CONV_SUB):
            u_scr[pl.ds(row0 * CONV_SUB + j, val.shape[0], stride=CONV_SUB), :] = (
                val[:, j * LANES:(j + 1) * LANES])

    put(0, jnp.where(i > 0, glu(ap_ref, gp_ref), 0.0))
    put(CONV_HALO, glu(ac_ref, gc_ref))
    put(CONV_HALO + tm, jnp.where(i < last, glu(an_ref, gn_ref), 0.0))

    def chunk(c, carry):
        r0 = c * CONV_CHUNK
        acc = jnp.zeros((CONV_CHUNK, CONV_SUB, LANES), f32) + bdw_ref[...][None]
        for k in range(CONV_WIDTH):
            start = pl.multiple_of((r0 + k + 1) * CONV_SUB, CONV_SUB)
            xk = u_scr[pl.ds(start, CONV_CHUNK * CONV_SUB), :].reshape(CONV_CHUNK, CONV_SUB, LANES)
            acc = acc + w_ref[k][None] * xk
        out0 = pl.multiple_of(r0 * CONV_SUB, CONV_CHUNK * CONV_SUB)
        y_scr[pl.ds(out0, CONV_CHUNK * CONV_SUB), :] = acc.reshape(CONV_CHUNK * CONV_SUB, LANES)
        return carry

    lax.fori_loop(0, tm // CONV_CHUNK, chunk, 0)

    def norm(c, carry):
        r0 = pl.multiple_of(c * NORM_CHUNK, NORM_CHUNK)
        acc = jnp.concatenate(
            [y_scr[pl.ds(r0 * CONV_SUB + j, NORM_CHUNK, stride=CONV_SUB), :] for j in range(CONV_SUB)],
            axis=1)
        mu = jnp.mean(acc, axis=-1, keepdims=True)
        cen = acc - mu
        var = jnp.mean(cen * cen, axis=-1, keepdims=True)
        y = cen * lax.rsqrt(var + EPS) * lng_ref[...] + lnb_ref[...]
        o_ref[pl.ds(r0, NORM_CHUNK), :] = (y * _sigmoid(y)).astype(bf16)
        return carry

    lax.fori_loop(0, tm // NORM_CHUNK, norm, 0, unroll=4)


def _conv(z3, w_dw, b_dw, ln_g, ln_b, *, tm):
    b, s, _ = z3.shape
    hb = tm // CONV_HALO
    nhb = s // CONV_HALO
    ca, cg = COL_GLU // CONV_CH, COL_GLU // CONV_CH + 1

    def halo(col, which):
        if which < 0:
            return pl.BlockSpec((None, CONV_HALO, CONV_CH),
                                lambda bb, i: (bb, jnp.maximum(i * hb - 1, 0), col))
        return pl.BlockSpec((None, CONV_HALO, CONV_CH),
                            lambda bb, i: (bb, jnp.minimum((i + 1) * hb, nhb - 1), col))

    cur = lambda col: pl.BlockSpec((None, tm, CONV_CH), lambda bb, i: (bb, i, col))
    vec = pl.BlockSpec((1, CONV_CH), lambda bb, i: (0, 0))
    return pl.pallas_call(
        functools.partial(_conv_kernel, tm=tm),
        grid=(b, s // tm),
        in_specs=[halo(ca, -1), cur(ca), halo(ca, 1), halo(cg, -1), cur(cg), halo(cg, 1),
                  pl.BlockSpec((CONV_WIDTH, CONV_SUB, LANES), lambda bb, i: (0, 0, 0)),
                  pl.BlockSpec((CONV_SUB, LANES), lambda bb, i: (0, 0)), vec, vec],
        out_specs=pl.BlockSpec((None, tm, CONV_CH), lambda bb, i: (bb, i, 0)),
        out_shape=jax.ShapeDtypeStruct((b, s, CONV_CH), bf16),
        scratch_shapes=[pltpu.VMEM(((tm + 2 * CONV_HALO) * CONV_SUB, LANES), f32),
                        pltpu.VMEM((tm * CONV_SUB, LANES), f32)],
        compiler_params=_cparams(("arbitrary", "arbitrary")),
        name="conv_branch",
    )(z3, z3, z3, z3, z3, z3, w_dw.reshape(CONV_WIDTH, CONV_SUB, LANES), b_dw.reshape(CONV_SUB, LANES),
      ln_g, ln_b)


def _t5_bucket_table(dil):
    delta = (np.arange(ATTN_WIN)[None, :] - ATTN_HALF) - np.arange(ATTN_QB)[:, None]
    rel = delta * dil
    nb = NUM_BUCKETS // 2
    max_exact = nb // 2
    n = np.abs(rel)
    nf = np.maximum(n, 1).astype(np.float32)
    large = max_exact + (np.log(nf / np.float32(max_exact)) / np.float32(math.log(REL_MAX_DISTANCE / max_exact))
                         * np.float32(nb - max_exact)).astype(np.int32)
    large = np.minimum(large, nb - 1)
    bucket = np.where(rel > 0, nb, 0) + np.where(n < max_exact, n, large)
    return np.where(np.abs(delta) <= ATTN_HALF, bucket, -1).astype(np.int32)


def _attn_kernel(rb_ref, bkt_ref, q_ref, kp_ref, kc_ref, kn_ref, vp_ref, vc_ref, vn_ref, o_ref, lse_ref,
                 q_scr, k_scr, v_scr, bias_scr, *, gi, dil, n_tiles, sub_len):
    i = pl.program_id(1)
    h = ATTN_HALF
    tile = INPROJ_TM
    per = tile // dil
    tq = n_tiles * per

    @pl.when((pl.program_id(0) == 0) & (i == 0))
    def _():
        bk = bkt_ref[...]
        for hh in range(HEADS_PER_GROUP):
            acc = jnp.full((ATTN_QB, ATTN_WIN), NEG_INF, f32)
            for b_id in range(NUM_BUCKETS):
                acc = jnp.where(bk == b_id, rb_ref[b_id * N_ATTN_HEADS + gi * HEADS_PER_GROUP + hh], acc)
            bias_scr[hh] = acc

    halo_rows = kp_ref.shape[0]
    for r in range(dil):
        prev_lo = (r + 1) * per - h if halo_rows == tile else 0
        next_lo = r * per if halo_rows == tile else 0
        for dst, prv, cur, nxt in ((k_scr, kp_ref, kc_ref, kn_ref), (v_scr, vp_ref, vc_ref, vn_ref)):
            dst[r, 0:h, :] = prv[prev_lo:prev_lo + h, :]
            for t in range(n_tiles):
                dst[r, h + t * per:h + (t + 1) * per, :] = cur[t * tile + r * per:t * tile + (r + 1) * per, :]
            dst[r, h + tq:, :] = nxt[next_lo:next_lo + h, :]
        for t in range(n_tiles):
            q_scr[r, t * per:(t + 1) * per, :] = q_ref[t * tile + r * per:t * tile + (r + 1) * per, :]

    scale = HEAD_DIM ** -0.5
    col = lax.broadcasted_iota(jnp.int32, (ATTN_QB, ATTN_WIN), 1)
    lane = lax.broadcasted_iota(jnp.int32, (ATTN_QB, LANES), 1)

    def rows(start, n):
        if dil == 1:
            return pl.ds(pl.multiple_of(start, ATTN_QB), n)
        return pl.ds(start, n, stride=dil)

    def unit(u, carry):
        r = jnp.bitwise_and(u, dil - 1)
        jb = lax.shift_right_logical(u, jnp.int32(dil.bit_length() - 1))
        start = jb * (ATTN_QB * dil) + r
        p0 = pl.multiple_of(jb * ATTN_QB, ATTN_QB)
        cols = [slice(hh * HEAD_DIM, (hh + 1) * HEAD_DIM) for hh in range(HEADS_PER_GROUP)]
        q = [q_scr[r, pl.ds(p0, ATTN_QB), cs] for cs in cols]
        kw = [k_scr[r, pl.ds(p0, ATTN_WIN), cs] for cs in cols]
        vw = [v_scr[r, pl.ds(p0, ATTN_WIN), cs] for cs in cols]
        key0 = i * tq + jb * ATTN_QB - ATTN_HALF
        valid = (col + key0 >= 0) & (col + key0 < sub_len)
        s = [lax.dot_general(q[hh], kw[hh], (((1,), (1,)), ((), ())), preferred_element_type=f32)
             for hh in range(HEADS_PER_GROUP)]
        s = [jnp.where(valid, s[hh] * scale + bias_scr[hh], NEG_INF) for hh in range(HEADS_PER_GROUP)]
        m = [jnp.max(x, axis=-1, keepdims=True) for x in s]
        p = [jnp.exp(s[hh] - m[hh]) for hh in range(HEADS_PER_GROUP)]
        den = [jnp.sum(x, axis=-1, keepdims=True) for x in p]
        pv = [jnp.dot(p[hh].astype(bf16), vw[hh], preferred_element_type=f32)
              for hh in range(HEADS_PER_GROUP)]
        lse_tile = jnp.zeros((ATTN_QB, LANES), f32)
        for hh in range(HEADS_PER_GROUP):
            o_ref[hh, rows(start, ATTN_QB), :] = pv[hh] / den[hh]
            lse_tile = jnp.where(lane == hh, m[hh] + jnp.log(den[hh]), lse_tile)
        lse_ref[rows(start, ATTN_QB), :] = lse_tile
        return carry

    lax.fori_loop(0, n_tiles * tile // ATTN_QB, unit, 0, unroll=8)


def _attn_group(qkv, rel_bias_flat, gi, dil, *, batch, seq):
    t = batch * seq
    tile = INPROJ_TM
    n_tiles = max(1, ATTN_QB * dil // tile)
    ts = n_tiles * tile
    steps = seq // ts
    tq = ts // dil
    halo = ATTN_HALF if dil == 1 else tile
    hpt = ts // halo
    hps = seq // halo

    def cur(which):
        return pl.BlockSpec((None, ts, GROUP_W), lambda b, i: (gi, b * steps + i, which))

    def prev(which):
        return pl.BlockSpec((None, halo, GROUP_W),
                            lambda b, i: (gi, b * hps + jnp.maximum(i * hpt - 1, 0), which))

    def nxt(which):
        return pl.BlockSpec((None, halo, GROUP_W),
                            lambda b, i: (gi, b * hps + jnp.minimum((i + 1) * hpt, hps - 1), which))

    bkt = jnp.asarray(_t5_bucket_table(dil))
    return pl.pallas_call(
        functools.partial(_attn_kernel, gi=gi, dil=dil, n_tiles=n_tiles, sub_len=seq // dil),
        grid=(batch, steps),
        in_specs=[pl.BlockSpec(memory_space=pltpu.SMEM),
                  pl.BlockSpec((ATTN_QB, ATTN_WIN), lambda b, i: (0, 0)),
                  cur(0), prev(1), cur(1), nxt(1), prev(2), cur(2), nxt(2)],
        out_specs=[pl.BlockSpec((HEADS_PER_GROUP, ts, LANES), lambda b, i: (0, b * steps + i, 0)),
                   pl.BlockSpec((ts, LANES), lambda b, i: (b * steps + i, 0))],
        out_shape=[jax.ShapeDtypeStruct((HEADS_PER_GROUP, t, LANES), f32),
                   jax.ShapeDtypeStruct((t, LANES), f32)],
        scratch_shapes=[pltpu.VMEM((dil, tq, GROUP_W), bf16),
                        pltpu.VMEM((dil, tq + 2 * ATTN_HALF, GROUP_W), bf16),
                        pltpu.VMEM((dil, tq + 2 * ATTN_HALF, GROUP_W), bf16),
                        pltpu.VMEM((HEADS_PER_GROUP, ATTN_QB, ATTN_WIN), f32)],
        compiler_params=_cparams(("arbitrary", "arbitrary")),
        name=f"attn_g{gi}",
    )(rel_bias_flat, bkt, qkv, qkv, qkv, qkv, qkv, qkv, qkv)


def _merge_kernel(x_ref, ca_ref, o0_ref, o1_ref, o2_ref, l0_ref, l1_ref, l2_ref, ga_ref, gb_ref,
                  wco_ref, bco_ref, wao_ref, wo_ref, gpost_ref, gate1_ref, gpre_ref, scale2_ref,
                  shift2_ref, wr_ref, br_ref, x1_ref, h2p_ref, logit_ref, *, n_sub):
    hm = x_ref.shape[0] // n_sub
    subs = [slice(k * hm, (k + 1) * hm) for k in range(n_sub)]

    branch_a = [jnp.dot(ca_ref[rs, :], wco_ref[...], preferred_element_type=f32) + bco_ref[...] for rs in subs]

    def attention_mix(rs):
        l0, l1, l2 = l0_ref[rs, :], l1_ref[rs, :], l2_ref[rs, :]
        lmax = jnp.maximum(jnp.maximum(l0, l1), l2)
        e0, e1, e2 = jnp.exp(l0 - lmax), jnp.exp(l1 - lmax), jnp.exp(l2 - lmax)
        inv = 1.0 / (e0 + e1 + e2)
        parts = []
        for hh in range(HEADS_PER_GROUP):
            ls = slice(hh, hh + 1)
            parts.append((e0[:, ls] * inv[:, ls]) * o0_ref[hh, rs, :] + (e1[:, ls] * inv[:, ls]) * o1_ref[hh, rs, :]
                         + (e2[:, ls] * inv[:, ls]) * o2_ref[hh, rs, :])
        return jnp.concatenate(parts, axis=1).astype(bf16)

    att = [attention_mix(rs) for rs in subs]
    branch_b = [jnp.dot(a, wao_ref[...], preferred_element_type=f32) for a in att]
    mix = [(_sigmoid(ga_ref[rs, :].astype(f32)) * branch_a[k]
            + _sigmoid(gb_ref[rs, :].astype(f32)) * branch_b[k]).astype(bf16) for k, rs in enumerate(subs)]
    y = [jnp.dot(m, wo_ref[...], preferred_element_type=f32) for m in mix]

    post_gain = gate1_ref[...] * gpost_ref[...]
    pre_gain = gpre_ref[...] * (1.0 + scale2_ref[...])
    h2s = []
    for k, rs in enumerate(subs):
        x1 = x_ref[rs, :] + (y[k] * lax.rsqrt(jnp.mean(y[k] * y[k], axis=-1, keepdims=True) + EPS)) * post_gain
        x1_ref[rs, :] = x1
        h2 = (x1 * lax.rsqrt(jnp.mean(x1 * x1, axis=-1, keepdims=True) + EPS)) * pre_gain + shift2_ref[...]
        _store_position_major(h2p_ref.at[pl.ds(k * hm * ROW_TILE, hm * ROW_TILE), :], h2)
        h2s.append(h2.astype(bf16))
    for k, rs in enumerate(subs):
        logit_ref[rs, :] = jnp.dot(h2s[k], wr_ref[...], preferred_element_type=f32) + br_ref[...]


def _merge(x2, conv_act, os_, lses, zt, w_co, b_co, w_ao, w_o, g_post, gate1, g_pre, scale2, shift2,
           w_r, b_r, *, seq, tm, n_sub):
    t, d = x2.shape
    rows = lambda w: pl.BlockSpec((tm, w), lambda i: (i, 0))
    heads = pl.BlockSpec((HEADS_PER_GROUP, tm, LANES), lambda i: (0, i, 0))
    full = lambda a: pl.BlockSpec(a.shape, lambda i: (0,) * a.ndim)
    perb = pl.BlockSpec((None, 1, d), lambda i: ((i * tm) // seq, 0, 0))
    return pl.pallas_call(
        functools.partial(_merge_kernel, n_sub=n_sub),
        grid=(t // tm,),
        in_specs=[rows(d), rows(CONV_CH), heads, heads, heads, rows(LANES), rows(LANES), rows(LANES),
                  pl.BlockSpec((tm, d), lambda i: (i, 0)), pl.BlockSpec((tm, d), lambda i: (i, 1)),
                  full(w_co), full(b_co), full(w_ao), full(w_o), full(g_post), perb, full(g_pre),
                  perb, perb, full(w_r), full(b_r)],
        out_specs=[rows(d), pl.BlockSpec((tm * ROW_TILE, LANES), lambda i: (i, 0)), rows(ROUTE_LANES)],
        out_shape=[jax.ShapeDtypeStruct((t, d), f32),
                   jax.ShapeDtypeStruct((t * ROW_TILE, LANES), u32),
                   jax.ShapeDtypeStruct((t, ROUTE_LANES), f32)],
        compiler_params=_cparams(("arbitrary",)),
        name="merge",
    )(x2, conv_act, *os_, *lses, zt, zt, w_co, b_co, w_ao, w_o, g_post, gate1, g_pre, scale2, shift2,
      w_r, b_r)


def _route_kernel(logit_ref, rec_ref, rect_ref, cnt_ref, carry):
    i = pl.program_id(0)

    @pl.when(i == 0)
    def _():
        carry[...] = jnp.zeros_like(carry)

    lg = logit_ref[...]
    tm = lg.shape[0]
    lane = lax.broadcasted_iota(jnp.int32, lg.shape, 1).astype(f32)
    big = float(2 * ROUTE_LANES)

    def first_max(mask):
        v = jnp.max(jnp.where(mask, lg, -jnp.inf), axis=-1, keepdims=True)
        idx = jnp.min(jnp.where(mask & (lg == v), lane, big), axis=-1, keepdims=True)
        return v, idx

    gmask = lane < N_GROUPS
    gmax, gsel = first_max(gmask)
    p_g = 1.0 / jnp.sum(jnp.where(gmask, jnp.exp(lg - gmax), 0.0), axis=-1, keepdims=True)
    e_lo = N_GROUPS + EXPERTS_PER_GROUP * gsel
    emask = (lane >= e_lo) & (lane < e_lo + EXPERTS_PER_GROUP)
    v0, i0 = first_max(emask)
    v1, i1 = first_max(emask & (lane != i0))
    t1 = jnp.exp(v1 - v0)
    w0 = p_g / (1.0 + t1)
    w1 = p_g * t1 / (1.0 + t1)
    e0 = i0 - N_GROUPS
    e1 = i1 - N_GROUPS

    hit0 = lane == e0
    hit1 = lane == e1
    onehot = jnp.where(hit0 | hit1, 1.0, 0.0)
    r_i = lax.broadcasted_iota(jnp.int32, (tm, tm), 0)
    c_i = lax.broadcasted_iota(jnp.int32, (tm, tm), 1)
    tril = jnp.where(c_i < r_i, 1.0, 0.0).astype(bf16)
    before = jnp.dot(tril, onehot.astype(bf16), preferred_element_type=f32) + carry[...]
    rank0 = jnp.sum(jnp.where(hit0, before, 0.0), axis=-1, keepdims=True)
    rank1 = jnp.sum(jnp.where(hit1, before, 0.0), axis=-1, keepdims=True)
    carry[...] = carry[...] + jnp.sum(onehot, axis=0, keepdims=True)
    cnt_ref[...] = carry[...]

    rec = jnp.zeros_like(lg)
    for slot, val in enumerate((e0, e1, rank0, rank1, w0, w1)):
        rec = jnp.where(lane == slot, val, rec)
    rec_ref[...] = rec
    rect_ref[...] = jnp.transpose(rec)[0:ROUTE_FIELDS, :]


def _route(logits, *, tm):
    t = logits.shape[0]
    return pl.pallas_call(
        _route_kernel,
        grid=(t // tm,),
        in_specs=[pl.BlockSpec((tm, ROUTE_LANES), lambda i: (i, 0))],
        out_specs=[pl.BlockSpec((tm, ROUTE_LANES), lambda i: (i, 0)),
                   pl.BlockSpec((ROUTE_FIELDS, tm), lambda i: (0, i)),
                   pl.BlockSpec((1, ROUTE_LANES), lambda i: (0, 0))],
        out_shape=[jax.ShapeDtypeStruct((t, ROUTE_LANES), f32),
                   jax.ShapeDtypeStruct((ROUTE_FIELDS, t), f32),
                   jax.ShapeDtypeStruct((1, ROUTE_LANES), f32)],
        scratch_shapes=[pltpu.VMEM((1, ROUTE_LANES), f32)],
        compiler_params=_cparams(("arbitrary",)),
        name="route",
    )(logits)


DMA_UNROLL = 8


def _dispatch_kernel(zero_blk_ref, dest_ref, h_ref, xs_ref, zero_scr, sem, zsem, *, tm):
    def row_copy(r, d):
        return pltpu.make_async_copy(h_ref.at[pl.ds(pl.multiple_of(r * ROW_TILE, ROW_TILE), ROW_TILE), :],
                                     xs_ref.at[pl.ds(pl.multiple_of(d * ROW_TILE, ROW_TILE), ROW_TILE), :],
                                     sem)

    def zero_copy(blk, zs):
        n = MOE_ROWS * ROW_TILE
        return pltpu.make_async_copy(zero_scr, xs_ref.at[pl.ds(pl.multiple_of(blk * n, n), n), :], zsem.at[zs])

    def start(j, zs):
        @pl.when(zero_blk_ref[j] >= 0)
        def _():
            zero_copy(zero_blk_ref[j], zs).start()

    def wait(j, zs):
        @pl.when(zero_blk_ref[j] >= 0)
        def _():
            zero_copy(0, zs).wait()

    @pl.when(pl.program_id(0) == 0)
    def _():
        zero_scr[...] = jnp.zeros_like(zero_scr)
        lax.fori_loop(0, 2 * N_EXPERTS, lambda j, c: (start(j, jnp.where(j < N_EXPERTS, 0, 1)), c)[1], 0)
        lax.fori_loop(0, N_EXPERTS, lambda j, c: (wait(j, 0), c)[1], 0)

    @pl.when(pl.program_id(0) == pl.num_programs(0) - 1)
    def _():
        lax.fori_loop(N_EXPERTS, 2 * N_EXPERTS, lambda j, c: (wait(j, 1), c)[1], 0)

    def issue(r, c):
        row_copy(r, dest_ref[TOP_K * r]).start(priority=0)
        row_copy(r, dest_ref[TOP_K * r + 1]).start(priority=1)
        return c

    lax.fori_loop(0, tm, issue, 0, unroll=DMA_UNROLL)
    for _ in range(TOP_K):
        pltpu.make_async_copy(h_ref, xs_ref.at[pl.ds(0, tm * ROW_TILE), :], sem).wait()


def _dispatch(zero_blk, dest, h2p, *, p_rows, tm):
    grid_spec = pltpu.PrefetchScalarGridSpec(
        num_scalar_prefetch=1,
        grid=(h2p.shape[0] // (tm * ROW_TILE),),
        in_specs=[pl.BlockSpec((TOP_K * tm,), lambda i, *_: (i,), memory_space=pltpu.SMEM),
                  pl.BlockSpec((tm * ROW_TILE, LANES), lambda i, *_: (i, 0))],
        out_specs=pl.BlockSpec(memory_space=pl.ANY),
        scratch_shapes=[pltpu.VMEM((MOE_ROWS * ROW_TILE, LANES), u32), pltpu.SemaphoreType.DMA(()),
                        pltpu.SemaphoreType.DMA((2,))],
    )
    return pl.pallas_call(
        functools.partial(_dispatch_kernel, tm=tm),
        grid_spec=grid_spec,
        out_shape=jax.ShapeDtypeStruct((p_rows * ROW_TILE, LANES), u32),
        compiler_params=_cparams(("arbitrary",)),
        name="dispatch",
    )(zero_blk, dest, h2p)


def _expert_kernel(bstart_ref, nblk_ref, tail_ref, cnt_ref, xs_ref, w1_ref, w3_ref, w2_ref, ys_ref,
                   wf1, wf3, wf2, w1b, w3b, w2b, xbuf, ybuf, zbuf, sem_in, sem_out, sem_w, sem_z):
    e = pl.program_id(0)
    last = pl.num_programs(0) - 1
    ws = jnp.bitwise_and(e, 1)
    n = nblk_ref[e]
    g0 = bstart_ref[e]
    n_used = tail_ref[0]
    blk = MOE_ROWS * ROW_TILE

    def hbm_rows(ref, g):
        return ref.at[pl.ds(pl.multiple_of(g * blk, blk), blk), :]

    def in_copy(g, slot):
        return pltpu.make_async_copy(hbm_rows(xs_ref, g), xbuf.at[slot], sem_in.at[slot])

    def out_copy(g, slot):
        return pltpu.make_async_copy(ybuf.at[slot], hbm_rows(ys_ref, g), sem_out.at[slot])

    def zero_copy(b):
        return pltpu.make_async_copy(zbuf, hbm_rows(ys_ref, n_used + b), sem_z)

    def weight_copies(ex, slot):
        return [pltpu.make_async_copy(w1_ref.at[ex], wf1.at[slot], sem_w.at[slot]),
                pltpu.make_async_copy(w3_ref.at[ex], wf3.at[slot], sem_w.at[slot]),
                pltpu.make_async_copy(w2_ref.at[ex], wf2.at[slot], sem_w.at[slot])]

    @pl.when(e == 0)
    def _():
        for cp in weight_copies(0, 0):
            cp.start(priority=1)

        @pl.when(n_used > 0)
        def _():
            in_copy(0, 0).start()

        zbuf[...] = jnp.zeros_like(zbuf)
        lax.fori_loop(0, tail_ref[1], lambda b, c: (zero_copy(b).start(priority=1), c)[1], 0)

    @pl.when(e < last)
    def _():
        for cp in weight_copies(e + 1, 1 - ws):
            cp.start(priority=1)

    for cp in weight_copies(e, ws):
        cp.wait()

    def cast(dst, src):
        dst[...] = src[ws].astype(bf16)

    @pl.when(n == 0)
    def _():
        cast(w1b, wf1)
        cast(w3b, wf3)
        cast(w2b, wf2)

    def block(b, cast_weights):
        g = g0 + b
        slot = jnp.bitwise_and(g, 1)
        in_copy(g, slot).wait()

        @pl.when(g + 1 < n_used)
        def _():
            in_copy(g + 1, 1 - slot).start()

        @pl.when(g >= 2)
        def _():
            out_copy(g - 2, slot).wait()

        def swiglu(rows):
            x = _load_position_major(xbuf.at[slot, pl.ds(0, rows * ROW_TILE)], rows).astype(bf16)
            if cast_weights:
                cast(w1b, wf1)
            h1 = jnp.dot(x, w1b[...], preferred_element_type=f32)
            if cast_weights:
                cast(w3b, wf3)
            h3 = jnp.dot(x, w3b[...], preferred_element_type=f32)
            a = (h1 * _sigmoid(h1) * h3).astype(bf16)
            if cast_weights:
                cast(w2b, wf2)
            _store_position_major(ybuf.at[slot, pl.ds(0, rows * ROW_TILE)],
                                  jnp.dot(a, w2b[...], preferred_element_type=f32))

        half = MOE_ROWS // 2
        real_rows = cnt_ref[e] - b * MOE_ROWS

        @pl.when(real_rows > half)
        def _():
            swiglu(MOE_ROWS)

        @pl.when(real_rows <= half)
        def _():
            swiglu(half)
            ybuf[slot, half * ROW_TILE:, :] = jnp.zeros((half * ROW_TILE, LANES), u32)

        out_copy(g, slot).start()

    @pl.when(n > 0)
    def _():
        block(0, True)

    lax.fori_loop(1, n, lambda b, c: (block(b, False), c)[1], 0)

    @pl.when(e == last)
    def _():
        @pl.when(n_used >= 2)
        def _():
            out_copy(n_used - 2, jnp.bitwise_and(n_used, 1)).wait()

        @pl.when(n_used >= 1)
        def _():
            out_copy(n_used - 1, jnp.bitwise_and(n_used - 1, 1)).wait()

        lax.fori_loop(0, tail_ref[1], lambda b, c: (zero_copy(0).wait(), c)[1], 0)


def _experts(bstart, nblk, tail, cnt, xs, w1, w3, w2):
    blk = MOE_ROWS * ROW_TILE
    n_exp, d, ff = w1.shape
    grid_spec = pltpu.PrefetchScalarGridSpec(
        num_scalar_prefetch=4,
        grid=(n_exp,),
        in_specs=[pl.BlockSpec(memory_space=pl.ANY)] * 4,
        out_specs=pl.BlockSpec(memory_space=pl.ANY),
        scratch_shapes=[pltpu.VMEM((2, d, ff), f32), pltpu.VMEM((2, d, ff), f32), pltpu.VMEM((2, ff, d), f32),
                        pltpu.VMEM((d, ff), bf16), pltpu.VMEM((d, ff), bf16), pltpu.VMEM((ff, d), bf16),
                        pltpu.VMEM((2, blk, LANES), u32), pltpu.VMEM((2, blk, LANES), u32),
                        pltpu.VMEM((blk, LANES), u32),
                        pltpu.SemaphoreType.DMA((2,)), pltpu.SemaphoreType.DMA((2,)),
                        pltpu.SemaphoreType.DMA((2,)), pltpu.SemaphoreType.DMA(())],
    )
    return pl.pallas_call(
        _expert_kernel,
        grid_spec=grid_spec,
        out_shape=jax.ShapeDtypeStruct(xs.shape, u32),
        compiler_params=_cparams(("arbitrary",)),
        name="experts",
    )(bstart, nblk, tail, cnt, xs, w1, w3, w2)


COMBINE_RING = 3
COMBINE_CHUNK = 32


def _combine_kernel(dest_ref, dest1_ref, dest2_ref, ys_ref, rec_ref, x1_ref, gpost_ref, gate2_ref, o_ref,
                    buf, sem, *, tm):
    i = pl.program_id(0)
    last = pl.num_programs(0) - 1
    ring = COMBINE_RING

    def row_copy(d_ref, s, k, r):
        d = d_ref[TOP_K * r + k]
        return pltpu.make_async_copy(
            ys_ref.at[pl.ds(pl.multiple_of(d * ROW_TILE, ROW_TILE), ROW_TILE), :],
            buf.at[s, k, pl.ds(r * ROW_TILE, ROW_TILE), :], sem.at[s])

    def wait_tile(s):
        for k in range(TOP_K):
            pltpu.make_async_copy(ys_ref.at[pl.ds(0, tm * ROW_TILE), :], buf.at[s, k], sem.at[s]).wait()

    @pl.when(i == 0)
    def _():
        def issue(r, c):
            for k in range(TOP_K):
                row_copy(dest_ref, 0, k, r).start(priority=k)
                row_copy(dest1_ref, 1, k, r).start(priority=k)
            return c

        lax.fori_loop(0, tm, issue, 0, unroll=DMA_UNROLL)

    slot = lax.rem(i, ring)
    ahead = lax.rem(i + 2, ring)
    wait_tile(slot)

    gain = gate2_ref[...] * gpost_ref[...]
    for c in range(tm // COMBINE_CHUNK):
        rows = slice(c * COMBINE_CHUNK, (c + 1) * COMBINE_CHUNK)
        words = pl.ds(c * COMBINE_CHUNK * ROW_TILE, COMBINE_CHUNK * ROW_TILE)
        w = rec_ref[rows, :]
        y = (w[:, 4:5] * _load_position_major(buf.at[slot, 0, words], COMBINE_CHUNK)
             + w[:, 5:6] * _load_position_major(buf.at[slot, 1, words], COMBINE_CHUNK))
        o_ref[rows, :] = x1_ref[rows, :] + (y * lax.rsqrt(jnp.mean(y * y, axis=-1, keepdims=True) + EPS)) * gain
        for r in range(c * COMBINE_CHUNK, (c + 1) * COMBINE_CHUNK):
            for k in range(TOP_K):
                row_copy(dest2_ref, ahead, k, r).start(priority=k)

    @pl.when(i == last)
    def _():
        wait_tile(lax.rem(i + 1, ring))
        wait_tile(ahead)


def _combine(dest, ys, rec, x1, g_post, gate2, *, seq, tm):
    t, d = x1.shape
    last = t // tm - 1
    ahead = lambda n: pl.BlockSpec((TOP_K * tm,), lambda i: (jnp.minimum(i + n, last),), memory_space=pltpu.SMEM)
    return pl.pallas_call(
        functools.partial(_combine_kernel, tm=tm),
        grid=(t // tm,),
        in_specs=[ahead(0), ahead(1), ahead(2),
                  pl.BlockSpec(memory_space=pl.ANY),
                  pl.BlockSpec((tm, ROUTE_LANES), lambda i: (i, 0)),
                  pl.BlockSpec((tm, d), lambda i: (i, 0)),
                  pl.BlockSpec((1, d), lambda i: (0, 0)),
                  pl.BlockSpec((None, 1, d), lambda i: ((i * tm) // seq, 0, 0))],
        out_specs=pl.BlockSpec((tm, d), lambda i: (i, 0)),
        out_shape=jax.ShapeDtypeStruct((t, d), f32),
        scratch_shapes=[pltpu.VMEM((COMBINE_RING, TOP_K, tm * ROW_TILE, LANES), u32),
                        pltpu.SemaphoreType.DMA((COMBINE_RING,))],
        compiler_params=_cparams(("arbitrary",)),
        name="combine",
    )(dest, dest, dest, ys, rec, x1, g_post, gate2)


def _layer(x, c, w_ada, b_ada, g_pre_mix, g_post_mix, w_in, b_in, w_dw, b_dw, ln_conv_g, ln_conv_b,
           w_conv_out, b_conv_out, rel_bias, w_attn_out, w_out, g_pre_ffn, g_post_ffn,
           w_router_group, b_router_group, w_router_expert, b_router_expert, w1, w3, w2):
    batch, seq, d = x.shape
    t = batch * seq
    row = lambda v: v.reshape(1, -1)

    c_pad = jnp.zeros((SUBLANES, d), f32).at[:batch].set(c)
    mod = _ada(c_pad, w_ada, row(b_ada))[:batch]
    shift1, scale1, gate1, shift2, scale2, gate2 = [m.reshape(batch, 1, d) for m in jnp.split(mod, 6, axis=-1)]

    x2 = x.reshape(t, d)
    zt, qkv = _inproj(x2, row(g_pre_mix), scale1, shift1, w_in.astype(bf16), row(b_in), seq=seq, tm=INPROJ_TM)

    conv_act = _conv(zt.reshape(batch, seq, ZT_COLS), w_dw, row(b_dw), row(ln_conv_g), row(ln_conv_b),
                     tm=512).reshape(t, CONV_CH)

    os_, lses = [], []
    for gi, (_, dil) in enumerate(DILATED_GROUPS):
        o, lse = _attn_group(qkv, rel_bias.reshape(-1), gi, dil, batch=batch, seq=seq)
        os_.append(o)
        lses.append(lse)

    pad = ROUTE_LANES - N_GROUPS - N_EXPERTS
    w_r = jnp.concatenate([w_router_group, w_router_expert, jnp.zeros((d, pad), f32)], axis=1).astype(bf16)
    b_r = row(jnp.concatenate([b_router_group, b_router_expert.reshape(-1), jnp.zeros((pad,), f32)]))
    x1, h2p, logits = _merge(x2, conv_act, os_, lses, zt, w_conv_out.astype(bf16), row(b_conv_out),
                             w_attn_out.astype(bf16), w_out.astype(bf16), row(g_post_mix), gate1,
                             row(g_pre_ffn), scale2, shift2, w_r, b_r, seq=seq, tm=MERGE_TM, n_sub=MERGE_SUB)

    rec, rec_t, cnt = _route(logits, tm=512)

    counts = cnt[0, :N_EXPERTS].astype(jnp.int32)
    pcounts = (counts + MOE_ROWS - 1) // MOE_ROWS * MOE_ROWS
    pends = jnp.cumsum(pcounts)
    pstarts = pends - pcounts
    n_blocks = (t * TOP_K + N_EXPERTS * (MOE_ROWS - 1) + MOE_ROWS - 1) // MOE_ROWS
    n_used = (pends[-1:] // MOE_ROWS).astype(jnp.int32)
    tail_blocks = jnp.concatenate([n_used, n_blocks - n_used])
    eid = rec_t[0:TOP_K].astype(jnp.int32)
    rank = rec_t[TOP_K:2 * TOP_K].astype(jnp.int32)
    start_of = jnp.sum(jnp.where(eid[..., None] == jnp.arange(N_EXPERTS), pstarts, 0), axis=-1)
    dest = jnp.transpose(start_of + rank).reshape(t * TOP_K)
    tail = n_blocks - N_EXPERTS + jnp.arange(N_EXPERTS, dtype=jnp.int32)
    zero_blk = jnp.concatenate([jnp.where(pcounts > counts, pends // MOE_ROWS - 1, -1),
                                jnp.where(tail >= n_used[0], tail, -1)]).astype(jnp.int32)

    xs = _dispatch(zero_blk, dest, h2p, p_rows=n_blocks * MOE_ROWS, tm=512)
    ys = _experts((pstarts // MOE_ROWS).astype(jnp.int32), (pcounts // MOE_ROWS).astype(jnp.int32),
                  tail_blocks, counts, xs, w1, w3, w2)
    out = _combine(dest, ys, rec, x1, row(g_post_ffn), gate2, seq=seq, tm=512)
    return out.reshape(batch, seq, d)


def kernel(x, c, w_ada, b_ada, g_pre_mix, g_post_mix, w_in, b_in, w_dw, b_dw, ln_conv_g, ln_conv_b,
           w_conv_out, b_conv_out, rel_bias, w_attn_out, w_out, g_pre_ffn, g_post_ffn,
           w_router_group, b_router_group, w_router_expert, b_router_expert, w1, w3, w2):
    depth = w_ada.shape[0]
    for l in range(depth):
        pick = (lambda a: a.reshape(a.shape[1:])) if depth == 1 else (lambda a, l=l: a[l])
        x = _layer(x, c, pick(w_ada), pick(b_ada), pick(g_pre_mix), pick(g_post_mix), pick(w_in),
                   pick(b_in), pick(w_dw), pick(b_dw), pick(ln_conv_g), pick(ln_conv_b),
                   pick(w_conv_out), pick(b_conv_out), rel_bias, pick(w_attn_out), pick(w_out),
                   pick(g_pre_ffn), pick(g_post_ffn), pick(w_router_group), pick(b_router_group),
                   pick(w_router_expert), pick(b_router_expert), pick(w1), pick(w3), pick(w2))
    return x
```
